```python
import jax
import jax.numpy as jnp
from jax import lax
import numpy as np

D_MODEL = 2048
BATCH = 8
SEQ = 4096
DEPTH = 4

GRID_W = 64
CTX_LEN = 256
EPS = 1e-6
ROPE_THETA = 10000.0

MLA_HEADS = 6
MLA_Q_RANK = 512
MLA_KV_RANK = 512
MLA_NOPE = 128
MLA_ROPE = 64
MLA_V = 128
MLA_Q_BLOCK = 128

SWA_HEADS = 6
SWA_KV_HEADS = 2
SWA_GROUP = SWA_HEADS // SWA_KV_HEADS
SWA_HEAD_DIM = 128
SWA_WINDOW = 128
SWA_BLOCK = SWA_WINDOW

GLA_HEADS = 4
GLA_DK = 64
GLA_DV = 128
GLA_GATE_RANK = 16
GLA_TAU = 16.0
GLA_CHUNK = 64

MIX_WIDTH = MLA_HEADS * MLA_V + SWA_HEADS * SWA_HEAD_DIM + GLA_HEADS * GLA_DV
FFN_HIDDEN = -(-8 * D_MODEL // (3 * 256)) * 256
N_MOD = 6

IN_SIZES = (MLA_Q_RANK, MLA_KV_RANK, MLA_ROPE,
            SWA_HEADS * SWA_HEAD_DIM, SWA_KV_HEADS * SWA_HEAD_DIM, SWA_KV_HEADS * SWA_HEAD_DIM,
            GLA_HEADS * GLA_DK, GLA_HEADS * GLA_DK, GLA_HEADS * GLA_DV,
            2 * GLA_GATE_RANK, GLA_HEADS * GLA_DV)
IN_WIDTH = sum(IN_SIZES)

kernel_name = 'hybrid_mla_swa_gla_dit'


def rmsnorm(x, g):
    x32 = x.astype(jnp.float32)
    y = x32 * lax.rsqrt(jnp.mean(x32 * x32, axis=-1, keepdims=True) + EPS)
    return (y * g.astype(jnp.float32)).astype(x.dtype)


def modulate(x, g, shift, scale):
    return rmsnorm(x, g) * (1 + scale) + shift


def rope_1d(x, pos):
    half = x.shape[-1] // 2
    freqs = ROPE_THETA ** (-jnp.arange(half, dtype=jnp.float32) / half)
    ang = pos.astype(jnp.float32)[:, None] * freqs
    cos = jnp.cos(ang)[:, None, :].astype(x.dtype)
    sin = jnp.sin(ang)[:, None, :].astype(x.dtype)
    x1, x2 = x[..., :half], x[..., half:]
    return jnp.concatenate([x1 * cos - x2 * sin, x2 * cos + x1 * sin], axis=-1)


def rope_2d(x, rows, cols):
    d = x.shape[-1] // 2
    return jnp.concatenate([rope_1d(x[..., :d], rows), rope_1d(x[..., d:], cols)], axis=-1)


def mla_q(cq, g_q, w_uq, pos):
    B, N, _ = cq.shape
    q = (rmsnorm(cq, g_q) @ w_uq).reshape(B, N, MLA_HEADS, MLA_NOPE + MLA_ROPE)
    q_nope, q_rope = q[..., :MLA_NOPE], q[..., MLA_NOPE:]
    if pos is not None:
        q_rope = rope_2d(q_rope, *pos)
    return q_nope, q_rope


def mla_kv(ckv, kr, g_kv, w_ukv, pos):
    B, N, _ = ckv.shape
    kv = (rmsnorm(ckv, g_kv) @ w_ukv).reshape(B, N, MLA_HEADS, MLA_NOPE + MLA_V)
    k_rope = kr[:, :, None, :]
    if pos is not None:
        k_rope = rope_2d(k_rope, *pos)
    return kv[..., :MLA_NOPE], k_rope[:, :, 0], kv[..., MLA_NOPE:]


def mla_attend(qn, qr, kn, kr, v):
    scale = (MLA_NOPE + MLA_ROPE) ** -0.5
    s = (jnp.einsum('bqhd,bkhd->bhqk', qn, kn)
         + jnp.einsum('bqhr,bkr->bhqk', qr, kr)).astype(jnp.float32) * scale
    p = jax.nn.softmax(s, axis=-1).astype(v.dtype)
    return jnp.einsum('bhqk,bkhd->bqhd', p, v)


def mla_latent(qn, qr, kn, kr, v):
    B, N, H, _ = qn.shape
    nb = N // MLA_Q_BLOCK
    to_blocks = lambda t: jnp.moveaxis(t.reshape(B, nb, MLA_Q_BLOCK, *t.shape[2:]), 1, 0)
    out = lax.map(lambda qs: mla_attend(qs[0], qs[1], kn, kr, v), (to_blocks(qn), to_blocks(qr)))
    return jnp.moveaxis(out, 0, 1).reshape(B, N, H * MLA_V)


def sink_softmax(scores, sink):
    sink = jnp.broadcast_to(sink.astype(jnp.float32), scores.shape[:-1] + (1,))
    p = jax.nn.softmax(jnp.concatenate([scores, sink], axis=-1), axis=-1)
    return p[..., :-1]


def swa_latent(q, k, v, kc, vc, sink):
    B, N = q.shape[:2]
    nb = N // SWA_BLOCK
    band_len = 3 * SWA_BLOCK
    scale = SWA_HEAD_DIM ** -0.5
    qb = q.reshape(B, nb, SWA_BLOCK, SWA_KV_HEADS, SWA_GROUP, SWA_HEAD_DIM)

    def band(t):
        tp = jnp.pad(t, ((0, 0), (SWA_BLOCK, SWA_BLOCK), (0, 0), (0, 0)))
        tp = tp.reshape(B, nb + 2, SWA_BLOCK, SWA_KV_HEADS, SWA_HEAD_DIM)
        return jnp.concatenate([tp[:, :-2], tp[:, 1:-1], tp[:, 2:]], axis=2)

    kb, vb = band(k), band(v)
    blk = jnp.arange(nb)[:, None, None] * SWA_BLOCK
    qpos = blk + jnp.arange(SWA_BLOCK)[None, :, None]
    kpos = blk - SWA_BLOCK + jnp.arange(band_len)[None, None, :]
    valid = (jnp.abs(qpos - kpos) <= SWA_WINDOW) & (kpos >= 0) & (kpos < N)
    s_loc = jnp.einsum('bnqhgd,bnkhd->bnhgqk', qb, kb).astype(jnp.float32) * scale
    s_loc = jnp.where(valid[None, :, None, None], s_loc, -jnp.inf)
    s_ctx = jnp.einsum('bnqhgd,bkhd->bnhgqk', qb, kc).astype(jnp.float32) * scale
    sink_b = sink.reshape(SWA_KV_HEADS, SWA_GROUP)[None, None, :, :, None, None]
    p = sink_softmax(jnp.concatenate([s_loc, s_ctx], axis=-1), sink_b).astype(v.dtype)
    o = (jnp.einsum('bnhgqk,bnkhd->bnqhgd', p[..., :band_len], vb)
         + jnp.einsum('bnhgqk,bkhd->bnqhgd', p[..., band_len:], vc))
    return o.reshape(B, N, SWA_HEADS * SWA_HEAD_DIM)


def swa_context(qc, kc, vc, sink):
    B, C = qc.shape[:2]
    qg = qc.reshape(B, C, SWA_KV_HEADS, SWA_GROUP, SWA_HEAD_DIM)
    s = jnp.einsum('bqhgd,bkhd->bhgqk', qg, kc).astype(jnp.float32) * SWA_HEAD_DIM ** -0.5
    p = sink_softmax(s, sink.reshape(SWA_KV_HEADS, SWA_GROUP)[None, :, :, None, None])
    o = jnp.einsum('bhgqk,bkhd->bqhgd', p.astype(vc.dtype), vc)
    return o.reshape(B, C, SWA_HEADS * SWA_HEAD_DIM)


def gla_prepare(parts, w_f, b_f, w_b, b_b):
    q, k, v, glr, r = parts
    B, N, _ = q.shape
    shp = (B, N, GLA_HEADS, GLA_DK)
    la_f = jax.nn.log_sigmoid((glr[..., :GLA_GATE_RANK] @ w_f + b_f).astype(jnp.float32)) / GLA_TAU
    la_b = jax.nn.log_sigmoid((glr[..., GLA_GATE_RANK:] @ w_b + b_b).astype(jnp.float32)) / GLA_TAU
    return (q.reshape(shp), k.reshape(shp), v.reshape(B, N, GLA_HEADS, GLA_DV),
            la_f.reshape(shp), la_b.reshape(shp), r)


def gla_chunked(q, k, v, log_a, s0, strict, want_out):
    B, N, H, DK = q.shape
    DV = v.shape[-1]
    L = GLA_CHUNK
    nc = N // L
    f32 = jnp.float32
    qc = q.astype(f32).reshape(B, nc, L, H, DK) * DK ** -0.5
    kc = k.astype(f32).reshape(B, nc, L, H, DK)
    vc = v.astype(f32).reshape(B, nc, L, H, DV)
    b = jnp.cumsum(log_a.astype(f32).reshape(B, nc, L, H, DK), axis=2)
    b_last = b[:, :, -1]
    dS = jnp.einsum('bclhd,bclhe->bchde', kc * jnp.exp(b_last[:, :, None] - b), vc)
    decay = jnp.exp(b_last)[..., None]

    def step(S, inp):
        dec, ds = inp
        return dec * S + ds, S

    s_fin, s_in = lax.scan(step, s0, (jnp.moveaxis(decay, 1, 0), jnp.moveaxis(dS, 1, 0)))
    if not want_out:
        return None, s_fin
    s_in = jnp.moveaxis(s_in, 0, 1)
    q_in = qc * jnp.exp(b)
    k_in = kc * jnp.exp(-b)
    mask = jnp.tril(jnp.ones((L, L), dtype=bool), -1 if strict else 0)
    A = jnp.where(mask, jnp.einsum('bclhd,bcmhd->bchlm', q_in, k_in), 0.0)
    o = (jnp.einsum('bchlm,bcmhe->bclhe', A, vc)
         + jnp.einsum('bclhd,bchde->bclhe', q_in, s_in))
    return o.reshape(B, N, H, DV).astype(v.dtype), s_fin


def gla_bidir(q, k, v, la_f, la_b, s0_f, s0_b, want_out):
    flip = lambda t: jnp.flip(t, axis=1)
    o_f, s_f = gla_chunked(q, k, v, la_f, s0_f, False, want_out)
    o_b, s_b = gla_chunked(flip(q), flip(k), flip(v), flip(la_b), s0_b, True, want_out)
    o = o_f + flip(o_b) if want_out else None
    return o, s_f, s_b


def gla_output(o, r, g_out):
    B, N = o.shape[:2]
    o = rmsnorm(o, g_out.reshape(GLA_HEADS, GLA_DV)) * jax.nn.silu(r).reshape(B, N, GLA_HEADS, GLA_DV)
    return o.reshape(B, N, GLA_HEADS * GLA_DV)


def swiglu(h, w_gu, w_down):
    gu = h @ w_gu
    return (jax.nn.silu(gu[..., :FFN_HIDDEN]) * gu[..., FFN_HIDDEN:]) @ w_down


def setup_inputs(seed: int = 0) -> dict:
    key = jax.random.key(seed)
    ks = jax.random.split(key, 24)
    f32 = jnp.float32
    nrm = lambda k, shape, s: jax.random.normal(k, shape, f32) * s
    gain = lambda k, shape: 1.0 + 0.02 * jax.random.normal(k, shape, f32)
    D, L = D_MODEL, DEPTH
    return {
        'x': nrm(ks[0], (BATCH, SEQ, D), 1.0),
        'c': nrm(ks[1], (BATCH, D), 1.0),
        'ctx': nrm(ks[2], (BATCH, CTX_LEN, D), 1.0),
        'c_ctx': nrm(ks[3], (D,), 1.0),
        'w_mod': nrm(ks[4], (L, D, N_MOD * D), 0.5 * D ** -0.5),
        'b_mod': nrm(ks[5], (L, N_MOD * D), 0.02),
        'g_mix': gain(ks[6], (L, D)),
        'g_ffn': gain(ks[7], (L, D)),
        'w_in': nrm(ks[8], (L, D, IN_WIDTH), D ** -0.5),
        'g_mla_q': gain(ks[9], (L, MLA_Q_RANK)),
        'g_mla_kv': gain(ks[10], (L, MLA_KV_RANK)),
        'w_mla_uq': nrm(ks[11], (L, MLA_Q_RANK, MLA_HEADS * (MLA_NOPE + MLA_ROPE)), MLA_Q_RANK ** -0.5),
        'w_mla_ukv': nrm(ks[12], (L, MLA_KV_RANK, MLA_HEADS * (MLA_NOPE + MLA_V)), MLA_KV_RANK ** -0.5),
        'swa_sink': nrm(ks[13], (L, SWA_HEADS), 0.5),
        'w_gla_gate_f': nrm(ks[14], (L, GLA_GATE_RANK, GLA_HEADS * GLA_DK), GLA_GATE_RANK ** -0.5),
        'b_gla_gate_f': nrm(ks[15], (L, GLA_HEADS * GLA_DK), 0.1),
        'w_gla_gate_b': nrm(ks[16], (L, GLA_GATE_RANK, GLA_HEADS * GLA_DK), GLA_GATE_RANK ** -0.5),
        'b_gla_gate_b': nrm(ks[17], (L, GLA_HEADS * GLA_DK), 0.1),
        'g_gla_out': gain(ks[18], (L, GLA_HEADS * GLA_DV)),
        'w_out': nrm(ks[19], (L, MIX_WIDTH, D), MIX_WIDTH ** -0.5),
        'w_ffn_gu': nrm(ks[20], (L, D, 2 * FFN_HIDDEN), D ** -0.5),
        'w_ffn_down': nrm(ks[21], (L, FFN_HIDDEN, D), FFN_HIDDEN ** -0.5),
        'g_final': gain(ks[22], (D,)),
    }


def reference(x, c, ctx, c_ctx, w_mod, b_mod, g_mix, g_ffn, w_in, g_mla_q, g_mla_kv,
              w_mla_uq, w_mla_ukv, swa_sink, w_gla_gate_f, b_gla_gate_f, w_gla_gate_b,
              b_gla_gate_b, g_gla_out, w_out, w_ffn_gu, w_ffn_down, g_final):
    B, N, D = x.shape
    C = ctx.shape[1]
    rows = N // GRID_W
    row_ids = jnp.repeat(jnp.arange(rows, dtype=jnp.int32), GRID_W)
    col_ids = jnp.tile(jnp.arange(GRID_W, dtype=jnp.int32), rows)
    pos = (row_ids, col_ids)
    split_pts = np.cumsum(IN_SIZES)[:-1].tolist()
    state0 = jnp.zeros((B, GLA_HEADS, GLA_DK, GLA_DV), jnp.float32)
    xc = ctx
    for l in range(DEPTH):
        last = l == DEPTH - 1
        mod_l = (jax.nn.silu(c) @ w_mod[l] + b_mod[l]).reshape(B, 1, N_MOD, D)
        mod_c = (jax.nn.silu(c_ctx) @ w_mod[l] + b_mod[l]).reshape(1, 1, N_MOD, D)

        pl = jnp.split(modulate(x, g_mix[l], mod_l[:, :, 0], mod_l[:, :, 1]) @ w_in[l], split_pts, axis=-1)
        pc = jnp.split(modulate(xc, g_mix[l], mod_c[:, :, 0], mod_c[:, :, 1]) @ w_in[l], split_pts, axis=-1)

        kn_l, kr_l, v_l = mla_kv(pl[1], pl[2], g_mla_kv[l], w_mla_ukv[l], pos)
        kn_c, kr_c, v_c = mla_kv(pc[1], pc[2], g_mla_kv[l], w_mla_ukv[l], None)
        qn_l, qr_l = mla_q(pl[0], g_mla_q[l], w_mla_uq[l], pos)
        mla_l = mla_latent(qn_l, qr_l,
                           jnp.concatenate([kn_l, kn_c], axis=1),
                           jnp.concatenate([kr_l, kr_c], axis=1),
                           jnp.concatenate([v_l, v_c], axis=1))

        q_sl = rope_2d(pl[3].reshape(B, N, SWA_HEADS, SWA_HEAD_DIM), *pos)
        k_sl = rope_2d(pl[4].reshape(B, N, SWA_KV_HEADS, SWA_HEAD_DIM), *pos)
        v_sl = pl[5].reshape(B, N, SWA_KV_HEADS, SWA_HEAD_DIM)
        k_sc = pc[4].reshape(B, C, SWA_KV_HEADS, SWA_HEAD_DIM)
        v_sc = pc[5].reshape(B, C, SWA_KV_HEADS, SWA_HEAD_DIM)
        swa_l = swa_latent(q_sl, k_sl, v_sl, k_sc, v_sc, swa_sink[l])

        gw = (w_gla_gate_f[l], b_gla_gate_f[l], w_gla_gate_b[l], b_gla_gate_b[l])
        q_gc, k_gc, v_gc, af_c, ab_c, r_gc = gla_prepare(pc[6:], *gw)
        o_gc, s_f, s_b = gla_bidir(q_gc, k_gc, v_gc, af_c, ab_c, state0, state0, not last)
        q_gl, k_gl, v_gl, af_l, ab_l, r_gl = gla_prepare(pl[6:], *gw)
        o_gl, _, _ = gla_bidir(q_gl, k_gl, v_gl, af_l, ab_l, s_f, s_b, True)
        gla_l = gla_output(o_gl, r_gl, g_gla_out[l])

        mix_l = jnp.concatenate([mla_l, swa_l, gla_l], axis=-1) @ w_out[l]
        x = x + mod_l[:, :, 2] * mix_l
        h = modulate(x, g_ffn[l], mod_l[:, :, 3], mod_l[:, :, 4])
        x = x + mod_l[:, :, 5] * swiglu(h, w_ffn_gu[l], w_ffn_down[l])

        if not last:
            qn_c, qr_c = mla_q(pc[0], g_mla_q[l], w_mla_uq[l], None)
            mla_c = mla_attend(qn_c, qr_c, kn_c, kr_c, v_c).reshape(B, C, MLA_HEADS * MLA_V)
            swa_c = swa_context(pc[3], k_sc, v_sc, swa_sink[l])
            gla_c = gla_output(o_gc, r_gc, g_gla_out[l])
            mix_c = jnp.concatenate([mla_c, swa_c, gla_c], axis=-1) @ w_out[l]
            xc = xc + mod_c[:, :, 2] * mix_c
            hc = modulate(xc, g_ffn[l], mod_c[:, :, 3], mod_c[:, :, 4])
            xc = xc + mod_c[:, :, 5] * swiglu(hc, w_ffn_gu[l], w_ffn_down[l])
    return rmsnorm(x, g_final)
```

```python
import functools

import jax
import jax.numpy as jnp
from jax import lax
from jax.experimental import pallas as pl
from jax.experimental.pallas import tpu as pltpu

F32 = jnp.float32
BF16 = jnp.bfloat16

GRID_W = 64
EPS = 1e-6
ROPE_THETA = 10000.0

MLA_HEADS = 6
MLA_RANK = 512
MLA_NOPE = 128
MLA_ROPE = 64
MLA_V = 128
MLA_QK = MLA_NOPE + MLA_ROPE

SWA_HEADS = 6
SWA_KV_HEADS = 2
SWA_GROUP = SWA_HEADS // SWA_KV_HEADS
SWA_DIM = 128
SWA_BLOCK = 128

GLA_HEADS = 4
GLA_DK = 64
GLA_DV = 128
GLA_GATE_RANK = 16
GLA_TAU = 16.0
GLA_CHUNK = 64

N_MOD = 6

P_SWA_Q = 0
P_SWA_K = 768
P_CQ = 1024
P_CKV = 1536
P_SWA_V = 2048
P_MISC = 2304
P_GLA_Q = 2560
P_GLA_K = 2816
P_GLA_V = 3072
P_GLA_R = 3584
P_WIDTH = 4096
MISC_GATE_OFF = MLA_ROPE

VMEM_LIMIT = 56 * 1024 * 1024


def _params(n_axes):
    return pltpu.CompilerParams(dimension_semantics=("arbitrary",) * n_axes,
                                vmem_limit_bytes=VMEM_LIMIT)


def _silu(x):
    return x / (1.0 + jnp.exp(-x))


def _rms(x, g):
    ms = jnp.mean(x * x, axis=-1, keepdims=True)
    return x * lax.rsqrt(ms + EPS) * g


def _dot(a, b):
    return jnp.dot(a, b, preferred_element_type=F32)


def _dot_nt(a, b):
    return lax.dot_general(a, b, (((1,), (1,)), ((), ())), preferred_element_type=F32)


def _dot_tn(a, b):
    return lax.dot_general(a, b, (((0,), (0,)), ((), ())), preferred_element_type=F32)


def _mod_kernel(c_ref, w_ref, b_ref, o_ref):
    a = _silu(c_ref[...]).astype(BF16)
    o_ref[0] = _dot(a, w_ref[0].astype(BF16)) + b_ref[0]


def _modulation(cvec, w_mod, b_mod):
    depth, d, width = w_mod.shape
    rows = cvec.shape[0]
    tn = 1024
    return pl.pallas_call(
        _mod_kernel,
        grid=(depth, width // tn),
        in_specs=[pl.BlockSpec((rows, d), lambda l, j: (0, 0)),
                  pl.BlockSpec((1, d, tn), lambda l, j: (l, 0, j)),
                  pl.BlockSpec((1, 1, tn), lambda l, j: (l, 0, j))],
        out_specs=pl.BlockSpec((1, rows, tn), lambda l, j: (l, 0, j)),
        out_shape=jax.ShapeDtypeStruct((depth, rows, width), F32),
        compiler_params=_params(2),
        name="modulation",
    )(cvec, w_mod, b_mod.reshape(depth, 1, width))


def _inproj_kernel(x_ref, mod_ref, g_ref, w_ref, o_ref):
    h = _rms(x_ref[0], g_ref[...]) * (1.0 + mod_ref[0, 1:2, :]) + mod_ref[0, 0:1, :]
    o_ref[0] = _dot(h.astype(BF16), w_ref[...])


def _in_projection(x, mod, g, w):
    bx, r, d = x.shape
    tm, tn = 512, 2048
    return pl.pallas_call(
        _inproj_kernel,
        grid=(P_WIDTH // tn, bx, r // tm),
        in_specs=[pl.BlockSpec((1, tm, d), lambda j, b, t: (b, t, 0)),
                  pl.BlockSpec((1, N_MOD, d), lambda j, b, t: (b, 0, 0)),
                  pl.BlockSpec((1, d), lambda j, b, t: (0, 0)),
                  pl.BlockSpec((d, tn), lambda j, b, t: (0, j))],
        out_specs=pl.BlockSpec((1, tm, tn), lambda j, b, t: (b, t, j)),
        out_shape=jax.ShapeDtypeStruct((bx, r, P_WIDTH), F32),
        compiler_params=_params(3),
        name="in_projection",
    )(x, mod, g, w)


def _rope(x, cos, sin, half):
    lane = lax.broadcasted_iota(jnp.int32, x.shape, 1)
    first = (lane % (2 * half)) < half
    width = x.shape[1]
    rot = jnp.where(first, pltpu.roll(x, width - half, 1), pltpu.roll(x, half, 1))
    return x * cos + rot * sin


def _mla_prep_kernel(cq_ref, ckv_ref, misc_ref, cos_ref, sin_ref, gq_ref, gkv_ref, wuq_ref, wukv_ref,
                     q_ref, k_ref, v_ref):
    scale = MLA_QK ** -0.5
    cos, sin = cos_ref[...], sin_ref[...]
    half = MLA_ROPE // 4
    qf = _dot(_rms(cq_ref[0], gq_ref[...]).astype(BF16), wuq_ref[...])
    kvf = _dot(_rms(ckv_ref[0], gkv_ref[...]).astype(BF16), wukv_ref[...])
    kr = _rope(misc_ref[0], cos, sin, half)[:, :MLA_ROPE].astype(BF16)
    nope_w = MLA_HEADS * MLA_NOPE
    for pair in range(MLA_HEADS // 2):
        qr = _rope(qf[:, nope_w + 128 * pair: nope_w + 128 * (pair + 1)], cos, sin, half) * scale
        for j in range(2):
            h = 2 * pair + j
            q_ref[0, h, :, MLA_NOPE:MLA_QK] = qr[:, MLA_ROPE * j: MLA_ROPE * (j + 1)].astype(BF16)
    for h in range(MLA_HEADS):
        q_ref[0, h, :, 0:MLA_NOPE] = (qf[:, MLA_NOPE * h: MLA_NOPE * (h + 1)] * scale).astype(BF16)
        k_ref[0, h, :, 0:MLA_NOPE] = kvf[:, MLA_NOPE * h: MLA_NOPE * (h + 1)].astype(BF16)
        k_ref[0, h, :, MLA_NOPE:MLA_QK] = kr
        v_ref[0, h] = kvf[:, nope_w + MLA_V * h: nope_w + MLA_V * (h + 1)].astype(BF16)


def _mla_prep(proj, cos, sin, g_q, g_kv, w_uq, w_ukv):
    b, r, _ = proj.shape
    tm = min(r, 512)
    rk = MLA_RANK
    const = lambda shape: pl.BlockSpec(shape, lambda i, t: (0,) * len(shape))
    return pl.pallas_call(
        _mla_prep_kernel,
        grid=(b, r // tm),
        in_specs=[pl.BlockSpec((1, tm, rk), lambda i, t: (i, t, P_CQ // rk)),
                  pl.BlockSpec((1, tm, rk), lambda i, t: (i, t, P_CKV // rk)),
                  pl.BlockSpec((1, tm, 128), lambda i, t: (i, t, P_MISC // 128)),
                  pl.BlockSpec((tm, 128), lambda i, t: (t, 0)),
                  pl.BlockSpec((tm, 128), lambda i, t: (t, 0)),
                  const((1, rk)), const((1, rk)),
                  const(w_uq.shape), const(w_ukv.shape)],
        out_specs=[pl.BlockSpec((1, MLA_HEADS, tm, MLA_QK), lambda i, t: (i, 0, t, 0)),
                   pl.BlockSpec((1, MLA_HEADS, tm, MLA_QK), lambda i, t: (i, 0, t, 0)),
                   pl.BlockSpec((1, MLA_HEADS, tm, MLA_V), lambda i, t: (i, 0, t, 0))],
        out_shape=[jax.ShapeDtypeStruct((b, MLA_HEADS, r, MLA_QK), BF16),
                   jax.ShapeDtypeStruct((b, MLA_HEADS, r, MLA_QK), BF16),
                   jax.ShapeDtypeStruct((b, MLA_HEADS, r, MLA_V), BF16)],
        compiler_params=_params(2),
        name="mla_prep",
    )(proj, proj, proj, cos, sin, g_q, g_kv, w_uq, w_ukv)


def _softmax_step(q, k, v, m, l, acc):
    s = _dot_nt(q, k)
    m_new = jnp.maximum(m, jnp.max(s, axis=-1, keepdims=True))
    alpha = jnp.exp(m - m_new)
    p = jnp.exp(s - m_new)
    l = alpha * l + jnp.sum(p, axis=-1, keepdims=True)
    acc = alpha * acc + _dot(p.astype(BF16), v)
    return m_new, l, acc


def _mla_attn_kernel(q_ref, kl_ref, vl_ref, kc_ref, vc_ref, o_ref, *, tk):
    q = q_ref[0, 0]
    tq = q.shape[0]
    n_chunks = kl_ref.shape[2] // tk

    def body(j, carry):
        start = pl.multiple_of(j * tk, tk)
        return _softmax_step(q, kl_ref[0, 0, pl.ds(start, tk), :], vl_ref[0, 0, pl.ds(start, tk), :], *carry)

    init = (jnp.full((tq, 1), -jnp.inf, F32), jnp.zeros((tq, 1), F32), jnp.zeros((tq, MLA_V), F32))
    carry = lax.fori_loop(0, n_chunks, body, init)
    _, l, acc = _softmax_step(q, kc_ref[0, 0], vc_ref[0, 0], *carry)
    o_ref[0] = (acc / l).astype(o_ref.dtype)


def _mla_attention(q, k_l, v_l, k_c, v_c):
    b, h, n, dq = q.shape
    c = k_c.shape[2]
    tq, tk = 256, 512
    return pl.pallas_call(
        functools.partial(_mla_attn_kernel, tk=tk),
        grid=(b, h, n // tq),
        in_specs=[pl.BlockSpec((1, 1, tq, dq), lambda i, j, t: (i, j, t, 0)),
                  pl.BlockSpec((1, 1, n, dq), lambda i, j, t: (i, j, 0, 0)),
                  pl.BlockSpec((1, 1, n, MLA_V), lambda i, j, t: (i, j, 0, 0)),
                  pl.BlockSpec((1, 1, c, dq), lambda i, j, t: (i, j, 0, 0)),
                  pl.BlockSpec((1, 1, c, MLA_V), lambda i, j, t: (i, j, 0, 0))],
        out_specs=pl.BlockSpec((1, tq, MLA_V), lambda i, j, t: (i, t, j)),
        out_shape=jax.ShapeDtypeStruct((b, n, h * MLA_V), BF16),
        compiler_params=_params(3),
        name="mla_attention",
    )(q, k_l, v_l, k_c, v_c)


def _ctx_attn_kernel(*refs, q_scale, has_sink):
    if has_sink:
        sink_ref, q_ref, k_ref, v_ref, o_ref = refs
    else:
        q_ref, k_ref, v_ref, o_ref = refs
    lead = (0,) * (len(q_ref.shape) - 2)
    q = q_ref[lead]
    k = k_ref[lead].astype(BF16)
    v = v_ref[lead].astype(BF16)
    if q_scale is not None:
        q = q * q_scale
    s = _dot_nt(q.astype(BF16), k)
    m = jnp.max(s, axis=-1, keepdims=True)
    if has_sink:
        sink = sink_ref[0, pl.program_id(1)]
        m = jnp.maximum(m, sink)
    p = jnp.exp(s - m)
    l = jnp.sum(p, axis=-1, keepdims=True)
    if has_sink:
        l = l + jnp.exp(sink - m)
    o = _dot(p.astype(BF16), v) / l
    o_ref[0] = o.astype(o_ref.dtype)


def _mla_ctx_attention(q, k, v):
    b, h, c, dq = q.shape
    return pl.pallas_call(
        functools.partial(_ctx_attn_kernel, q_scale=None, has_sink=False),
        grid=(b, h),
        in_specs=[pl.BlockSpec((1, 1, c, dq), lambda i, j: (i, j, 0, 0)),
                  pl.BlockSpec((1, 1, c, dq), lambda i, j: (i, j, 0, 0)),
                  pl.BlockSpec((1, 1, c, MLA_V), lambda i, j: (i, j, 0, 0))],
        out_specs=pl.BlockSpec((1, c, MLA_V), lambda i, j: (i, 0, j)),
        out_shape=jax.ShapeDtypeStruct((b, c, h * MLA_V), BF16),
        compiler_params=_params(2),
        name="mla_ctx_attention",
    )(q, k, v)


def _swa_ctx_attention(proj_c, sink):
    b, c, _ = proj_c.shape
    d = SWA_DIM
    return pl.pallas_call(
        functools.partial(_ctx_attn_kernel, q_scale=d ** -0.5, has_sink=True),
        grid=(b, SWA_HEADS),
        in_specs=[pl.BlockSpec(memory_space=pltpu.SMEM),
                  pl.BlockSpec((1, c, d), lambda i, j: (i, 0, P_SWA_Q // d + j)),
                  pl.BlockSpec((1, c, d), lambda i, j: (i, 0, P_SWA_K // d + j // SWA_GROUP)),
                  pl.BlockSpec((1, c, d), lambda i, j: (i, 0, P_SWA_V // d + j // SWA_GROUP))],
        out_specs=pl.BlockSpec((1, c, d), lambda i, j: (i, 0, j)),
        out_shape=jax.ShapeDtypeStruct((b, c, SWA_HEADS * d), BF16),
        compiler_params=_params(2),
        name="swa_ctx_attention",
    )(sink, proj_c, proj_c, proj_c)


def _swa_kernel(sink_ref, q_ref, kp_ref, ko_ref, kn_ref, vp_ref, vo_ref, vn_ref, kc_ref, vc_ref,
                cp_ref, co_ref, cn_ref, sp_ref, so_ref, sn_ref, o_ref):
    g = pl.program_id(1)
    n = pl.program_id(2)
    nb = pl.num_programs(2)
    blk, d = SWA_BLOCK, SWA_DIM
    half = d // 4
    scale = d ** -0.5
    cos_o, sin_o = co_ref[...], so_ref[...]
    q = jnp.concatenate(
        [(_rope(q_ref[0, :, d * i: d * (i + 1)], cos_o, sin_o, half) * scale).astype(BF16)
         for i in range(SWA_GROUP)], axis=0)
    k = jnp.concatenate(
        [_rope(kp_ref[0], cp_ref[...], sp_ref[...], half).astype(BF16),
         _rope(ko_ref[0], cos_o, sin_o, half).astype(BF16),
         _rope(kn_ref[0], cn_ref[...], sn_ref[...], half).astype(BF16),
         kc_ref[0].astype(BF16)], axis=0)
    v = jnp.concatenate([vp_ref[0], vo_ref[0], vn_ref[0], vc_ref[0]], axis=0).astype(BF16)
    s = _dot_nt(q, k)
    row = lax.broadcasted_iota(jnp.int32, s.shape, 0) % blk
    col = lax.broadcasted_iota(jnp.int32, s.shape, 1)
    prev_off = jnp.where(n > 0, 0, blk)
    next_off = jnp.where(n < nb - 1, 0, blk)
    valid = ((col < blk) & (col >= row + prev_off)) \
        | ((col >= blk) & (col < 2 * blk)) \
        | ((col >= 2 * blk) & (col < 3 * blk) & (col - 2 * blk <= row - next_off)) \
        | (col >= 3 * blk)
    s = jnp.where(valid, s, -jnp.inf)
    sink = jnp.concatenate(
        [jnp.full((blk, 1), sink_ref[0, g * SWA_GROUP + i], F32) for i in range(SWA_GROUP)], axis=0)
    m = jnp.maximum(jnp.max(s, axis=-1, keepdims=True), sink)
    p = jnp.exp(s - m)
    l = jnp.sum(p, axis=-1, keepdims=True) + jnp.exp(sink - m)
    o = _dot(p.astype(BF16), v) / l
    for i in range(SWA_GROUP):
        o_ref[0, :, d * i: d * (i + 1)] = o[blk * i: blk * (i + 1)].astype(o_ref.dtype)


def _swa_attention(proj_l, proj_c, cos, sin, sink):
    b, n, _ = proj_l.shape
    c = proj_c.shape[1]
    blk, d = SWA_BLOCK, SWA_DIM
    nb = n // blk
    kcol = lambda g: P_SWA_K // d + g
    vcol = lambda g: P_SWA_V // d + g
    prev = lambda t: jnp.maximum(t - 1, 0)
    nxt = lambda t: jnp.minimum(t + 1, nb - 1)
    tile = lambda rowf, colf: pl.BlockSpec((1, blk, d), lambda i, g, t: (i, rowf(t), colf(g)))
    tab = lambda rowf: pl.BlockSpec((blk, d), lambda i, g, t: (rowf(t), 0))
    same = lambda t: t
    return pl.pallas_call(
        _swa_kernel,
        grid=(b, SWA_KV_HEADS, nb),
        in_specs=[pl.BlockSpec(memory_space=pltpu.SMEM),
                  pl.BlockSpec((1, blk, SWA_GROUP * d), lambda i, g, t: (i, t, g)),
                  tile(prev, kcol), tile(same, kcol), tile(nxt, kcol),
                  tile(prev, vcol), tile(same, vcol), tile(nxt, vcol),
                  pl.BlockSpec((1, c, d), lambda i, g, t: (i, 0, kcol(g))),
                  pl.BlockSpec((1, c, d), lambda i, g, t: (i, 0, vcol(g))),
                  tab(prev), tab(same), tab(nxt), tab(prev), tab(same), tab(nxt)],
        out_specs=pl.BlockSpec((1, blk, SWA_GROUP * d), lambda i, g, t: (i, t, g)),
        out_shape=jax.ShapeDtypeStruct((b, n, SWA_HEADS * d), BF16),
        compiler_params=_params(3),
        name="swa_attention",
    )(sink, proj_l, proj_l, proj_l, proj_l, proj_l, proj_l, proj_l, proj_c, proj_c,
      cos, cos, cos, sin, sin, sin)


def _log_sigmoid(x):
    return jnp.minimum(x, 0.0) - jnp.log(1.0 + jnp.exp(-jnp.abs(x)))


def _gla_direction(q_ref, k_ref, v_ref, misc_ref, wg, bg, state_ref, o_ref, reverse):
    rows = q_ref.shape[1]
    n_chunks = rows // GLA_CHUNK
    L = GLA_CHUNK
    la = _log_sigmoid(_dot(misc_ref[0].astype(BF16), wg) + bg) * (1.0 / GLA_TAU)
    ti = lax.broadcasted_iota(jnp.int32, (L, L), 0)
    tj = lax.broadcasted_iota(jnp.int32, (L, L), 1)
    if reverse:
        cum = (tj >= ti).astype(F32)
        amask = tj > ti
    else:
        cum = (tj <= ti).astype(F32)
        amask = tj <= ti
    scale = GLA_DK ** -0.5
    order = range(n_chunks - 1, -1, -1) if reverse else range(n_chunks)
    for c in order:
        sl = slice(c * L, (c + 1) * L)
        bcum = jnp.dot(cum, la[sl], preferred_element_type=F32, precision=lax.Precision.HIGHEST)
        b_last = bcum[0:1] if reverse else bcum[L - 1:L]
        q_in = q_ref[0, sl, :] * scale * jnp.exp(bcum)
        k_c = k_ref[0, sl, :]
        k_in = k_c * jnp.exp(-bcum)
        k_dec = k_c * jnp.exp(b_last - bcum)
        decay = jnp.exp(b_last)
        for h in range(GLA_HEADS):
            ks = slice(h * GLA_DK, (h + 1) * GLA_DK)
            vs = slice(h * GLA_DV, (h + 1) * GLA_DV)
            qh = q_in[:, ks].astype(BF16)
            vh = v_ref[0, sl, vs].astype(BF16)
            a = jnp.where(amask, _dot_nt(qh, k_in[:, ks].astype(BF16)), 0.0)
            st = state_ref[h]
            o_ref[0, sl, vs] = _dot(a.astype(BF16), vh) + _dot_nt(qh, st.astype(BF16))
            state_ref[h] = decay[:, ks] * st + _dot_tn(vh, k_dec[:, ks].astype(BF16))


def _gla_kernel(qf_ref, kf_ref, vf_ref, mf_ref, qb_ref, kb_ref, vb_ref, mb_ref, wg_ref, bg_ref,
                sf0_ref, sb0_ref, of_ref, ob_ref, sf_ref, sb_ref, stf, stb):
    t = pl.program_id(1)

    @pl.when(t == 0)
    def _():
        stf[...] = sf0_ref[0]
        stb[...] = sb0_ref[0]

    width = GLA_HEADS * GLA_DK
    _gla_direction(qf_ref, kf_ref, vf_ref, mf_ref, wg_ref[:, :width], bg_ref[:, :width], stf, of_ref, False)
    _gla_direction(qb_ref, kb_ref, vb_ref, mb_ref, wg_ref[:, width:], bg_ref[:, width:], stb, ob_ref, True)

    @pl.when(t == pl.num_programs(1) - 1)
    def _():
        sf_ref[0] = stf[...]
        sb_ref[0] = stb[...]


def _gla_scan(proj, wg, bg, sf0, sb0):
    b, r, _ = proj.shape
    tm = 256
    nt = r // tm
    kw, vw = GLA_HEADS * GLA_DK, GLA_HEADS * GLA_DV
    fwd = lambda t: t
    bwd = lambda t: nt - 1 - t

    def specs(order):
        return [pl.BlockSpec((1, tm, kw), lambda i, t: (i, order(t), P_GLA_Q // kw)),
                pl.BlockSpec((1, tm, kw), lambda i, t: (i, order(t), P_GLA_K // kw)),
                pl.BlockSpec((1, tm, vw), lambda i, t: (i, order(t), P_GLA_V // vw)),
                pl.BlockSpec((1, tm, 128), lambda i, t: (i, order(t), P_MISC // 128))]

    state_spec = pl.BlockSpec((1, GLA_HEADS, GLA_DV, GLA_DK), lambda i, t: (i, 0, 0, 0))
    state_shape = jax.ShapeDtypeStruct((b, GLA_HEADS, GLA_DV, GLA_DK), F32)
    return pl.pallas_call(
        _gla_kernel,
        grid=(b, nt),
        in_specs=specs(fwd) + specs(bwd) + [
            pl.BlockSpec(wg.shape, lambda i, t: (0, 0)),
            pl.BlockSpec(bg.shape, lambda i, t: (0, 0)),
            state_spec, state_spec],
        out_specs=[pl.BlockSpec((1, tm, vw), lambda i, t: (i, fwd(t), 0)),
                   pl.BlockSpec((1, tm, vw), lambda i, t: (i, bwd(t), 0)),
                   state_spec, state_spec],
        out_shape=[jax.ShapeDtypeStruct((b, r, vw), F32), jax.ShapeDtypeStruct((b, r, vw), F32),
                   state_shape, state_shape],
        scratch_shapes=[pltpu.VMEM((GLA_HEADS, GLA_DV, GLA_DK), F32),
                        pltpu.VMEM((GLA_HEADS, GLA_DV, GLA_DK), F32)],
        compiler_params=_params(2),
        name="gla_scan",
    )(proj, proj, proj, proj, proj, proj, proj, proj, wg, bg, sf0, sb0)


def _outproj_kernel(x_ref, mod_ref, mla_ref, swa_ref, of_ref, ob_ref, r_ref, g_ref, w_ref, o_ref):
    o = of_ref[0] + ob_ref[0]
    gate = _silu(r_ref[0])
    parts = []
    for h in range(GLA_HEADS):
        vs = slice(h * GLA_DV, (h + 1) * GLA_DV)
        parts.append((_rms(o[:, vs], g_ref[:, vs]) * gate[:, vs]).astype(BF16))
    gla = jnp.concatenate(parts, axis=1)
    a, b = mla_ref.shape[2], mla_ref.shape[2] + swa_ref.shape[2]
    mix = _dot(mla_ref[0], w_ref[0:a, :]) + _dot(swa_ref[0], w_ref[a:b, :]) + _dot(gla, w_ref[b:, :])
    o_ref[0] = x_ref[0] + mod_ref[0, 2:3, :] * mix


def _out_projection(x, mod, mla, swa, o_f, o_b, proj, g_out, w_out):
    bx, r, d = x.shape
    tm = 512
    row = lambda width: pl.BlockSpec((1, tm, width), lambda b, t: (b, t, 0))
    gw = GLA_HEADS * GLA_DV
    return pl.pallas_call(
        _outproj_kernel,
        grid=(bx, r // tm),
        in_specs=[row(d),
                  pl.BlockSpec((1, N_MOD, d), lambda b, t: (b, 0, 0)),
                  row(mla.shape[2]), row(swa.shape[2]), row(gw), row(gw),
                  pl.BlockSpec((1, tm, gw), lambda b, t: (b, t, P_GLA_R // gw)),
                  pl.BlockSpec((1, gw), lambda b, t: (0, 0)),
                  pl.BlockSpec(w_out.shape, lambda b, t: (0, 0))],
        out_specs=row(d),
        out_shape=jax.ShapeDtypeStruct(x.shape, F32),
        compiler_params=_params(2),
        name="out_projection",
    )(x, mod, mla, swa, o_f, o_b, proj, g_out, w_out)


def _ffn_kernel(x_ref, mod_ref, g_ref, wg_ref, wu_ref, wd_ref, gf_ref, o_ref, h_ref, acc_ref, *, final_norm):
    f = pl.program_id(2)

    @pl.when(f == 0)
    def _():
        h = _rms(x_ref[0], g_ref[...]) * (1.0 + mod_ref[0, 4:5, :]) + mod_ref[0, 3:4, :]
        h_ref[...] = h.astype(BF16)
        acc_ref[...] = jnp.zeros_like(acc_ref)

    h = h_ref[...]
    act = _silu(_dot(h, wg_ref[...])) * _dot(h, wu_ref[...])
    acc_ref[...] += _dot(act.astype(BF16), wd_ref[...])

    @pl.when(f == pl.num_programs(2) - 1)
    def _():
        y = x_ref[0] + mod_ref[0, 5:6, :] * acc_ref[...]
        if final_norm:
            y = _rms(y, gf_ref[...])
        o_ref[0] = y


def _ffn(x, mod, g, w_gu, w_down, g_final, final_norm):
    bx, r, d = x.shape
    hidden = w_down.shape[0]
    tm, tf = 512, 512
    nf = hidden // tf
    return pl.pallas_call(
        functools.partial(_ffn_kernel, final_norm=final_norm),
        grid=(bx, r // tm, nf),
        in_specs=[pl.BlockSpec((1, tm, d), lambda b, t, f: (b, t, 0)),
                  pl.BlockSpec((1, N_MOD, d), lambda b, t, f: (b, 0, 0)),
                  pl.BlockSpec((1, d), lambda b, t, f: (0, 0)),
                  pl.BlockSpec((d, tf), lambda b, t, f: (0, f)),
                  pl.BlockSpec((d, tf), lambda b, t, f: (0, nf + f)),
                  pl.BlockSpec((tf, d), lambda b, t, f: (f, 0)),
                  pl.BlockSpec((1, d), lambda b, t, f: (0, 0))],
        out_specs=pl.BlockSpec((1, tm, d), lambda b, t, f: (b, t, 0)),
        out_shape=jax.ShapeDtypeStruct(x.shape, F32),
        scratch_shapes=[pltpu.VMEM((tm, d), BF16), pltpu.VMEM((tm, d), F32)],
        compiler_params=_params(3),
        name="ffn",
    )(x, mod, g, w_gu, w_gu, w_down, g_final)


def _rope_tables(n, dim):
    half = dim // 4
    freqs = ROPE_THETA ** (-jnp.arange(half, dtype=F32) / half)
    pos = jnp.arange(n, dtype=jnp.int32)
    ang_r = (pos // GRID_W).astype(F32)[:, None] * freqs
    ang_c = (pos % GRID_W).astype(F32)[:, None] * freqs
    cos = jnp.concatenate([jnp.cos(ang_r)] * 2 + [jnp.cos(ang_c)] * 2, axis=1)
    sin = jnp.concatenate([-jnp.sin(ang_r), jnp.sin(ang_r), -jnp.sin(ang_c), jnp.sin(ang_c)], axis=1)
    reps = 128 // dim
    return jnp.tile(cos, (1, reps)), jnp.tile(sin, (1, reps))


def _relayout_w_in(w_in):
    depth, d, _ = w_in.shape
    sizes = (MLA_RANK, MLA_RANK, MLA_ROPE, SWA_HEADS * SWA_DIM, SWA_KV_HEADS * SWA_DIM, SWA_KV_HEADS * SWA_DIM,
             GLA_HEADS * GLA_DK, GLA_HEADS * GLA_DK, GLA_HEADS * GLA_DV, 2 * GLA_GATE_RANK, GLA_HEADS * GLA_DV)
    offs = [0]
    for s in sizes:
        offs.append(offs[-1] + s)
    part = lambda i: w_in[:, :, offs[i]:offs[i + 1]]
    cq, ckv, kr, sq, sk, sv, gq, gk, gv, glr, gr = (part(i) for i in range(len(sizes)))
    zeros = lambda w: jnp.zeros((depth, d, w), w_in.dtype)
    misc_pad = 128 - MLA_ROPE - 2 * GLA_GATE_RANK
    out = jnp.concatenate([sq, sk, cq, ckv, sv, kr, glr, zeros(misc_pad), zeros(P_GLA_Q - P_MISC - 128),
                           gq, gk, gv, gr], axis=-1)
    assert out.shape[-1] == P_WIDTH
    return out.astype(BF16)


def _relayout_mla(w_uq, w_ukv):
    depth, rk, _ = w_uq.shape
    uq = w_uq.reshape(depth, rk, MLA_HEADS, MLA_QK)
    uq = jnp.concatenate([uq[..., :MLA_NOPE].reshape(depth, rk, -1), uq[..., MLA_NOPE:].reshape(depth, rk, -1)], -1)
    ukv = w_ukv.reshape(depth, rk, MLA_HEADS, MLA_NOPE + MLA_V)
    ukv = jnp.concatenate([ukv[..., :MLA_NOPE].reshape(depth, rk, -1), ukv[..., MLA_NOPE:].reshape(depth, rk, -1)], -1)
    return uq.astype(BF16), ukv.astype(BF16)


def _relayout_gate(w_f, b_f, w_b, b_b):
    depth, rank, width = w_f.shape
    wg = jnp.zeros((depth, 128, 2 * width), F32)
    wg = wg.at[:, MISC_GATE_OFF:MISC_GATE_OFF + rank, :width].set(w_f)
    wg = wg.at[:, MISC_GATE_OFF + rank:MISC_GATE_OFF + 2 * rank, width:].set(w_b)
    bg = jnp.concatenate([b_f, b_b], axis=-1).reshape(depth, 1, 2 * width)
    return wg.astype(BF16), bg


def kernel(x, c, ctx, c_ctx, w_mod, b_mod, g_mix, g_ffn, w_in, g_mla_q, g_mla_kv, w_mla_uq, w_mla_ukv,
           swa_sink, w_gla_gate_f, b_gla_gate_f, w_gla_gate_b, b_gla_gate_b, g_gla_out, w_out, w_ffn_gu,
           w_ffn_down, g_final):
    B, N, D = x.shape
    C = ctx.shape[1]
    depth = w_mod.shape[0]

    cvec = jnp.concatenate([c, c_ctx[None, :], jnp.zeros((16 - B - 1, D), F32)], axis=0)
    mod = _modulation(cvec, w_mod, b_mod).reshape(depth, 16, N_MOD, D)

    w_in_p = _relayout_w_in(w_in)
    w_uq_p, w_ukv_p = _relayout_mla(w_mla_uq, w_mla_ukv)
    wg_p, bg_p = _relayout_gate(w_gla_gate_f, b_gla_gate_f, w_gla_gate_b, b_gla_gate_b)
    w_out_b = w_out.astype(BF16)
    w_gu_b = w_ffn_gu.astype(BF16)
    w_down_b = w_ffn_down.astype(BF16)

    cos_m, sin_m = _rope_tables(N, MLA_ROPE)
    cos_s, sin_s = _rope_tables(N, SWA_DIM)
    cos_id, sin_id = jnp.ones((C, 128), F32), jnp.zeros((C, 128), F32)
    state0 = jnp.zeros((B, GLA_HEADS, GLA_DV, GLA_DK), F32)
    g_fin = g_final.reshape(1, D)

    xc = ctx.reshape(1, B * C, D)
    for l in range(depth):
        last = l == depth - 1
        mod_l, mod_c = mod[l, :B], mod[l, B:B + 1]
        g_mix_l, g_ffn_l = g_mix[l].reshape(1, D), g_ffn[l].reshape(1, D)
        g_q, g_kv = g_mla_q[l].reshape(1, -1), g_mla_kv[l].reshape(1, -1)
        g_out = g_gla_out[l].reshape(1, -1)
        sink = swa_sink[l].reshape(1, -1)

        proj_l = _in_projection(x, mod_l, g_mix_l, w_in_p[l])
        proj_c = _in_projection(xc, mod_c, g_mix_l, w_in_p[l]).reshape(B, C, P_WIDTH)

        q_l, k_l, v_l = _mla_prep(proj_l, cos_m, sin_m, g_q, g_kv, w_uq_p[l], w_ukv_p[l])
        q_c, k_c, v_c = _mla_prep(proj_c, cos_id, sin_id, g_q, g_kv, w_uq_p[l], w_ukv_p[l])
        mla_l = _mla_attention(q_l, k_l, v_l, k_c, v_c)
        swa_l = _swa_attention(proj_l, proj_c, cos_s, sin_s, sink)

        of_c, ob_c, s_f, s_b = _gla_scan(proj_c, wg_p[l], bg_p[l], state0, state0)
        of_l, ob_l, _, _ = _gla_scan(proj_l, wg_p[l], bg_p[l], s_f, s_b)

        x = _out_projection(x, mod_l, mla_l, swa_l, of_l, ob_l, proj_l, g_out, w_out_b[l])
        x = _ffn(x, mod_l, g_ffn_l, w_gu_b[l], w_down_b[l], g_fin, last)

        if not last:
            mla_c = _mla_ctx_attention(q_c, k_c, v_c)
            swa_c = _swa_ctx_attention(proj_c, sink)
            flat = lambda a: a.reshape(1, B * C, a.shape[-1])
            xc = _out_projection(xc, mod_c, flat(mla_c), flat(swa_c), flat(of_c), flat(ob_c), flat(proj_c),
                                 g_out, w_out_b[l])
            xc = _ffn(xc, mod_c, g_ffn_l, w_gu_b[l], w_down_b[l], g_fin, False)
    return x
```

```python
import functools

import jax
import jax.numpy as jnp
from jax import lax
from jax.experimental import pallas as pl
from jax.experimental.pallas import tpu as pltpu

F32 = jnp.float32
BF16 = jnp.bfloat16

GRID_W = 64
EPS = 1e-6
ROPE_THETA = 10000.0
LOG2E = 1.4426950408889634

MLA_HEADS = 6
MLA_RANK = 512
MLA_NOPE = 128
MLA_ROPE = 64
MLA_V = 128
MLA_QK = MLA_NOPE + MLA_ROPE

SWA_HEADS = 6
SWA_KV_HEADS = 2
SWA_GROUP = SWA_HEADS // SWA_KV_HEADS
SWA_DIM = 128
SWA_BLOCK = 128

GLA_HEADS = 4
GLA_DK = 64
GLA_DV = 128
GLA_GATE_RANK = 16
GLA_TAU = 16.0
GLA_CHUNK = 64

N_MOD = 6

P_SWA_Q = 0
P_SWA_K = 768
P_CQ = 1024
P_CKV = 1536
P_SWA_V = 2048
P_MISC = 2304
P_GLA_Q = 2560
P_GLA_K = 2816
P_GLA_V = 3072
P_GLA_R = 3584
P_WIDTH = 4096
MISC_GATE_OFF = MLA_ROPE

VMEM_LIMIT = 56 * 1024 * 1024


def _params(n_axes):
    return pltpu.CompilerParams(dimension_semantics=("arbitrary",) * n_axes,
                                vmem_limit_bytes=VMEM_LIMIT)


def _silu(x):
    return x / (1.0 + jnp.exp(-x))


def _rms(x, g):
    ms = jnp.mean(x * x, axis=-1, keepdims=True)
    return x * lax.rsqrt(ms + EPS) * g


def _dot(a, b):
    return jnp.dot(a, b, preferred_element_type=F32)


def _dot_nt(a, b):
    return lax.dot_general(a, b, (((1,), (1,)), ((), ())), preferred_element_type=F32)


def _dot_tn(a, b):
    return lax.dot_general(a, b, (((0,), (0,)), ((), ())), preferred_element_type=F32)


def _mod_kernel(c_ref, w_ref, b_ref, o_ref):
    a = _silu(c_ref[...]).astype(BF16)
    o_ref[0] = _dot(a, w_ref[0].astype(BF16)) + b_ref[0]


def _modulation(cvec, w_mod, b_mod):
    depth, d, width = w_mod.shape
    rows = cvec.shape[0]
    tn = 1024
    return pl.pallas_call(
        _mod_kernel,
        grid=(depth, width // tn),
        in_specs=[pl.BlockSpec((rows, d), lambda l, j: (0, 0)),
                  pl.BlockSpec((1, d, tn), lambda l, j: (l, 0, j)),
                  pl.BlockSpec((1, 1, tn), lambda l, j: (l, 0, j))],
        out_specs=pl.BlockSpec((1, rows, tn), lambda l, j: (l, 0, j)),
        out_shape=jax.ShapeDtypeStruct((depth, rows, width), F32),
        compiler_params=_params(2),
        name="modulation",
    )(cvec, w_mod, b_mod.reshape(depth, 1, width))


def _inproj_kernel(x_ref, mod_ref, g_ref, w_ref, o_ref):
    h = _rms(x_ref[0], g_ref[...]) * (1.0 + mod_ref[0, 1:2, :]) + mod_ref[0, 0:1, :]
    o_ref[0] = _dot(h.astype(BF16), w_ref[...])


def _in_projection(x, mod, g, w):
    bx, r, d = x.shape
    tm, tn = 512, 2048
    return pl.pallas_call(
        _inproj_kernel,
        grid=(P_WIDTH // tn, bx, r // tm),
        in_specs=[pl.BlockSpec((1, tm, d), lambda j, b, t: (b, t, 0)),
                  pl.BlockSpec((1, N_MOD, d), lambda j, b, t: (b, 0, 0)),
                  pl.BlockSpec((1, d), lambda j, b, t: (0, 0)),
                  pl.BlockSpec((d, tn), lambda j, b, t: (0, j))],
        out_specs=pl.BlockSpec((1, tm, tn), lambda j, b, t: (b, t, j)),
        out_shape=jax.ShapeDtypeStruct((bx, r, P_WIDTH), F32),
        compiler_params=_params(3),
        name="in_projection",
    )(x, mod, g, w)


def _rope(x, cos, sin, half):
    lane = lax.broadcasted_iota(jnp.int32, x.shape, 1)
    first = (lane % (2 * half)) < half
    width = x.shape[1]
    rot = jnp.where(first, pltpu.roll(x, width - half, 1), pltpu.roll(x, half, 1))
    return x * cos + rot * sin


def _mla_prep_kernel(cq_ref, ckv_ref, misc_ref, cos_ref, sin_ref, gq_ref, gkv_ref, wuq_ref, wukv_ref,
                     qt_ref, k_ref, vt_ref):
    scale = MLA_QK ** -0.5 * LOG2E
    cos, sin = cos_ref[...], sin_ref[...]
    half = MLA_ROPE // 4
    qf = _dot(_rms(cq_ref[0], gq_ref[...]).astype(BF16), wuq_ref[...])
    kvf = _dot(_rms(ckv_ref[0], gkv_ref[...]).astype(BF16), wukv_ref[...])
    kr = _rope(misc_ref[0], cos, sin, half)[:, :MLA_ROPE].astype(BF16)
    nope_w = MLA_HEADS * MLA_NOPE
    for pair in range(MLA_HEADS // 2):
        qr = _rope(qf[:, nope_w + 128 * pair: nope_w + 128 * (pair + 1)], cos, sin, half) * scale
        qrt = qr.T.astype(BF16)
        for j in range(2):
            qt_ref[0, 2 * pair + j, MLA_NOPE:MLA_QK, :] = qrt[MLA_ROPE * j: MLA_ROPE * (j + 1), :]
    for h in range(MLA_HEADS):
        qt_ref[0, h, 0:MLA_NOPE, :] = (qf[:, MLA_NOPE * h: MLA_NOPE * (h + 1)] * scale).T.astype(BF16)
        k_ref[0, h, :, 0:MLA_NOPE] = kvf[:, MLA_NOPE * h: MLA_NOPE * (h + 1)].astype(BF16)
        k_ref[0, h, :, MLA_NOPE:MLA_QK] = kr
        vt_ref[0, h] = kvf[:, nope_w + MLA_V * h: nope_w + MLA_V * (h + 1)].T.astype(BF16)


def _mla_prep(proj, cos, sin, g_q, g_kv, w_uq, w_ukv):
    b, r, _ = proj.shape
    tm = min(r, 512)
    rk = MLA_RANK
    const = lambda shape: pl.BlockSpec(shape, lambda i, t: (0,) * len(shape))
    return pl.pallas_call(
        _mla_prep_kernel,
        grid=(b, r // tm),
        in_specs=[pl.BlockSpec((1, tm, rk), lambda i, t: (i, t, P_CQ // rk)),
                  pl.BlockSpec((1, tm, rk), lambda i, t: (i, t, P_CKV // rk)),
                  pl.BlockSpec((1, tm, 128), lambda i, t: (i, t, P_MISC // 128)),
                  pl.BlockSpec((tm, 128), lambda i, t: (t, 0)),
                  pl.BlockSpec((tm, 128), lambda i, t: (t, 0)),
                  const((1, rk)), const((1, rk)),
                  const(w_uq.shape), const(w_ukv.shape)],
        out_specs=[pl.BlockSpec((1, MLA_HEADS, MLA_QK, tm), lambda i, t: (i, 0, 0, t)),
                   pl.BlockSpec((1, MLA_HEADS, tm, MLA_QK), lambda i, t: (i, 0, t, 0)),
                   pl.BlockSpec((1, MLA_HEADS, MLA_V, tm), lambda i, t: (i, 0, 0, t))],
        out_shape=[jax.ShapeDtypeStruct((b, MLA_HEADS, MLA_QK, r), BF16),
                   jax.ShapeDtypeStruct((b, MLA_HEADS, r, MLA_QK), BF16),
                   jax.ShapeDtypeStruct((b, MLA_HEADS, MLA_V, r), BF16)],
        compiler_params=_params(2),
        name="mla_prep",
    )(proj, proj, proj, cos, sin, g_q, g_kv, w_uq, w_ukv)


def _mla_attn_kernel(*refs, tk, has_latent):
    if has_latent:
        qt_ref, kl_ref, vlt_ref, kc_ref, vct_ref, o_ref = refs
        n_lat = kl_ref.shape[2] // tk
    else:
        qt_ref, kc_ref, vct_ref, o_ref = refs
        n_lat = 0
    qt = qt_ref[0, 0]
    tq = qt.shape[1]

    def keys(j):
        return kc_ref[0, 0] if j == n_lat else kl_ref[0, 0, j * tk:(j + 1) * tk, :]

    def values_t(j):
        return vct_ref[0, 0] if j == n_lat else vlt_ref[0, 0, :, j * tk:(j + 1) * tk]

    m = jnp.full((1, tq), -jnp.inf, F32)
    l = jnp.zeros((1, tq), F32)
    acc = jnp.zeros((MLA_V, tq), F32)
    s_next = _dot(keys(0), qt)
    for j in range(n_lat + 1):
        s = s_next
        if j < n_lat:
            s_next = _dot(keys(j + 1), qt)
        m_new = jnp.maximum(m, jnp.max(s, axis=0, keepdims=True))
        alpha = jnp.exp2(m - m_new)
        p = jnp.exp2(s - m_new)
        l = alpha * l + jnp.sum(p, axis=0, keepdims=True)
        acc = alpha * acc + _dot(values_t(j), p.astype(BF16))
        m = m_new
    o_ref[0] = (acc / l).T.astype(o_ref.dtype)


def _mla_attention(qt, k_l, vt_l, k_c, vt_c):
    b, h, dq, n = qt.shape
    c = k_c.shape[2]
    tq, tk = 512, 256
    return pl.pallas_call(
        functools.partial(_mla_attn_kernel, tk=tk, has_latent=True),
        grid=(b, h, n // tq),
        in_specs=[pl.BlockSpec((1, 1, dq, tq), lambda i, j, t: (i, j, 0, t)),
                  pl.BlockSpec((1, 1, n, dq), lambda i, j, t: (i, j, 0, 0)),
                  pl.BlockSpec((1, 1, MLA_V, n), lambda i, j, t: (i, j, 0, 0)),
                  pl.BlockSpec((1, 1, c, dq), lambda i, j, t: (i, j, 0, 0)),
                  pl.BlockSpec((1, 1, MLA_V, c), lambda i, j, t: (i, j, 0, 0))],
        out_specs=pl.BlockSpec((1, tq, MLA_V), lambda i, j, t: (i, t, j)),
        out_shape=jax.ShapeDtypeStruct((b, n, h * MLA_V), BF16),
        compiler_params=_params(3),
        name="mla_attention",
    )(qt, k_l, vt_l, k_c, vt_c)


def _mla_ctx_attention(qt, k, vt):
    b, h, dq, c = qt.shape
    return pl.pallas_call(
        functools.partial(_mla_attn_kernel, tk=c, has_latent=False),
        grid=(b, h),
        in_specs=[pl.BlockSpec((1, 1, dq, c), lambda i, j: (i, j, 0, 0)),
                  pl.BlockSpec((1, 1, c, dq), lambda i, j: (i, j, 0, 0)),
                  pl.BlockSpec((1, 1, MLA_V, c), lambda i, j: (i, j, 0, 0))],
        out_specs=pl.BlockSpec((1, c, MLA_V), lambda i, j: (i, 0, j)),
        out_shape=jax.ShapeDtypeStruct((b, c, h * MLA_V), BF16),
        compiler_params=_params(2),
        name="mla_ctx_attention",
    )(qt, k, vt)


def _swa_ctx_kernel(sink_ref, q_ref, k_ref, v_ref, o_ref):
    q = (q_ref[0] * SWA_DIM ** -0.5).astype(BF16)
    s = _dot_nt(q, k_ref[0].astype(BF16))
    sink = sink_ref[0, pl.program_id(1)]
    m = jnp.maximum(jnp.max(s, axis=-1, keepdims=True), sink)
    p = jnp.exp(s - m)
    l = jnp.sum(p, axis=-1, keepdims=True) + jnp.exp(sink - m)
    o_ref[0] = (_dot(p.astype(BF16), v_ref[0].astype(BF16)) / l).astype(o_ref.dtype)


def _swa_ctx_attention(proj_c, sink):
    b, c, _ = proj_c.shape
    d = SWA_DIM
    return pl.pallas_call(
        _swa_ctx_kernel,
        grid=(b, SWA_HEADS),
        in_specs=[pl.BlockSpec(memory_space=pltpu.SMEM),
                  pl.BlockSpec((1, c, d), lambda i, j: (i, 0, P_SWA_Q // d + j)),
                  pl.BlockSpec((1, c, d), lambda i, j: (i, 0, P_SWA_K // d + j // SWA_GROUP)),
                  pl.BlockSpec((1, c, d), lambda i, j: (i, 0, P_SWA_V // d + j // SWA_GROUP))],
        out_specs=pl.BlockSpec((1, c, d), lambda i, j: (i, 0, j)),
        out_shape=jax.ShapeDtypeStruct((b, c, SWA_HEADS * d), BF16),
        compiler_params=_params(2),
        name="swa_ctx_attention",
    )(sink, proj_c, proj_c, proj_c)


def _swa_kernel(sink_ref, q_ref, kp_ref, ko_ref, kn_ref, vp_ref, vo_ref, vn_ref, kc_ref, vc_ref,
                cp_ref, co_ref, cn_ref, sp_ref, so_ref, sn_ref, o_ref):
    g = pl.program_id(1)
    n = pl.program_id(2)
    nb = pl.num_programs(2)
    blk, d = SWA_BLOCK, SWA_DIM
    half = d // 4
    scale = d ** -0.5
    cos_o, sin_o = co_ref[...], so_ref[...]
    q = jnp.concatenate(
        [(_rope(q_ref[0, :, d * i: d * (i + 1)], cos_o, sin_o, half) * scale).astype(BF16)
         for i in range(SWA_GROUP)], axis=0)
    k = jnp.concatenate(
        [_rope(kp_ref[0], cp_ref[...], sp_ref[...], half).astype(BF16),
         _rope(ko_ref[0], cos_o, sin_o, half).astype(BF16),
         _rope(kn_ref[0], cn_ref[...], sn_ref[...], half).astype(BF16),
         kc_ref[0].astype(BF16)], axis=0)
    v = jnp.concatenate([vp_ref[0], vo_ref[0], vn_ref[0], vc_ref[0]], axis=0).astype(BF16)
    s = _dot_nt(q, k)
    row = lax.broadcasted_iota(jnp.int32, s.shape, 0) % blk
    col = lax.broadcasted_iota(jnp.int32, s.shape, 1)
    prev_off = jnp.where(n > 0, 0, blk)
    next_off = jnp.where(n < nb - 1, 0, blk)
    valid = ((col < blk) & (col >= row + prev_off)) \
        | ((col >= blk) & (col < 2 * blk)) \
        | ((col >= 2 * blk) & (col < 3 * blk) & (col - 2 * blk <= row - next_off)) \
        | (col >= 3 * blk)
    s = jnp.where(valid, s, -jnp.inf)
    sink = jnp.concatenate(
        [jnp.full((blk, 1), sink_ref[0, g * SWA_GROUP + i], F32) for i in range(SWA_GROUP)], axis=0)
    m = jnp.maximum(jnp.max(s, axis=-1, keepdims=True), sink)
    p = jnp.exp(s - m)
    l = jnp.sum(p, axis=-1, keepdims=True) + jnp.exp(sink - m)
    o = _dot(p.astype(BF16), v) / l
    for i in range(SWA_GROUP):
        o_ref[0, :, d * i: d * (i + 1)] = o[blk * i: blk * (i + 1)].astype(o_ref.dtype)


def _swa_attention(proj_l, proj_c, cos, sin, sink):
    b, n, _ = proj_l.shape
    c = proj_c.shape[1]
    blk, d = SWA_BLOCK, SWA_DIM
    nb = n // blk
    kcol = lambda g: P_SWA_K // d + g
    vcol = lambda g: P_SWA_V // d + g
    prev = lambda t: jnp.maximum(t - 1, 0)
    nxt = lambda t: jnp.minimum(t + 1, nb - 1)
    tile = lambda rowf, colf: pl.BlockSpec((1, blk, d), lambda i, g, t: (i, rowf(t), colf(g)))
    tab = lambda rowf: pl.BlockSpec((blk, d), lambda i, g, t: (rowf(t), 0))
    same = lambda t: t
    return pl.pallas_call(
        _swa_kernel,
        grid=(b, SWA_KV_HEADS, nb),
        in_specs=[pl.BlockSpec(memory_space=pltpu.SMEM),
                  pl.BlockSpec((1, blk, SWA_GROUP * d), lambda i, g, t: (i, t, g)),
                  tile(prev, kcol), tile(same, kcol), tile(nxt, kcol),
                  tile(prev, vcol), tile(same, vcol), tile(nxt, vcol),
                  pl.BlockSpec((1, c, d), lambda i, g, t: (i, 0, kcol(g))),
                  pl.BlockSpec((1, c, d), lambda i, g, t: (i, 0, vcol(g))),
                  tab(prev), tab(same), tab(nxt), tab(prev), tab(same), tab(nxt)],
        out_specs=pl.BlockSpec((1, blk, SWA_GROUP * d), lambda i, g, t: (i, t, g)),
        out_shape=jax.ShapeDtypeStruct((b, n, SWA_HEADS * d), BF16),
        compiler_params=_params(3),
        name="swa_attention",
    )(sink, proj_l, proj_l, proj_l, proj_l, proj_l, proj_l, proj_l, proj_c, proj_c,
      cos, cos, cos, sin, sin, sin)


def _log_sigmoid(x):
    return jnp.minimum(x, 0.0) - jnp.log(1.0 + jnp.exp(-jnp.abs(x)))


def _gla_direction(q_ref, k_ref, v_ref, misc_ref, wg, bg, state_ref, o_ref, reverse):
    rows = q_ref.shape[1]
    n_chunks = rows // GLA_CHUNK
    L = GLA_CHUNK
    la = _log_sigmoid(_dot(misc_ref[0].astype(BF16), wg) + bg) * (1.0 / GLA_TAU)
    ti = lax.broadcasted_iota(jnp.int32, (L, L), 0)
    tj = lax.broadcasted_iota(jnp.int32, (L, L), 1)
    if reverse:
        cum = (tj >= ti).astype(F32)
        amask = tj > ti
    else:
        cum = (tj <= ti).astype(F32)
        amask = tj <= ti
    scale = GLA_DK ** -0.5
    order = range(n_chunks - 1, -1, -1) if reverse else range(n_chunks)
    for c in order:
        sl = slice(c * L, (c + 1) * L)
        bcum = jnp.dot(cum, la[sl], preferred_element_type=F32, precision=lax.Precision.HIGHEST)
        b_last = bcum[0:1] if reverse else bcum[L - 1:L]
        q_in = q_ref[0, sl, :] * scale * jnp.exp(bcum)
        k_c = k_ref[0, sl, :]
        k_in = k_c * jnp.exp(-bcum)
        k_dec = k_c * jnp.exp(b_last - bcum)
        decay = jnp.exp(b_last)
        for h in range(GLA_HEADS):
            ks = slice(h * GLA_DK, (h + 1) * GLA_DK)
            vs = slice(h * GLA_DV, (h + 1) * GLA_DV)
            qh = q_in[:, ks].astype(BF16)
            vh = v_ref[0, sl, vs].astype(BF16)
            a = jnp.where(amask, _dot_nt(qh, k_in[:, ks].astype(BF16)), 0.0)
            st = state_ref[h]
            o_ref[0, sl, vs] = _dot(a.astype(BF16), vh) + _dot_nt(qh, st.astype(BF16))
            state_ref[h] = decay[:, ks] * st + _dot_tn(vh, k_dec[:, ks].astype(BF16))


def _gla_kernel(qf_ref, kf_ref, vf_ref, mf_ref, qb_ref, kb_ref, vb_ref, mb_ref, wg_ref, bg_ref,
                sf0_ref, sb0_ref, of_ref, ob_ref, sf_ref, sb_ref, stf, stb):
    t = pl.program_id(1)

    @pl.when(t == 0)
    def _():
        stf[...] = sf0_ref[0]
        stb[...] = sb0_ref[0]

    width = GLA_HEADS * GLA_DK
    _gla_direction(qf_ref, kf_ref, vf_ref, mf_ref, wg_ref[:, :width], bg_ref[:, :width], stf, of_ref, False)
    _gla_direction(qb_ref, kb_ref, vb_ref, mb_ref, wg_ref[:, width:], bg_ref[:, width:], stb, ob_ref, True)

    @pl.when(t == pl.num_programs(1) - 1)
    def _():
        sf_ref[0] = stf[...]
        sb_ref[0] = stb[...]


def _gla_scan(proj, wg, bg, sf0, sb0):
    b, r, _ = proj.shape
    tm = 256
    nt = r // tm
    kw, vw = GLA_HEADS * GLA_DK, GLA_HEADS * GLA_DV
    fwd = lambda t: t
    bwd = lambda t: nt - 1 - t

    def specs(order):
        return [pl.BlockSpec((1, tm, kw), lambda i, t: (i, order(t), P_GLA_Q // kw)),
                pl.BlockSpec((1, tm, kw), lambda i, t: (i, order(t), P_GLA_K // kw)),
                pl.BlockSpec((1, tm, vw), lambda i, t: (i, order(t), P_GLA_V // vw)),
                pl.BlockSpec((1, tm, 128), lambda i, t: (i, order(t), P_MISC // 128))]

    state_spec = pl.BlockSpec((1, GLA_HEADS, GLA_DV, GLA_DK), lambda i, t: (i, 0, 0, 0))
    state_shape = jax.ShapeDtypeStruct((b, GLA_HEADS, GLA_DV, GLA_DK), F32)
    return pl.pallas_call(
        _gla_kernel,
        grid=(b, nt),
        in_specs=specs(fwd) + specs(bwd) + [
            pl.BlockSpec(wg.shape, lambda i, t: (0, 0)),
            pl.BlockSpec(bg.shape, lambda i, t: (0, 0)),
            state_spec, state_spec],
        out_specs=[pl.BlockSpec((1, tm, vw), lambda i, t: (i, fwd(t), 0)),
                   pl.BlockSpec((1, tm, vw), lambda i, t: (i, bwd(t), 0)),
                   state_spec, state_spec],
        out_shape=[jax.ShapeDtypeStruct((b, r, vw), F32), jax.ShapeDtypeStruct((b, r, vw), F32),
                   state_shape, state_shape],
        scratch_shapes=[pltpu.VMEM((GLA_HEADS, GLA_DV, GLA_DK), F32),
                        pltpu.VMEM((GLA_HEADS, GLA_DV, GLA_DK), F32)],
        compiler_params=_params(2),
        name="gla_scan",
    )(proj, proj, proj, proj, proj, proj, proj, proj, wg, bg, sf0, sb0)


def _outproj_kernel(x_ref, mod_ref, mla_ref, swa_ref, of_ref, ob_ref, r_ref, g_ref, w_ref, o_ref):
    o = of_ref[0] + ob_ref[0]
    gate = _silu(r_ref[0])
    parts = []
    for h in range(GLA_HEADS):
        vs = slice(h * GLA_DV, (h + 1) * GLA_DV)
        parts.append((_rms(o[:, vs], g_ref[:, vs]) * gate[:, vs]).astype(BF16))
    gla = jnp.concatenate(parts, axis=1)
    a, b = mla_ref.shape[2], mla_ref.shape[2] + swa_ref.shape[2]
    mix = _dot(mla_ref[0], w_ref[0:a, :]) + _dot(swa_ref[0], w_ref[a:b, :]) + _dot(gla, w_ref[b:, :])
    o_ref[0] = x_ref[0] + mod_ref[0, 2:3, :] * mix


def _out_projection(x, mod, mla, swa, o_f, o_b, proj, g_out, w_out):
    bx, r, d = x.shape
    tm = 512
    row = lambda width: pl.BlockSpec((1, tm, width), lambda b, t: (b, t, 0))
    gw = GLA_HEADS * GLA_DV
    return pl.pallas_call(
        _outproj_kernel,
        grid=(bx, r // tm),
        in_specs=[row(d),
                  pl.BlockSpec((1, N_MOD, d), lambda b, t: (b, 0, 0)),
                  row(mla.shape[2]), row(swa.shape[2]), row(gw), row(gw),
                  pl.BlockSpec((1, tm, gw), lambda b, t: (b, t, P_GLA_R // gw)),
                  pl.BlockSpec((1, gw), lambda b, t: (0, 0)),
                  pl.BlockSpec(w_out.shape, lambda b, t: (0, 0))],
        out_specs=row(d),
        out_shape=jax.ShapeDtypeStruct(x.shape, F32),
        compiler_params=_params(2),
        name="out_projection",
    )(x, mod, mla, swa, o_f, o_b, proj, g_out, w_out)


def _ffn_kernel(x_ref, mod_ref, g_ref, wg_ref, wu_ref, wd_ref, gf_ref, o_ref, h_ref, acc_ref, *, final_norm):
    f = pl.program_id(2)

    @pl.when(f == 0)
    def _():
        h = _rms(x_ref[0], g_ref[...]) * (1.0 + mod_ref[0, 4:5, :]) + mod_ref[0, 3:4, :]
        h_ref[...] = h.astype(BF16)
        acc_ref[...] = jnp.zeros_like(acc_ref)

    h = h_ref[...]
    act = _silu(_dot(h, wg_ref[...])) * _dot(h, wu_ref[...])
    acc_ref[...] += _dot(act.astype(BF16), wd_ref[...])

    @pl.when(f == pl.num_programs(2) - 1)
    def _():
        y = x_ref[0] + mod_ref[0, 5:6, :] * acc_ref[...]
        if final_norm:
            y = _rms(y, gf_ref[...])
        o_ref[0] = y


def _ffn(x, mod, g, w_gu, w_down, g_final, final_norm):
    bx, r, d = x.shape
    hidden = w_down.shape[0]
    tm, tf = 512, 512
    nf = hidden // tf
    return pl.pallas_call(
        functools.partial(_ffn_kernel, final_norm=final_norm),
        grid=(bx, r // tm, nf),
        in_specs=[pl.BlockSpec((1, tm, d), lambda b, t, f: (b, t, 0)),
                  pl.BlockSpec((1, N_MOD, d), lambda b, t, f: (b, 0, 0)),
                  pl.BlockSpec((1, d), lambda b, t, f: (0, 0)),
                  pl.BlockSpec((d, tf), lambda b, t, f: (0, f)),
                  pl.BlockSpec((d, tf), lambda b, t, f: (0, nf + f)),
                  pl.BlockSpec((tf, d), lambda b, t, f: (f, 0)),
                  pl.BlockSpec((1, d), lambda b, t, f: (0, 0))],
        out_specs=pl.BlockSpec((1, tm, d), lambda b, t, f: (b, t, 0)),
        out_shape=jax.ShapeDtypeStruct(x.shape, F32),
        scratch_shapes=[pltpu.VMEM((tm, d), BF16), pltpu.VMEM((tm, d), F32)],
        compiler_params=_params(3),
        name="ffn",
    )(x, mod, g, w_gu, w_gu, w_down, g_final)


def _rope_tables(n, dim):
    half = dim // 4
    freqs = ROPE_THETA ** (-jnp.arange(half, dtype=F32) / half)
    pos = jnp.arange(n, dtype=jnp.int32)
    ang_r = (pos // GRID_W).astype(F32)[:, None] * freqs
    ang_c = (pos % GRID_W).astype(F32)[:, None] * freqs
    cos = jnp.concatenate([jnp.cos(ang_r)] * 2 + [jnp.cos(ang_c)] * 2, axis=1)
    sin = jnp.concatenate([-jnp.sin(ang_r), jnp.sin(ang_r), -jnp.sin(ang_c), jnp.sin(ang_c)], axis=1)
    reps = 128 // dim
    return jnp.tile(cos, (1, reps)), jnp.tile(sin, (1, reps))


def _relayout_w_in(w_in):
    depth, d, _ = w_in.shape
    sizes = (MLA_RANK, MLA_RANK, MLA_ROPE, SWA_HEADS * SWA_DIM, SWA_KV_HEADS * SWA_DIM, SWA_KV_HEADS * SWA_DIM,
             GLA_HEADS * GLA_DK, GLA_HEADS * GLA_DK, GLA_HEADS * GLA_DV, 2 * GLA_GATE_RANK, GLA_HEADS * GLA_DV)
    offs = [0]
    for s in sizes:
        offs.append(offs[-1] + s)
    part = lambda i: w_in[:, :, offs[i]:offs[i + 1]]
    cq, ckv, kr, sq, sk, sv, gq, gk, gv, glr, gr = (part(i) for i in range(len(sizes)))
    zeros = lambda w: jnp.zeros((depth, d, w), w_in.dtype)
    misc_pad = 128 - MLA_ROPE - 2 * GLA_GATE_RANK
    out = jnp.concatenate([sq, sk, cq, ckv, sv, kr, glr, zeros(misc_pad), zeros(P_GLA_Q - P_MISC - 128),
                           gq, gk, gv, gr], axis=-1)
    assert out.shape[-1] == P_WIDTH
    return out.astype(BF16)


def _relayout_mla(w_uq, w_ukv):
    depth, rk, _ = w_uq.shape
    uq = w_uq.reshape(depth, rk, MLA_HEADS, MLA_QK)
    uq = jnp.concatenate([uq[..., :MLA_NOPE].reshape(depth, rk, -1), uq[..., MLA_NOPE:].reshape(depth, rk, -1)], -1)
    ukv = w_ukv.reshape(depth, rk, MLA_HEADS, MLA_NOPE + MLA_V)
    ukv = jnp.concatenate([ukv[..., :MLA_NOPE].reshape(depth, rk, -1), ukv[..., MLA_NOPE:].reshape(depth, rk, -1)], -1)
    return uq.astype(BF16), ukv.astype(BF16)


def _relayout_gate(w_f, b_f, w_b, b_b):
    depth, rank, width = w_f.shape
    wg = jnp.zeros((depth, 128, 2 * width), F32)
    wg = wg.at[:, MISC_GATE_OFF:MISC_GATE_OFF + rank, :width].set(w_f)
    wg = wg.at[:, MISC_GATE_OFF + rank:MISC_GATE_OFF + 2 * rank, width:].set(w_b)
    bg = jnp.concatenate([b_f, b_b], axis=-1).reshape(depth, 1, 2 * width)
    return wg.astype(BF16), bg


def kernel(x, c, ctx, c_ctx, w_mod, b_mod, g_mix, g_ffn, w_in, g_mla_q, g_mla_kv, w_mla_uq, w_mla_ukv,
           swa_sink, w_gla_gate_f, b_gla_gate_f, w_gla_gate_b, b_gla_gate_b, g_gla_out, w_out, w_ffn_gu,
           w_ffn_down, g_final):
    B, N, D = x.shape
    C = ctx.shape[1]
    depth = w_mod.shape[0]

    cvec = jnp.concatenate([c, c_ctx[None, :], jnp.zeros((16 - B - 1, D), F32)], axis=0)
    mod = _modulation(cvec, w_mod, b_mod).reshape(depth, 16, N_MOD, D)

    w_in_p = _relayout_w_in(w_in)
    w_uq_p, w_ukv_p = _relayout_mla(w_mla_uq, w_mla_ukv)
    wg_p, bg_p = _relayout_gate(w_gla_gate_f, b_gla_gate_f, w_gla_gate_b, b_gla_gate_b)
    w_out_b = w_out.astype(BF16)
    w_gu_b = w_ffn_gu.astype(BF16)
    w_down_b = w_ffn_down.astype(BF16)

    cos_m, sin_m = _rope_tables(N, MLA_ROPE)
    cos_s, sin_s = _rope_tables(N, SWA_DIM)
    cos_id, sin_id = jnp.ones((C, 128), F32), jnp.zeros((C, 128), F32)
    state0 = jnp.zeros((B, GLA_HEADS, GLA_DV, GLA_DK), F32)
    g_fin = g_final.reshape(1, D)

    xc = ctx.reshape(1, B * C, D)
    for l in range(depth):
        last = l == depth - 1
        mod_l, mod_c = mod[l, :B], mod[l, B:B + 1]
        g_mix_l, g_ffn_l = g_mix[l].reshape(1, D), g_ffn[l].reshape(1, D)
        g_q, g_kv = g_mla_q[l].reshape(1, -1), g_mla_kv[l].reshape(1, -1)
        g_out = g_gla_out[l].reshape(1, -1)
        sink = swa_sink[l].reshape(1, -1)

        proj_l = _in_projection(x, mod_l, g_mix_l, w_in_p[l])
        proj_c = _in_projection(xc, mod_c, g_mix_l, w_in_p[l]).reshape(B, C, P_WIDTH)

        q_l, k_l, v_l = _mla_prep(proj_l, cos_m, sin_m, g_q, g_kv, w_uq_p[l], w_ukv_p[l])
        q_c, k_c, v_c = _mla_prep(proj_c, cos_id, sin_id, g_q, g_kv, w_uq_p[l], w_ukv_p[l])
        mla_l = _mla_attention(q_l, k_l, v_l, k_c, v_c)
        swa_l = _swa_attention(proj_l, proj_c, cos_s, sin_s, sink)

        of_c, ob_c, s_f, s_b = _gla_scan(proj_c, wg_p[l], bg_p[l], state0, state0)
        of_l, ob_l, _, _ = _gla_scan(proj_l, wg_p[l], bg_p[l], s_f, s_b)

        x = _out_projection(x, mod_l, mla_l, swa_l, of_l, ob_l, proj_l, g_out, w_out_b[l])
        x = _ffn(x, mod_l, g_ffn_l, w_gu_b[l], w_down_b[l], g_fin, last)

        if not last:
            mla_c = _mla_ctx_attention(q_c, k_c, v_c)
            swa_c = _swa_ctx_attention(proj_c, sink)
            flat = lambda a: a.reshape(1, B * C, a.shape[-1])
            xc = _out_projection(xc, mod_c, flat(mla_c), flat(swa_c), flat(of_c), flat(ob_c), flat(proj_c),
                                 g_out, w_out_b[l])
            xc = _ffn(xc, mod_c, g_ffn_l, w_gu_b[l], w_down_b[l], g_fin, False)
    return x
```

```python
import functools

import jax
import jax.numpy as jnp
from jax import lax
from jax.experimental import pallas as pl
from jax.experimental.pallas import tpu as pltpu

F32 = jnp.float32
BF16 = jnp.bfloat16

GRID_W = 64
EPS = 1e-6
ROPE_THETA = 10000.0
LOG2E = 1.4426950408889634

MLA_HEADS = 6
MLA_RANK = 512
MLA_NOPE = 128
MLA_ROPE = 64
MLA_V = 128
MLA_QK = MLA_NOPE + MLA_ROPE
MLA_VA = MLA_V + 16

SWA_HEADS = 6
SWA_KV_HEADS = 2
SWA_GROUP = SWA_HEADS // SWA_KV_HEADS
SWA_DIM = 128
SWA_BLOCK = 128

GLA_HEADS = 4
GLA_DK = 64
GLA_DV = 128
GLA_GATE_RANK = 16
GLA_TAU = 16.0
GLA_CHUNK = 64

N_MOD = 6

P_SWA_Q = 0
P_SWA_K = 768
P_CQ = 1024
P_CKV = 1536
P_SWA_V = 2048
P_MISC = 2304
P_GLA_Q = 2560
P_GLA_K = 2816
P_GLA_V = 3072
P_GLA_R = 3584
P_WIDTH = 4096
MISC_GATE_OFF = MLA_ROPE

VMEM_LIMIT = 56 * 1024 * 1024


def _params(n_axes, flags=None):
    return pltpu.CompilerParams(dimension_semantics=("arbitrary",) * n_axes,
                                vmem_limit_bytes=VMEM_LIMIT, flags=flags)


def _silu(x):
    return x / (1.0 + jnp.exp(-x))


def _rms(x, g):
    ms = jnp.mean(x * x, axis=-1, keepdims=True)
    return x * lax.rsqrt(ms + EPS) * g


def _dot(a, b):
    return jnp.dot(a, b, preferred_element_type=F32)


def _dot_nt(a, b):
    return lax.dot_general(a, b, (((1,), (1,)), ((), ())), preferred_element_type=F32)


def _dot_tn(a, b):
    return lax.dot_general(a, b, (((0,), (0,)), ((), ())), preferred_element_type=F32)


def _mod_kernel(c_ref, w_ref, b_ref, o_ref):
    a = _silu(c_ref[...]).astype(BF16)
    o_ref[0] = _dot(a, w_ref[0].astype(BF16)) + b_ref[0]


def _modulation(cvec, w_mod, b_mod):
    depth, d, width = w_mod.shape
    rows = cvec.shape[0]
    tn = 1024
    return pl.pallas_call(
        _mod_kernel,
        grid=(depth, width // tn),
        in_specs=[pl.BlockSpec((rows, d), lambda l, j: (0, 0)),
                  pl.BlockSpec((1, d, tn), lambda l, j: (l, 0, j)),
                  pl.BlockSpec((1, 1, tn), lambda l, j: (l, 0, j))],
        out_specs=pl.BlockSpec((1, rows, tn), lambda l, j: (l, 0, j)),
        out_shape=jax.ShapeDtypeStruct((depth, rows, width), F32),
        compiler_params=_params(2),
        name="modulation",
    )(cvec, w_mod, b_mod.reshape(depth, 1, width))


def _inproj_kernel(x_ref, mod_ref, g_ref, w_ref, o_ref):
    h = _rms(x_ref[0], g_ref[...]) * (1.0 + mod_ref[0, 1:2, :]) + mod_ref[0, 0:1, :]
    o_ref[0] = _dot(h.astype(BF16), w_ref[...])


def _in_projection(x, mod, g, w):
    bx, r, d = x.shape
    tm, tn = 512, 2048
    return pl.pallas_call(
        _inproj_kernel,
        grid=(P_WIDTH // tn, bx, r // tm),
        in_specs=[pl.BlockSpec((1, tm, d), lambda j, b, t: (b, t, 0)),
                  pl.BlockSpec((1, N_MOD, d), lambda j, b, t: (b, 0, 0)),
                  pl.BlockSpec((1, d), lambda j, b, t: (0, 0)),
                  pl.BlockSpec((d, tn), lambda j, b, t: (0, j))],
        out_specs=pl.BlockSpec((1, tm, tn), lambda j, b, t: (b, t, j)),
        out_shape=jax.ShapeDtypeStruct((bx, r, P_WIDTH), F32),
        compiler_params=_params(3),
        name="in_projection",
    )(x, mod, g, w)


def _rope(x, cos, sin, half):
    lane = lax.broadcasted_iota(jnp.int32, x.shape, 1)
    first = (lane % (2 * half)) < half
    width = x.shape[1]
    rot = jnp.where(first, pltpu.roll(x, width - half, 1), pltpu.roll(x, half, 1))
    return x * cos + rot * sin


def _mla_prep_kernel(cq_ref, ckv_ref, misc_ref, cos_ref, sin_ref, gq_ref, gkv_ref, wuq_ref, wukv_ref,
                     qt_ref, k_ref, vt_ref):
    scale = MLA_QK ** -0.5 * LOG2E
    cos, sin = cos_ref[...], sin_ref[...]
    half = MLA_ROPE // 4
    qf = _dot(_rms(cq_ref[0], gq_ref[...]).astype(BF16), wuq_ref[...])
    kvf = _dot(_rms(ckv_ref[0], gkv_ref[...]).astype(BF16), wukv_ref[...])
    kr = _rope(misc_ref[0], cos, sin, half)[:, :MLA_ROPE].astype(BF16)
    nope_w = MLA_HEADS * MLA_NOPE
    for pair in range(MLA_HEADS // 2):
        qr = _rope(qf[:, nope_w + 128 * pair: nope_w + 128 * (pair + 1)], cos, sin, half) * scale
        qrt = qr.T.astype(BF16)
        for j in range(2):
            qt_ref[0, 2 * pair + j, MLA_NOPE:MLA_QK, :] = qrt[MLA_ROPE * j: MLA_ROPE * (j + 1), :]
    for h in range(MLA_HEADS):
        qt_ref[0, h, 0:MLA_NOPE, :] = (qf[:, MLA_NOPE * h: MLA_NOPE * (h + 1)] * scale).T.astype(BF16)
        k_ref[0, h, :, 0:MLA_NOPE] = kvf[:, MLA_NOPE * h: MLA_NOPE * (h + 1)].astype(BF16)
        k_ref[0, h, :, MLA_NOPE:MLA_QK] = kr
        vt_ref[0, h, 0:MLA_V, :] = kvf[:, nope_w + MLA_V * h: nope_w + MLA_V * (h + 1)].T.astype(BF16)
        vt_ref[0, h, MLA_V:, :] = jnp.ones((MLA_VA - MLA_V, vt_ref.shape[3]), BF16)


def _mla_prep(proj, cos, sin, g_q, g_kv, w_uq, w_ukv):
    b, r, _ = proj.shape
    tm = min(r, 512)
    rk = MLA_RANK
    const = lambda shape: pl.BlockSpec(shape, lambda i, t: (0,) * len(shape))
    return pl.pallas_call(
        _mla_prep_kernel,
        grid=(b, r // tm),
        in_specs=[pl.BlockSpec((1, tm, rk), lambda i, t: (i, t, P_CQ // rk)),
                  pl.BlockSpec((1, tm, rk), lambda i, t: (i, t, P_CKV // rk)),
                  pl.BlockSpec((1, tm, 128), lambda i, t: (i, t, P_MISC // 128)),
                  pl.BlockSpec((tm, 128), lambda i, t: (t, 0)),
                  pl.BlockSpec((tm, 128), lambda i, t: (t, 0)),
                  const((1, rk)), const((1, rk)),
                  const(w_uq.shape), const(w_ukv.shape)],
        out_specs=[pl.BlockSpec((1, MLA_HEADS, MLA_QK, tm), lambda i, t: (i, 0, 0, t)),
                   pl.BlockSpec((1, MLA_HEADS, tm, MLA_QK), lambda i, t: (i, 0, t, 0)),
                   pl.BlockSpec((1, MLA_HEADS, MLA_VA, tm), lambda i, t: (i, 0, 0, t))],
        out_shape=[jax.ShapeDtypeStruct((b, MLA_HEADS, MLA_QK, r), BF16),
                   jax.ShapeDtypeStruct((b, MLA_HEADS, r, MLA_QK), BF16),
                   jax.ShapeDtypeStruct((b, MLA_HEADS, MLA_VA, r), BF16)],
        compiler_params=_params(2),
        name="mla_prep",
    )(proj, proj, proj, cos, sin, g_q, g_kv, w_uq, w_ukv)


def _softmax_pv(s_parts, vt_parts):
    m = functools.reduce(jnp.maximum, [jnp.max(s, axis=0, keepdims=True) for s in s_parts])
    acc = sum(_dot(vt, jnp.exp2(s - m).astype(BF16)) for s, vt in zip(s_parts, vt_parts))
    return (acc[:MLA_V] / acc[MLA_V:MLA_V + 1]).T


def _mla_attn_kernel(qt_ref, kl_ref, kc_ref, vlt_ref, vct_ref, o_ref, s_scr, m_scr, *, tk):
    @pl.when(pl.program_id(0) == 0)
    def _():
        s_scr[...] = jnp.zeros_like(s_scr)
        m_scr[...] = jnp.zeros_like(m_scr)

    n_lat = kl_ref.shape[2] // tk
    qt = qt_ref[0, 0]
    tq = qt.shape[1]
    m_prev = m_scr[...]
    m_new = jnp.full((1, tq), -jnp.inf, F32)
    acc = jnp.zeros((MLA_VA, tq), F32)
    for j in range(n_lat + 1):
        rows = slice(j * tk, (j + 1) * tk)
        p = jnp.exp2(s_scr[rows, :] - m_prev).astype(BF16)
        k = kc_ref[0, 0] if j == n_lat else kl_ref[0, 0, rows, :]
        vt = vct_ref[0, 0] if j == n_lat else vlt_ref[0, 0, :, rows]
        s = _dot(k, qt)
        s_scr[rows, :] = s
        m_new = jnp.maximum(m_new, jnp.max(s, axis=0, keepdims=True))
        acc = acc + _dot(vt, p)
    m_scr[...] = m_new
    o_ref[0] = (acc[:MLA_V] / acc[MLA_V:MLA_V + 1]).T.astype(o_ref.dtype)


def _mla_attention(qt, k_l, vt_l, k_c, vt_c):
    b, h, dq, n = qt.shape
    c = k_c.shape[2]
    tq = 512
    nt = n // tq
    tiles = b * h * nt

    def cur(u):
        t = jnp.minimum(u, tiles - 1)
        return t // (h * nt), (t // nt) % h, t % nt

    def prev(u):
        t = jnp.maximum(u - 1, 0)
        return t // (h * nt), (t // nt) % h, t % nt

    assert n % c == 0
    return pl.pallas_call(
        functools.partial(_mla_attn_kernel, tk=c),
        grid=(tiles + 1,),
        in_specs=[pl.BlockSpec((1, 1, dq, tq), lambda u: (cur(u)[0], cur(u)[1], 0, cur(u)[2])),
                  pl.BlockSpec((1, 1, n, dq), lambda u: (cur(u)[0], cur(u)[1], 0, 0)),
                  pl.BlockSpec((1, 1, c, dq), lambda u: (cur(u)[0], cur(u)[1], 0, 0)),
                  pl.BlockSpec((1, 1, MLA_VA, n), lambda u: (prev(u)[0], prev(u)[1], 0, 0)),
                  pl.BlockSpec((1, 1, MLA_VA, c), lambda u: (prev(u)[0], prev(u)[1], 0, 0))],
        out_specs=pl.BlockSpec((1, tq, MLA_V), lambda u: (prev(u)[0], prev(u)[2], prev(u)[1])),
        out_shape=jax.ShapeDtypeStruct((b, n, h * MLA_V), BF16),
        scratch_shapes=[pltpu.VMEM((n + c, tq), F32), pltpu.VMEM((1, tq), F32)],
        compiler_params=_params(1),
        name="mla_attention",
    )(qt, k_l, k_c, vt_l, vt_c)


def _mla_ctx_kernel(qt_ref, k_ref, vt_ref, o_ref):
    o_ref[0] = _softmax_pv([_dot(k_ref[0, 0], qt_ref[0, 0])], [vt_ref[0, 0]]).astype(o_ref.dtype)


def _mla_ctx_attention(qt, k, vt):
    b, h, dq, c = qt.shape
    return pl.pallas_call(
        _mla_ctx_kernel,
        grid=(b, h),
        in_specs=[pl.BlockSpec((1, 1, dq, c), lambda i, j: (i, j, 0, 0)),
                  pl.BlockSpec((1, 1, c, dq), lambda i, j: (i, j, 0, 0)),
                  pl.BlockSpec((1, 1, MLA_VA, c), lambda i, j: (i, j, 0, 0))],
        out_specs=pl.BlockSpec((1, c, MLA_V), lambda i, j: (i, 0, j)),
        out_shape=jax.ShapeDtypeStruct((b, c, h * MLA_V), BF16),
        compiler_params=_params(2),
        name="mla_ctx_attention",
    )(qt, k, vt)


def _swa_ctx_kernel(sink_ref, q_ref, k_ref, v_ref, o_ref):
    q = (q_ref[0] * SWA_DIM ** -0.5).astype(BF16)
    s = _dot_nt(q, k_ref[0].astype(BF16))
    sink = sink_ref[0, pl.program_id(1)]
    m = jnp.maximum(jnp.max(s, axis=-1, keepdims=True), sink)
    p = jnp.exp(s - m)
    l = jnp.sum(p, axis=-1, keepdims=True) + jnp.exp(sink - m)
    o_ref[0] = (_dot(p.astype(BF16), v_ref[0].astype(BF16)) / l).astype(o_ref.dtype)


def _swa_ctx_attention(proj_c, sink):
    b, c, _ = proj_c.shape
    d = SWA_DIM
    return pl.pallas_call(
        _swa_ctx_kernel,
        grid=(b, SWA_HEADS),
        in_specs=[pl.BlockSpec(memory_space=pltpu.SMEM),
                  pl.BlockSpec((1, c, d), lambda i, j: (i, 0, P_SWA_Q // d + j)),
                  pl.BlockSpec((1, c, d), lambda i, j: (i, 0, P_SWA_K // d + j // SWA_GROUP)),
                  pl.BlockSpec((1, c, d), lambda i, j: (i, 0, P_SWA_V // d + j // SWA_GROUP))],
        out_specs=pl.BlockSpec((1, c, d), lambda i, j: (i, 0, j)),
        out_shape=jax.ShapeDtypeStruct((b, c, SWA_HEADS * d), BF16),
        compiler_params=_params(2),
        name="swa_ctx_attention",
    )(sink, proj_c, proj_c, proj_c)


def _swa_kernel(sink_ref, q_ref, kp_ref, ko_ref, kn_ref, vp_ref, vo_ref, vn_ref, kc_ref, vc_ref,
                cp_ref, co_ref, cn_ref, sp_ref, so_ref, sn_ref, o_ref):
    g = pl.program_id(1)
    n = pl.program_id(2)
    nb = pl.num_programs(2)
    blk, d = SWA_BLOCK, SWA_DIM
    half = d // 4
    scale = d ** -0.5
    cos_o, sin_o = co_ref[...], so_ref[...]
    q = jnp.concatenate(
        [(_rope(q_ref[0, :, d * i: d * (i + 1)], cos_o, sin_o, half) * scale).astype(BF16)
         for i in range(SWA_GROUP)], axis=0)
    k = jnp.concatenate(
        [_rope(kp_ref[0], cp_ref[...], sp_ref[...], half).astype(BF16),
         _rope(ko_ref[0], cos_o, sin_o, half).astype(BF16),
         _rope(kn_ref[0], cn_ref[...], sn_ref[...], half).astype(BF16),
         kc_ref[0].astype(BF16)], axis=0)
    v = jnp.concatenate([vp_ref[0], vo_ref[0], vn_ref[0], vc_ref[0]], axis=0).astype(BF16)
    s = _dot_nt(q, k)
    row = lax.broadcasted_iota(jnp.int32, s.shape, 0) % blk
    col = lax.broadcasted_iota(jnp.int32, s.shape, 1)
    prev_off = jnp.where(n > 0, 0, blk)
    next_off = jnp.where(n < nb - 1, 0, blk)
    valid = ((col < blk) & (col >= row + prev_off)) \
        | ((col >= blk) & (col < 2 * blk)) \
        | ((col >= 2 * blk) & (col < 3 * blk) & (col - 2 * blk <= row - next_off)) \
        | (col >= 3 * blk)
    s = jnp.where(valid, s, -jnp.inf)
    sink = jnp.concatenate(
        [jnp.full((blk, 1), sink_ref[0, g * SWA_GROUP + i], F32) for i in range(SWA_GROUP)], axis=0)
    m = jnp.maximum(jnp.max(s, axis=-1, keepdims=True), sink)
    p = jnp.exp(s - m)
    l = jnp.sum(p, axis=-1, keepdims=True) + jnp.exp(sink - m)
    o = _dot(p.astype(BF16), v) / l
    for i in range(SWA_GROUP):
        o_ref[0, :, d * i: d * (i + 1)] = o[blk * i: blk * (i + 1)].astype(o_ref.dtype)


def _swa_attention(proj_l, proj_c, cos, sin, sink):
    b, n, _ = proj_l.shape
    c = proj_c.shape[1]
    blk, d = SWA_BLOCK, SWA_DIM
    nb = n // blk
    kcol = lambda g: P_SWA_K // d + g
    vcol = lambda g: P_SWA_V // d + g
    prev = lambda t: jnp.maximum(t - 1, 0)
    nxt = lambda t: jnp.minimum(t + 1, nb - 1)
    tile = lambda rowf, colf: pl.BlockSpec((1, blk, d), lambda i, g, t: (i, rowf(t), colf(g)))
    tab = lambda rowf: pl.BlockSpec((blk, d), lambda i, g, t: (rowf(t), 0))
    same = lambda t: t
    return pl.pallas_call(
        _swa_kernel,
        grid=(b, SWA_KV_HEADS, nb),
        in_specs=[pl.BlockSpec(memory_space=pltpu.SMEM),
                  pl.BlockSpec((1, blk, SWA_GROUP * d), lambda i, g, t: (i, t, g)),
                  tile(prev, kcol), tile(same, kcol), tile(nxt, kcol),
                  tile(prev, vcol), tile(same, vcol), tile(nxt, vcol),
                  pl.BlockSpec((1, c, d), lambda i, g, t: (i, 0, kcol(g))),
                  pl.BlockSpec((1, c, d), lambda i, g, t: (i, 0, vcol(g))),
                  tab(prev), tab(same), tab(nxt), tab(prev), tab(same), tab(nxt)],
        out_specs=pl.BlockSpec((1, blk, SWA_GROUP * d), lambda i, g, t: (i, t, g)),
        out_shape=jax.ShapeDtypeStruct((b, n, SWA_HEADS * d), BF16),
        compiler_params=_params(3),
        name="swa_attention",
    )(sink, proj_l, proj_l, proj_l, proj_l, proj_l, proj_l, proj_l, proj_c, proj_c,
      cos, cos, cos, sin, sin, sin)


def _log_sigmoid(x):
    return jnp.minimum(x, 0.0) - jnp.log(1.0 + jnp.exp(-jnp.abs(x)))


def _gla_direction(q_ref, k_ref, v_ref, misc_ref, wg, bg, state_ref, o_ref, reverse):
    rows = q_ref.shape[1]
    n_chunks = rows // GLA_CHUNK
    L = GLA_CHUNK
    la = _log_sigmoid(_dot(misc_ref[0].astype(BF16), wg) + bg) * (1.0 / GLA_TAU)
    ti = lax.broadcasted_iota(jnp.int32, (L, L), 0)
    tj = lax.broadcasted_iota(jnp.int32, (L, L), 1)
    if reverse:
        cum = (tj >= ti).astype(F32)
        amask = tj > ti
    else:
        cum = (tj <= ti).astype(F32)
        amask = tj <= ti
    scale = GLA_DK ** -0.5
    order = range(n_chunks - 1, -1, -1) if reverse else range(n_chunks)
    for c in order:
        sl = slice(c * L, (c + 1) * L)
        bcum = jnp.dot(cum, la[sl], preferred_element_type=F32, precision=lax.Precision.HIGHEST)
        b_last = bcum[0:1] if reverse else bcum[L - 1:L]
        q_in = q_ref[0, sl, :] * scale * jnp.exp(bcum)
        k_c = k_ref[0, sl, :]
        k_in = k_c * jnp.exp(-bcum)
        k_dec = k_c * jnp.exp(b_last - bcum)
        decay = jnp.exp(b_last)
        for h in range(GLA_HEADS):
            ks = slice(h * GLA_DK, (h + 1) * GLA_DK)
            vs = slice(h * GLA_DV, (h + 1) * GLA_DV)
            qh = q_in[:, ks].astype(BF16)
            vh = v_ref[0, sl, vs].astype(BF16)
            a = jnp.where(amask, _dot_nt(qh, k_in[:, ks].astype(BF16)), 0.0)
            st = state_ref[h]
            o_ref[0, sl, vs] = _dot(a.astype(BF16), vh) + _dot_nt(qh, st.astype(BF16))
            state_ref[h] = decay[:, ks] * st + _dot_tn(vh, k_dec[:, ks].astype(BF16))


def _gla_kernel(qf_ref, kf_ref, vf_ref, mf_ref, qb_ref, kb_ref, vb_ref, mb_ref, wg_ref, bg_ref,
                sf0_ref, sb0_ref, of_ref, ob_ref, sf_ref, sb_ref, stf, stb):
    t = pl.program_id(1)

    @pl.when(t == 0)
    def _():
        stf[...] = sf0_ref[0]
        stb[...] = sb0_ref[0]

    width = GLA_HEADS * GLA_DK
    _gla_direction(qf_ref, kf_ref, vf_ref, mf_ref, wg_ref[:, :width], bg_ref[:, :width], stf, of_ref, False)
    _gla_direction(qb_ref, kb_ref, vb_ref, mb_ref, wg_ref[:, width:], bg_ref[:, width:], stb, ob_ref, True)

    @pl.when(t == pl.num_programs(1) - 1)
    def _():
        sf_ref[0] = stf[...]
        sb_ref[0] = stb[...]


def _gla_scan(proj, wg, bg, sf0, sb0):
    b, r, _ = proj.shape
    tm = 256
    nt = r // tm
    kw, vw = GLA_HEADS * GLA_DK, GLA_HEADS * GLA_DV
    fwd = lambda t: t
    bwd = lambda t: nt - 1 - t

    def specs(order):
        return [pl.BlockSpec((1, tm, kw), lambda i, t: (i, order(t), P_GLA_Q // kw)),
                pl.BlockSpec((1, tm, kw), lambda i, t: (i, order(t), P_GLA_K // kw)),
                pl.BlockSpec((1, tm, vw), lambda i, t: (i, order(t), P_GLA_V // vw)),
                pl.BlockSpec((1, tm, 128), lambda i, t: (i, order(t), P_MISC // 128))]

    state_spec = pl.BlockSpec((1, GLA_HEADS, GLA_DV, GLA_DK), lambda i, t: (i, 0, 0, 0))
    state_shape = jax.ShapeDtypeStruct((b, GLA_HEADS, GLA_DV, GLA_DK), F32)
    return pl.pallas_call(
        _gla_kernel,
        grid=(b, nt),
        in_specs=specs(fwd) + specs(bwd) + [
            pl.BlockSpec(wg.shape, lambda i, t: (0, 0)),
            pl.BlockSpec(bg.shape, lambda i, t: (0, 0)),
            state_spec, state_spec],
        out_specs=[pl.BlockSpec((1, tm, vw), lambda i, t: (i, fwd(t), 0)),
                   pl.BlockSpec((1, tm, vw), lambda i, t: (i, bwd(t), 0)),
                   state_spec, state_spec],
        out_shape=[jax.ShapeDtypeStruct((b, r, vw), F32), jax.ShapeDtypeStruct((b, r, vw), F32),
                   state_shape, state_shape],
        scratch_shapes=[pltpu.VMEM((GLA_HEADS, GLA_DV, GLA_DK), F32),
                        pltpu.VMEM((GLA_HEADS, GLA_DV, GLA_DK), F32)],
        compiler_params=_params(2),
        name="gla_scan",
    )(proj, proj, proj, proj, proj, proj, proj, proj, wg, bg, sf0, sb0)


def _outproj_kernel(x_ref, mod_ref, mla_ref, swa_ref, of_ref, ob_ref, r_ref, g_ref, w_ref, o_ref):
    o = of_ref[0] + ob_ref[0]
    gate = _silu(r_ref[0])
    parts = []
    for h in range(GLA_HEADS):
        vs = slice(h * GLA_DV, (h + 1) * GLA_DV)
        parts.append((_rms(o[:, vs], g_ref[:, vs]) * gate[:, vs]).astype(BF16))
    gla = jnp.concatenate(parts, axis=1)
    a, b = mla_ref.shape[2], mla_ref.shape[2] + swa_ref.shape[2]
    mix = _dot(mla_ref[0], w_ref[0:a, :]) + _dot(swa_ref[0], w_ref[a:b, :]) + _dot(gla, w_ref[b:, :])
    o_ref[0] = x_ref[0] + mod_ref[0, 2:3, :] * mix


def _out_projection(x, mod, mla, swa, o_f, o_b, proj, g_out, w_out):
    bx, r, d = x.shape
    tm = 512
    row = lambda width: pl.BlockSpec((1, tm, width), lambda b, t: (b, t, 0))
    gw = GLA_HEADS * GLA_DV
    return pl.pallas_call(
        _outproj_kernel,
        grid=(bx, r // tm),
        in_specs=[row(d),
                  pl.BlockSpec((1, N_MOD, d), lambda b, t: (b, 0, 0)),
                  row(mla.shape[2]), row(swa.shape[2]), row(gw), row(gw),
                  pl.BlockSpec((1, tm, gw), lambda b, t: (b, t, P_GLA_R // gw)),
                  pl.BlockSpec((1, gw), lambda b, t: (0, 0)),
                  pl.BlockSpec(w_out.shape, lambda b, t: (0, 0))],
        out_specs=row(d),
        out_shape=jax.ShapeDtypeStruct(x.shape, F32),
        compiler_params=_params(2),
        name="out_projection",
    )(x, mod, mla, swa, o_f, o_b, proj, g_out, w_out)


def _ffn_kernel(x_ref, mod_ref, g_ref, wg_ref, wu_ref, wd_ref, gf_ref, o_ref, h_ref, acc_ref, *, final_norm):
    f = pl.program_id(2)

    @pl.when(f == 0)
    def _():
        h = _rms(x_ref[0], g_ref[...]) * (1.0 + mod_ref[0, 4:5, :]) + mod_ref[0, 3:4, :]
        h_ref[...] = h.astype(BF16)
        acc_ref[...] = jnp.zeros_like(acc_ref)

    h = h_ref[...]
    act = _silu(_dot(h, wg_ref[...])) * _dot(h, wu_ref[...])
    acc_ref[...] += _dot(act.astype(BF16), wd_ref[...])

    @pl.when(f == pl.num_programs(2) - 1)
    def _():
        y = x_ref[0] + mod_ref[0, 5:6, :] * acc_ref[...]
        if final_norm:
            y = _rms(y, gf_ref[...])
        o_ref[0] = y


def _ffn(x, mod, g, w_gu, w_down, g_final, final_norm):
    bx, r, d = x.shape
    hidden = w_down.shape[0]
    tm, tf = 512, 512
    nf = hidden // tf
    return pl.pallas_call(
        functools.partial(_ffn_kernel, final_norm=final_norm),
        grid=(bx, r // tm, nf),
        in_specs=[pl.BlockSpec((1, tm, d), lambda b, t, f: (b, t, 0)),
                  pl.BlockSpec((1, N_MOD, d), lambda b, t, f: (b, 0, 0)),
                  pl.BlockSpec((1, d), lambda b, t, f: (0, 0)),
                  pl.BlockSpec((d, tf), lambda b, t, f: (0, f)),
                  pl.BlockSpec((d, tf), lambda b, t, f: (0, nf + f)),
                  pl.BlockSpec((tf, d), lambda b, t, f: (f, 0)),
                  pl.BlockSpec((1, d), lambda b, t, f: (0, 0))],
        out_specs=pl.BlockSpec((1, tm, d), lambda b, t, f: (b, t, 0)),
        out_shape=jax.ShapeDtypeStruct(x.shape, F32),
        scratch_shapes=[pltpu.VMEM((tm, d), BF16), pltpu.VMEM((tm, d), F32)],
        compiler_params=_params(3),
        name="ffn",
    )(x, mod, g, w_gu, w_gu, w_down, g_final)


def _rope_tables(n, dim):
    half = dim // 4
    freqs = ROPE_THETA ** (-jnp.arange(half, dtype=F32) / half)
    pos = jnp.arange(n, dtype=jnp.int32)
    ang_r = (pos // GRID_W).astype(F32)[:, None] * freqs
    ang_c = (pos % GRID_W).astype(F32)[:, None] * freqs
    cos = jnp.concatenate([jnp.cos(ang_r)] * 2 + [jnp.cos(ang_c)] * 2, axis=1)
    sin = jnp.concatenate([-jnp.sin(ang_r), jnp.sin(ang_r), -jnp.sin(ang_c), jnp.sin(ang_c)], axis=1)
    reps = 128 // dim
    return jnp.tile(cos, (1, reps)), jnp.tile(sin, (1, reps))


def _relayout_w_in(w_in):
    depth, d, _ = w_in.shape
    sizes = (MLA_RANK, MLA_RANK, MLA_ROPE, SWA_HEADS * SWA_DIM, SWA_KV_HEADS * SWA_DIM, SWA_KV_HEADS * SWA_DIM,
             GLA_HEADS * GLA_DK, GLA_HEADS * GLA_DK, GLA_HEADS * GLA_DV, 2 * GLA_GATE_RANK, GLA_HEADS * GLA_DV)
    offs = [0]
    for s in sizes:
        offs.append(offs[-1] + s)
    part = lambda i: w_in[:, :, offs[i]:offs[i + 1]]
    cq, ckv, kr, sq, sk, sv, gq, gk, gv, glr, gr = (part(i) for i in range(len(sizes)))
    zeros = lambda w: jnp.zeros((depth, d, w), w_in.dtype)
    misc_pad = 128 - MLA_ROPE - 2 * GLA_GATE_RANK
    out = jnp.concatenate([sq, sk, cq, ckv, sv, kr, glr, zeros(misc_pad), zeros(P_GLA_Q - P_MISC - 128),
                           gq, gk, gv, gr], axis=-1)
    assert out.shape[-1] == P_WIDTH
    return out.astype(BF16)


def _relayout_mla(w_uq, w_ukv):
    depth, rk, _ = w_uq.shape
    uq = w_uq.reshape(depth, rk, MLA_HEADS, MLA_QK)
    uq = jnp.concatenate([uq[..., :MLA_NOPE].reshape(depth, rk, -1), uq[..., MLA_NOPE:].reshape(depth, rk, -1)], -1)
    ukv = w_ukv.reshape(depth, rk, MLA_HEADS, MLA_NOPE + MLA_V)
    ukv = jnp.concatenate([ukv[..., :MLA_NOPE].reshape(depth, rk, -1), ukv[..., MLA_NOPE:].reshape(depth, rk, -1)], -1)
    return uq.astype(BF16), ukv.astype(BF16)


def _relayout_gate(w_f, b_f, w_b, b_b):
    depth, rank, width = w_f.shape
    wg = jnp.zeros((depth, 128, 2 * width), F32)
    wg = wg.at[:, MISC_GATE_OFF:MISC_GATE_OFF + rank, :width].set(w_f)
    wg = wg.at[:, MISC_GATE_OFF + rank:MISC_GATE_OFF + 2 * rank, width:].set(w_b)
    bg = jnp.concatenate([b_f, b_b], axis=-1).reshape(depth, 1, 2 * width)
    return wg.astype(BF16), bg


def kernel(x, c, ctx, c_ctx, w_mod, b_mod, g_mix, g_ffn, w_in, g_mla_q, g_mla_kv, w_mla_uq, w_mla_ukv,
           swa_sink, w_gla_gate_f, b_gla_gate_f, w_gla_gate_b, b_gla_gate_b, g_gla_out, w_out, w_ffn_gu,
           w_ffn_down, g_final):
    B, N, D = x.shape
    C = ctx.shape[1]
    depth = w_mod.shape[0]

    cvec = jnp.concatenate([c, c_ctx[None, :], jnp.zeros((16 - B - 1, D), F32)], axis=0)
    mod = _modulation(cvec, w_mod, b_mod).reshape(depth, 16, N_MOD, D)

    w_in_p = _relayout_w_in(w_in)
    w_uq_p, w_ukv_p = _relayout_mla(w_mla_uq, w_mla_ukv)
    wg_p, bg_p = _relayout_gate(w_gla_gate_f, b_gla_gate_f, w_gla_gate_b, b_gla_gate_b)
    w_out_b = w_out.astype(BF16)
    w_gu_b = w_ffn_gu.astype(BF16)
    w_down_b = w_ffn_down.astype(BF16)

    cos_m, sin_m = _rope_tables(N, MLA_ROPE)
    cos_s, sin_s = _rope_tables(N, SWA_DIM)
    cos_id, sin_id = jnp.ones((C, 128), F32), jnp.zeros((C, 128), F32)
    state0 = jnp.zeros((B, GLA_HEADS, GLA_DV, GLA_DK), F32)
    g_fin = g_final.reshape(1, D)

    xc = ctx.reshape(1, B * C, D)
    for l in range(depth):
        last = l == depth - 1
        mod_l, mod_c = mod[l, :B], mod[l, B:B + 1]
        g_mix_l, g_ffn_l = g_mix[l].reshape(1, D), g_ffn[l].reshape(1, D)
        g_q, g_kv = g_mla_q[l].reshape(1, -1), g_mla_kv[l].reshape(1, -1)
        g_out = g_gla_out[l].reshape(1, -1)
        sink = swa_sink[l].reshape(1, -1)

        proj_l = _in_projection(x, mod_l, g_mix_l, w_in_p[l])
        proj_c = _in_projection(xc, mod_c, g_mix_l, w_in_p[l]).reshape(B, C, P_WIDTH)

        q_l, k_l, v_l = _mla_prep(proj_l, cos_m, sin_m, g_q, g_kv, w_uq_p[l], w_ukv_p[l])
        q_c, k_c, v_c = _mla_prep(proj_c, cos_id, sin_id, g_q, g_kv, w_uq_p[l], w_ukv_p[l])
        mla_l = _mla_attention(q_l, k_l, v_l, k_c, v_c)
        swa_l = _swa_attention(proj_l, proj_c, cos_s, sin_s, sink)

        of_c, ob_c, s_f, s_b = _gla_scan(proj_c, wg_p[l], bg_p[l], state0, state0)
        of_l, ob_l, _, _ = _gla_scan(proj_l, wg_p[l], bg_p[l], s_f, s_b)

        x = _out_projection(x, mod_l, mla_l, swa_l, of_l, ob_l, proj_l, g_out, w_out_b[l])
        x = _ffn(x, mod_l, g_ffn_l, w_gu_b[l], w_down_b[l], g_fin, last)

        if not last:
            mla_c = _mla_ctx_attention(q_c, k_c, v_c)
            swa_c = _swa_ctx_attention(proj_c, sink)
            flat = lambda a: a.reshape(1, B * C, a.shape[-1])
            xc = _out_projection(xc, mod_c, flat(mla_c), flat(swa_c), flat(of_c), flat(ob_c), flat(proj_c),
                                 g_out, w_out_b[l])
            xc = _ffn(xc, mod_c, g_ffn_l, w_gu_b[l], w_down_b[l], g_fin, False)
    return x
```

```python
import functools

import jax
import jax.numpy as jnp
from jax import lax
from jax.experimental import pallas as pl
from jax.experimental.pallas import tpu as pltpu

F32 = jnp.float32
BF16 = jnp.bfloat16

GRID_W = 64
EPS = 1e-6
ROPE_THETA = 10000.0
LOG2E = 1.4426950408889634

MLA_HEADS = 6
MLA_RANK = 512
MLA_NOPE = 128
MLA_ROPE = 64
MLA_V = 128
MLA_QK = MLA_NOPE + MLA_ROPE
MLA_VA = MLA_V + 16

SWA_HEADS = 6
SWA_KV_HEADS = 2
SWA_GROUP = SWA_HEADS // SWA_KV_HEADS
SWA_DIM = 128
SWA_BLOCK = 128

GLA_HEADS = 4
GLA_DK = 64
GLA_DV = 128
GLA_GATE_RANK = 16
GLA_TAU = 16.0
GLA_CHUNK = 64
GLA_STATE = (GLA_HEADS // 2, GLA_DV, 2 * GLA_DK)

N_MOD = 6

P_SWA_Q = 0
P_SWA_K = 768
P_CQ = 1024
P_CKV = 1536
P_SWA_V = 2048
P_MISC = 2304
P_GLA_Q = 2560
P_GLA_K = 2816
P_GLA_V = 3072
P_GLA_R = 3584
P_WIDTH = 4096
MISC_GATE_OFF = MLA_ROPE

VMEM_LIMIT = 56 * 1024 * 1024


def _params(n_axes, flags=None):
    return pltpu.CompilerParams(dimension_semantics=("arbitrary",) * n_axes,
                                vmem_limit_bytes=VMEM_LIMIT, flags=flags)


def _silu(x):
    return x / (1.0 + jnp.exp(-x))


def _rms(x, g):
    ms = jnp.mean(x * x, axis=-1, keepdims=True)
    return x * lax.rsqrt(ms + EPS) * g


def _dot(a, b):
    return jnp.dot(a, b, preferred_element_type=F32)


def _dot_nt(a, b):
    return lax.dot_general(a, b, (((1,), (1,)), ((), ())), preferred_element_type=F32)


def _dot_tn(a, b):
    return lax.dot_general(a, b, (((0,), (0,)), ((), ())), preferred_element_type=F32)


def _mod_kernel(c_ref, w_ref, b_ref, o_ref):
    a = _silu(c_ref[...]).astype(BF16)
    o_ref[0] = _dot(a, w_ref[0].astype(BF16)) + b_ref[0]


def _modulation(cvec, w_mod, b_mod):
    depth, d, width = w_mod.shape
    rows = cvec.shape[0]
    tn = 1024
    return pl.pallas_call(
        _mod_kernel,
        grid=(depth, width // tn),
        in_specs=[pl.BlockSpec((rows, d), lambda l, j: (0, 0)),
                  pl.BlockSpec((1, d, tn), lambda l, j: (l, 0, j)),
                  pl.BlockSpec((1, 1, tn), lambda l, j: (l, 0, j))],
        out_specs=pl.BlockSpec((1, rows, tn), lambda l, j: (l, 0, j)),
        out_shape=jax.ShapeDtypeStruct((depth, rows, width), F32),
        compiler_params=_params(2),
        name="modulation",
    )(cvec, w_mod, b_mod.reshape(depth, 1, width))


def _inproj_kernel(x_ref, mod_ref, g_ref, w_ref, o_ref):
    h = _rms(x_ref[0], g_ref[...]) * (1.0 + mod_ref[0, 1:2, :]) + mod_ref[0, 0:1, :]
    o_ref[0] = _dot(h.astype(BF16), w_ref[...])


def _in_projection(x, mod, g, w):
    bx, r, d = x.shape
    tm, tn = 512, 2048
    return pl.pallas_call(
        _inproj_kernel,
        grid=(P_WIDTH // tn, bx, r // tm),
        in_specs=[pl.BlockSpec((1, tm, d), lambda j, b, t: (b, t, 0)),
                  pl.BlockSpec((1, N_MOD, d), lambda j, b, t: (b, 0, 0)),
                  pl.BlockSpec((1, d), lambda j, b, t: (0, 0)),
                  pl.BlockSpec((d, tn), lambda j, b, t: (0, j))],
        out_specs=pl.BlockSpec((1, tm, tn), lambda j, b, t: (b, t, j)),
        out_shape=jax.ShapeDtypeStruct((bx, r, P_WIDTH), F32),
        compiler_params=_params(3),
        name="in_projection",
    )(x, mod, g, w)


def _rope(x, cos, sin, half):
    lane = lax.broadcasted_iota(jnp.int32, x.shape, 1)
    first = (lane % (2 * half)) < half
    width = x.shape[1]
    rot = jnp.where(first, pltpu.roll(x, width - half, 1), pltpu.roll(x, half, 1))
    return x * cos + rot * sin


def _mla_prep_kernel(cq_ref, ckv_ref, misc_ref, cos_ref, sin_ref, gq_ref, gkv_ref, wuq_ref, wukv_ref,
                     qt_ref, k_ref, vt_ref):
    scale = MLA_QK ** -0.5 * LOG2E
    cos, sin = cos_ref[...], sin_ref[...]
    half = MLA_ROPE // 4
    qf = _dot(_rms(cq_ref[0], gq_ref[...]).astype(BF16), wuq_ref[...])
    kvf = _dot(_rms(ckv_ref[0], gkv_ref[...]).astype(BF16), wukv_ref[...])
    kr = _rope(misc_ref[0], cos, sin, half)[:, :MLA_ROPE].astype(BF16)
    nope_w = MLA_HEADS * MLA_NOPE
    for pair in range(MLA_HEADS // 2):
        qr = _rope(qf[:, nope_w + 128 * pair: nope_w + 128 * (pair + 1)], cos, sin, half) * scale
        qrt = qr.T.astype(BF16)
        for j in range(2):
            qt_ref[0, 2 * pair + j, MLA_NOPE:MLA_QK, :] = qrt[MLA_ROPE * j: MLA_ROPE * (j + 1), :]
    for h in range(MLA_HEADS):
        qt_ref[0, h, 0:MLA_NOPE, :] = (qf[:, MLA_NOPE * h: MLA_NOPE * (h + 1)] * scale).T.astype(BF16)
        k_ref[0, h, :, 0:MLA_NOPE] = kvf[:, MLA_NOPE * h: MLA_NOPE * (h + 1)].astype(BF16)
        k_ref[0, h, :, MLA_NOPE:MLA_QK] = kr
        vt_ref[0, h, 0:MLA_V, :] = kvf[:, nope_w + MLA_V * h: nope_w + MLA_V * (h + 1)].T.astype(BF16)
        vt_ref[0, h, MLA_V:, :] = jnp.ones((MLA_VA - MLA_V, vt_ref.shape[3]), BF16)


def _mla_prep(proj, cos, sin, g_q, g_kv, w_uq, w_ukv):
    b, r, _ = proj.shape
    tm = min(r, 512)
    rk = MLA_RANK
    const = lambda shape: pl.BlockSpec(shape, lambda i, t: (0,) * len(shape))
    return pl.pallas_call(
        _mla_prep_kernel,
        grid=(b, r // tm),
        in_specs=[pl.BlockSpec((1, tm, rk), lambda i, t: (i, t, P_CQ // rk)),
                  pl.BlockSpec((1, tm, rk), lambda i, t: (i, t, P_CKV // rk)),
                  pl.BlockSpec((1, tm, 128), lambda i, t: (i, t, P_MISC // 128)),
                  pl.BlockSpec((tm, 128), lambda i, t: (t, 0)),
                  pl.BlockSpec((tm, 128), lambda i, t: (t, 0)),
                  const((1, rk)), const((1, rk)),
                  const(w_uq.shape), const(w_ukv.shape)],
        out_specs=[pl.BlockSpec((1, MLA_HEADS, MLA_QK, tm), lambda i, t: (i, 0, 0, t)),
                   pl.BlockSpec((1, MLA_HEADS, tm, MLA_QK), lambda i, t: (i, 0, t, 0)),
                   pl.BlockSpec((1, MLA_HEADS, MLA_VA, tm), lambda i, t: (i, 0, 0, t))],
        out_shape=[jax.ShapeDtypeStruct((b, MLA_HEADS, MLA_QK, r), BF16),
                   jax.ShapeDtypeStruct((b, MLA_HEADS, r, MLA_QK), BF16),
                   jax.ShapeDtypeStruct((b, MLA_HEADS, MLA_VA, r), BF16)],
        compiler_params=_params(2),
        name="mla_prep",
    )(proj, proj, proj, cos, sin, g_q, g_kv, w_uq, w_ukv)


def _softmax_pv(s_parts, vt_parts):
    m = functools.reduce(jnp.maximum, [jnp.max(s, axis=0, keepdims=True) for s in s_parts])
    acc = sum(_dot(vt, jnp.exp2(s - m).astype(BF16)) for s, vt in zip(s_parts, vt_parts))
    return (acc[:MLA_V] / acc[MLA_V:MLA_V + 1]).T


def _mla_attn_kernel(qt_ref, kl_ref, kc_ref, vlt_ref, vct_ref, o_ref, s_scr, m_scr, *, tk):
    @pl.when(pl.program_id(0) == 0)
    def _():
        s_scr[...] = jnp.zeros_like(s_scr)
        m_scr[...] = jnp.zeros_like(m_scr)

    n_lat = kl_ref.shape[2] // tk
    qt = qt_ref[0, 0]
    tq = qt.shape[1]
    m_prev = m_scr[...]
    m_new = jnp.full((1, tq), -jnp.inf, F32)
    acc = jnp.zeros((MLA_VA, tq), F32)
    for j in range(n_lat + 1):
        rows = slice(j * tk, (j + 1) * tk)
        p = jnp.exp2(s_scr[rows, :] - m_prev).astype(BF16)
        k = kc_ref[0, 0] if j == n_lat else kl_ref[0, 0, rows, :]
        vt = vct_ref[0, 0] if j == n_lat else vlt_ref[0, 0, :, rows]
        s = _dot(k, qt)
        s_scr[rows, :] = s
        m_new = jnp.maximum(m_new, jnp.max(s, axis=0, keepdims=True))
        acc = acc + _dot(vt, p)
    m_scr[...] = m_new
    o_ref[0] = (acc[:MLA_V] / acc[MLA_V:MLA_V + 1]).T.astype(o_ref.dtype)


def _mla_attention(qt, k_l, vt_l, k_c, vt_c):
    b, h, dq, n = qt.shape
    c = k_c.shape[2]
    tq = 512
    nt = n // tq
    tiles = b * h * nt

    def cur(u):
        t = jnp.minimum(u, tiles - 1)
        return t // (h * nt), (t // nt) % h, t % nt

    def prev(u):
        t = jnp.maximum(u - 1, 0)
        return t // (h * nt), (t // nt) % h, t % nt

    assert n % c == 0
    return pl.pallas_call(
        functools.partial(_mla_attn_kernel, tk=c),
        grid=(tiles + 1,),
        in_specs=[pl.BlockSpec((1, 1, dq, tq), lambda u: (cur(u)[0], cur(u)[1], 0, cur(u)[2])),
                  pl.BlockSpec((1, 1, n, dq), lambda u: (cur(u)[0], cur(u)[1], 0, 0)),
                  pl.BlockSpec((1, 1, c, dq), lambda u: (cur(u)[0], cur(u)[1], 0, 0)),
                  pl.BlockSpec((1, 1, MLA_VA, n), lambda u: (prev(u)[0], prev(u)[1], 0, 0)),
                  pl.BlockSpec((1, 1, MLA_VA, c), lambda u: (prev(u)[0], prev(u)[1], 0, 0))],
        out_specs=pl.BlockSpec((1, tq, MLA_V), lambda u: (prev(u)[0], prev(u)[2], prev(u)[1])),
        out_shape=jax.ShapeDtypeStruct((b, n, h * MLA_V), BF16),
        scratch_shapes=[pltpu.VMEM((n + c, tq), F32), pltpu.VMEM((1, tq), F32)],
        compiler_params=_params(1),
        name="mla_attention",
    )(qt, k_l, k_c, vt_l, vt_c)


def _mla_ctx_kernel(qt_ref, k_ref, vt_ref, o_ref):
    o_ref[0] = _softmax_pv([_dot(k_ref[0, 0], qt_ref[0, 0])], [vt_ref[0, 0]]).astype(o_ref.dtype)


def _mla_ctx_attention(qt, k, vt):
    b, h, dq, c = qt.shape
    return pl.pallas_call(
        _mla_ctx_kernel,
        grid=(b, h),
        in_specs=[pl.BlockSpec((1, 1, dq, c), lambda i, j: (i, j, 0, 0)),
                  pl.BlockSpec((1, 1, c, dq), lambda i, j: (i, j, 0, 0)),
                  pl.BlockSpec((1, 1, MLA_VA, c), lambda i, j: (i, j, 0, 0))],
        out_specs=pl.BlockSpec((1, c, MLA_V), lambda i, j: (i, 0, j)),
        out_shape=jax.ShapeDtypeStruct((b, c, h * MLA_V), BF16),
        compiler_params=_params(2),
        name="mla_ctx_attention",
    )(qt, k, vt)


def _swa_ctx_kernel(sink_ref, q_ref, k_ref, v_ref, o_ref):
    q = (q_ref[0] * SWA_DIM ** -0.5).astype(BF16)
    s = _dot_nt(q, k_ref[0].astype(BF16))
    sink = sink_ref[0, pl.program_id(1)]
    m = jnp.maximum(jnp.max(s, axis=-1, keepdims=True), sink)
    p = jnp.exp(s - m)
    l = jnp.sum(p, axis=-1, keepdims=True) + jnp.exp(sink - m)
    o_ref[0] = (_dot(p.astype(BF16), v_ref[0].astype(BF16)) / l).astype(o_ref.dtype)


def _swa_ctx_attention(proj_c, sink):
    b, c, _ = proj_c.shape
    d = SWA_DIM
    return pl.pallas_call(
        _swa_ctx_kernel,
        grid=(b, SWA_HEADS),
        in_specs=[pl.BlockSpec(memory_space=pltpu.SMEM),
                  pl.BlockSpec((1, c, d), lambda i, j: (i, 0, P_SWA_Q // d + j)),
                  pl.BlockSpec((1, c, d), lambda i, j: (i, 0, P_SWA_K // d + j // SWA_GROUP)),
                  pl.BlockSpec((1, c, d), lambda i, j: (i, 0, P_SWA_V // d + j // SWA_GROUP))],
        out_specs=pl.BlockSpec((1, c, d), lambda i, j: (i, 0, j)),
        out_shape=jax.ShapeDtypeStruct((b, c, SWA_HEADS * d), BF16),
        compiler_params=_params(2),
        name="swa_ctx_attention",
    )(sink, proj_c, proj_c, proj_c)


def _swa_kernel(sink_ref, q_ref, k_ref, v_ref, kc_ref, vc_ref, cos_ref, sin_ref, o_ref, *, qb):
    g = pl.program_id(1)
    n = pl.program_id(2)
    blk, d = SWA_BLOCK, SWA_DIM
    half = d // 4
    nq = qb * blk
    nw = (qb + 2) * blk
    nb = k_ref.shape[1] // blk
    q_start = pl.multiple_of(n * nq, blk)
    k_start = pl.multiple_of(jnp.clip(n * qb - 1, 0, nb - (qb + 2)) * blk, blk)
    scale = d ** -0.5 * LOG2E
    cos_q, sin_q = cos_ref[pl.ds(q_start, nq), :], sin_ref[pl.ds(q_start, nq), :]
    q = jnp.concatenate(
        [(_rope(q_ref[0, :, d * i: d * (i + 1)], cos_q, sin_q, half) * scale).astype(BF16)
         for i in range(SWA_GROUP)], axis=0)
    k_w = _rope(k_ref[0, pl.ds(k_start, nw), :], cos_ref[pl.ds(k_start, nw), :], sin_ref[pl.ds(k_start, nw), :],
                half).astype(BF16)
    s_w = _dot_nt(k_w, q)
    s_c = _dot_nt(kc_ref[0].astype(BF16), q)

    key = lax.broadcasted_iota(jnp.int32, (nw, nq), 0)
    tok = lax.broadcasted_iota(jnp.int32, (nw, nq), 1)
    dist = key - tok + (k_start - q_start)
    valid = (dist <= blk) & (dist >= -blk)
    s_w = jnp.concatenate(
        [jnp.where(valid, s_w[:, nq * i: nq * (i + 1)], -jnp.inf) for i in range(SWA_GROUP)], axis=1)
    sink = jnp.concatenate(
        [jnp.full((1, nq), sink_ref[0, g * SWA_GROUP + i] * LOG2E, F32) for i in range(SWA_GROUP)], axis=1)
    m = jnp.maximum(jnp.maximum(jnp.max(s_w, axis=0, keepdims=True), jnp.max(s_c, axis=0, keepdims=True)), sink)

    def values_t(v):
        return jnp.concatenate([v.T.astype(BF16), jnp.ones((16, v.shape[0]), BF16)], axis=0)

    acc = (_dot(values_t(v_ref[0, pl.ds(k_start, nw), :]), jnp.exp2(s_w - m).astype(BF16))
           + _dot(values_t(vc_ref[0]), jnp.exp2(s_c - m).astype(BF16)))
    out_t = acc[:d] / (acc[d:d + 1] + jnp.exp2(sink - m))
    for i in range(SWA_GROUP):
        o_ref[0, :, d * i: d * (i + 1)] = out_t[:, nq * i: nq * (i + 1)].T.astype(o_ref.dtype)


def _swa_attention(proj_l, proj_c, cos, sin, sink):
    b, n, _ = proj_l.shape
    c = proj_c.shape[1]
    blk, d = SWA_BLOCK, SWA_DIM
    qb = 2
    kcol = lambda g: P_SWA_K // d + g
    vcol = lambda g: P_SWA_V // d + g
    return pl.pallas_call(
        functools.partial(_swa_kernel, qb=qb),
        grid=(b, SWA_KV_HEADS, n // (qb * blk)),
        in_specs=[pl.BlockSpec(memory_space=pltpu.SMEM),
                  pl.BlockSpec((1, qb * blk, SWA_GROUP * d), lambda i, g, t: (i, t, g)),
                  pl.BlockSpec((1, n, d), lambda i, g, t: (i, 0, kcol(g))),
                  pl.BlockSpec((1, n, d), lambda i, g, t: (i, 0, vcol(g))),
                  pl.BlockSpec((1, c, d), lambda i, g, t: (i, 0, kcol(g))),
                  pl.BlockSpec((1, c, d), lambda i, g, t: (i, 0, vcol(g))),
                  pl.BlockSpec((n, d), lambda i, g, t: (0, 0)),
                  pl.BlockSpec((n, d), lambda i, g, t: (0, 0))],
        out_specs=pl.BlockSpec((1, qb * blk, SWA_GROUP * d), lambda i, g, t: (i, t, g)),
        out_shape=jax.ShapeDtypeStruct((b, n, SWA_HEADS * d), BF16),
        compiler_params=_params(3),
        name="swa_attention",
    )(sink, proj_l, proj_l, proj_l, proj_c, proj_c, cos, sin)


def _log_sigmoid(x):
    return jnp.minimum(x, 0.0) - jnp.log(1.0 + jnp.exp(-jnp.abs(x)))


def _gla_factors(q_ref, k_ref, misc_ref, wg, bg, reverse):
    rows = q_ref.shape[1]
    L = GLA_CHUNK
    assert rows == 4 * L
    width = GLA_HEADS * GLA_DK
    la = _log_sigmoid(_dot(misc_ref[0].astype(BF16), wg) + bg) * (1.0 / GLA_TAU)
    hi = la.astype(BF16)
    lo = (la - hi.astype(F32)).astype(BF16)
    ti = lax.broadcasted_iota(jnp.int32, (rows, rows), 0)
    tj = lax.broadcasted_iota(jnp.int32, (rows, rows), 1)
    ci, cj = ti // L, tj // L
    upto = (tj >= ti) if reverse else (tj <= ti)
    ones_in = jnp.where((ci == cj) & upto, 1.0, 0.0).astype(BF16)
    ones_abs = jnp.where(upto, 1.0, 0.0).astype(BF16)
    sums = _dot(jnp.concatenate([ones_in, ones_abs], axis=0), jnp.concatenate([hi, lo], axis=1))
    sums = sums[:, :width] + sums[:, width:]
    b_in, b_abs = sums[:rows], sums[rows:]
    if reverse:
        g_mid, g_end = b_abs[rows // 2:rows // 2 + 1], b_abs[0:1]
        diag, cross = (ci == cj) & (tj > ti), cj > ci
    else:
        g_mid, g_end = b_abs[rows // 2 - 1:rows // 2], b_abs[rows - 1:rows]
        diag, cross = (ci == cj) & (tj <= ti), cj < ci
    q = q_ref[0] * GLA_DK ** -0.5
    k = k_ref[0]
    q_d, k_d = q * jnp.exp(b_in), k * jnp.exp(-b_in)
    q_x, k_x = q * jnp.exp(b_abs - g_mid), k * jnp.exp(g_mid - b_abs)
    q_s, k_s = q * jnp.exp(b_abs), k * jnp.exp(g_end - b_abs)
    decay = jnp.exp(g_end)
    return q_d, k_d, q_x, k_x, q_s, k_s, decay, diag, cross


def _gla_apply(factors, v_ref, state_ref, o_ref):
    q_d, k_d, q_x, k_x, q_s, k_s, decay, diag, cross = factors
    rows = q_d.shape[0]
    lane = lax.broadcasted_iota(jnp.int32, (rows, 2 * GLA_DK), 1)
    lane_s = lax.broadcasted_iota(jnp.int32, (GLA_DV, 2 * GLA_DK), 1)
    for pair in range(GLA_HEADS // 2):
        ps = slice(2 * GLA_DK * pair, 2 * GLA_DK * (pair + 1))
        kd_p, kx_p, ks_p = k_d[:, ps].astype(BF16), k_x[:, ps].astype(BF16), k_s[:, ps].astype(BF16)
        st = state_ref[pair]
        st_b = st.astype(BF16)
        ds = []
        for j in range(2):
            h = 2 * pair + j
            vs = slice(h * GLA_DV, (h + 1) * GLA_DV)
            mine = (lane // GLA_DK) == j
            qd_h = jnp.where(mine, q_d[:, ps], 0.0).astype(BF16)
            qx_h = jnp.where(mine, q_x[:, ps], 0.0).astype(BF16)
            qs_h = jnp.where(mine, q_s[:, ps], 0.0).astype(BF16)
            vh = v_ref[0, :, vs].astype(BF16)
            a = jnp.where(diag, _dot_nt(qd_h, kd_p), jnp.where(cross, _dot_nt(qx_h, kx_p), 0.0))
            o_ref[0, :, vs] = _dot(a.astype(BF16), vh) + _dot_nt(qs_h, st_b)
            ds.append(_dot_tn(vh, ks_p))
        state_ref[pair] = decay[:, ps] * st + jnp.where(lane_s < GLA_DK, ds[0], ds[1])


def _gla_kernel(qf_ref, kf_ref, vf_ref, mf_ref, qb_ref, kb_ref, vb_ref, mb_ref, wg_ref, bg_ref,
                sf0_ref, sb0_ref, of_ref, ob_ref, sf_ref, sb_ref, stf, stb):
    t = pl.program_id(1)

    @pl.when(t == 0)
    def _():
        stf[...] = sf0_ref[0]
        stb[...] = sb0_ref[0]

    width = GLA_HEADS * GLA_DK
    fwd = _gla_factors(qf_ref, kf_ref, mf_ref, wg_ref[:, :width], bg_ref[:, :width], False)
    bwd = _gla_factors(qb_ref, kb_ref, mb_ref, wg_ref[:, width:], bg_ref[:, width:], True)
    _gla_apply(fwd, vf_ref, stf, of_ref)
    _gla_apply(bwd, vb_ref, stb, ob_ref)

    @pl.when(t == pl.num_programs(1) - 1)
    def _():
        sf_ref[0] = stf[...]
        sb_ref[0] = stb[...]


def _gla_scan(proj, wg, bg, sf0, sb0):
    b, r, _ = proj.shape
    tm = 256
    nt = r // tm
    kw, vw = GLA_HEADS * GLA_DK, GLA_HEADS * GLA_DV
    fwd = lambda t: t
    bwd = lambda t: nt - 1 - t

    def specs(order):
        return [pl.BlockSpec((1, tm, kw), lambda i, t: (i, order(t), P_GLA_Q // kw)),
                pl.BlockSpec((1, tm, kw), lambda i, t: (i, order(t), P_GLA_K // kw)),
                pl.BlockSpec((1, tm, vw), lambda i, t: (i, order(t), P_GLA_V // vw)),
                pl.BlockSpec((1, tm, 128), lambda i, t: (i, order(t), P_MISC // 128))]

    state_spec = pl.BlockSpec((1,) + GLA_STATE, lambda i, t: (i, 0, 0, 0))
    state_shape = jax.ShapeDtypeStruct((b,) + GLA_STATE, F32)
    return pl.pallas_call(
        _gla_kernel,
        grid=(b, nt),
        in_specs=specs(fwd) + specs(bwd) + [
            pl.BlockSpec(wg.shape, lambda i, t: (0, 0)),
            pl.BlockSpec(bg.shape, lambda i, t: (0, 0)),
            state_spec, state_spec],
        out_specs=[pl.BlockSpec((1, tm, vw), lambda i, t: (i, fwd(t), 0)),
                   pl.BlockSpec((1, tm, vw), lambda i, t: (i, bwd(t), 0)),
                   state_spec, state_spec],
        out_shape=[jax.ShapeDtypeStruct((b, r, vw), F32), jax.ShapeDtypeStruct((b, r, vw), F32),
                   state_shape, state_shape],
        scratch_shapes=[pltpu.VMEM(GLA_STATE, F32), pltpu.VMEM(GLA_STATE, F32)],
        compiler_params=_params(2),
        name="gla_scan",
    )(proj, proj, proj, proj, proj, proj, proj, proj, wg, bg, sf0, sb0)


def _outproj_kernel(x_ref, mod_ref, mla_ref, swa_ref, of_ref, ob_ref, r_ref, g_ref, w_ref, o_ref):
    o = of_ref[0] + ob_ref[0]
    gate = _silu(r_ref[0])
    parts = []
    for h in range(GLA_HEADS):
        vs = slice(h * GLA_DV, (h + 1) * GLA_DV)
        parts.append((_rms(o[:, vs], g_ref[:, vs]) * gate[:, vs]).astype(BF16))
    gla = jnp.concatenate(parts, axis=1)
    a, b = mla_ref.shape[2], mla_ref.shape[2] + swa_ref.shape[2]
    mix = _dot(mla_ref[0], w_ref[0:a, :]) + _dot(swa_ref[0], w_ref[a:b, :]) + _dot(gla, w_ref[b:, :])
    o_ref[0] = x_ref[0] + mod_ref[0, 2:3, :] * mix


def _out_projection(x, mod, mla, swa, o_f, o_b, proj, g_out, w_out):
    bx, r, d = x.shape
    tm = 512
    row = lambda width: pl.BlockSpec((1, tm, width), lambda b, t: (b, t, 0))
    gw = GLA_HEADS * GLA_DV
    return pl.pallas_call(
        _outproj_kernel,
        grid=(bx, r // tm),
        in_specs=[row(d),
                  pl.BlockSpec((1, N_MOD, d), lambda b, t: (b, 0, 0)),
                  row(mla.shape[2]), row(swa.shape[2]), row(gw), row(gw),
                  pl.BlockSpec((1, tm, gw), lambda b, t: (b, t, P_GLA_R // gw)),
                  pl.BlockSpec((1, gw), lambda b, t: (0, 0)),
                  pl.BlockSpec(w_out.shape, lambda b, t: (0, 0))],
        out_specs=row(d),
        out_shape=jax.ShapeDtypeStruct(x.shape, F32),
        compiler_params=_params(2),
        name="out_projection",
    )(x, mod, mla, swa, o_f, o_b, proj, g_out, w_out)


def _ffn_kernel(x_ref, mod_ref, g_ref, wg_ref, wu_ref, wd_ref, gf_ref, o_ref, h_ref, acc_ref, *, final_norm):
    f = pl.program_id(2)

    @pl.when(f == 0)
    def _():
        h = _rms(x_ref[0], g_ref[...]) * (1.0 + mod_ref[0, 4:5, :]) + mod_ref[0, 3:4, :]
        h_ref[...] = h.astype(BF16)
        acc_ref[...] = jnp.zeros_like(acc_ref)

    h = h_ref[...]
    act = _silu(_dot(h, wg_ref[...])) * _dot(h, wu_ref[...])
    acc_ref[...] += _dot(act.astype(BF16), wd_ref[...])

    @pl.when(f == pl.num_programs(2) - 1)
    def _():
        y = x_ref[0] + mod_ref[0, 5:6, :] * acc_ref[...]
        if final_norm:
            y = _rms(y, gf_ref[...])
        o_ref[0] = y


def _ffn(x, mod, g, w_gu, w_down, g_final, final_norm):
    bx, r, d = x.shape
    hidden = w_down.shape[0]
    tm, tf = 512, 512
    nf = hidden // tf
    return pl.pallas_call(
        functools.partial(_ffn_kernel, final_norm=final_norm),
        grid=(bx, r // tm, nf),
        in_specs=[pl.BlockSpec((1, tm, d), lambda b, t, f: (b, t, 0)),
                  pl.BlockSpec((1, N_MOD, d), lambda b, t, f: (b, 0, 0)),
                  pl.BlockSpec((1, d), lambda b, t, f: (0, 0)),
                  pl.BlockSpec((d, tf), lambda b, t, f: (0, f)),
                  pl.BlockSpec((d, tf), lambda b, t, f: (0, nf + f)),
                  pl.BlockSpec((tf, d), lambda b, t, f: (f, 0)),
                  pl.BlockSpec((1, d), lambda b, t, f: (0, 0))],
        out_specs=pl.BlockSpec((1, tm, d), lambda b, t, f: (b, t, 0)),
        out_shape=jax.ShapeDtypeStruct(x.shape, F32),
        scratch_shapes=[pltpu.VMEM((tm, d), BF16), pltpu.VMEM((tm, d), F32)],
        compiler_params=_params(3),
        name="ffn",
    )(x, mod, g, w_gu, w_gu, w_down, g_final)


def _rope_tables(n, dim):
    half = dim // 4
    freqs = ROPE_THETA ** (-jnp.arange(half, dtype=F32) / half)
    pos = jnp.arange(n, dtype=jnp.int32)
    ang_r = (pos // GRID_W).astype(F32)[:, None] * freqs
    ang_c = (pos % GRID_W).astype(F32)[:, None] * freqs
    cos = jnp.concatenate([jnp.cos(ang_r)] * 2 + [jnp.cos(ang_c)] * 2, axis=1)
    sin = jnp.concatenate([-jnp.sin(ang_r), jnp.sin(ang_r), -jnp.sin(ang_c), jnp.sin(ang_c)], axis=1)
    reps = 128 // dim
    return jnp.tile(cos, (1, reps)), jnp.tile(sin, (1, reps))


def _relayout_w_in(w_in):
    depth, d, _ = w_in.shape
    sizes = (MLA_RANK, MLA_RANK, MLA_ROPE, SWA_HEADS * SWA_DIM, SWA_KV_HEADS * SWA_DIM, SWA_KV_HEADS * SWA_DIM,
             GLA_HEADS * GLA_DK, GLA_HEADS * GLA_DK, GLA_HEADS * GLA_DV, 2 * GLA_GATE_RANK, GLA_HEADS * GLA_DV)
    offs = [0]
    for s in sizes:
        offs.append(offs[-1] + s)
    part = lambda i: w_in[:, :, offs[i]:offs[i + 1]]
    cq, ckv, kr, sq, sk, sv, gq, gk, gv, glr, gr = (part(i) for i in range(len(sizes)))
    zeros = lambda w: jnp.zeros((depth, d, w), w_in.dtype)
    misc_pad = 128 - MLA_ROPE - 2 * GLA_GATE_RANK
    out = jnp.concatenate([sq, sk, cq, ckv, sv, kr, glr, zeros(misc_pad), zeros(P_GLA_Q - P_MISC - 128),
                           gq, gk, gv, gr], axis=-1)
    assert out.shape[-1] == P_WIDTH
    return out.astype(BF16)


def _relayout_mla(w_uq, w_ukv):
    depth, rk, _ = w_uq.shape
    uq = w_uq.reshape(depth, rk, MLA_HEADS, MLA_QK)
    uq = jnp.concatenate([uq[..., :MLA_NOPE].reshape(depth, rk, -1), uq[..., MLA_NOPE:].reshape(depth, rk, -1)], -1)
    ukv = w_ukv.reshape(depth, rk, MLA_HEADS, MLA_NOPE + MLA_V)
    ukv = jnp.concatenate([ukv[..., :MLA_NOPE].reshape(depth, rk, -1), ukv[..., MLA_NOPE:].reshape(depth, rk, -1)], -1)
    return uq.astype(BF16), ukv.astype(BF16)


def _relayout_gate(w_f, b_f, w_b, b_b):
    depth, rank, width = w_f.shape
    wg = jnp.zeros((depth, 128, 2 * width), F32)
    wg = wg.at[:, MISC_GATE_OFF:MISC_GATE_OFF + rank, :width].set(w_f)
    wg = wg.at[:, MISC_GATE_OFF + rank:MISC_GATE_OFF + 2 * rank, width:].set(w_b)
    bg = jnp.concatenate([b_f, b_b], axis=-1).reshape(depth, 1, 2 * width)
    return wg.astype(BF16), bg


def kernel(x, c, ctx, c_ctx, w_mod, b_mod, g_mix, g_ffn, w_in, g_mla_q, g_mla_kv, w_mla_uq, w_mla_ukv,
           swa_sink, w_gla_gate_f, b_gla_gate_f, w_gla_gate_b, b_gla_gate_b, g_gla_out, w_out, w_ffn_gu,
           w_ffn_down, g_final):
    B, N, D = x.shape
    C = ctx.shape[1]
    depth = w_mod.shape[0]

    cvec = jnp.concatenate([c, c_ctx[None, :], jnp.zeros((16 - B - 1, D), F32)], axis=0)
    mod = _modulation(cvec, w_mod, b_mod).reshape(depth, 16, N_MOD, D)

    w_in_p = _relayout_w_in(w_in)
    w_uq_p, w_ukv_p = _relayout_mla(w_mla_uq, w_mla_ukv)
    wg_p, bg_p = _relayout_gate(w_gla_gate_f, b_gla_gate_f, w_gla_gate_b, b_gla_gate_b)
    w_out_b = w_out.astype(BF16)
    w_gu_b = w_ffn_gu.astype(BF16)
    w_down_b = w_ffn_down.astype(BF16)

    cos_m, sin_m = _rope_tables(N, MLA_ROPE)
    cos_s, sin_s = _rope_tables(N, SWA_DIM)
    cos_id, sin_id = jnp.ones((C, 128), F32), jnp.zeros((C, 128), F32)
    state0 = jnp.zeros((B,) + GLA_STATE, F32)
    g_fin = g_final.reshape(1, D)

    xc = ctx.reshape(1, B * C, D)
    for l in range(depth):
        last = l == depth - 1
        mod_l, mod_c = mod[l, :B], mod[l, B:B + 1]
        g_mix_l, g_ffn_l = g_mix[l].reshape(1, D), g_ffn[l].reshape(1, D)
        g_q, g_kv = g_mla_q[l].reshape(1, -1), g_mla_kv[l].reshape(1, -1)
        g_out = g_gla_out[l].reshape(1, -1)
        sink = swa_sink[l].reshape(1, -1)

        proj_l = _in_projection(x, mod_l, g_mix_l, w_in_p[l])
        proj_c = _in_projection(xc, mod_c, g_mix_l, w_in_p[l]).reshape(B, C, P_WIDTH)

        q_l, k_l, v_l = _mla_prep(proj_l, cos_m, sin_m, g_q, g_kv, w_uq_p[l], w_ukv_p[l])
        q_c, k_c, v_c = _mla_prep(proj_c, cos_id, sin_id, g_q, g_kv, w_uq_p[l], w_ukv_p[l])
        mla_l = _mla_attention(q_l, k_l, v_l, k_c, v_c)
        swa_l = _swa_attention(proj_l, proj_c, cos_s, sin_s, sink)

        of_c, ob_c, s_f, s_b = _gla_scan(proj_c, wg_p[l], bg_p[l], state0, state0)
        of_l, ob_l, _, _ = _gla_scan(proj_l, wg_p[l], bg_p[l], s_f, s_b)

        x = _out_projection(x, mod_l, mla_l, swa_l, of_l, ob_l, proj_l, g_out, w_out_b[l])
        x = _ffn(x, mod_l, g_ffn_l, w_gu_b[l], w_down_b[l], g_fin, last)

        if not last:
            mla_c = _mla_ctx_attention(q_c, k_c, v_c)
            swa_c = _swa_ctx_attention(proj_c, sink)
            flat = lambda a: a.reshape(1, B * C, a.shape[-1])
            xc = _out_projection(xc, mod_c, flat(mla_c), flat(swa_c), flat(of_c), flat(ob_c), flat(proj_c),
                                 g_out, w_out_b[l])
            xc = _ffn(xc, mod_c, g_ffn_l, w_gu_b[l], w_down_b[l], g_fin, False)
    return x
```

```python
import functools

import jax
import jax.numpy as jnp
from jax import lax
from jax.experimental import pallas as pl
from jax.experimental.pallas import tpu as pltpu

F32 = jnp.float32
BF16 = jnp.bfloat16

GRID_W = 64
EPS = 1e-6
ROPE_THETA = 10000.0
LOG2E = 1.4426950408889634

MLA_HEADS = 6
MLA_RANK = 512
MLA_NOPE = 128
MLA_ROPE = 64
MLA_V = 128
MLA_QK = MLA_NOPE + MLA_ROPE
MLA_VA = MLA_V + 16

SWA_HEADS = 6
SWA_KV_HEADS = 2
SWA_GROUP = SWA_HEADS // SWA_KV_HEADS
SWA_DIM = 128
SWA_BLOCK = 128

GLA_HEADS = 4
GLA_DK = 64
GLA_DV = 128
GLA_GATE_RANK = 16
GLA_TAU = 16.0
GLA_CHUNK = 64
GLA_STATE = (GLA_HEADS // 2, GLA_DV, 2 * GLA_DK)

N_MOD = 6

P_SWA_Q = 0
P_SWA_K = 768
P_CQ = 1024
P_CKV = 1536
P_SWA_V = 2048
P_MISC = 2304
P_GLA_Q = 2560
P_GLA_K = 2816
P_GLA_V = 3072
P_GLA_R = 3584
P_WIDTH = 4096
MISC_GATE_OFF = MLA_ROPE

VMEM_LIMIT = 56 * 1024 * 1024
ROW_CHUNK = 16


def _params(n_axes, flags=None):
    return pltpu.CompilerParams(dimension_semantics=("arbitrary",) * n_axes,
                                vmem_limit_bytes=VMEM_LIMIT, flags=flags)


def _silu(x):
    return x / (1.0 + jnp.exp(-x))


def _rms(x, g):
    ms = jnp.mean(x * x, axis=-1, keepdims=True)
    return x * lax.rsqrt(ms + EPS) * g


def _dot(a, b):
    return jnp.dot(a, b, preferred_element_type=F32)


def _dot_nt(a, b):
    return lax.dot_general(a, b, (((1,), (1,)), ((), ())), preferred_element_type=F32)


def _dot_tn(a, b):
    return lax.dot_general(a, b, (((0,), (0,)), ((), ())), preferred_element_type=F32)


def _mod_kernel(c_ref, w_ref, b_ref, o_ref):
    a = _silu(c_ref[...]).astype(BF16)
    o_ref[0] = _dot(a, w_ref[0].astype(BF16)) + b_ref[0]


def _modulation(cvec, w_mod, b_mod):
    depth, d, width = w_mod.shape
    rows = cvec.shape[0]
    tn = 1024
    return pl.pallas_call(
        _mod_kernel,
        grid=(depth, width // tn),
        in_specs=[pl.BlockSpec((rows, d), lambda l, j: (0, 0)),
                  pl.BlockSpec((1, d, tn), lambda l, j: (l, 0, j)),
                  pl.BlockSpec((1, 1, tn), lambda l, j: (l, 0, j))],
        out_specs=pl.BlockSpec((1, rows, tn), lambda l, j: (l, 0, j)),
        out_shape=jax.ShapeDtypeStruct((depth, rows, width), F32),
        compiler_params=_params(2),
        name="modulation",
    )(cvec, w_mod, b_mod.reshape(depth, 1, width))


def _inproj_kernel(x_ref, mod_ref, g_ref, w_ref, o_ref):
    h = _rms(x_ref[0], g_ref[...]) * (1.0 + mod_ref[0, 1:2, :]) + mod_ref[0, 0:1, :]
    o_ref[0] = _dot(h.astype(BF16), w_ref[...])


def _in_projection(x, mod, g, w):
    bx, r, d = x.shape
    tm, tn = 512, 2048
    return pl.pallas_call(
        _inproj_kernel,
        grid=(P_WIDTH // tn, bx, r // tm),
        in_specs=[pl.BlockSpec((1, tm, d), lambda j, b, t: (b, t, 0)),
                  pl.BlockSpec((1, N_MOD, d), lambda j, b, t: (b, 0, 0)),
                  pl.BlockSpec((1, d), lambda j, b, t: (0, 0)),
                  pl.BlockSpec((d, tn), lambda j, b, t: (0, j))],
        out_specs=pl.BlockSpec((1, tm, tn), lambda j, b, t: (b, t, j)),
        out_shape=jax.ShapeDtypeStruct((bx, r, P_WIDTH), F32),
        compiler_params=_params(3),
        name="in_projection",
    )(x, mod, g, w)


def _rope(x, cos, sin, half):
    lane = lax.broadcasted_iota(jnp.int32, x.shape, 1)
    first = (lane % (2 * half)) < half
    width = x.shape[1]
    rot = jnp.where(first, pltpu.roll(x, width - half, 1), pltpu.roll(x, half, 1))
    return x * cos + rot * sin


def _mla_prep_kernel(cq_ref, ckv_ref, misc_ref, cos_ref, sin_ref, gq_ref, gkv_ref, wuq_ref, wukv_ref,
                     qt_ref, k_ref, vt_ref):
    scale = MLA_QK ** -0.5 * LOG2E
    cos, sin = cos_ref[...], sin_ref[...]
    half = MLA_ROPE // 4
    qf = _dot(_rms(cq_ref[0], gq_ref[...]).astype(BF16), wuq_ref[...])
    kvf = _dot(_rms(ckv_ref[0], gkv_ref[...]).astype(BF16), wukv_ref[...])
    kr = _rope(misc_ref[0], cos, sin, half)[:, :MLA_ROPE].astype(BF16)
    nope_w = MLA_HEADS * MLA_NOPE
    for pair in range(MLA_HEADS // 2):
        qr = _rope(qf[:, nope_w + 128 * pair: nope_w + 128 * (pair + 1)], cos, sin, half) * scale
        qrt = qr.T.astype(BF16)
        for j in range(2):
            qt_ref[0, 2 * pair + j, MLA_NOPE:MLA_QK, :] = qrt[MLA_ROPE * j: MLA_ROPE * (j + 1), :]
    for h in range(MLA_HEADS):
        qt_ref[0, h, 0:MLA_NOPE, :] = (qf[:, MLA_NOPE * h: MLA_NOPE * (h + 1)] * scale).T.astype(BF16)
        k_ref[0, h, :, 0:MLA_NOPE] = kvf[:, MLA_NOPE * h: MLA_NOPE * (h + 1)].astype(BF16)
        k_ref[0, h, :, MLA_NOPE:MLA_QK] = kr
        vt_ref[0, h, 0:MLA_V, :] = kvf[:, nope_w + MLA_V * h: nope_w + MLA_V * (h + 1)].T.astype(BF16)
        vt_ref[0, h, MLA_V:, :] = jnp.ones((MLA_VA - MLA_V, vt_ref.shape[3]), BF16)


def _mla_prep(proj, cos, sin, g_q, g_kv, w_uq, w_ukv):
    b, r, _ = proj.shape
    tm = min(r, 512)
    rk = MLA_RANK
    const = lambda shape: pl.BlockSpec(shape, lambda i, t: (0,) * len(shape))
    return pl.pallas_call(
        _mla_prep_kernel,
        grid=(b, r // tm),
        in_specs=[pl.BlockSpec((1, tm, rk), lambda i, t: (i, t, P_CQ // rk)),
                  pl.BlockSpec((1, tm, rk), lambda i, t: (i, t, P_CKV // rk)),
                  pl.BlockSpec((1, tm, 128), lambda i, t: (i, t, P_MISC // 128)),
                  pl.BlockSpec((tm, 128), lambda i, t: (t, 0)),
                  pl.BlockSpec((tm, 128), lambda i, t: (t, 0)),
                  const((1, rk)), const((1, rk)),
                  const(w_uq.shape), const(w_ukv.shape)],
        out_specs=[pl.BlockSpec((1, MLA_HEADS, MLA_QK, tm), lambda i, t: (i, 0, 0, t)),
                   pl.BlockSpec((1, MLA_HEADS, tm, MLA_QK), lambda i, t: (i, 0, t, 0)),
                   pl.BlockSpec((1, MLA_HEADS, MLA_VA, tm), lambda i, t: (i, 0, 0, t))],
        out_shape=[jax.ShapeDtypeStruct((b, MLA_HEADS, MLA_QK, r), BF16),
                   jax.ShapeDtypeStruct((b, MLA_HEADS, r, MLA_QK), BF16),
                   jax.ShapeDtypeStruct((b, MLA_HEADS, MLA_VA, r), BF16)],
        compiler_params=_params(2),
        name="mla_prep",
    )(proj, proj, proj, cos, sin, g_q, g_kv, w_uq, w_ukv)


def _softmax_pv(s_parts, vt_parts):
    m = functools.reduce(jnp.maximum, [jnp.max(s, axis=0, keepdims=True) for s in s_parts])
    acc = sum(_dot(vt, jnp.exp2(s - m).astype(BF16)) for s, vt in zip(s_parts, vt_parts))
    return (acc[:MLA_V] / acc[MLA_V:MLA_V + 1]).T


def _mla_attn_kernel(qt_ref, kl_ref, kc_ref, vlt_ref, vct_ref, o_ref, s_scr, m_scr, *, tk):
    @pl.when(pl.program_id(0) == 0)
    def _():
        s_scr[...] = jnp.zeros_like(s_scr)
        m_scr[...] = jnp.zeros_like(m_scr)

    n_lat = kl_ref.shape[2] // tk
    qt = qt_ref[0, 0]
    tq = qt.shape[1]
    m_prev = m_scr[...]
    m_new = jnp.full((1, tq), -jnp.inf, F32)
    acc = jnp.zeros((MLA_VA, tq), F32)
    for j in range(n_lat + 1):
        rows = slice(j * tk, (j + 1) * tk)
        p = jnp.exp2(s_scr[rows, :] - m_prev).astype(BF16)
        k = kc_ref[0, 0] if j == n_lat else kl_ref[0, 0, rows, :]
        vt = vct_ref[0, 0] if j == n_lat else vlt_ref[0, 0, :, rows]
        s = _dot(k, qt)
        s_scr[rows, :] = s
        m_new = jnp.maximum(m_new, jnp.max(s, axis=0, keepdims=True))
        acc = acc + _dot(vt, p)
    m_scr[...] = m_new
    o_ref[0] = (acc[:MLA_V] / acc[MLA_V:MLA_V + 1]).T.astype(o_ref.dtype)


def _mla_attention(qt, k_l, vt_l, k_c, vt_c):
    b, h, dq, n = qt.shape
    c = k_c.shape[2]
    tq = 512
    nt = n // tq
    tiles = b * h * nt

    def cur(u):
        t = jnp.minimum(u, tiles - 1)
        return t // (h * nt), (t // nt) % h, t % nt

    def prev(u):
        t = jnp.maximum(u - 1, 0)
        return t // (h * nt), (t // nt) % h, t % nt

    assert n % c == 0
    return pl.pallas_call(
        functools.partial(_mla_attn_kernel, tk=c),
        grid=(tiles + 1,),
        in_specs=[pl.BlockSpec((1, 1, dq, tq), lambda u: (cur(u)[0], cur(u)[1], 0, cur(u)[2])),
                  pl.BlockSpec((1, 1, n, dq), lambda u: (cur(u)[0], cur(u)[1], 0, 0)),
                  pl.BlockSpec((1, 1, c, dq), lambda u: (cur(u)[0], cur(u)[1], 0, 0)),
                  pl.BlockSpec((1, 1, MLA_VA, n), lambda u: (prev(u)[0], prev(u)[1], 0, 0)),
                  pl.BlockSpec((1, 1, MLA_VA, c), lambda u: (prev(u)[0], prev(u)[1], 0, 0))],
        out_specs=pl.BlockSpec((1, tq, MLA_V), lambda u: (prev(u)[0], prev(u)[2], prev(u)[1])),
        out_shape=jax.ShapeDtypeStruct((b, n, h * MLA_V), BF16),
        scratch_shapes=[pltpu.VMEM((n + c, tq), F32), pltpu.VMEM((1, tq), F32)],
        compiler_params=_params(1),
        name="mla_attention",
    )(qt, k_l, k_c, vt_l, vt_c)


def _mla_ctx_kernel(qt_ref, k_ref, vt_ref, o_ref):
    o_ref[0] = _softmax_pv([_dot(k_ref[0, 0], qt_ref[0, 0])], [vt_ref[0, 0]]).astype(o_ref.dtype)


def _mla_ctx_attention(qt, k, vt):
    b, h, dq, c = qt.shape
    return pl.pallas_call(
        _mla_ctx_kernel,
        grid=(b, h),
        in_specs=[pl.BlockSpec((1, 1, dq, c), lambda i, j: (i, j, 0, 0)),
                  pl.BlockSpec((1, 1, c, dq), lambda i, j: (i, j, 0, 0)),
                  pl.BlockSpec((1, 1, MLA_VA, c), lambda i, j: (i, j, 0, 0))],
        out_specs=pl.BlockSpec((1, c, MLA_V), lambda i, j: (i, 0, j)),
        out_shape=jax.ShapeDtypeStruct((b, c, h * MLA_V), BF16),
        compiler_params=_params(2),
        name="mla_ctx_attention",
    )(qt, k, vt)


def _swa_ctx_kernel(sink_ref, q_ref, k_ref, v_ref, o_ref):
    q = (q_ref[0] * SWA_DIM ** -0.5).astype(BF16)
    s = _dot_nt(q, k_ref[0].astype(BF16))
    sink = sink_ref[0, pl.program_id(1)]
    m = jnp.maximum(jnp.max(s, axis=-1, keepdims=True), sink)
    p = jnp.exp(s - m)
    l = jnp.sum(p, axis=-1, keepdims=True) + jnp.exp(sink - m)
    o_ref[0] = (_dot(p.astype(BF16), v_ref[0].astype(BF16)) / l).astype(o_ref.dtype)


def _swa_ctx_attention(proj_c, sink):
    b, c, _ = proj_c.shape
    d = SWA_DIM
    return pl.pallas_call(
        _swa_ctx_kernel,
        grid=(b, SWA_HEADS),
        in_specs=[pl.BlockSpec(memory_space=pltpu.SMEM),
                  pl.BlockSpec((1, c, d), lambda i, j: (i, 0, P_SWA_Q // d + j)),
                  pl.BlockSpec((1, c, d), lambda i, j: (i, 0, P_SWA_K // d + j // SWA_GROUP)),
                  pl.BlockSpec((1, c, d), lambda i, j: (i, 0, P_SWA_V // d + j // SWA_GROUP))],
        out_specs=pl.BlockSpec((1, c, d), lambda i, j: (i, 0, j)),
        out_shape=jax.ShapeDtypeStruct((b, c, SWA_HEADS * d), BF16),
        compiler_params=_params(2),
        name="swa_ctx_attention",
    )(sink, proj_c, proj_c, proj_c)


def _swa_window_start(t, qb, nb):
    return pl.multiple_of(jnp.clip(t * qb - 1, 0, nb - (qb + 2)) * SWA_BLOCK, SWA_BLOCK)


def _swa_kernel(sink_ref, q_ref, k_ref, kc_ref, v_ref, vc_ref, cos_ref, sin_ref, o_ref, s_scr, m_scr, *,
                qb, tiles, nt):
    u = pl.program_id(0)

    @pl.when(u == 0)
    def _():
        s_scr[...] = jnp.zeros_like(s_scr)
        m_scr[...] = jnp.zeros_like(m_scr)

    blk, d = SWA_BLOCK, SWA_DIM
    half = d // 4
    nq = qb * blk
    nw = (qb + 2) * blk
    nb = k_ref.shape[1] // blk
    ck = kc_ref.shape[1]
    cur, prev = jnp.minimum(u, tiles - 1), jnp.maximum(u - 1, 0)
    g_cur, t_cur = (cur // nt) % SWA_KV_HEADS, cur % nt
    g_prev, t_prev = (prev // nt) % SWA_KV_HEADS, prev % nt
    q_start = pl.multiple_of(t_cur * nq, blk)
    k_start = _swa_window_start(t_cur, qb, nb)
    v_start = _swa_window_start(t_prev, qb, nb)

    def sink_row(g):
        return jnp.concatenate(
            [jnp.full((1, nq), sink_ref[0, g * SWA_GROUP + i] * LOG2E, F32) for i in range(SWA_GROUP)], axis=1)

    def values_t(v):
        return jnp.concatenate([v.T.astype(BF16), jnp.ones((16, v.shape[0]), BF16)], axis=0)

    scale = d ** -0.5 * LOG2E
    cos_q, sin_q = cos_ref[pl.ds(q_start, nq), :], sin_ref[pl.ds(q_start, nq), :]
    q = jnp.concatenate(
        [(_rope(q_ref[0, :, d * i: d * (i + 1)], cos_q, sin_q, half) * scale).astype(BF16)
         for i in range(SWA_GROUP)], axis=0)
    key = lax.broadcasted_iota(jnp.int32, (ck, nq), 0)
    tok = lax.broadcasted_iota(jnp.int32, (ck, nq), 1)
    m_prev = m_scr[...]
    m_new = sink_row(g_cur)
    acc = jnp.zeros((d + 16, SWA_GROUP * nq), F32)
    n_win = nw // ck
    for j in range(n_win + 1):
        rows = slice(j * ck, (j + 1) * ck)
        p = jnp.exp2(s_scr[rows, :] - m_prev).astype(BF16)
        if j < n_win:
            vt = values_t(v_ref[0, pl.ds(v_start + j * ck, ck), :])
            kr = pl.ds(k_start + j * ck, ck)
            k = _rope(k_ref[0, kr, :], cos_ref[kr, :], sin_ref[kr, :], half).astype(BF16)
            dist = key - tok + (k_start + j * ck - q_start)
            valid = (dist <= blk) & (dist >= -blk)
            s = _dot_nt(k, q)
            s = jnp.concatenate(
                [jnp.where(valid, s[:, nq * i: nq * (i + 1)], -jnp.inf) for i in range(SWA_GROUP)], axis=1)
        else:
            vt = values_t(vc_ref[0])
            s = _dot_nt(kc_ref[0].astype(BF16), q)
        s_scr[rows, :] = s
        m_new = jnp.maximum(m_new, jnp.max(s, axis=0, keepdims=True))
        acc = acc + _dot(vt, p)
    m_scr[...] = m_new
    out_t = acc[:d] / (acc[d:d + 1] + jnp.exp2(sink_row(g_prev) - m_prev))
    for i in range(SWA_GROUP):
        o_ref[0, :, d * i: d * (i + 1)] = out_t[:, nq * i: nq * (i + 1)].T.astype(o_ref.dtype)


def _swa_attention(proj_l, proj_c, cos, sin, sink):
    b, n, _ = proj_l.shape
    c = proj_c.shape[1]
    blk, d = SWA_BLOCK, SWA_DIM
    qb = 2
    nq = qb * blk
    nt = n // nq
    tiles = b * SWA_KV_HEADS * nt
    assert ((qb + 2) * blk) % c == 0 and n // blk >= qb + 2

    def cur(u):
        t = jnp.minimum(u, tiles - 1)
        return t // (SWA_KV_HEADS * nt), (t // nt) % SWA_KV_HEADS, t % nt

    def prev(u):
        t = jnp.maximum(u - 1, 0)
        return t // (SWA_KV_HEADS * nt), (t // nt) % SWA_KV_HEADS, t % nt

    kcol, vcol = P_SWA_K // d, P_SWA_V // d
    return pl.pallas_call(
        functools.partial(_swa_kernel, qb=qb, tiles=tiles, nt=nt),
        grid=(tiles + 1,),
        in_specs=[pl.BlockSpec(memory_space=pltpu.SMEM),
                  pl.BlockSpec((1, nq, SWA_GROUP * d), lambda u: (cur(u)[0], cur(u)[2], cur(u)[1])),
                  pl.BlockSpec((1, n, d), lambda u: (cur(u)[0], 0, kcol + cur(u)[1])),
                  pl.BlockSpec((1, c, d), lambda u: (cur(u)[0], 0, kcol + cur(u)[1])),
                  pl.BlockSpec((1, n, d), lambda u: (prev(u)[0], 0, vcol + prev(u)[1])),
                  pl.BlockSpec((1, c, d), lambda u: (prev(u)[0], 0, vcol + prev(u)[1])),
                  pl.BlockSpec((n, d), lambda u: (0, 0)),
                  pl.BlockSpec((n, d), lambda u: (0, 0))],
        out_specs=pl.BlockSpec((1, nq, SWA_GROUP * d), lambda u: (prev(u)[0], prev(u)[2], prev(u)[1])),
        out_shape=jax.ShapeDtypeStruct((b, n, SWA_HEADS * d), BF16),
        scratch_shapes=[pltpu.VMEM(((qb + 2) * blk + c, SWA_GROUP * nq), F32),
                        pltpu.VMEM((1, SWA_GROUP * nq), F32)],
        compiler_params=_params(1),
        name="swa_attention",
    )(sink, proj_l, proj_l, proj_c, proj_l, proj_c, cos, sin)


def _log_sigmoid(x):
    return jnp.minimum(x, 0.0) - jnp.log(1.0 + jnp.exp(-jnp.abs(x)))


def _gla_factors(q_ref, k_ref, misc_ref, wg, bg, reverse):
    rows = q_ref.shape[1]
    L = GLA_CHUNK
    assert rows == 4 * L
    width = GLA_HEADS * GLA_DK
    la = _log_sigmoid(_dot(misc_ref[0].astype(BF16), wg) + bg) * (1.0 / GLA_TAU)
    hi = la.astype(BF16)
    lo = (la - hi.astype(F32)).astype(BF16)
    ti = lax.broadcasted_iota(jnp.int32, (rows, rows), 0)
    tj = lax.broadcasted_iota(jnp.int32, (rows, rows), 1)
    ci, cj = ti // L, tj // L
    upto = (tj >= ti) if reverse else (tj <= ti)
    ones_in = jnp.where((ci == cj) & upto, 1.0, 0.0).astype(BF16)
    ones_abs = jnp.where(upto, 1.0, 0.0).astype(BF16)
    sums = _dot(jnp.concatenate([ones_in, ones_abs], axis=0), jnp.concatenate([hi, lo], axis=1))
    sums = sums[:, :width] + sums[:, width:]
    b_in, b_abs = sums[:rows], sums[rows:]
    if reverse:
        g_mid, g_end = b_abs[rows // 2:rows // 2 + 1], b_abs[0:1]
        diag, cross = (ci == cj) & (tj > ti), cj > ci
    else:
        g_mid, g_end = b_abs[rows // 2 - 1:rows // 2], b_abs[rows - 1:rows]
        diag, cross = (ci == cj) & (tj <= ti), cj < ci
    q = q_ref[0] * GLA_DK ** -0.5
    k = k_ref[0]
    q_d, k_d = q * jnp.exp(b_in), k * jnp.exp(-b_in)
    q_x, k_x = q * jnp.exp(b_abs - g_mid), k * jnp.exp(g_mid - b_abs)
    q_s, k_s = q * jnp.exp(b_abs), k * jnp.exp(g_end - b_abs)
    decay = jnp.exp(g_end)
    return q_d, k_d, q_x, k_x, q_s, k_s, decay, diag, cross


def _gla_apply(factors, v_ref, state_ref, o_ref):
    q_d, k_d, q_x, k_x, q_s, k_s, decay, diag, cross = factors
    rows = q_d.shape[0]
    lane = lax.broadcasted_iota(jnp.int32, (rows, 2 * GLA_DK), 1)
    lane_s = lax.broadcasted_iota(jnp.int32, (GLA_DV, 2 * GLA_DK), 1)
    for pair in range(GLA_HEADS // 2):
        ps = slice(2 * GLA_DK * pair, 2 * GLA_DK * (pair + 1))
        kd_p, kx_p, ks_p = k_d[:, ps].astype(BF16), k_x[:, ps].astype(BF16), k_s[:, ps].astype(BF16)
        st = state_ref[pair]
        st_b = st.astype(BF16)
        ds = []
        for j in range(2):
            h = 2 * pair + j
            vs = slice(h * GLA_DV, (h + 1) * GLA_DV)
            mine = (lane // GLA_DK) == j
            qd_h = jnp.where(mine, q_d[:, ps], 0.0).astype(BF16)
            qx_h = jnp.where(mine, q_x[:, ps], 0.0).astype(BF16)
            qs_h = jnp.where(mine, q_s[:, ps], 0.0).astype(BF16)
            vh = v_ref[0, :, vs].astype(BF16)
            a = jnp.where(diag, _dot_nt(qd_h, kd_p), jnp.where(cross, _dot_nt(qx_h, kx_p), 0.0))
            o_ref[0, :, vs] = _dot(a.astype(BF16), vh) + _dot_nt(qs_h, st_b)
            ds.append(_dot_tn(vh, ks_p))
        state_ref[pair] = decay[:, ps] * st + jnp.where(lane_s < GLA_DK, ds[0], ds[1])


def _gla_kernel(qf_ref, kf_ref, vf_ref, mf_ref, qb_ref, kb_ref, vb_ref, mb_ref, wg_ref, bg_ref,
                sf0_ref, sb0_ref, of_ref, ob_ref, sf_ref, sb_ref, stf, stb):
    t = pl.program_id(1)

    @pl.when(t == 0)
    def _():
        stf[...] = sf0_ref[0]
        stb[...] = sb0_ref[0]

    width = GLA_HEADS * GLA_DK
    fwd = _gla_factors(qf_ref, kf_ref, mf_ref, wg_ref[:, :width], bg_ref[:, :width], False)
    bwd = _gla_factors(qb_ref, kb_ref, mb_ref, wg_ref[:, width:], bg_ref[:, width:], True)
    _gla_apply(fwd, vf_ref, stf, of_ref)
    _gla_apply(bwd, vb_ref, stb, ob_ref)

    @pl.when(t == pl.num_programs(1) - 1)
    def _():
        sf_ref[0] = stf[...]
        sb_ref[0] = stb[...]


def _gla_scan(proj, wg, bg, sf0, sb0):
    b, r, _ = proj.shape
    tm = 256
    nt = r // tm
    kw, vw = GLA_HEADS * GLA_DK, GLA_HEADS * GLA_DV
    fwd = lambda t: t
    bwd = lambda t: nt - 1 - t

    def specs(order):
        return [pl.BlockSpec((1, tm, kw), lambda i, t: (i, order(t), P_GLA_Q // kw)),
                pl.BlockSpec((1, tm, kw), lambda i, t: (i, order(t), P_GLA_K // kw)),
                pl.BlockSpec((1, tm, vw), lambda i, t: (i, order(t), P_GLA_V // vw)),
                pl.BlockSpec((1, tm, 128), lambda i, t: (i, order(t), P_MISC // 128))]

    state_spec = pl.BlockSpec((1,) + GLA_STATE, lambda i, t: (i, 0, 0, 0))
    state_shape = jax.ShapeDtypeStruct((b,) + GLA_STATE, F32)
    return pl.pallas_call(
        _gla_kernel,
        grid=(b, nt),
        in_specs=specs(fwd) + specs(bwd) + [
            pl.BlockSpec(wg.shape, lambda i, t: (0, 0)),
            pl.BlockSpec(bg.shape, lambda i, t: (0, 0)),
            state_spec, state_spec],
        out_specs=[pl.BlockSpec((1, tm, vw), lambda i, t: (i, fwd(t), 0)),
                   pl.BlockSpec((1, tm, vw), lambda i, t: (i, bwd(t), 0)),
                   state_spec, state_spec],
        out_shape=[jax.ShapeDtypeStruct((b, r, vw), F32), jax.ShapeDtypeStruct((b, r, vw), F32),
                   state_shape, state_shape],
        scratch_shapes=[pltpu.VMEM(GLA_STATE, F32), pltpu.VMEM(GLA_STATE, F32)],
        compiler_params=_params(2),
        name="gla_scan",
    )(proj, proj, proj, proj, proj, proj, proj, proj, wg, bg, sf0, sb0)


def _outproj_kernel(x_ref, mod_ref, mla_ref, swa_ref, of_ref, ob_ref, r_ref, g_ref, w_ref, o_ref):
    o = of_ref[0] + ob_ref[0]
    gate = _silu(r_ref[0])
    parts = []
    for h in range(GLA_HEADS):
        vs = slice(h * GLA_DV, (h + 1) * GLA_DV)
        parts.append((_rms(o[:, vs], g_ref[:, vs]) * gate[:, vs]).astype(BF16))
    gla = jnp.concatenate(parts, axis=1)
    a, b = mla_ref.shape[2], mla_ref.shape[2] + swa_ref.shape[2]
    mix = _dot(mla_ref[0], w_ref[0:a, :]) + _dot(swa_ref[0], w_ref[a:b, :]) + _dot(gla, w_ref[b:, :])
    o_ref[0] = x_ref[0] + mod_ref[0, 2:3, :] * mix


def _out_projection(x, mod, mla, swa, o_f, o_b, proj, g_out, w_out):
    bx, r, d = x.shape
    tm = 512
    row = lambda width: pl.BlockSpec((1, tm, width), lambda b, t: (b, t, 0))
    gw = GLA_HEADS * GLA_DV
    return pl.pallas_call(
        _outproj_kernel,
        grid=(bx, r // tm),
        in_specs=[row(d),
                  pl.BlockSpec((1, N_MOD, d), lambda b, t: (b, 0, 0)),
                  row(mla.shape[2]), row(swa.shape[2]), row(gw), row(gw),
                  pl.BlockSpec((1, tm, gw), lambda b, t: (b, t, P_GLA_R // gw)),
                  pl.BlockSpec((1, gw), lambda b, t: (0, 0)),
                  pl.BlockSpec(w_out.shape, lambda b, t: (0, 0))],
        out_specs=row(d),
        out_shape=jax.ShapeDtypeStruct(x.shape, F32),
        compiler_params=_params(2),
        name="out_projection",
    )(x, mod, mla, swa, o_f, o_b, proj, g_out, w_out)


def _ffn_kernel(x_ref, mod_ref, g_ref, wg_ref, wu_ref, wd_ref, gf_ref, o_ref, h_ref, acc_ref, *, final_norm):
    f = pl.program_id(2)
    row_chunks = [slice(r, r + ROW_CHUNK) for r in range(0, x_ref.shape[1], ROW_CHUNK)]

    @pl.when(f == 0)
    def _():
        for rows in row_chunks:
            h = _rms(x_ref[0, rows, :], g_ref[...]) * (1.0 + mod_ref[0, 4:5, :]) + mod_ref[0, 3:4, :]
            h_ref[rows, :] = h.astype(BF16)
        acc_ref[...] = jnp.zeros_like(acc_ref)

    h = h_ref[...]
    act = _silu(_dot(h, wg_ref[...])) * _dot(h, wu_ref[...])
    acc_ref[...] += _dot(act.astype(BF16), wd_ref[...])

    @pl.when(f == pl.num_programs(2) - 1)
    def _():
        for rows in row_chunks:
            y = x_ref[0, rows, :] + mod_ref[0, 5:6, :] * acc_ref[rows, :]
            if final_norm:
                y = _rms(y, gf_ref[...])
            o_ref[0, rows, :] = y


def _ffn(x, mod, g, w_gu, w_down, g_final, final_norm):
    bx, r, d = x.shape
    hidden = w_down.shape[0]
    tm, tf = 512, 512
    nf = hidden // tf
    return pl.pallas_call(
        functools.partial(_ffn_kernel, final_norm=final_norm),
        grid=(bx, r // tm, nf),
        in_specs=[pl.BlockSpec((1, tm, d), lambda b, t, f: (b, t, 0)),
                  pl.BlockSpec((1, N_MOD, d), lambda b, t, f: (b, 0, 0)),
                  pl.BlockSpec((1, d), lambda b, t, f: (0, 0)),
                  pl.BlockSpec((d, tf), lambda b, t, f: (0, f)),
                  pl.BlockSpec((d, tf), lambda b, t, f: (0, nf + f)),
                  pl.BlockSpec((tf, d), lambda b, t, f: (f, 0)),
                  pl.BlockSpec((1, d), lambda b, t, f: (0, 0))],
        out_specs=pl.BlockSpec((1, tm, d), lambda b, t, f: (b, t, 0)),
        out_shape=jax.ShapeDtypeStruct(x.shape, F32),
        scratch_shapes=[pltpu.VMEM((tm, d), BF16), pltpu.VMEM((tm, d), F32)],
        compiler_params=_params(3),
        name="ffn",
    )(x, mod, g, w_gu, w_gu, w_down, g_final)


def _rope_tables(n, dim):
    half = dim // 4
    freqs = ROPE_THETA ** (-jnp.arange(half, dtype=F32) / half)
    pos = jnp.arange(n, dtype=jnp.int32)
    ang_r = (pos // GRID_W).astype(F32)[:, None] * freqs
    ang_c = (pos % GRID_W).astype(F32)[:, None] * freqs
    cos = jnp.concatenate([jnp.cos(ang_r)] * 2 + [jnp.cos(ang_c)] * 2, axis=1)
    sin = jnp.concatenate([-jnp.sin(ang_r), jnp.sin(ang_r), -jnp.sin(ang_c), jnp.sin(ang_c)], axis=1)
    reps = 128 // dim
    return jnp.tile(cos, (1, reps)), jnp.tile(sin, (1, reps))


def _relayout_w_in(w_in):
    depth, d, _ = w_in.shape
    sizes = (MLA_RANK, MLA_RANK, MLA_ROPE, SWA_HEADS * SWA_DIM, SWA_KV_HEADS * SWA_DIM, SWA_KV_HEADS * SWA_DIM,
             GLA_HEADS * GLA_DK, GLA_HEADS * GLA_DK, GLA_HEADS * GLA_DV, 2 * GLA_GATE_RANK, GLA_HEADS * GLA_DV)
    offs = [0]
    for s in sizes:
        offs.append(offs[-1] + s)
    part = lambda i: w_in[:, :, offs[i]:offs[i + 1]]
    cq, ckv, kr, sq, sk, sv, gq, gk, gv, glr, gr = (part(i) for i in range(len(sizes)))
    zeros = lambda w: jnp.zeros((depth, d, w), w_in.dtype)
    misc_pad = 128 - MLA_ROPE - 2 * GLA_GATE_RANK
    out = jnp.concatenate([sq, sk, cq, ckv, sv, kr, glr, zeros(misc_pad), zeros(P_GLA_Q - P_MISC - 128),
                           gq, gk, gv, gr], axis=-1)
    assert out.shape[-1] == P_WIDTH
    return out.astype(BF16)


def _relayout_mla(w_uq, w_ukv):
    depth, rk, _ = w_uq.shape
    uq = w_uq.reshape(depth, rk, MLA_HEADS, MLA_QK)
    uq = jnp.concatenate([uq[..., :MLA_NOPE].reshape(depth, rk, -1), uq[..., MLA_NOPE:].reshape(depth, rk, -1)], -1)
    ukv = w_ukv.reshape(depth, rk, MLA_HEADS, MLA_NOPE + MLA_V)
    ukv = jnp.concatenate([ukv[..., :MLA_NOPE].reshape(depth, rk, -1), ukv[..., MLA_NOPE:].reshape(depth, rk, -1)], -1)
    return uq.astype(BF16), ukv.astype(BF16)


def _relayout_gate(w_f, b_f, w_b, b_b):
    depth, rank, width = w_f.shape
    wg = jnp.zeros((depth, 128, 2 * width), F32)
    wg = wg.at[:, MISC_GATE_OFF:MISC_GATE_OFF + rank, :width].set(w_f)
    wg = wg.at[:, MISC_GATE_OFF + rank:MISC_GATE_OFF + 2 * rank, width:].set(w_b)
    bg = jnp.concatenate([b_f, b_b], axis=-1).reshape(depth, 1, 2 * width)
    return wg.astype(BF16), bg


def kernel(x, c, ctx, c_ctx, w_mod, b_mod, g_mix, g_ffn, w_in, g_mla_q, g_mla_kv, w_mla_uq, w_mla_ukv,
           swa_sink, w_gla_gate_f, b_gla_gate_f, w_gla_gate_b, b_gla_gate_b, g_gla_out, w_out, w_ffn_gu,
           w_ffn_down, g_final):
    B, N, D = x.shape
    C = ctx.shape[1]
    depth = w_mod.shape[0]

    cvec = jnp.concatenate([c, c_ctx[None, :], jnp.zeros((16 - B - 1, D), F32)], axis=0)
    mod = _modulation(cvec, w_mod, b_mod).reshape(depth, 16, N_MOD, D)

    w_in_p = _relayout_w_in(w_in)
    w_uq_p, w_ukv_p = _relayout_mla(w_mla_uq, w_mla_ukv)
    wg_p, bg_p = _relayout_gate(w_gla_gate_f, b_gla_gate_f, w_gla_gate_b, b_gla_gate_b)
    w_out_b = w_out.astype(BF16)
    w_gu_b = w_ffn_gu.astype(BF16)
    w_down_b = w_ffn_down.astype(BF16)

    cos_m, sin_m = _rope_tables(N, MLA_ROPE)
    cos_s, sin_s = _rope_tables(N, SWA_DIM)
    cos_id, sin_id = jnp.ones((C, 128), F32), jnp.zeros((C, 128), F32)
    state0 = jnp.zeros((B,) + GLA_STATE, F32)
    g_fin = g_final.reshape(1, D)

    xc = ctx.reshape(1, B * C, D)
    for l in range(depth):
        last = l == depth - 1
        mod_l, mod_c = mod[l, :B], mod[l, B:B + 1]
        g_mix_l, g_ffn_l = g_mix[l].reshape(1, D), g_ffn[l].reshape(1, D)
        g_q, g_kv = g_mla_q[l].reshape(1, -1), g_mla_kv[l].reshape(1, -1)
        g_out = g_gla_out[l].reshape(1, -1)
        sink = swa_sink[l].reshape(1, -1)

        proj_l = _in_projection(x, mod_l, g_mix_l, w_in_p[l])
        proj_c = _in_projection(xc, mod_c, g_mix_l, w_in_p[l]).reshape(B, C, P_WIDTH)

        q_l, k_l, v_l = _mla_prep(proj_l, cos_m, sin_m, g_q, g_kv, w_uq_p[l], w_ukv_p[l])
        q_c, k_c, v_c = _mla_prep(proj_c, cos_id, sin_id, g_q, g_kv, w_uq_p[l], w_ukv_p[l])
        mla_l = _mla_attention(q_l, k_l, v_l, k_c, v_c)
        swa_l = _swa_attention(proj_l, proj_c, cos_s, sin_s, sink)

        of_c, ob_c, s_f, s_b = _gla_scan(proj_c, wg_p[l], bg_p[l], state0, state0)
        of_l, ob_l, _, _ = _gla_scan(proj_l, wg_p[l], bg_p[l], s_f, s_b)

        x = _out_projection(x, mod_l, mla_l, swa_l, of_l, ob_l, proj_l, g_out, w_out_b[l])
        x = _ffn(x, mod_l, g_ffn_l, w_gu_b[l], w_down_b[l], g_fin, last)

        if not last:
            mla_c = _mla_ctx_attention(q_c, k_c, v_c)
            swa_c = _swa_ctx_attention(proj_c, sink)
            flat = lambda a: a.reshape(1, B * C, a.shape[-1])
            xc = _out_projection(xc, mod_c, flat(mla_c), flat(swa_c), flat(of_c), flat(ob_c), flat(proj_c),
                                 g_out, w_out_b[l])
            xc = _ffn(xc, mod_c, g_ffn_l, w_gu_b[l], w_down_b[l], g_fin, False)
    return x
```

```python
import functools

import jax
import jax.numpy as jnp
from jax import lax
from jax.experimental import pallas as pl
from jax.experimental.pallas import tpu as pltpu

F32 = jnp.float32
BF16 = jnp.bfloat16

GRID_W = 64
EPS = 1e-6
ROPE_THETA = 10000.0
LOG2E = 1.4426950408889634

MLA_HEADS = 6
MLA_RANK = 512
MLA_NOPE = 128
MLA_ROPE = 64
MLA_V = 128
MLA_QK = MLA_NOPE + MLA_ROPE
MLA_VA = MLA_V + 16

SWA_HEADS = 6
SWA_KV_HEADS = 2
SWA_GROUP = SWA_HEADS // SWA_KV_HEADS
SWA_DIM = 128
SWA_BLOCK = 128

GLA_HEADS = 4
GLA_DK = 64
GLA_DV = 128
GLA_GATE_RANK = 16
GLA_TAU = 16.0
GLA_CHUNK = 64
GLA_STATE = (GLA_HEADS // 2, GLA_DV, 2 * GLA_DK)

N_MOD = 6

P_SWA_Q = 0
P_SWA_K = 768
P_CQ = 1024
P_CKV = 1536
P_SWA_V = 2048
P_MISC = 2304
P_GLA_Q = 2560
P_GLA_K = 2816
P_GLA_V = 3072
P_GLA_R = 3584
P_WIDTH = 4096
MISC_GATE_OFF = MLA_ROPE

VMEM_LIMIT = 56 * 1024 * 1024
ROW_CHUNK = 16


def _params(n_axes, flags=None):
    return pltpu.CompilerParams(dimension_semantics=("arbitrary",) * n_axes,
                                vmem_limit_bytes=VMEM_LIMIT, flags=flags)


def _silu(x):
    return x / (1.0 + jnp.exp(-x))


def _rms(x, g):
    ms = jnp.mean(x * x, axis=-1, keepdims=True)
    return x * lax.rsqrt(ms + EPS) * g


def _dot(a, b):
    return jnp.dot(a, b, preferred_element_type=F32)


def _dot_nt(a, b):
    return lax.dot_general(a, b, (((1,), (1,)), ((), ())), preferred_element_type=F32)


def _dot_tn(a, b):
    return lax.dot_general(a, b, (((0,), (0,)), ((), ())), preferred_element_type=F32)


def _mod_kernel(c_ref, w_ref, b_ref, o_ref):
    a = _silu(c_ref[...]).astype(BF16)
    o_ref[0] = _dot(a, w_ref[0].astype(BF16)) + b_ref[0]


def _modulation(cvec, w_mod, b_mod):
    depth, d, width = w_mod.shape
    rows = cvec.shape[0]
    tn = 1024
    return pl.pallas_call(
        _mod_kernel,
        grid=(depth, width // tn),
        in_specs=[pl.BlockSpec((rows, d), lambda l, j: (0, 0)),
                  pl.BlockSpec((1, d, tn), lambda l, j: (l, 0, j)),
                  pl.BlockSpec((1, 1, tn), lambda l, j: (l, 0, j))],
        out_specs=pl.BlockSpec((1, rows, tn), lambda l, j: (l, 0, j)),
        out_shape=jax.ShapeDtypeStruct((depth, rows, width), F32),
        compiler_params=_params(2),
        name="modulation",
    )(cvec, w_mod, b_mod.reshape(depth, 1, width))


def _inproj_kernel(x_ref, mod_ref, g_ref, w_ref, o_ref):
    h = _rms(x_ref[0], g_ref[...]) * (1.0 + mod_ref[0, 1:2, :]) + mod_ref[0, 0:1, :]
    o_ref[0] = _dot(h.astype(BF16), w_ref[...])


def _in_projection(x, mod, g, w):
    bx, r, d = x.shape
    tm, tn = 512, 2048
    return pl.pallas_call(
        _inproj_kernel,
        grid=(P_WIDTH // tn, bx, r // tm),
        in_specs=[pl.BlockSpec((1, tm, d), lambda j, b, t: (b, t, 0)),
                  pl.BlockSpec((1, N_MOD, d), lambda j, b, t: (b, 0, 0)),
                  pl.BlockSpec((1, d), lambda j, b, t: (0, 0)),
                  pl.BlockSpec((d, tn), lambda j, b, t: (0, j))],
        out_specs=pl.BlockSpec((1, tm, tn), lambda j, b, t: (b, t, j)),
        out_shape=jax.ShapeDtypeStruct((bx, r, P_WIDTH), F32),
        compiler_params=_params(3),
        name="in_projection",
    )(x, mod, g, w)


def _rope(x, cos, sin, half):
    lane = lax.broadcasted_iota(jnp.int32, x.shape, 1)
    first = (lane % (2 * half)) < half
    width = x.shape[1]
    rot = jnp.where(first, pltpu.roll(x, width - half, 1), pltpu.roll(x, half, 1))
    return x * cos + rot * sin


def _mla_prep_kernel(cq_ref, ckv_ref, misc_ref, cos_ref, sin_ref, gq_ref, gkv_ref, wuq_ref, wukv_ref,
                     qt_ref, k_ref, vt_ref):
    scale = MLA_QK ** -0.5 * LOG2E
    cos, sin = cos_ref[...], sin_ref[...]
    half = MLA_ROPE // 4
    qf = _dot(_rms(cq_ref[0], gq_ref[...]).astype(BF16), wuq_ref[...])
    kvf = _dot(_rms(ckv_ref[0], gkv_ref[...]).astype(BF16), wukv_ref[...])
    kr = _rope(misc_ref[0], cos, sin, half)[:, :MLA_ROPE].astype(BF16)
    nope_w = MLA_HEADS * MLA_NOPE
    for pair in range(MLA_HEADS // 2):
        qr = _rope(qf[:, nope_w + 128 * pair: nope_w + 128 * (pair + 1)], cos, sin, half) * scale
        qrt = qr.T.astype(BF16)
        for j in range(2):
            qt_ref[0, 2 * pair + j, MLA_NOPE:MLA_QK, :] = qrt[MLA_ROPE * j: MLA_ROPE * (j + 1), :]
    for h in range(MLA_HEADS):
        qt_ref[0, h, 0:MLA_NOPE, :] = (qf[:, MLA_NOPE * h: MLA_NOPE * (h + 1)] * scale).T.astype(BF16)
        k_ref[0, h, :, 0:MLA_NOPE] = kvf[:, MLA_NOPE * h: MLA_NOPE * (h + 1)].astype(BF16)
        k_ref[0, h, :, MLA_NOPE:MLA_QK] = kr
        vt_ref[0, h, 0:MLA_V, :] = kvf[:, nope_w + MLA_V * h: nope_w + MLA_V * (h + 1)].T.astype(BF16)
        vt_ref[0, h, MLA_V:, :] = jnp.ones((MLA_VA - MLA_V, vt_ref.shape[3]), BF16)


def _mla_prep(proj, cos, sin, g_q, g_kv, w_uq, w_ukv):
    b, r, _ = proj.shape
    tm = min(r, 512)
    rk = MLA_RANK
    const = lambda shape: pl.BlockSpec(shape, lambda i, t: (0,) * len(shape))
    return pl.pallas_call(
        _mla_prep_kernel,
        grid=(b, r // tm),
        in_specs=[pl.BlockSpec((1, tm, rk), lambda i, t: (i, t, P_CQ // rk)),
                  pl.BlockSpec((1, tm, rk), lambda i, t: (i, t, P_CKV // rk)),
                  pl.BlockSpec((1, tm, 128), lambda i, t: (i, t, P_MISC // 128)),
                  pl.BlockSpec((tm, 128), lambda i, t: (t, 0)),
                  pl.BlockSpec((tm, 128), lambda i, t: (t, 0)),
                  const((1, rk)), const((1, rk)),
                  const(w_uq.shape), const(w_ukv.shape)],
        out_specs=[pl.BlockSpec((1, MLA_HEADS, MLA_QK, tm), lambda i, t: (i, 0, 0, t)),
                   pl.BlockSpec((1, MLA_HEADS, tm, MLA_QK), lambda i, t: (i, 0, t, 0)),
                   pl.BlockSpec((1, MLA_HEADS, MLA_VA, tm), lambda i, t: (i, 0, 0, t))],
        out_shape=[jax.ShapeDtypeStruct((b, MLA_HEADS, MLA_QK, r), BF16),
                   jax.ShapeDtypeStruct((b, MLA_HEADS, r, MLA_QK), BF16),
                   jax.ShapeDtypeStruct((b, MLA_HEADS, MLA_VA, r), BF16)],
        compiler_params=_params(2),
        name="mla_prep",
    )(proj, proj, proj, cos, sin, g_q, g_kv, w_uq, w_ukv)


def _softmax_pv(s_parts, vt_parts):
    m = functools.reduce(jnp.maximum, [jnp.max(s, axis=0, keepdims=True) for s in s_parts])
    acc = sum(_dot(vt, jnp.exp2(s - m).astype(BF16)) for s, vt in zip(s_parts, vt_parts))
    return (acc[:MLA_V] / acc[MLA_V:MLA_V + 1]).T


def _mla_attn_kernel(qt_ref, kl_ref, kc_ref, vlt_ref, vct_ref, o_ref, s_scr, m_scr, *, tk):
    @pl.when(pl.program_id(0) == 0)
    def _():
        s_scr[...] = jnp.zeros_like(s_scr)
        m_scr[...] = jnp.zeros_like(m_scr)

    n_lat = kl_ref.shape[2] // tk
    qt = qt_ref[0, 0]
    tq = qt.shape[1]
    m_prev = m_scr[...]
    m_new = jnp.full((1, tq), -jnp.inf, F32)
    acc = jnp.zeros((MLA_VA, tq), F32)
    for j in range(n_lat + 1):
        rows = slice(j * tk, (j + 1) * tk)
        p = jnp.exp2(s_scr[rows, :] - m_prev).astype(BF16)
        k = kc_ref[0, 0] if j == n_lat else kl_ref[0, 0, rows, :]
        vt = vct_ref[0, 0] if j == n_lat else vlt_ref[0, 0, :, rows]
        s = _dot(k, qt)
        s_scr[rows, :] = s
        m_new = jnp.maximum(m_new, jnp.max(s, axis=0, keepdims=True))
        acc = acc + _dot(vt, p)
    m_scr[...] = m_new
    o_ref[0] = (acc[:MLA_V] / acc[MLA_V:MLA_V + 1]).T.astype(o_ref.dtype)


def _mla_attention(qt, k_l, vt_l, k_c, vt_c):
    b, h, dq, n = qt.shape
    c = k_c.shape[2]
    tq = min(n, 1024)
    nt = n // tq
    tiles = b * h * nt

    def cur(u):
        t = jnp.minimum(u, tiles - 1)
        return t // (h * nt), (t // nt) % h, t % nt

    def prev(u):
        t = jnp.maximum(u - 1, 0)
        return t // (h * nt), (t // nt) % h, t % nt

    assert n % c == 0
    return pl.pallas_call(
        functools.partial(_mla_attn_kernel, tk=c),
        grid=(tiles + 1,),
        in_specs=[pl.BlockSpec((1, 1, dq, tq), lambda u: (cur(u)[0], cur(u)[1], 0, cur(u)[2])),
                  pl.BlockSpec((1, 1, n, dq), lambda u: (cur(u)[0], cur(u)[1], 0, 0)),
                  pl.BlockSpec((1, 1, c, dq), lambda u: (cur(u)[0], cur(u)[1], 0, 0)),
                  pl.BlockSpec((1, 1, MLA_VA, n), lambda u: (prev(u)[0], prev(u)[1], 0, 0)),
                  pl.BlockSpec((1, 1, MLA_VA, c), lambda u: (prev(u)[0], prev(u)[1], 0, 0))],
        out_specs=pl.BlockSpec((1, tq, MLA_V), lambda u: (prev(u)[0], prev(u)[2], prev(u)[1])),
        out_shape=jax.ShapeDtypeStruct((b, n, h * MLA_V), BF16),
        scratch_shapes=[pltpu.VMEM((n + c, tq), F32), pltpu.VMEM((1, tq), F32)],
        compiler_params=_params(1),
        name="mla_attention",
    )(qt, k_l, k_c, vt_l, vt_c)


def _mla_ctx_kernel(qt_ref, k_ref, vt_ref, o_ref):
    o_ref[0] = _softmax_pv([_dot(k_ref[0, 0], qt_ref[0, 0])], [vt_ref[0, 0]]).astype(o_ref.dtype)


def _mla_ctx_attention(qt, k, vt):
    b, h, dq, c = qt.shape
    return pl.pallas_call(
        _mla_ctx_kernel,
        grid=(b, h),
        in_specs=[pl.BlockSpec((1, 1, dq, c), lambda i, j: (i, j, 0, 0)),
                  pl.BlockSpec((1, 1, c, dq), lambda i, j: (i, j, 0, 0)),
                  pl.BlockSpec((1, 1, MLA_VA, c), lambda i, j: (i, j, 0, 0))],
        out_specs=pl.BlockSpec((1, c, MLA_V), lambda i, j: (i, 0, j)),
        out_shape=jax.ShapeDtypeStruct((b, c, h * MLA_V), BF16),
        compiler_params=_params(2),
        name="mla_ctx_attention",
    )(qt, k, vt)


def _swa_ctx_kernel(sink_ref, q_ref, k_ref, v_ref, o_ref):
    q = (q_ref[0] * SWA_DIM ** -0.5).astype(BF16)
    s = _dot_nt(q, k_ref[0].astype(BF16))
    sink = sink_ref[0, pl.program_id(1)]
    m = jnp.maximum(jnp.max(s, axis=-1, keepdims=True), sink)
    p = jnp.exp(s - m)
    l = jnp.sum(p, axis=-1, keepdims=True) + jnp.exp(sink - m)
    o_ref[0] = (_dot(p.astype(BF16), v_ref[0].astype(BF16)) / l).astype(o_ref.dtype)


def _swa_ctx_attention(proj_c, sink):
    b, c, _ = proj_c.shape
    d = SWA_DIM
    return pl.pallas_call(
        _swa_ctx_kernel,
        grid=(b, SWA_HEADS),
        in_specs=[pl.BlockSpec(memory_space=pltpu.SMEM),
                  pl.BlockSpec((1, c, d), lambda i, j: (i, 0, P_SWA_Q // d + j)),
                  pl.BlockSpec((1, c, d), lambda i, j: (i, 0, P_SWA_K // d + j // SWA_GROUP)),
                  pl.BlockSpec((1, c, d), lambda i, j: (i, 0, P_SWA_V // d + j // SWA_GROUP))],
        out_specs=pl.BlockSpec((1, c, d), lambda i, j: (i, 0, j)),
        out_shape=jax.ShapeDtypeStruct((b, c, SWA_HEADS * d), BF16),
        compiler_params=_params(2),
        name="swa_ctx_attention",
    )(sink, proj_c, proj_c, proj_c)


def _swa_window_start(t, qb, nb):
    return pl.multiple_of(jnp.clip(t * qb - 1, 0, nb - (qb + 2)) * SWA_BLOCK, SWA_BLOCK)


def _swa_kernel(sink_ref, q_ref, k_ref, kc_ref, v_ref, vc_ref, cos_ref, sin_ref, o_ref, s_scr, m_scr, *,
                qb, tiles, nt):
    u = pl.program_id(0)

    @pl.when(u == 0)
    def _():
        s_scr[...] = jnp.zeros_like(s_scr)
        m_scr[...] = jnp.zeros_like(m_scr)

    blk, d = SWA_BLOCK, SWA_DIM
    half = d // 4
    nq = qb * blk
    nw = (qb + 2) * blk
    nb = k_ref.shape[1] // blk
    ck = kc_ref.shape[1]
    cur, prev = jnp.minimum(u, tiles - 1), jnp.maximum(u - 1, 0)
    g_cur, t_cur = (cur // nt) % SWA_KV_HEADS, cur % nt
    g_prev, t_prev = (prev // nt) % SWA_KV_HEADS, prev % nt
    q_start = pl.multiple_of(t_cur * nq, blk)
    k_start = _swa_window_start(t_cur, qb, nb)
    v_start = _swa_window_start(t_prev, qb, nb)

    def sink_row(g):
        return jnp.concatenate(
            [jnp.full((1, nq), sink_ref[0, g * SWA_GROUP + i] * LOG2E, F32) for i in range(SWA_GROUP)], axis=1)

    def values_t(v):
        return jnp.concatenate([v.T.astype(BF16), jnp.ones((16, v.shape[0]), BF16)], axis=0)

    scale = d ** -0.5 * LOG2E
    cos_q, sin_q = cos_ref[pl.ds(q_start, nq), :], sin_ref[pl.ds(q_start, nq), :]
    q = jnp.concatenate(
        [(_rope(q_ref[0, :, d * i: d * (i + 1)], cos_q, sin_q, half) * scale).astype(BF16)
         for i in range(SWA_GROUP)], axis=0)
    key = lax.broadcasted_iota(jnp.int32, (ck, nq), 0)
    tok = lax.broadcasted_iota(jnp.int32, (ck, nq), 1)
    m_prev = m_scr[...]
    m_new = sink_row(g_cur)
    acc = jnp.zeros((d + 16, SWA_GROUP * nq), F32)
    n_win = nw // ck
    for j in range(n_win + 1):
        rows = slice(j * ck, (j + 1) * ck)
        p = jnp.exp2(s_scr[rows, :] - m_prev).astype(BF16)
        if j < n_win:
            vt = values_t(v_ref[0, pl.ds(v_start + j * ck, ck), :])
            kr = pl.ds(k_start + j * ck, ck)
            k = _rope(k_ref[0, kr, :], cos_ref[kr, :], sin_ref[kr, :], half).astype(BF16)
            dist = key - tok + (k_start + j * ck - q_start)
            valid = (dist <= blk) & (dist >= -blk)
            s = _dot_nt(k, q)
            s = jnp.concatenate(
                [jnp.where(valid, s[:, nq * i: nq * (i + 1)], -jnp.inf) for i in range(SWA_GROUP)], axis=1)
        else:
            vt = values_t(vc_ref[0])
            s = _dot_nt(kc_ref[0].astype(BF16), q)
        s_scr[rows, :] = s
        m_new = jnp.maximum(m_new, jnp.max(s, axis=0, keepdims=True))
        acc = acc + _dot(vt, p)
    m_scr[...] = m_new
    out_t = acc[:d] / (acc[d:d + 1] + jnp.exp2(sink_row(g_prev) - m_prev))
    for i in range(SWA_GROUP):
        o_ref[0, :, d * i: d * (i + 1)] = out_t[:, nq * i: nq * (i + 1)].T.astype(o_ref.dtype)


def _swa_attention(proj_l, proj_c, cos, sin, sink):
    b, n, _ = proj_l.shape
    c = proj_c.shape[1]
    blk, d = SWA_BLOCK, SWA_DIM
    qb = 2
    nq = qb * blk
    nt = n // nq
    tiles = b * SWA_KV_HEADS * nt
    assert ((qb + 2) * blk) % c == 0 and n // blk >= qb + 2

    def cur(u):
        t = jnp.minimum(u, tiles - 1)
        return t // (SWA_KV_HEADS * nt), (t // nt) % SWA_KV_HEADS, t % nt

    def prev(u):
        t = jnp.maximum(u - 1, 0)
        return t // (SWA_KV_HEADS * nt), (t // nt) % SWA_KV_HEADS, t % nt

    kcol, vcol = P_SWA_K // d, P_SWA_V // d
    return pl.pallas_call(
        functools.partial(_swa_kernel, qb=qb, tiles=tiles, nt=nt),
        grid=(tiles + 1,),
        in_specs=[pl.BlockSpec(memory_space=pltpu.SMEM),
                  pl.BlockSpec((1, nq, SWA_GROUP * d), lambda u: (cur(u)[0], cur(u)[2], cur(u)[1])),
                  pl.BlockSpec((1, n, d), lambda u: (cur(u)[0], 0, kcol + cur(u)[1])),
                  pl.BlockSpec((1, c, d), lambda u: (cur(u)[0], 0, kcol + cur(u)[1])),
                  pl.BlockSpec((1, n, d), lambda u: (prev(u)[0], 0, vcol + prev(u)[1])),
                  pl.BlockSpec((1, c, d), lambda u: (prev(u)[0], 0, vcol + prev(u)[1])),
                  pl.BlockSpec((n, d), lambda u: (0, 0)),
                  pl.BlockSpec((n, d), lambda u: (0, 0))],
        out_specs=pl.BlockSpec((1, nq, SWA_GROUP * d), lambda u: (prev(u)[0], prev(u)[2], prev(u)[1])),
        out_shape=jax.ShapeDtypeStruct((b, n, SWA_HEADS * d), BF16),
        scratch_shapes=[pltpu.VMEM(((qb + 2) * blk + c, SWA_GROUP * nq), F32),
                        pltpu.VMEM((1, SWA_GROUP * nq), F32)],
        compiler_params=_params(1),
        name="swa_attention",
    )(sink, proj_l, proj_l, proj_c, proj_l, proj_c, cos, sin)


def _log_sigmoid(x):
    return jnp.minimum(x, 0.0) - jnp.log(1.0 + jnp.exp(-jnp.abs(x)))


def _gla_factors(q_ref, k_ref, misc_ref, wg, bg, reverse):
    rows = q_ref.shape[1]
    L = GLA_CHUNK
    assert rows == 4 * L
    width = GLA_HEADS * GLA_DK
    la = _log_sigmoid(_dot(misc_ref[0].astype(BF16), wg) + bg) * (1.0 / GLA_TAU)
    hi = la.astype(BF16)
    lo = (la - hi.astype(F32)).astype(BF16)
    ti = lax.broadcasted_iota(jnp.int32, (rows, rows), 0)
    tj = lax.broadcasted_iota(jnp.int32, (rows, rows), 1)
    ci, cj = ti // L, tj // L
    upto = (tj >= ti) if reverse else (tj <= ti)
    ones_in = jnp.where((ci == cj) & upto, 1.0, 0.0).astype(BF16)
    ones_abs = jnp.where(upto, 1.0, 0.0).astype(BF16)
    sums = _dot(jnp.concatenate([ones_in, ones_abs], axis=0), jnp.concatenate([hi, lo], axis=1))
    sums = sums[:, :width] + sums[:, width:]
    b_in, b_abs = sums[:rows], sums[rows:]
    if reverse:
        g_mid, g_end = b_abs[rows // 2:rows // 2 + 1], b_abs[0:1]
        diag, cross = (ci == cj) & (tj > ti), cj > ci
    else:
        g_mid, g_end = b_abs[rows // 2 - 1:rows // 2], b_abs[rows - 1:rows]
        diag, cross = (ci == cj) & (tj <= ti), cj < ci
    q = q_ref[0] * GLA_DK ** -0.5
    k = k_ref[0]
    q_d, k_d = q * jnp.exp(b_in), k * jnp.exp(-b_in)
    q_x, k_x = q * jnp.exp(b_abs - g_mid), k * jnp.exp(g_mid - b_abs)
    q_s, k_s = q * jnp.exp(b_abs), k * jnp.exp(g_end - b_abs)
    decay = jnp.exp(g_end)
    return q_d, k_d, q_x, k_x, q_s, k_s, decay, diag, cross


def _gla_apply(factors, v_ref, state_ref, o_ref):
    q_d, k_d, q_x, k_x, q_s, k_s, decay, diag, cross = factors
    rows = q_d.shape[0]
    lane = lax.broadcasted_iota(jnp.int32, (rows, 2 * GLA_DK), 1)
    lane_s = lax.broadcasted_iota(jnp.int32, (GLA_DV, 2 * GLA_DK), 1)
    for pair in range(GLA_HEADS // 2):
        ps = slice(2 * GLA_DK * pair, 2 * GLA_DK * (pair + 1))
        kd_p, kx_p, ks_p = k_d[:, ps].astype(BF16), k_x[:, ps].astype(BF16), k_s[:, ps].astype(BF16)
        st = state_ref[pair]
        st_b = st.astype(BF16)
        ds = []
        for j in range(2):
            h = 2 * pair + j
            vs = slice(h * GLA_DV, (h + 1) * GLA_DV)
            mine = (lane // GLA_DK) == j
            qd_h = jnp.where(mine, q_d[:, ps], 0.0).astype(BF16)
            qx_h = jnp.where(mine, q_x[:, ps], 0.0).astype(BF16)
            qs_h = jnp.where(mine, q_s[:, ps], 0.0).astype(BF16)
            vh = v_ref[0, :, vs].astype(BF16)
            a = jnp.where(diag, _dot_nt(qd_h, kd_p), jnp.where(cross, _dot_nt(qx_h, kx_p), 0.0))
            o_ref[0, :, vs] = _dot(a.astype(BF16), vh) + _dot_nt(qs_h, st_b)
            ds.append(_dot_tn(vh, ks_p))
        state_ref[pair] = decay[:, ps] * st + jnp.where(lane_s < GLA_DK, ds[0], ds[1])


def _gla_kernel(qf_ref, kf_ref, vf_ref, mf_ref, qb_ref, kb_ref, vb_ref, mb_ref, wg_ref, bg_ref,
                sf0_ref, sb0_ref, of_ref, ob_ref, sf_ref, sb_ref, stf, stb):
    t = pl.program_id(1)

    @pl.when(t == 0)
    def _():
        stf[...] = sf0_ref[0]
        stb[...] = sb0_ref[0]

    width = GLA_HEADS * GLA_DK
    fwd = _gla_factors(qf_ref, kf_ref, mf_ref, wg_ref[:, :width], bg_ref[:, :width], False)
    bwd = _gla_factors(qb_ref, kb_ref, mb_ref, wg_ref[:, width:], bg_ref[:, width:], True)
    _gla_apply(fwd, vf_ref, stf, of_ref)
    _gla_apply(bwd, vb_ref, stb, ob_ref)

    @pl.when(t == pl.num_programs(1) - 1)
    def _():
        sf_ref[0] = stf[...]
        sb_ref[0] = stb[...]


def _gla_scan(proj, wg, bg, sf0, sb0):
    b, r, _ = proj.shape
    tm = 256
    nt = r // tm
    kw, vw = GLA_HEADS * GLA_DK, GLA_HEADS * GLA_DV
    fwd = lambda t: t
    bwd = lambda t: nt - 1 - t

    def specs(order):
        return [pl.BlockSpec((1, tm, kw), lambda i, t: (i, order(t), P_GLA_Q // kw)),
                pl.BlockSpec((1, tm, kw), lambda i, t: (i, order(t), P_GLA_K // kw)),
                pl.BlockSpec((1, tm, vw), lambda i, t: (i, order(t), P_GLA_V // vw)),
                pl.BlockSpec((1, tm, 128), lambda i, t: (i, order(t), P_MISC // 128))]

    state_spec = pl.BlockSpec((1,) + GLA_STATE, lambda i, t: (i, 0, 0, 0))
    state_shape = jax.ShapeDtypeStruct((b,) + GLA_STATE, F32)
    return pl.pallas_call(
        _gla_kernel,
        grid=(b, nt),
        in_specs=specs(fwd) + specs(bwd) + [
            pl.BlockSpec(wg.shape, lambda i, t: (0, 0)),
            pl.BlockSpec(bg.shape, lambda i, t: (0, 0)),
            state_spec, state_spec],
        out_specs=[pl.BlockSpec((1, tm, vw), lambda i, t: (i, fwd(t), 0)),
                   pl.BlockSpec((1, tm, vw), lambda i, t: (i, bwd(t), 0)),
                   state_spec, state_spec],
        out_shape=[jax.ShapeDtypeStruct((b, r, vw), F32), jax.ShapeDtypeStruct((b, r, vw), F32),
                   state_shape, state_shape],
        scratch_shapes=[pltpu.VMEM(GLA_STATE, F32), pltpu.VMEM(GLA_STATE, F32)],
        compiler_params=_params(2),
        name="gla_scan",
    )(proj, proj, proj, proj, proj, proj, proj, proj, wg, bg, sf0, sb0)


def _outproj_kernel(x_ref, mod_ref, mla_ref, swa_ref, of_ref, ob_ref, r_ref, g_ref, w_ref, o_ref):
    o = of_ref[0] + ob_ref[0]
    gate = _silu(r_ref[0])
    parts = []
    for h in range(GLA_HEADS):
        vs = slice(h * GLA_DV, (h + 1) * GLA_DV)
        parts.append((_rms(o[:, vs], g_ref[:, vs]) * gate[:, vs]).astype(BF16))
    gla = jnp.concatenate(parts, axis=1)
    a, b = mla_ref.shape[2], mla_ref.shape[2] + swa_ref.shape[2]
    mix = _dot(mla_ref[0], w_ref[0:a, :]) + _dot(swa_ref[0], w_ref[a:b, :]) + _dot(gla, w_ref[b:, :])
    o_ref[0] = x_ref[0] + mod_ref[0, 2:3, :] * mix


def _out_projection(x, mod, mla, swa, o_f, o_b, proj, g_out, w_out):
    bx, r, d = x.shape
    tm = 512
    row = lambda width: pl.BlockSpec((1, tm, width), lambda b, t: (b, t, 0))
    gw = GLA_HEADS * GLA_DV
    return pl.pallas_call(
        _outproj_kernel,
        grid=(bx, r // tm),
        in_specs=[row(d),
                  pl.BlockSpec((1, N_MOD, d), lambda b, t: (b, 0, 0)),
                  row(mla.shape[2]), row(swa.shape[2]), row(gw), row(gw),
                  pl.BlockSpec((1, tm, gw), lambda b, t: (b, t, P_GLA_R // gw)),
                  pl.BlockSpec((1, gw), lambda b, t: (0, 0)),
                  pl.BlockSpec(w_out.shape, lambda b, t: (0, 0))],
        out_specs=row(d),
        out_shape=jax.ShapeDtypeStruct(x.shape, F32),
        compiler_params=_params(2),
        name="out_projection",
    )(x, mod, mla, swa, o_f, o_b, proj, g_out, w_out)


def _ffn_kernel(x_ref, mod_ref, g_ref, wg_ref, wu_ref, wd_ref, gf_ref, o_ref, h_ref, *, final_norm):
    f = pl.program_id(2)
    row_chunks = [slice(r, r + ROW_CHUNK) for r in range(0, x_ref.shape[1], ROW_CHUNK)]
    chunk = (ROW_CHUNK, x_ref.shape[2])

    def gated_partial():
        h = h_ref[...]
        act = _silu(_dot(h, wg_ref[...])) * _dot(h, wu_ref[...])
        return mod_ref[0, 5:6, :] * _dot(act.astype(BF16), wd_ref[...])

    @pl.when(f == 0)
    def _():
        gain = jnp.broadcast_to(g_ref[...] * (1.0 + mod_ref[0, 4:5, :]), chunk)
        shift = jnp.broadcast_to(mod_ref[0, 3:4, :], chunk)
        for rows in row_chunks:
            x = x_ref[0, rows, :]
            rs = lax.rsqrt(jnp.mean(x * x, axis=-1, keepdims=True) + EPS)
            h_ref[rows, :] = (x * rs * gain + shift).astype(BF16)
        o_ref[0] = x_ref[0] + gated_partial()

    @pl.when(f > 0)
    def _():
        o_ref[0] += gated_partial()

    if final_norm:
        @pl.when(f == pl.num_programs(2) - 1)
        def _():
            g_fin = jnp.broadcast_to(gf_ref[...], chunk)
            for rows in row_chunks:
                y = o_ref[0, rows, :]
                o_ref[0, rows, :] = y * lax.rsqrt(jnp.mean(y * y, axis=-1, keepdims=True) + EPS) * g_fin


def _ffn(x, mod, g, w_gu, w_down, g_final, final_norm):
    bx, r, d = x.shape
    hidden = w_down.shape[0]
    tm, tf = 512, 512
    nf = hidden // tf
    return pl.pallas_call(
        functools.partial(_ffn_kernel, final_norm=final_norm),
        grid=(bx, r // tm, nf),
        in_specs=[pl.BlockSpec((1, tm, d), lambda b, t, f: (b, t, 0)),
                  pl.BlockSpec((1, N_MOD, d), lambda b, t, f: (b, 0, 0)),
                  pl.BlockSpec((1, d), lambda b, t, f: (0, 0)),
                  pl.BlockSpec((d, tf), lambda b, t, f: (0, f)),
                  pl.BlockSpec((d, tf), lambda b, t, f: (0, nf + f)),
                  pl.BlockSpec((tf, d), lambda b, t, f: (f, 0)),
                  pl.BlockSpec((1, d), lambda b, t, f: (0, 0))],
        out_specs=pl.BlockSpec((1, tm, d), lambda b, t, f: (b, t, 0)),
        out_shape=jax.ShapeDtypeStruct(x.shape, F32),
        scratch_shapes=[pltpu.VMEM((tm, d), BF16)],
        compiler_params=_params(3),
        name="ffn",
    )(x, mod, g, w_gu, w_gu, w_down, g_final)


def _rope_tables(n, dim):
    half = dim // 4
    freqs = ROPE_THETA ** (-jnp.arange(half, dtype=F32) / half)
    pos = jnp.arange(n, dtype=jnp.int32)
    ang_r = (pos // GRID_W).astype(F32)[:, None] * freqs
    ang_c = (pos % GRID_W).astype(F32)[:, None] * freqs
    cos = jnp.concatenate([jnp.cos(ang_r)] * 2 + [jnp.cos(ang_c)] * 2, axis=1)
    sin = jnp.concatenate([-jnp.sin(ang_r), jnp.sin(ang_r), -jnp.sin(ang_c), jnp.sin(ang_c)], axis=1)
    reps = 128 // dim
    return jnp.tile(cos, (1, reps)), jnp.tile(sin, (1, reps))


def _relayout_w_in(w_in):
    depth, d, _ = w_in.shape
    sizes = (MLA_RANK, MLA_RANK, MLA_ROPE, SWA_HEADS * SWA_DIM, SWA_KV_HEADS * SWA_DIM, SWA_KV_HEADS * SWA_DIM,
             GLA_HEADS * GLA_DK, GLA_HEADS * GLA_DK, GLA_HEADS * GLA_DV, 2 * GLA_GATE_RANK, GLA_HEADS * GLA_DV)
    offs = [0]
    for s in sizes:
        offs.append(offs[-1] + s)
    part = lambda i: w_in[:, :, offs[i]:offs[i + 1]]
    cq, ckv, kr, sq, sk, sv, gq, gk, gv, glr, gr = (part(i) for i in range(len(sizes)))
    zeros = lambda w: jnp.zeros((depth, d, w), w_in.dtype)
    misc_pad = 128 - MLA_ROPE - 2 * GLA_GATE_RANK
    out = jnp.concatenate([sq, sk, cq, ckv, sv, kr, glr, zeros(misc_pad), zeros(P_GLA_Q - P_MISC - 128),
                           gq, gk, gv, gr], axis=-1)
    assert out.shape[-1] == P_WIDTH
    return out.astype(BF16)


def _relayout_mla(w_uq, w_ukv):
    depth, rk, _ = w_uq.shape
    uq = w_uq.reshape(depth, rk, MLA_HEADS, MLA_QK)
    uq = jnp.concatenate([uq[..., :MLA_NOPE].reshape(depth, rk, -1), uq[..., MLA_NOPE:].reshape(depth, rk, -1)], -1)
    ukv = w_ukv.reshape(depth, rk, MLA_HEADS, MLA_NOPE + MLA_V)
    ukv = jnp.concatenate([ukv[..., :MLA_NOPE].reshape(depth, rk, -1), ukv[..., MLA_NOPE:].reshape(depth, rk, -1)], -1)
    return uq.astype(BF16), ukv.astype(BF16)


def _relayout_gate(w_f, b_f, w_b, b_b):
    depth, rank, width = w_f.shape
    wg = jnp.zeros((depth, 128, 2 * width), F32)
    wg = wg.at[:, MISC_GATE_OFF:MISC_GATE_OFF + rank, :width].set(w_f)
    wg = wg.at[:, MISC_GATE_OFF + rank:MISC_GATE_OFF + 2 * rank, width:].set(w_b)
    bg = jnp.concatenate([b_f, b_b], axis=-1).reshape(depth, 1, 2 * width)
    return wg.astype(BF16), bg


def kernel(x, c, ctx, c_ctx, w_mod, b_mod, g_mix, g_ffn, w_in, g_mla_q, g_mla_kv, w_mla_uq, w_mla_ukv,
           swa_sink, w_gla_gate_f, b_gla_gate_f, w_gla_gate_b, b_gla_gate_b, g_gla_out, w_out, w_ffn_gu,
           w_ffn_down, g_final):
    B, N, D = x.shape
    C = ctx.shape[1]
    depth = w_mod.shape[0]

    cvec = jnp.concatenate([c, c_ctx[None, :], jnp.zeros((16 - B - 1, D), F32)], axis=0)
    mod = _modulation(cvec, w_mod, b_mod).reshape(depth, 16, N_MOD, D)

    w_in_p = _relayout_w_in(w_in)
    w_uq_p, w_ukv_p = _relayout_mla(w_mla_uq, w_mla_ukv)
    wg_p, bg_p = _relayout_gate(w_gla_gate_f, b_gla_gate_f, w_gla_gate_b, b_gla_gate_b)
    w_out_b = w_out.astype(BF16)
    w_gu_b = w_ffn_gu.astype(BF16)
    w_down_b = w_ffn_down.astype(BF16)

    cos_m, sin_m = _rope_tables(N, MLA_ROPE)
    cos_s, sin_s = _rope_tables(N, SWA_DIM)
    cos_id, sin_id = jnp.ones((C, 128), F32), jnp.zeros((C, 128), F32)
    state0 = jnp.zeros((B,) + GLA_STATE, F32)
    g_fin = g_final.reshape(1, D)

    xc = ctx.reshape(1, B * C, D)
    for l in range(depth):
        last = l == depth - 1
        mod_l, mod_c = mod[l, :B], mod[l, B:B + 1]
        g_mix_l, g_ffn_l = g_mix[l].reshape(1, D), g_ffn[l].reshape(1, D)
        g_q, g_kv = g_mla_q[l].reshape(1, -1), g_mla_kv[l].reshape(1, -1)
        g_out = g_gla_out[l].reshape(1, -1)
        sink = swa_sink[l].reshape(1, -1)

        proj_l = _in_projection(x, mod_l, g_mix_l, w_in_p[l])
        proj_c = _in_projection(xc, mod_c, g_mix_l, w_in_p[l]).reshape(B, C, P_WIDTH)

        q_l, k_l, v_l = _mla_prep(proj_l, cos_m, sin_m, g_q, g_kv, w_uq_p[l], w_ukv_p[l])
        q_c, k_c, v_c = _mla_prep(proj_c, cos_id, sin_id, g_q, g_kv, w_uq_p[l], w_ukv_p[l])
        mla_l = _mla_attention(q_l, k_l, v_l, k_c, v_c)
        swa_l = _swa_attention(proj_l, proj_c, cos_s, sin_s, sink)

        of_c, ob_c, s_f, s_b = _gla_scan(proj_c, wg_p[l], bg_p[l], state0, state0)
        of_l, ob_l, _, _ = _gla_scan(proj_l, wg_p[l], bg_p[l], s_f, s_b)

        x = _out_projection(x, mod_l, mla_l, swa_l, of_l, ob_l, proj_l, g_out, w_out_b[l])
        x = _ffn(x, mod_l, g_ffn_l, w_gu_b[l], w_down_b[l], g_fin, last)

        if not last:
            mla_c = _mla_ctx_attention(q_c, k_c, v_c)
            swa_c = _swa_ctx_attention(proj_c, sink)
            flat = lambda a: a.reshape(1, B * C, a.shape[-1])
            xc = _out_projection(xc, mod_c, flat(mla_c), flat(swa_c), flat(of_c), flat(ob_c), flat(proj_c),
                                 g_out, w_out_b[l])
            xc = _ffn(xc, mod_c, g_ffn_l, w_gu_b[l], w_down_b[l], g_fin, False)
    return x
```

```python
import functools

import jax
import jax.numpy as jnp
import numpy as np
from jax import lax
from jax.experimental import pallas as pl
from jax.experimental.pallas import tpu as pltpu

F32 = jnp.float32
BF16 = jnp.bfloat16

GRID_W = 64
EPS = 1e-6
ROPE_THETA = 10000.0
LOG2E = 1.4426950408889634

MLA_HEADS = 6
MLA_RANK = 512
MLA_NOPE = 128
MLA_ROPE = 64
MLA_V = 128
MLA_QK = MLA_NOPE + MLA_ROPE
MLA_VA = MLA_V + 16

SWA_HEADS = 6
SWA_KV_HEADS = 2
SWA_GROUP = SWA_HEADS // SWA_KV_HEADS
SWA_DIM = 128
SWA_BLOCK = 128

GLA_HEADS = 4
GLA_DK = 64
GLA_DV = 128
GLA_GATE_RANK = 16
GLA_TAU = 16.0
GLA_CHUNK = 64
GLA_STATE = (GLA_HEADS // 2, GLA_DV, 2 * GLA_DK)

N_MOD = 6

P_SWA_Q = 0
P_SWA_K = 768
P_CQ = 1024
P_CKV = 1536
P_SWA_V = 2048
P_MISC = 2304
P_GLA_Q = 2560
P_GLA_K = 2816
P_GLA_V = 3072
P_GLA_R = 3584
P_WIDTH = 4096
MISC_GATE_OFF = MLA_ROPE

VMEM_LIMIT = 56 * 1024 * 1024
ROW_CHUNK = 16


def _params(n_axes, flags=None):
    return pltpu.CompilerParams(dimension_semantics=("arbitrary",) * n_axes,
                                vmem_limit_bytes=VMEM_LIMIT, flags=flags)


def _silu(x):
    return x / (1.0 + jnp.exp(-x))


def _rms(x, g):
    ms = jnp.mean(x * x, axis=-1, keepdims=True)
    return x * lax.rsqrt(ms + EPS) * g


def _dot(a, b):
    return jnp.dot(a, b, preferred_element_type=F32)


def _dot_nt(a, b):
    return lax.dot_general(a, b, (((1,), (1,)), ((), ())), preferred_element_type=F32)


def _dot_tn(a, b):
    return lax.dot_general(a, b, (((0,), (0,)), ((), ())), preferred_element_type=F32)


def _mod_kernel(c_ref, w_ref, b_ref, o_ref):
    a = _silu(c_ref[...]).astype(BF16)
    o_ref[0] = _dot(a, w_ref[0].astype(BF16)) + b_ref[0]


def _modulation(cvec, w_mod, b_mod):
    depth, d, width = w_mod.shape
    rows = cvec.shape[0]
    tn = 1024
    return pl.pallas_call(
        _mod_kernel,
        grid=(depth, width // tn),
        in_specs=[pl.BlockSpec((rows, d), lambda l, j: (0, 0)),
                  pl.BlockSpec((1, d, tn), lambda l, j: (l, 0, j)),
                  pl.BlockSpec((1, 1, tn), lambda l, j: (l, 0, j))],
        out_specs=pl.BlockSpec((1, rows, tn), lambda l, j: (l, 0, j)),
        out_shape=jax.ShapeDtypeStruct((depth, rows, width), F32),
        compiler_params=_params(2),
        name="modulation",
    )(cvec, w_mod, b_mod.reshape(depth, 1, width))


def _inproj_kernel(x_ref, mod_ref, g_ref, w_ref, o_ref):
    h = _rms(x_ref[0], g_ref[...]) * (1.0 + mod_ref[0, 1:2, :]) + mod_ref[0, 0:1, :]
    o_ref[0] = _dot(h.astype(BF16), w_ref[...])


def _in_projection(x, mod, g, w):
    bx, r, d = x.shape
    tm, tn = 512, 2048
    return pl.pallas_call(
        _inproj_kernel,
        grid=(P_WIDTH // tn, bx, r // tm),
        in_specs=[pl.BlockSpec((1, tm, d), lambda j, b, t: (b, t, 0)),
                  pl.BlockSpec((1, N_MOD, d), lambda j, b, t: (b, 0, 0)),
                  pl.BlockSpec((1, d), lambda j, b, t: (0, 0)),
                  pl.BlockSpec((d, tn), lambda j, b, t: (0, j))],
        out_specs=pl.BlockSpec((1, tm, tn), lambda j, b, t: (b, t, j)),
        out_shape=jax.ShapeDtypeStruct((bx, r, P_WIDTH), F32),
        compiler_params=_params(3),
        name="in_projection",
    )(x, mod, g, w)


def _rope(x, cos, sin, half):
    lane = lax.broadcasted_iota(jnp.int32, x.shape, 1)
    first = (lane % (2 * half)) < half
    width = x.shape[1]
    rot = jnp.where(first, pltpu.roll(x, width - half, 1), pltpu.roll(x, half, 1))
    return x * cos + rot * sin


def _mla_prep_kernel(cq_ref, ckv_ref, misc_ref, cos_ref, sin_ref, gq_ref, gkv_ref, wuq_ref, wukv_ref,
                     qt_ref, k_ref, vt_ref):
    scale = MLA_QK ** -0.5 * LOG2E
    cos, sin = cos_ref[...], sin_ref[...]
    half = MLA_ROPE // 4
    qf = _dot(_rms(cq_ref[0], gq_ref[...]).astype(BF16), wuq_ref[...])
    kvf = _dot(_rms(ckv_ref[0], gkv_ref[...]).astype(BF16), wukv_ref[...])
    kr = _rope(misc_ref[0], cos, sin, half)[:, :MLA_ROPE].astype(BF16)
    nope_w = MLA_HEADS * MLA_NOPE
    for pair in range(MLA_HEADS // 2):
        qr = _rope(qf[:, nope_w + 128 * pair: nope_w + 128 * (pair + 1)], cos, sin, half) * scale
        qrt = qr.astype(BF16).T
        for j in range(2):
            qt_ref[0, 2 * pair + j, MLA_NOPE:MLA_QK, :] = qrt[MLA_ROPE * j: MLA_ROPE * (j + 1), :]
    for h in range(MLA_HEADS):
        qt_ref[0, h, 0:MLA_NOPE, :] = (qf[:, MLA_NOPE * h: MLA_NOPE * (h + 1)] * scale).astype(BF16).T
        k_ref[0, h, :, 0:MLA_NOPE] = kvf[:, MLA_NOPE * h: MLA_NOPE * (h + 1)].astype(BF16)
        k_ref[0, h, :, MLA_NOPE:MLA_QK] = kr
        vt_ref[0, h, 0:MLA_V, :] = kvf[:, nope_w + MLA_V * h: nope_w + MLA_V * (h + 1)].astype(BF16).T
        vt_ref[0, h, MLA_V:, :] = jnp.ones((MLA_VA - MLA_V, vt_ref.shape[3]), BF16)


def _mla_prep(proj, cos, sin, g_q, g_kv, w_uq, w_ukv):
    b, r, _ = proj.shape
    tm = min(r, 1024)
    rk = MLA_RANK
    const = lambda shape: pl.BlockSpec(shape, lambda i, t: (0,) * len(shape))
    return pl.pallas_call(
        _mla_prep_kernel,
        grid=(b, r // tm),
        in_specs=[pl.BlockSpec((1, tm, rk), lambda i, t: (i, t, P_CQ // rk)),
                  pl.BlockSpec((1, tm, rk), lambda i, t: (i, t, P_CKV // rk)),
                  pl.BlockSpec((1, tm, 128), lambda i, t: (i, t, P_MISC // 128)),
                  pl.BlockSpec((tm, 128), lambda i, t: (t, 0)),
                  pl.BlockSpec((tm, 128), lambda i, t: (t, 0)),
                  const((1, rk)), const((1, rk)),
                  const(w_uq.shape), const(w_ukv.shape)],
        out_specs=[pl.BlockSpec((1, MLA_HEADS, MLA_QK, tm), lambda i, t: (i, 0, 0, t)),
                   pl.BlockSpec((1, MLA_HEADS, tm, MLA_QK), lambda i, t: (i, 0, t, 0)),
                   pl.BlockSpec((1, MLA_HEADS, MLA_VA, tm), lambda i, t: (i, 0, 0, t))],
        out_shape=[jax.ShapeDtypeStruct((b, MLA_HEADS, MLA_QK, r), BF16),
                   jax.ShapeDtypeStruct((b, MLA_HEADS, r, MLA_QK), BF16),
                   jax.ShapeDtypeStruct((b, MLA_HEADS, MLA_VA, r), BF16)],
        compiler_params=_params(2),
        name="mla_prep",
    )(proj, proj, proj, cos, sin, g_q, g_kv, w_uq, w_ukv)


def _softmax_pv(s_parts, vt_parts):
    m = functools.reduce(jnp.maximum, [jnp.max(s, axis=0, keepdims=True) for s in s_parts])
    acc = sum(_dot(vt, jnp.exp2(s - m).astype(BF16)) for s, vt in zip(s_parts, vt_parts))
    return (acc[:MLA_V] / acc[MLA_V:MLA_V + 1]).T


def _mla_attn_kernel(qt_ref, kl_ref, kc_ref, vlt_ref, vct_ref, o_ref, s_scr, m_scr, *, tk):
    @pl.when(pl.program_id(0) == 0)
    def _():
        s_scr[...] = jnp.zeros_like(s_scr)
        m_scr[...] = jnp.zeros_like(m_scr)

    n_lat = kl_ref.shape[2] // tk
    qt = qt_ref[0, 0]
    tq = qt.shape[1]
    m_prev = m_scr[...]
    m_new = jnp.full((1, tq), -jnp.inf, F32)
    acc = jnp.zeros((MLA_VA, tq), F32)
    for j in range(n_lat + 1):
        rows = slice(j * tk, (j + 1) * tk)
        p = jnp.exp2(s_scr[rows, :] - m_prev).astype(BF16)
        k = kc_ref[0, 0] if j == n_lat else kl_ref[0, 0, rows, :]
        vt = vct_ref[0, 0] if j == n_lat else vlt_ref[0, 0, :, rows]
        s = _dot(k, qt)
        s_scr[rows, :] = s
        m_new = jnp.maximum(m_new, jnp.max(s, axis=0, keepdims=True))
        acc = acc + _dot(vt, p)
    m_scr[...] = m_new
    o_ref[0] = (acc[:MLA_V] / acc[MLA_V:MLA_V + 1]).T.astype(o_ref.dtype)


def _mla_attention(qt, k_l, vt_l, k_c, vt_c):
    b, h, dq, n = qt.shape
    c = k_c.shape[2]
    tq = min(n, 1024)
    nt = n // tq
    tiles = b * h * nt

    def cur(u):
        t = jnp.minimum(u, tiles - 1)
        return t // (h * nt), (t // nt) % h, t % nt

    def prev(u):
        t = jnp.maximum(u - 1, 0)
        return t // (h * nt), (t // nt) % h, t % nt

    assert n % c == 0
    return pl.pallas_call(
        functools.partial(_mla_attn_kernel, tk=c),
        grid=(tiles + 1,),
        in_specs=[pl.BlockSpec((1, 1, dq, tq), lambda u: (cur(u)[0], cur(u)[1], 0, cur(u)[2])),
                  pl.BlockSpec((1, 1, n, dq), lambda u: (cur(u)[0], cur(u)[1], 0, 0)),
                  pl.BlockSpec((1, 1, c, dq), lambda u: (cur(u)[0], cur(u)[1], 0, 0)),
                  pl.BlockSpec((1, 1, MLA_VA, n), lambda u: (prev(u)[0], prev(u)[1], 0, 0)),
                  pl.BlockSpec((1, 1, MLA_VA, c), lambda u: (prev(u)[0], prev(u)[1], 0, 0))],
        out_specs=pl.BlockSpec((1, tq, MLA_V), lambda u: (prev(u)[0], prev(u)[2], prev(u)[1])),
        out_shape=jax.ShapeDtypeStruct((b, n, h * MLA_V), BF16),
        scratch_shapes=[pltpu.VMEM((n + c, tq), F32), pltpu.VMEM((1, tq), F32)],
        compiler_params=_params(1),
        name="mla_attention",
    )(qt, k_l, k_c, vt_l, vt_c)


def _mla_ctx_kernel(qt_ref, k_ref, vt_ref, o_ref):
    o_ref[0] = _softmax_pv([_dot(k_ref[0, 0], qt_ref[0, 0])], [vt_ref[0, 0]]).astype(o_ref.dtype)


def _mla_ctx_attention(qt, k, vt):
    b, h, dq, c = qt.shape
    return pl.pallas_call(
        _mla_ctx_kernel,
        grid=(b, h),
        in_specs=[pl.BlockSpec((1, 1, dq, c), lambda i, j: (i, j, 0, 0)),
                  pl.BlockSpec((1, 1, c, dq), lambda i, j: (i, j, 0, 0)),
                  pl.BlockSpec((1, 1, MLA_VA, c), lambda i, j: (i, j, 0, 0))],
        out_specs=pl.BlockSpec((1, c, MLA_V), lambda i, j: (i, 0, j)),
        out_shape=jax.ShapeDtypeStruct((b, c, h * MLA_V), BF16),
        compiler_params=_params(2),
        name="mla_ctx_attention",
    )(qt, k, vt)


def _swa_ctx_kernel(sink_ref, q_ref, k_ref, v_ref, o_ref):
    q = (q_ref[0] * SWA_DIM ** -0.5).astype(BF16)
    s = _dot_nt(q, k_ref[0].astype(BF16))
    sink = sink_ref[0, pl.program_id(1)]
    m = jnp.maximum(jnp.max(s, axis=-1, keepdims=True), sink)
    p = jnp.exp(s - m)
    l = jnp.sum(p, axis=-1, keepdims=True) + jnp.exp(sink - m)
    o_ref[0] = (_dot(p.astype(BF16), v_ref[0].astype(BF16)) / l).astype(o_ref.dtype)


def _swa_ctx_attention(proj_c, sink):
    b, c, _ = proj_c.shape
    d = SWA_DIM
    return pl.pallas_call(
        _swa_ctx_kernel,
        grid=(b, SWA_HEADS),
        in_specs=[pl.BlockSpec(memory_space=pltpu.SMEM),
                  pl.BlockSpec((1, c, d), lambda i, j: (i, 0, P_SWA_Q // d + j)),
                  pl.BlockSpec((1, c, d), lambda i, j: (i, 0, P_SWA_K // d + j // SWA_GROUP)),
                  pl.BlockSpec((1, c, d), lambda i, j: (i, 0, P_SWA_V // d + j // SWA_GROUP))],
        out_specs=pl.BlockSpec((1, c, d), lambda i, j: (i, 0, j)),
        out_shape=jax.ShapeDtypeStruct((b, c, SWA_HEADS * d), BF16),
        compiler_params=_params(2),
        name="swa_ctx_attention",
    )(sink, proj_c, proj_c, proj_c)


def _swa_window_start(t, qb, nb):
    return pl.multiple_of(jnp.clip(t * qb - 1, 0, nb - (qb + 2)) * SWA_BLOCK, SWA_BLOCK)


def _swa_kernel(sink_ref, q_ref, k_ref, kc_ref, v_ref, vc_ref, cos_ref, sin_ref, o_ref, s_scr, m_scr, *,
                qb, tiles, nt):
    u = pl.program_id(0)

    @pl.when(u == 0)
    def _():
        s_scr[...] = jnp.zeros_like(s_scr)
        m_scr[...] = jnp.zeros_like(m_scr)

    blk, d = SWA_BLOCK, SWA_DIM
    half = d // 4
    nq = qb * blk
    nw = (qb + 2) * blk
    nb = k_ref.shape[1] // blk
    ck = kc_ref.shape[1]
    cur, prev = jnp.minimum(u, tiles - 1), jnp.maximum(u - 1, 0)
    g_cur, t_cur = (cur // nt) % SWA_KV_HEADS, cur % nt
    g_prev, t_prev = (prev // nt) % SWA_KV_HEADS, prev % nt
    q_start = pl.multiple_of(t_cur * nq, blk)
    k_start = _swa_window_start(t_cur, qb, nb)
    v_start = _swa_window_start(t_prev, qb, nb)

    def sink_row(g):
        return jnp.concatenate(
            [jnp.full((1, nq), sink_ref[0, g * SWA_GROUP + i] * LOG2E, F32) for i in range(SWA_GROUP)], axis=1)

    def values_t(v):
        return jnp.concatenate([v.T.astype(BF16), jnp.ones((16, v.shape[0]), BF16)], axis=0)

    scale = d ** -0.5 * LOG2E
    cos_q, sin_q = cos_ref[pl.ds(q_start, nq), :], sin_ref[pl.ds(q_start, nq), :]
    q = jnp.concatenate(
        [(_rope(q_ref[0, :, d * i: d * (i + 1)], cos_q, sin_q, half) * scale).astype(BF16)
         for i in range(SWA_GROUP)], axis=0)
    key = lax.broadcasted_iota(jnp.int32, (ck, nq), 0)
    tok = lax.broadcasted_iota(jnp.int32, (ck, nq), 1)
    m_prev = m_scr[...]
    m_new = sink_row(g_cur)
    acc = jnp.zeros((d + 16, SWA_GROUP * nq), F32)
    n_win = nw // ck
    for j in range(n_win + 1):
        rows = slice(j * ck, (j + 1) * ck)
        p = jnp.exp2(s_scr[rows, :] - m_prev).astype(BF16)
        if j < n_win:
            vt = values_t(v_ref[0, pl.ds(v_start + j * ck, ck), :])
            kr = pl.ds(k_start + j * ck, ck)
            k = _rope(k_ref[0, kr, :], cos_ref[kr, :], sin_ref[kr, :], half).astype(BF16)
            dist = key - tok + (k_start + j * ck - q_start)
            valid = (dist <= blk) & (dist >= -blk)
            s = _dot_nt(k, q)
            s = jnp.concatenate(
                [jnp.where(valid, s[:, nq * i: nq * (i + 1)], -jnp.inf) for i in range(SWA_GROUP)], axis=1)
        else:
            vt = values_t(vc_ref[0])
            s = _dot_nt(kc_ref[0].astype(BF16), q)
        s_scr[rows, :] = s
        m_new = jnp.maximum(m_new, jnp.max(s, axis=0, keepdims=True))
        acc = acc + _dot(vt, p)
    m_scr[...] = m_new
    out_t = acc[:d] / (acc[d:d + 1] + jnp.exp2(sink_row(g_prev) - m_prev))
    for i in range(SWA_GROUP):
        o_ref[0, :, d * i: d * (i + 1)] = out_t[:, nq * i: nq * (i + 1)].T.astype(o_ref.dtype)


def _swa_attention(proj_l, proj_c, cos, sin, sink):
    b, n, _ = proj_l.shape
    c = proj_c.shape[1]
    blk, d = SWA_BLOCK, SWA_DIM
    qb = 2
    nq = qb * blk
    nt = n // nq
    tiles = b * SWA_KV_HEADS * nt
    assert ((qb + 2) * blk) % c == 0 and n // blk >= qb + 2

    def cur(u):
        t = jnp.minimum(u, tiles - 1)
        return t // (SWA_KV_HEADS * nt), (t // nt) % SWA_KV_HEADS, t % nt

    def prev(u):
        t = jnp.maximum(u - 1, 0)
        return t // (SWA_KV_HEADS * nt), (t // nt) % SWA_KV_HEADS, t % nt

    kcol, vcol = P_SWA_K // d, P_SWA_V // d
    return pl.pallas_call(
        functools.partial(_swa_kernel, qb=qb, tiles=tiles, nt=nt),
        grid=(tiles + 1,),
        in_specs=[pl.BlockSpec(memory_space=pltpu.SMEM),
                  pl.BlockSpec((1, nq, SWA_GROUP * d), lambda u: (cur(u)[0], cur(u)[2], cur(u)[1])),
                  pl.BlockSpec((1, n, d), lambda u: (cur(u)[0], 0, kcol + cur(u)[1])),
                  pl.BlockSpec((1, c, d), lambda u: (cur(u)[0], 0, kcol + cur(u)[1])),
                  pl.BlockSpec((1, n, d), lambda u: (prev(u)[0], 0, vcol + prev(u)[1])),
                  pl.BlockSpec((1, c, d), lambda u: (prev(u)[0], 0, vcol + prev(u)[1])),
                  pl.BlockSpec((n, d), lambda u: (0, 0)),
                  pl.BlockSpec((n, d), lambda u: (0, 0))],
        out_specs=pl.BlockSpec((1, nq, SWA_GROUP * d), lambda u: (prev(u)[0], prev(u)[2], prev(u)[1])),
        out_shape=jax.ShapeDtypeStruct((b, n, SWA_HEADS * d), BF16),
        scratch_shapes=[pltpu.VMEM(((qb + 2) * blk + c, SWA_GROUP * nq), F32),
                        pltpu.VMEM((1, SWA_GROUP * nq), F32)],
        compiler_params=_params(1),
        name="swa_attention",
    )(sink, proj_l, proj_l, proj_c, proj_l, proj_c, cos, sin)


def _log_sigmoid(x):
    return jnp.minimum(x, 0.0) - jnp.log(1.0 + jnp.exp(-jnp.abs(x)))


def _gla_block_constants(rows):
    t = np.arange(rows)
    ti, tj = t[:, None], t[None, :]
    ci, cj = ti // GLA_CHUNK, tj // GLA_CHUNK
    ones, sel = [], []
    for reverse in (False, True):
        upto = (tj >= ti) if reverse else (tj <= ti)
        visible = (tj > ti) if reverse else (tj <= ti)
        earlier = (cj > ci) if reverse else (cj < ci)
        ones.append(np.concatenate([(ci == cj) & upto, upto], axis=0))
        sel.append(np.where((ci == cj) & visible, 1, np.where(earlier, 2, 0)))
    return jnp.asarray(np.stack(ones), BF16), jnp.asarray(np.stack(sel), jnp.int32)


def _gla_factors(q_ref, k_ref, misc_ref, i, wg, bg, ones, sel, reverse):
    rows = q_ref.shape[1]
    L = GLA_CHUNK
    assert rows == 4 * L
    width = GLA_HEADS * GLA_DK
    la = _log_sigmoid(_dot(misc_ref[i].astype(BF16), wg) + bg) * (1.0 / GLA_TAU)
    hi = la.astype(BF16)
    lo = (la - hi.astype(F32)).astype(BF16)
    sums = _dot(ones, jnp.concatenate([hi, lo], axis=1))
    sums = sums[:, :width] + sums[:, width:]
    b_in, b_abs = sums[:rows], sums[rows:]
    if reverse:
        g_mid, g_end = b_abs[rows // 2:rows // 2 + 1], b_abs[0:1]
    else:
        g_mid, g_end = b_abs[rows // 2 - 1:rows // 2], b_abs[rows - 1:rows]
    diag, cross = sel == 1, sel == 2
    q = q_ref[i] * GLA_DK ** -0.5
    k = k_ref[i]
    q_d, k_d = q * jnp.exp(b_in), k * jnp.exp(-b_in)
    q_x, k_x = q * jnp.exp(b_abs - g_mid), k * jnp.exp(g_mid - b_abs)
    q_s, k_s = q * jnp.exp(b_abs), k * jnp.exp(g_end - b_abs)
    decay = jnp.exp(g_end)
    return q_d, k_d, q_x, k_x, q_s, k_s, decay, diag, cross


def _gla_apply(factors, v_ref, state_ref, o_ref, i):
    q_d, k_d, q_x, k_x, q_s, k_s, decay, diag, cross = factors
    rows = q_d.shape[0]
    lane = lax.broadcasted_iota(jnp.int32, (rows, 2 * GLA_DK), 1)
    lane_s = lax.broadcasted_iota(jnp.int32, (GLA_DV, 2 * GLA_DK), 1)
    for pair in range(GLA_HEADS // 2):
        ps = slice(2 * GLA_DK * pair, 2 * GLA_DK * (pair + 1))
        kd_p, kx_p, ks_p = k_d[:, ps].astype(BF16), k_x[:, ps].astype(BF16), k_s[:, ps].astype(BF16)
        st = state_ref[i, pair]
        st_b = st.astype(BF16)
        ds = []
        for j in range(2):
            h = 2 * pair + j
            vs = slice(h * GLA_DV, (h + 1) * GLA_DV)
            mine = (lane // GLA_DK) == j
            qd_h = jnp.where(mine, q_d[:, ps], 0.0).astype(BF16)
            qx_h = jnp.where(mine, q_x[:, ps], 0.0).astype(BF16)
            qs_h = jnp.where(mine, q_s[:, ps], 0.0).astype(BF16)
            vh = v_ref[i, :, vs].astype(BF16)
            a = jnp.where(diag, _dot_nt(qd_h, kd_p), jnp.where(cross, _dot_nt(qx_h, kx_p), 0.0))
            o_ref[i, :, vs] = _dot(a.astype(BF16), vh) + _dot_nt(qs_h, st_b)
            ds.append(_dot_tn(vh, ks_p))
        state_ref[i, pair] = decay[:, ps] * st + jnp.where(lane_s < GLA_DK, ds[0], ds[1])


def _gla_kernel(qf_ref, kf_ref, vf_ref, mf_ref, qb_ref, kb_ref, vb_ref, mb_ref, wg_ref, bg_ref, ones_ref, sel_ref,
                sf0_ref, sb0_ref, of_ref, ob_ref, sf_ref, sb_ref, stf, stb):
    t = pl.program_id(1)

    @pl.when(t == 0)
    def _():
        stf[...] = sf0_ref[...]
        stb[...] = sb0_ref[...]

    width = GLA_HEADS * GLA_DK
    entries = range(qf_ref.shape[0])
    fwd = [_gla_factors(qf_ref, kf_ref, mf_ref, i, wg_ref[:, :width], bg_ref[:, :width], ones_ref[0], sel_ref[0],
                        False) for i in entries]
    bwd = [_gla_factors(qb_ref, kb_ref, mb_ref, i, wg_ref[:, width:], bg_ref[:, width:], ones_ref[1], sel_ref[1],
                        True) for i in entries]
    for i in entries:
        _gla_apply(fwd[i], vf_ref, stf, of_ref, i)
        _gla_apply(bwd[i], vb_ref, stb, ob_ref, i)

    @pl.when(t == pl.num_programs(1) - 1)
    def _():
        sf_ref[...] = stf[...]
        sb_ref[...] = stb[...]


def _gla_scan(proj, wg, bg, sf0, sb0):
    b, r, _ = proj.shape
    tm = 256
    nt = r // tm
    kw, vw = GLA_HEADS * GLA_DK, GLA_HEADS * GLA_DV
    fwd = lambda t: t
    bwd = lambda t: nt - 1 - t

    nb = 2 if b % 2 == 0 else 1
    ones, sel = _gla_block_constants(tm)

    def specs(order):
        return [pl.BlockSpec((nb, tm, kw), lambda i, t: (i, order(t), P_GLA_Q // kw)),
                pl.BlockSpec((nb, tm, kw), lambda i, t: (i, order(t), P_GLA_K // kw)),
                pl.BlockSpec((nb, tm, vw), lambda i, t: (i, order(t), P_GLA_V // vw)),
                pl.BlockSpec((nb, tm, 128), lambda i, t: (i, order(t), P_MISC // 128))]

    state_spec = pl.BlockSpec((nb,) + GLA_STATE, lambda i, t: (i, 0, 0, 0))
    state_shape = jax.ShapeDtypeStruct((b,) + GLA_STATE, F32)
    return pl.pallas_call(
        _gla_kernel,
        grid=(b // nb, nt),
        in_specs=specs(fwd) + specs(bwd) + [
            pl.BlockSpec(wg.shape, lambda i, t: (0, 0)),
            pl.BlockSpec(bg.shape, lambda i, t: (0, 0)),
            pl.BlockSpec(ones.shape, lambda i, t: (0, 0, 0)),
            pl.BlockSpec(sel.shape, lambda i, t: (0, 0, 0)),
            state_spec, state_spec],
        out_specs=[pl.BlockSpec((nb, tm, vw), lambda i, t: (i, fwd(t), 0)),
                   pl.BlockSpec((nb, tm, vw), lambda i, t: (i, bwd(t), 0)),
                   state_spec, state_spec],
        out_shape=[jax.ShapeDtypeStruct((b, r, vw), F32), jax.ShapeDtypeStruct((b, r, vw), F32),
                   state_shape, state_shape],
        scratch_shapes=[pltpu.VMEM((nb,) + GLA_STATE, F32), pltpu.VMEM((nb,) + GLA_STATE, F32)],
        compiler_params=_params(2),
        name="gla_scan",
    )(proj, proj, proj, proj, proj, proj, proj, proj, wg, bg, ones, sel, sf0, sb0)


def _outproj_kernel(x_ref, mod_ref, mla_ref, swa_ref, of_ref, ob_ref, r_ref, g_ref, w_ref, o_ref):
    o = of_ref[0] + ob_ref[0]
    gate = _silu(r_ref[0])
    parts = []
    for h in range(GLA_HEADS):
        vs = slice(h * GLA_DV, (h + 1) * GLA_DV)
        parts.append((_rms(o[:, vs], g_ref[:, vs]) * gate[:, vs]).astype(BF16))
    gla = jnp.concatenate(parts, axis=1)
    a, b = mla_ref.shape[2], mla_ref.shape[2] + swa_ref.shape[2]
    mix = _dot(mla_ref[0], w_ref[0:a, :]) + _dot(swa_ref[0], w_ref[a:b, :]) + _dot(gla, w_ref[b:, :])
    o_ref[0] = x_ref[0] + mod_ref[0, 2:3, :] * mix


def _out_projection(x, mod, mla, swa, o_f, o_b, proj, g_out, w_out):
    bx, r, d = x.shape
    tm = 512
    row = lambda width: pl.BlockSpec((1, tm, width), lambda b, t: (b, t, 0))
    gw = GLA_HEADS * GLA_DV
    return pl.pallas_call(
        _outproj_kernel,
        grid=(bx, r // tm),
        in_specs=[row(d),
                  pl.BlockSpec((1, N_MOD, d), lambda b, t: (b, 0, 0)),
                  row(mla.shape[2]), row(swa.shape[2]), row(gw), row(gw),
                  pl.BlockSpec((1, tm, gw), lambda b, t: (b, t, P_GLA_R // gw)),
                  pl.BlockSpec((1, gw), lambda b, t: (0, 0)),
                  pl.BlockSpec(w_out.shape, lambda b, t: (0, 0))],
        out_specs=row(d),
        out_shape=jax.ShapeDtypeStruct(x.shape, F32),
        compiler_params=_params(2),
        name="out_projection",
    )(x, mod, mla, swa, o_f, o_b, proj, g_out, w_out)


def _ffn_kernel(x_ref, mod_ref, g_ref, wg_ref, wu_ref, wd_ref, gf_ref, o_ref, h_ref, *, final_norm):
    f = pl.program_id(2)
    row_chunks = [slice(r, r + ROW_CHUNK) for r in range(0, x_ref.shape[1], ROW_CHUNK)]
    chunk = (ROW_CHUNK, x_ref.shape[2])

    def gated_partial():
        h = h_ref[...]
        act = _silu(_dot(h, wg_ref[...])) * _dot(h, wu_ref[...])
        return mod_ref[0, 5:6, :] * _dot(act.astype(BF16), wd_ref[...])

    @pl.when(f == 0)
    def _():
        gain = jnp.broadcast_to(g_ref[...] * (1.0 + mod_ref[0, 4:5, :]), chunk)
        shift = jnp.broadcast_to(mod_ref[0, 3:4, :], chunk)
        for rows in row_chunks:
            x = x_ref[0, rows, :]
            rs = lax.rsqrt(jnp.mean(x * x, axis=-1, keepdims=True) + EPS)
            h_ref[rows, :] = (x * rs * gain + shift).astype(BF16)
        o_ref[0] = x_ref[0] + gated_partial()

    @pl.when(f > 0)
    def _():
        o_ref[0] += gated_partial()

    if final_norm:
        @pl.when(f == pl.num_programs(2) - 1)
        def _():
            g_fin = jnp.broadcast_to(gf_ref[...], chunk)
            for rows in row_chunks:
                y = o_ref[0, rows, :]
                o_ref[0, rows, :] = y * lax.rsqrt(jnp.mean(y * y, axis=-1, keepdims=True) + EPS) * g_fin


def _ffn(x, mod, g, w_gu, w_down, g_final, final_norm):
    bx, r, d = x.shape
    hidden = w_down.shape[0]
    tm, tf = 512, 512
    nf = hidden // tf
    return pl.pallas_call(
        functools.partial(_ffn_kernel, final_norm=final_norm),
        grid=(bx, r // tm, nf),
        in_specs=[pl.BlockSpec((1, tm, d), lambda b, t, f: (b, t, 0)),
                  pl.BlockSpec((1, N_MOD, d), lambda b, t, f: (b, 0, 0)),
                  pl.BlockSpec((1, d), lambda b, t, f: (0, 0)),
                  pl.BlockSpec((d, tf), lambda b, t, f: (0, f)),
                  pl.BlockSpec((d, tf), lambda b, t, f: (0, nf + f)),
                  pl.BlockSpec((tf, d), lambda b, t, f: (f, 0)),
                  pl.BlockSpec((1, d), lambda b, t, f: (0, 0))],
        out_specs=pl.BlockSpec((1, tm, d), lambda b, t, f: (b, t, 0)),
        out_shape=jax.ShapeDtypeStruct(x.shape, F32),
        scratch_shapes=[pltpu.VMEM((tm, d), BF16)],
        compiler_params=_params(3),
        name="ffn",
    )(x, mod, g, w_gu, w_gu, w_down, g_final)


def _rope_tables(n, dim):
    half = dim // 4
    freqs = ROPE_THETA ** (-jnp.arange(half, dtype=F32) / half)
    pos = jnp.arange(n, dtype=jnp.int32)
    ang_r = (pos // GRID_W).astype(F32)[:, None] * freqs
    ang_c = (pos % GRID_W).astype(F32)[:, None] * freqs
    cos = jnp.concatenate([jnp.cos(ang_r)] * 2 + [jnp.cos(ang_c)] * 2, axis=1)
    sin = jnp.concatenate([-jnp.sin(ang_r), jnp.sin(ang_r), -jnp.sin(ang_c), jnp.sin(ang_c)], axis=1)
    reps = 128 // dim
    return jnp.tile(cos, (1, reps)), jnp.tile(sin, (1, reps))


def _relayout_w_in(w_in):
    depth, d, _ = w_in.shape
    sizes = (MLA_RANK, MLA_RANK, MLA_ROPE, SWA_HEADS * SWA_DIM, SWA_KV_HEADS * SWA_DIM, SWA_KV_HEADS * SWA_DIM,
             GLA_HEADS * GLA_DK, GLA_HEADS * GLA_DK, GLA_HEADS * GLA_DV, 2 * GLA_GATE_RANK, GLA_HEADS * GLA_DV)
    offs = [0]
    for s in sizes:
        offs.append(offs[-1] + s)
    part = lambda i: w_in[:, :, offs[i]:offs[i + 1]]
    cq, ckv, kr, sq, sk, sv, gq, gk, gv, glr, gr = (part(i) for i in range(len(sizes)))
    zeros = lambda w: jnp.zeros((depth, d, w), w_in.dtype)
    misc_pad = 128 - MLA_ROPE - 2 * GLA_GATE_RANK
    out = jnp.concatenate([sq, sk, cq, ckv, sv, kr, glr, zeros(misc_pad), zeros(P_GLA_Q - P_MISC - 128),
                           gq, gk, gv, gr], axis=-1)
    assert out.shape[-1] == P_WIDTH
    return out.astype(BF16)


def _relayout_mla(w_uq, w_ukv):
    depth, rk, _ = w_uq.shape
    uq = w_uq.reshape(depth, rk, MLA_HEADS, MLA_QK)
    uq = jnp.concatenate([uq[..., :MLA_NOPE].reshape(depth, rk, -1), uq[..., MLA_NOPE:].reshape(depth, rk, -1)], -1)
    ukv = w_ukv.reshape(depth, rk, MLA_HEADS, MLA_NOPE + MLA_V)
    ukv = jnp.concatenate([ukv[..., :MLA_NOPE].reshape(depth, rk, -1), ukv[..., MLA_NOPE:].reshape(depth, rk, -1)], -1)
    return uq.astype(BF16), ukv.astype(BF16)


def _relayout_gate(w_f, b_f, w_b, b_b):
    depth, rank, width = w_f.shape
    wg = jnp.zeros((depth, 128, 2 * width), F32)
    wg = wg.at[:, MISC_GATE_OFF:MISC_GATE_OFF + rank, :width].set(w_f)
    wg = wg.at[:, MISC_GATE_OFF + rank:MISC_GATE_OFF + 2 * rank, width:].set(w_b)
    bg = jnp.concatenate([b_f, b_b], axis=-1).reshape(depth, 1, 2 * width)
    return wg.astype(BF16), bg


def kernel(x, c, ctx, c_ctx, w_mod, b_mod, g_mix, g_ffn, w_in, g_mla_q, g_mla_kv, w_mla_uq, w_mla_ukv,
           swa_sink, w_gla_gate_f, b_gla_gate_f, w_gla_gate_b, b_gla_gate_b, g_gla_out, w_out, w_ffn_gu,
           w_ffn_down, g_final):
    B, N, D = x.shape
    C = ctx.shape[1]
    depth = w_mod.shape[0]

    cvec = jnp.concatenate([c, c_ctx[None, :], jnp.zeros((16 - B - 1, D), F32)], axis=0)
    mod = _modulation(cvec, w_mod, b_mod).reshape(depth, 16, N_MOD, D)

    w_in_p = _relayout_w_in(w_in)
    w_uq_p, w_ukv_p = _relayout_mla(w_mla_uq, w_mla_ukv)
    wg_p, bg_p = _relayout_gate(w_gla_gate_f, b_gla_gate_f, w_gla_gate_b, b_gla_gate_b)
    w_out_b = w_out.astype(BF16)
    w_gu_b = w_ffn_gu.astype(BF16)
    w_down_b = w_ffn_down.astype(BF16)

    cos_m, sin_m = _rope_tables(N, MLA_ROPE)
    cos_s, sin_s = _rope_tables(N, SWA_DIM)
    cos_id, sin_id = jnp.ones((C, 128), F32), jnp.zeros((C, 128), F32)
    state0 = jnp.zeros((B,) + GLA_STATE, F32)
    g_fin = g_final.reshape(1, D)

    xc = ctx.reshape(1, B * C, D)
    for l in range(depth):
        last = l == depth - 1
        mod_l, mod_c = mod[l, :B], mod[l, B:B + 1]
        g_mix_l, g_ffn_l = g_mix[l].reshape(1, D), g_ffn[l].reshape(1, D)
        g_q, g_kv = g_mla_q[l].reshape(1, -1), g_mla_kv[l].reshape(1, -1)
        g_out = g_gla_out[l].reshape(1, -1)
        sink = swa_sink[l].reshape(1, -1)

        proj_l = _in_projection(x, mod_l, g_mix_l, w_in_p[l])
        proj_c = _in_projection(xc, mod_c, g_mix_l, w_in_p[l]).reshape(B, C, P_WIDTH)

        q_l, k_l, v_l = _mla_prep(proj_l, cos_m, sin_m, g_q, g_kv, w_uq_p[l], w_ukv_p[l])
        q_c, k_c, v_c = _mla_prep(proj_c, cos_id, sin_id, g_q, g_kv, w_uq_p[l], w_ukv_p[l])
        mla_l = _mla_attention(q_l, k_l, v_l, k_c, v_c)
        swa_l = _swa_attention(proj_l, proj_c, cos_s, sin_s, sink)

        of_c, ob_c, s_f, s_b = _gla_scan(proj_c, wg_p[l], bg_p[l], state0, state0)
        of_l, ob_l, _, _ = _gla_scan(proj_l, wg_p[l], bg_p[l], s_f, s_b)

        x = _out_projection(x, mod_l, mla_l, swa_l, of_l, ob_l, proj_l, g_out, w_out_b[l])
        x = _ffn(x, mod_l, g_ffn_l, w_gu_b[l], w_down_b[l], g_fin, last)

        if not last:
            mla_c = _mla_ctx_attention(q_c, k_c, v_c)
            swa_c = _swa_ctx_attention(proj_c, sink)
            flat = lambda a: a.reshape(1, B * C, a.shape[-1])
            xc = _out_projection(xc, mod_c, flat(mla_c), flat(swa_c), flat(of_c), flat(ob_c), flat(proj_c),
                                 g_out, w_out_b[l])
            xc = _ffn(xc, mod_c, g_ffn_l, w_gu_b[l], w_down_b[l], g_fin, False)
    return x
```

```python
import functools

import jax
import jax.numpy as jnp
import numpy as np
from jax import lax
from jax.experimental import pallas as pl
from jax.experimental.pallas import tpu as pltpu

F32 = jnp.float32
BF16 = jnp.bfloat16

GRID_W = 64
EPS = 1e-6
ROPE_THETA = 10000.0
LOG2E = 1.4426950408889634

MLA_HEADS = 6
MLA_RANK = 512
MLA_NOPE = 128
MLA_ROPE = 64
MLA_V = 128
MLA_QK = MLA_NOPE + MLA_ROPE
MLA_VA = MLA_V + 16

SWA_HEADS = 6
SWA_KV_HEADS = 2
SWA_GROUP = SWA_HEADS // SWA_KV_HEADS
SWA_DIM = 128
SWA_BLOCK = 128

GLA_HEADS = 4
GLA_DK = 64
GLA_DV = 128
GLA_GATE_RANK = 16
GLA_TAU = 16.0
GLA_CHUNK = 64
GLA_STATE = (GLA_HEADS // 2, GLA_DV, 2 * GLA_DK)

N_MOD = 6

P_SWA_Q = 0
P_SWA_K = 768
P_CQ = 1024
P_CKV = 1536
P_SWA_V = 2048
P_MISC = 2304
P_GLA_Q = 2560
P_GLA_K = 2816
P_GLA_V = 3072
P_GLA_R = 3584
P_WIDTH = 4096
MISC_GATE_OFF = MLA_ROPE

VMEM_LIMIT = 56 * 1024 * 1024
ROW_CHUNK = 16


def _params(n_axes, flags=None):
    return pltpu.CompilerParams(dimension_semantics=("arbitrary",) * n_axes,
                                vmem_limit_bytes=VMEM_LIMIT, flags=flags)


def _silu(x):
    return x / (1.0 + jnp.exp(-x))


def _rms(x, g):
    ms = jnp.mean(x * x, axis=-1, keepdims=True)
    return x * lax.rsqrt(ms + EPS) * g


def _dot(a, b):
    return jnp.dot(a, b, preferred_element_type=F32)


def _dot_nt(a, b):
    return lax.dot_general(a, b, (((1,), (1,)), ((), ())), preferred_element_type=F32)


def _dot_tn(a, b):
    return lax.dot_general(a, b, (((0,), (0,)), ((), ())), preferred_element_type=F32)


def _mod_kernel(c_ref, w_ref, b_ref, o_ref):
    a = _silu(c_ref[...]).astype(BF16)
    o_ref[0] = _dot(a, w_ref[0].astype(BF16)) + b_ref[0]


def _modulation(cvec, w_mod, b_mod):
    depth, d, width = w_mod.shape
    rows = cvec.shape[0]
    tn = 1024
    return pl.pallas_call(
        _mod_kernel,
        grid=(depth, width // tn),
        in_specs=[pl.BlockSpec((rows, d), lambda l, j: (0, 0)),
                  pl.BlockSpec((1, d, tn), lambda l, j: (l, 0, j)),
                  pl.BlockSpec((1, 1, tn), lambda l, j: (l, 0, j))],
        out_specs=pl.BlockSpec((1, rows, tn), lambda l, j: (l, 0, j)),
        out_shape=jax.ShapeDtypeStruct((depth, rows, width), F32),
        compiler_params=_params(2),
        name="modulation",
    )(cvec, w_mod, b_mod.reshape(depth, 1, width))


def _inproj_kernel(x_ref, mod_ref, g_ref, w_ref, o_ref):
    h = _rms(x_ref[0], g_ref[...]) * (1.0 + mod_ref[0, 1:2, :]) + mod_ref[0, 0:1, :]
    o_ref[0] = _dot(h.astype(BF16), w_ref[...]).astype(o_ref.dtype)


def _in_projection(x, mod, g, w):
    bx, r, d = x.shape
    tm, tn = 512, 2048
    return pl.pallas_call(
        _inproj_kernel,
        grid=(P_WIDTH // tn, bx, r // tm),
        in_specs=[pl.BlockSpec((1, tm, d), lambda j, b, t: (b, t, 0)),
                  pl.BlockSpec((1, N_MOD, d), lambda j, b, t: (b, 0, 0)),
                  pl.BlockSpec((1, d), lambda j, b, t: (0, 0)),
                  pl.BlockSpec((d, tn), lambda j, b, t: (0, j))],
        out_specs=pl.BlockSpec((1, tm, tn), lambda j, b, t: (b, t, j)),
        out_shape=jax.ShapeDtypeStruct((bx, r, P_WIDTH), BF16),
        compiler_params=_params(3),
        name="in_projection",
    )(x, mod, g, w)


def _rope(x, cos, sin, half):
    lane = lax.broadcasted_iota(jnp.int32, x.shape, 1)
    first = (lane % (2 * half)) < half
    width = x.shape[1]
    rot = jnp.where(first, pltpu.roll(x, width - half, 1), pltpu.roll(x, half, 1))
    return x * cos + rot * sin


def _mla_prep_kernel(cq_ref, ckv_ref, misc_ref, cos_ref, sin_ref, gq_ref, gkv_ref, wuq_ref, wukv_ref,
                     qt_ref, k_ref, vt_ref):
    scale = MLA_QK ** -0.5 * LOG2E
    cos, sin = cos_ref[...], sin_ref[...]
    half = MLA_ROPE // 4
    qf = _dot(_rms(cq_ref[0].astype(F32), gq_ref[...]).astype(BF16), wuq_ref[...])
    kvf = _dot(_rms(ckv_ref[0].astype(F32), gkv_ref[...]).astype(BF16), wukv_ref[...])
    kr = _rope(misc_ref[0].astype(F32), cos, sin, half)[:, :MLA_ROPE].astype(BF16)
    nope_w = MLA_HEADS * MLA_NOPE
    for pair in range(MLA_HEADS // 2):
        qr = _rope(qf[:, nope_w + 128 * pair: nope_w + 128 * (pair + 1)], cos, sin, half) * scale
        qrt = qr.astype(BF16).T
        for j in range(2):
            qt_ref[0, 2 * pair + j, MLA_NOPE:MLA_QK, :] = qrt[MLA_ROPE * j: MLA_ROPE * (j + 1), :]
    for h in range(MLA_HEADS):
        qt_ref[0, h, 0:MLA_NOPE, :] = (qf[:, MLA_NOPE * h: MLA_NOPE * (h + 1)] * scale).astype(BF16).T
        k_ref[0, h, :, 0:MLA_NOPE] = kvf[:, MLA_NOPE * h: MLA_NOPE * (h + 1)].astype(BF16)
        k_ref[0, h, :, MLA_NOPE:MLA_QK] = kr
        vt_ref[0, h, 0:MLA_V, :] = kvf[:, nope_w + MLA_V * h: nope_w + MLA_V * (h + 1)].astype(BF16).T
        vt_ref[0, h, MLA_V:, :] = jnp.ones((MLA_VA - MLA_V, vt_ref.shape[3]), BF16)


def _mla_prep(proj, cos, sin, g_q, g_kv, w_uq, w_ukv):
    b, r, _ = proj.shape
    tm = min(r, 1024)
    rk = MLA_RANK
    const = lambda shape: pl.BlockSpec(shape, lambda i, t: (0,) * len(shape))
    return pl.pallas_call(
        _mla_prep_kernel,
        grid=(b, r // tm),
        in_specs=[pl.BlockSpec((1, tm, rk), lambda i, t: (i, t, P_CQ // rk)),
                  pl.BlockSpec((1, tm, rk), lambda i, t: (i, t, P_CKV // rk)),
                  pl.BlockSpec((1, tm, 128), lambda i, t: (i, t, P_MISC // 128)),
                  pl.BlockSpec((tm, 128), lambda i, t: (t, 0)),
                  pl.BlockSpec((tm, 128), lambda i, t: (t, 0)),
                  const((1, rk)), const((1, rk)),
                  const(w_uq.shape), const(w_ukv.shape)],
        out_specs=[pl.BlockSpec((1, MLA_HEADS, MLA_QK, tm), lambda i, t: (i, 0, 0, t)),
                   pl.BlockSpec((1, MLA_HEADS, tm, MLA_QK), lambda i, t: (i, 0, t, 0)),
                   pl.BlockSpec((1, MLA_HEADS, MLA_VA, tm), lambda i, t: (i, 0, 0, t))],
        out_shape=[jax.ShapeDtypeStruct((b, MLA_HEADS, MLA_QK, r), BF16),
                   jax.ShapeDtypeStruct((b, MLA_HEADS, r, MLA_QK), BF16),
                   jax.ShapeDtypeStruct((b, MLA_HEADS, MLA_VA, r), BF16)],
        compiler_params=_params(2),
        name="mla_prep",
    )(proj, proj, proj, cos, sin, g_q, g_kv, w_uq, w_ukv)


def _softmax_pv(s_parts, vt_parts):
    m = functools.reduce(jnp.maximum, [jnp.max(s, axis=0, keepdims=True) for s in s_parts])
    acc = sum(_dot(vt, jnp.exp2(s - m).astype(BF16)) for s, vt in zip(s_parts, vt_parts))
    return (acc[:MLA_V] / acc[MLA_V:MLA_V + 1]).T


def _mla_attn_kernel(qt_ref, kl_ref, kc_ref, vlt_ref, vct_ref, o_ref, s_scr, m_scr, *, tk):
    @pl.when(pl.program_id(0) == 0)
    def _():
        s_scr[...] = jnp.zeros_like(s_scr)
        m_scr[...] = jnp.zeros_like(m_scr)

    n_lat = kl_ref.shape[2] // tk
    qt = qt_ref[0, 0]
    tq = qt.shape[1]
    m_prev = m_scr[...]
    m_new = jnp.full((1, tq), -jnp.inf, F32)
    acc = jnp.zeros((MLA_VA, tq), F32)
    for j in range(n_lat + 1):
        rows = slice(j * tk, (j + 1) * tk)
        p = jnp.exp2(s_scr[rows, :] - m_prev).astype(BF16)
        k = kc_ref[0, 0] if j == n_lat else kl_ref[0, 0, rows, :]
        vt = vct_ref[0, 0] if j == n_lat else vlt_ref[0, 0, :, rows]
        s = _dot(k, qt)
        s_scr[rows, :] = s
        m_new = jnp.maximum(m_new, jnp.max(s, axis=0, keepdims=True))
        acc = acc + _dot(vt, p)
    m_scr[...] = m_new
    o_ref[0] = (acc[:MLA_V] / acc[MLA_V:MLA_V + 1]).T.astype(o_ref.dtype)


def _mla_attention(qt, k_l, vt_l, k_c, vt_c):
    b, h, dq, n = qt.shape
    c = k_c.shape[2]
    tq = min(n, 1024)
    nt = n // tq
    tiles = b * h * nt

    def cur(u):
        t = jnp.minimum(u, tiles - 1)
        return t // (h * nt), (t // nt) % h, t % nt

    def prev(u):
        t = jnp.maximum(u - 1, 0)
        return t // (h * nt), (t // nt) % h, t % nt

    assert n % c == 0
    return pl.pallas_call(
        functools.partial(_mla_attn_kernel, tk=c),
        grid=(tiles + 1,),
        in_specs=[pl.BlockSpec((1, 1, dq, tq), lambda u: (cur(u)[0], cur(u)[1], 0, cur(u)[2])),
                  pl.BlockSpec((1, 1, n, dq), lambda u: (cur(u)[0], cur(u)[1], 0, 0)),
                  pl.BlockSpec((1, 1, c, dq), lambda u: (cur(u)[0], cur(u)[1], 0, 0)),
                  pl.BlockSpec((1, 1, MLA_VA, n), lambda u: (prev(u)[0], prev(u)[1], 0, 0)),
                  pl.BlockSpec((1, 1, MLA_VA, c), lambda u: (prev(u)[0], prev(u)[1], 0, 0))],
        out_specs=pl.BlockSpec((1, tq, MLA_V), lambda u: (prev(u)[0], prev(u)[2], prev(u)[1])),
        out_shape=jax.ShapeDtypeStruct((b, n, h * MLA_V), BF16),
        scratch_shapes=[pltpu.VMEM((n + c, tq), F32), pltpu.VMEM((1, tq), F32)],
        compiler_params=_params(1),
        name="mla_attention",
    )(qt, k_l, k_c, vt_l, vt_c)


def _mla_ctx_kernel(qt_ref, k_ref, vt_ref, o_ref):
    o_ref[0] = _softmax_pv([_dot(k_ref[0, 0], qt_ref[0, 0])], [vt_ref[0, 0]]).astype(o_ref.dtype)


def _mla_ctx_attention(qt, k, vt):
    b, h, dq, c = qt.shape
    return pl.pallas_call(
        _mla_ctx_kernel,
        grid=(b, h),
        in_specs=[pl.BlockSpec((1, 1, dq, c), lambda i, j: (i, j, 0, 0)),
                  pl.BlockSpec((1, 1, c, dq), lambda i, j: (i, j, 0, 0)),
                  pl.BlockSpec((1, 1, MLA_VA, c), lambda i, j: (i, j, 0, 0))],
        out_specs=pl.BlockSpec((1, c, MLA_V), lambda i, j: (i, 0, j)),
        out_shape=jax.ShapeDtypeStruct((b, c, h * MLA_V), BF16),
        compiler_params=_params(2),
        name="mla_ctx_attention",
    )(qt, k, vt)


def _swa_ctx_kernel(sink_ref, q_ref, k_ref, v_ref, o_ref):
    q = (q_ref[0].astype(F32) * SWA_DIM ** -0.5).astype(BF16)
    s = _dot_nt(q, k_ref[0].astype(BF16))
    sink = sink_ref[0, pl.program_id(1)]
    m = jnp.maximum(jnp.max(s, axis=-1, keepdims=True), sink)
    p = jnp.exp(s - m)
    l = jnp.sum(p, axis=-1, keepdims=True) + jnp.exp(sink - m)
    o_ref[0] = (_dot(p.astype(BF16), v_ref[0].astype(BF16)) / l).astype(o_ref.dtype)


def _swa_ctx_attention(proj_c, sink):
    b, c, _ = proj_c.shape
    d = SWA_DIM
    return pl.pallas_call(
        _swa_ctx_kernel,
        grid=(b, SWA_HEADS),
        in_specs=[pl.BlockSpec(memory_space=pltpu.SMEM),
                  pl.BlockSpec((1, c, d), lambda i, j: (i, 0, P_SWA_Q // d + j)),
                  pl.BlockSpec((1, c, d), lambda i, j: (i, 0, P_SWA_K // d + j // SWA_GROUP)),
                  pl.BlockSpec((1, c, d), lambda i, j: (i, 0, P_SWA_V // d + j // SWA_GROUP))],
        out_specs=pl.BlockSpec((1, c, d), lambda i, j: (i, 0, j)),
        out_shape=jax.ShapeDtypeStruct((b, c, SWA_HEADS * d), BF16),
        compiler_params=_params(2),
        name="swa_ctx_attention",
    )(sink, proj_c, proj_c, proj_c)


def _swa_window_start(t, qb, nb):
    return pl.multiple_of(jnp.clip(t * qb - 1, 0, nb - (qb + 2)) * SWA_BLOCK, SWA_BLOCK)


def _swa_kernel(sink_ref, q_ref, k_ref, kc_ref, v_ref, vc_ref, cos_ref, sin_ref, o_ref, s_scr, m_scr, *,
                qb, tiles, nt):
    u = pl.program_id(0)

    @pl.when(u == 0)
    def _():
        s_scr[...] = jnp.zeros_like(s_scr)
        m_scr[...] = jnp.zeros_like(m_scr)

    blk, d = SWA_BLOCK, SWA_DIM
    half = d // 4
    nq = qb * blk
    nw = (qb + 2) * blk
    nb = k_ref.shape[1] // blk
    ck = kc_ref.shape[1]
    cur, prev = jnp.minimum(u, tiles - 1), jnp.maximum(u - 1, 0)
    g_cur, t_cur = (cur // nt) % SWA_KV_HEADS, cur % nt
    g_prev, t_prev = (prev // nt) % SWA_KV_HEADS, prev % nt
    q_start = pl.multiple_of(t_cur * nq, blk)
    k_start = _swa_window_start(t_cur, qb, nb)
    v_start = _swa_window_start(t_prev, qb, nb)

    def sink_row(g):
        return jnp.concatenate(
            [jnp.full((1, nq), sink_ref[0, g * SWA_GROUP + i] * LOG2E, F32) for i in range(SWA_GROUP)], axis=1)

    def values_t(v):
        return jnp.concatenate([v.astype(F32).T.astype(BF16), jnp.ones((16, v.shape[0]), BF16)], axis=0)

    scale = d ** -0.5 * LOG2E
    cos_q, sin_q = cos_ref[pl.ds(q_start, nq), :], sin_ref[pl.ds(q_start, nq), :]
    q = jnp.concatenate(
        [(_rope(q_ref[0, :, d * i: d * (i + 1)].astype(F32), cos_q, sin_q, half) * scale).astype(BF16)
         for i in range(SWA_GROUP)], axis=0)
    key = lax.broadcasted_iota(jnp.int32, (ck, nq), 0)
    tok = lax.broadcasted_iota(jnp.int32, (ck, nq), 1)
    m_prev = m_scr[...]
    m_new = sink_row(g_cur)
    acc = jnp.zeros((d + 16, SWA_GROUP * nq), F32)
    n_win = nw // ck
    for j in range(n_win + 1):
        rows = slice(j * ck, (j + 1) * ck)
        p = jnp.exp2(s_scr[rows, :] - m_prev).astype(BF16)
        if j < n_win:
            vt = values_t(v_ref[0, pl.ds(v_start + j * ck, ck), :])
            kr = pl.ds(k_start + j * ck, ck)
            k = _rope(k_ref[0, kr, :].astype(F32), cos_ref[kr, :], sin_ref[kr, :], half).astype(BF16)
            dist = key - tok + (k_start + j * ck - q_start)
            valid = (dist <= blk) & (dist >= -blk)
            s = _dot_nt(k, q)
            s = jnp.concatenate(
                [jnp.where(valid, s[:, nq * i: nq * (i + 1)], -jnp.inf) for i in range(SWA_GROUP)], axis=1)
        else:
            vt = values_t(vc_ref[0])
            s = _dot_nt(kc_ref[0].astype(BF16), q)
        s_scr[rows, :] = s
        m_new = jnp.maximum(m_new, jnp.max(s, axis=0, keepdims=True))
        acc = acc + _dot(vt, p)
    m_scr[...] = m_new
    out_t = acc[:d] / (acc[d:d + 1] + jnp.exp2(sink_row(g_prev) - m_prev))
    for i in range(SWA_GROUP):
        o_ref[0, :, d * i: d * (i + 1)] = out_t[:, nq * i: nq * (i + 1)].T.astype(o_ref.dtype)


def _swa_attention(proj_l, proj_c, cos, sin, sink):
    b, n, _ = proj_l.shape
    c = proj_c.shape[1]
    blk, d = SWA_BLOCK, SWA_DIM
    qb = 2
    nq = qb * blk
    nt = n // nq
    tiles = b * SWA_KV_HEADS * nt
    assert ((qb + 2) * blk) % c == 0 and n // blk >= qb + 2

    def cur(u):
        t = jnp.minimum(u, tiles - 1)
        return t // (SWA_KV_HEADS * nt), (t // nt) % SWA_KV_HEADS, t % nt

    def prev(u):
        t = jnp.maximum(u - 1, 0)
        return t // (SWA_KV_HEADS * nt), (t // nt) % SWA_KV_HEADS, t % nt

    kcol, vcol = P_SWA_K // d, P_SWA_V // d
    return pl.pallas_call(
        functools.partial(_swa_kernel, qb=qb, tiles=tiles, nt=nt),
        grid=(tiles + 1,),
        in_specs=[pl.BlockSpec(memory_space=pltpu.SMEM),
                  pl.BlockSpec((1, nq, SWA_GROUP * d), lambda u: (cur(u)[0], cur(u)[2], cur(u)[1])),
                  pl.BlockSpec((1, n, d), lambda u: (cur(u)[0], 0, kcol + cur(u)[1])),
                  pl.BlockSpec((1, c, d), lambda u: (cur(u)[0], 0, kcol + cur(u)[1])),
                  pl.BlockSpec((1, n, d), lambda u: (prev(u)[0], 0, vcol + prev(u)[1])),
                  pl.BlockSpec((1, c, d), lambda u: (prev(u)[0], 0, vcol + prev(u)[1])),
                  pl.BlockSpec((n, d), lambda u: (0, 0)),
                  pl.BlockSpec((n, d), lambda u: (0, 0))],
        out_specs=pl.BlockSpec((1, nq, SWA_GROUP * d), lambda u: (prev(u)[0], prev(u)[2], prev(u)[1])),
        out_shape=jax.ShapeDtypeStruct((b, n, SWA_HEADS * d), BF16),
        scratch_shapes=[pltpu.VMEM(((qb + 2) * blk + c, SWA_GROUP * nq), F32),
                        pltpu.VMEM((1, SWA_GROUP * nq), F32)],
        compiler_params=_params(1),
        name="swa_attention",
    )(sink, proj_l, proj_l, proj_c, proj_l, proj_c, cos, sin)


def _log_sigmoid(x):
    return jnp.minimum(x, 0.0) - jnp.log(1.0 + jnp.exp(-jnp.abs(x)))


def _gla_block_constants(rows):
    t = np.arange(rows)
    ti, tj = t[:, None], t[None, :]
    ci, cj = ti // GLA_CHUNK, tj // GLA_CHUNK
    ones, sel = [], []
    for reverse in (False, True):
        upto = (tj >= ti) if reverse else (tj <= ti)
        visible = (tj > ti) if reverse else (tj <= ti)
        earlier = (cj > ci) if reverse else (cj < ci)
        ones.append(np.concatenate([(ci == cj) & upto, upto], axis=0))
        sel.append(np.where((ci == cj) & visible, 1, np.where(earlier, 2, 0)))
    return jnp.asarray(np.stack(ones), BF16), jnp.asarray(np.stack(sel), jnp.int32)


def _gla_factors(q_ref, k_ref, misc_ref, i, wg, bg, ones, sel, reverse):
    rows = q_ref.shape[1]
    L = GLA_CHUNK
    assert rows == 4 * L
    width = GLA_HEADS * GLA_DK
    la = _log_sigmoid(_dot(misc_ref[i].astype(BF16), wg) + bg) * (1.0 / GLA_TAU)
    hi = la.astype(BF16)
    lo = (la - hi.astype(F32)).astype(BF16)
    sums = _dot(ones, jnp.concatenate([hi, lo], axis=1))
    sums = sums[:, :width] + sums[:, width:]
    b_in, b_abs = sums[:rows], sums[rows:]
    if reverse:
        g_mid, g_end = b_abs[rows // 2:rows // 2 + 1], b_abs[0:1]
    else:
        g_mid, g_end = b_abs[rows // 2 - 1:rows // 2], b_abs[rows - 1:rows]
    diag, cross = sel == 1, sel == 2
    q = q_ref[i].astype(F32) * GLA_DK ** -0.5
    k = k_ref[i].astype(F32)
    q_d, k_d = q * jnp.exp(b_in), k * jnp.exp(-b_in)
    q_x, k_x = q * jnp.exp(b_abs - g_mid), k * jnp.exp(g_mid - b_abs)
    q_s, k_s = q * jnp.exp(b_abs), k * jnp.exp(g_end - b_abs)
    decay = jnp.exp(g_end)
    return q_d, k_d, q_x, k_x, q_s, k_s, decay, diag, cross


def _gla_apply(factors, v_ref, state_ref, o_ref, i):
    q_d, k_d, q_x, k_x, q_s, k_s, decay, diag, cross = factors
    rows = q_d.shape[0]
    lane = lax.broadcasted_iota(jnp.int32, (rows, 2 * GLA_DK), 1)
    lane_s = lax.broadcasted_iota(jnp.int32, (GLA_DV, 2 * GLA_DK), 1)
    for pair in range(GLA_HEADS // 2):
        ps = slice(2 * GLA_DK * pair, 2 * GLA_DK * (pair + 1))
        kd_p, kx_p, ks_p = k_d[:, ps].astype(BF16), k_x[:, ps].astype(BF16), k_s[:, ps].astype(BF16)
        st = state_ref[i, pair]
        st_b = st.astype(BF16)
        ds = []
        for j in range(2):
            h = 2 * pair + j
            vs = slice(h * GLA_DV, (h + 1) * GLA_DV)
            mine = (lane // GLA_DK) == j
            qd_h = jnp.where(mine, q_d[:, ps], 0.0).astype(BF16)
            qx_h = jnp.where(mine, q_x[:, ps], 0.0).astype(BF16)
            qs_h = jnp.where(mine, q_s[:, ps], 0.0).astype(BF16)
            vh = v_ref[i, :, vs].astype(BF16)
            a = jnp.where(diag, _dot_nt(qd_h, kd_p), jnp.where(cross, _dot_nt(qx_h, kx_p), 0.0))
            o_ref[i, :, vs] = (_dot(a.astype(BF16), vh) + _dot_nt(qs_h, st_b)).astype(o_ref.dtype)
            ds.append(_dot_tn(vh, ks_p))
        state_ref[i, pair] = decay[:, ps] * st + jnp.where(lane_s < GLA_DK, ds[0], ds[1])


def _gla_kernel(qf_ref, kf_ref, vf_ref, mf_ref, qb_ref, kb_ref, vb_ref, mb_ref, wg_ref, bg_ref, ones_ref, sel_ref,
                sf0_ref, sb0_ref, of_ref, ob_ref, sf_ref, sb_ref, stf, stb):
    t = pl.program_id(1)

    @pl.when(t == 0)
    def _():
        stf[...] = sf0_ref[...]
        stb[...] = sb0_ref[...]

    width = GLA_HEADS * GLA_DK
    entries = range(qf_ref.shape[0])
    fwd = [_gla_factors(qf_ref, kf_ref, mf_ref, i, wg_ref[:, :width], bg_ref[:, :width], ones_ref[0], sel_ref[0],
                        False) for i in entries]
    bwd = [_gla_factors(qb_ref, kb_ref, mb_ref, i, wg_ref[:, width:], bg_ref[:, width:], ones_ref[1], sel_ref[1],
                        True) for i in entries]
    for i in entries:
        _gla_apply(fwd[i], vf_ref, stf, of_ref, i)
        _gla_apply(bwd[i], vb_ref, stb, ob_ref, i)

    @pl.when(t == pl.num_programs(1) - 1)
    def _():
        sf_ref[...] = stf[...]
        sb_ref[...] = stb[...]


def _gla_scan(proj, wg, bg, sf0, sb0):
    b, r, _ = proj.shape
    tm = 256
    nt = r // tm
    kw, vw = GLA_HEADS * GLA_DK, GLA_HEADS * GLA_DV
    fwd = lambda t: t
    bwd = lambda t: nt - 1 - t

    nb = 2 if b % 2 == 0 else 1
    ones, sel = _gla_block_constants(tm)

    def specs(order):
        return [pl.BlockSpec((nb, tm, kw), lambda i, t: (i, order(t), P_GLA_Q // kw)),
                pl.BlockSpec((nb, tm, kw), lambda i, t: (i, order(t), P_GLA_K // kw)),
                pl.BlockSpec((nb, tm, vw), lambda i, t: (i, order(t), P_GLA_V // vw)),
                pl.BlockSpec((nb, tm, 128), lambda i, t: (i, order(t), P_MISC // 128))]

    state_spec = pl.BlockSpec((nb,) + GLA_STATE, lambda i, t: (i, 0, 0, 0))
    state_shape = jax.ShapeDtypeStruct((b,) + GLA_STATE, F32)
    return pl.pallas_call(
        _gla_kernel,
        grid=(b // nb, nt),
        in_specs=specs(fwd) + specs(bwd) + [
            pl.BlockSpec(wg.shape, lambda i, t: (0, 0)),
            pl.BlockSpec(bg.shape, lambda i, t: (0, 0)),
            pl.BlockSpec(ones.shape, lambda i, t: (0, 0, 0)),
            pl.BlockSpec(sel.shape, lambda i, t: (0, 0, 0)),
            state_spec, state_spec],
        out_specs=[pl.BlockSpec((nb, tm, vw), lambda i, t: (i, fwd(t), 0)),
                   pl.BlockSpec((nb, tm, vw), lambda i, t: (i, bwd(t), 0)),
                   state_spec, state_spec],
        out_shape=[jax.ShapeDtypeStruct((b, r, vw), BF16), jax.ShapeDtypeStruct((b, r, vw), BF16),
                   state_shape, state_shape],
        scratch_shapes=[pltpu.VMEM((nb,) + GLA_STATE, F32), pltpu.VMEM((nb,) + GLA_STATE, F32)],
        compiler_params=_params(2),
        name="gla_scan",
    )(proj, proj, proj, proj, proj, proj, proj, proj, wg, bg, ones, sel, sf0, sb0)


def _outproj_kernel(x_ref, mod_ref, mla_ref, swa_ref, of_ref, ob_ref, r_ref, g_ref, w_ref, o_ref):
    o = of_ref[0].astype(F32) + ob_ref[0].astype(F32)
    gate = _silu(r_ref[0].astype(F32))
    parts = []
    for h in range(GLA_HEADS):
        vs = slice(h * GLA_DV, (h + 1) * GLA_DV)
        parts.append((_rms(o[:, vs], g_ref[:, vs]) * gate[:, vs]).astype(BF16))
    gla = jnp.concatenate(parts, axis=1)
    a, b = mla_ref.shape[2], mla_ref.shape[2] + swa_ref.shape[2]
    mix = _dot(mla_ref[0], w_ref[0:a, :]) + _dot(swa_ref[0], w_ref[a:b, :]) + _dot(gla, w_ref[b:, :])
    o_ref[0] = x_ref[0] + mod_ref[0, 2:3, :] * mix


def _out_projection(x, mod, mla, swa, o_f, o_b, proj, g_out, w_out):
    bx, r, d = x.shape
    tm = 512
    row = lambda width: pl.BlockSpec((1, tm, width), lambda b, t: (b, t, 0))
    gw = GLA_HEADS * GLA_DV
    return pl.pallas_call(
        _outproj_kernel,
        grid=(bx, r // tm),
        in_specs=[row(d),
                  pl.BlockSpec((1, N_MOD, d), lambda b, t: (b, 0, 0)),
                  row(mla.shape[2]), row(swa.shape[2]), row(gw), row(gw),
                  pl.BlockSpec((1, tm, gw), lambda b, t: (b, t, P_GLA_R // gw)),
                  pl.BlockSpec((1, gw), lambda b, t: (0, 0)),
                  pl.BlockSpec(w_out.shape, lambda b, t: (0, 0))],
        out_specs=row(d),
        out_shape=jax.ShapeDtypeStruct(x.shape, F32),
        compiler_params=_params(2),
        name="out_projection",
    )(x, mod, mla, swa, o_f, o_b, proj, g_out, w_out)


def _ffn_kernel(x_ref, mod_ref, g_ref, wg_ref, wu_ref, wd_ref, gf_ref, o_ref, h_ref, *, final_norm):
    f = pl.program_id(2)
    row_chunks = [slice(r, r + ROW_CHUNK) for r in range(0, x_ref.shape[1], ROW_CHUNK)]
    chunk = (ROW_CHUNK, x_ref.shape[2])

    def gated_partial():
        h = h_ref[...]
        act = _silu(_dot(h, wg_ref[...])) * _dot(h, wu_ref[...])
        return mod_ref[0, 5:6, :] * _dot(act.astype(BF16), wd_ref[...])

    @pl.when(f == 0)
    def _():
        gain = jnp.broadcast_to(g_ref[...] * (1.0 + mod_ref[0, 4:5, :]), chunk)
        shift = jnp.broadcast_to(mod_ref[0, 3:4, :], chunk)
        for rows in row_chunks:
            x = x_ref[0, rows, :]
            rs = lax.rsqrt(jnp.mean(x * x, axis=-1, keepdims=True) + EPS)
            h_ref[rows, :] = (x * rs * gain + shift).astype(BF16)
        o_ref[0] = x_ref[0] + gated_partial()

    @pl.when(f > 0)
    def _():
        o_ref[0] += gated_partial()

    if final_norm:
        @pl.when(f == pl.num_programs(2) - 1)
        def _():
            g_fin = jnp.broadcast_to(gf_ref[...], chunk)
            for rows in row_chunks:
                y = o_ref[0, rows, :]
                o_ref[0, rows, :] = y * lax.rsqrt(jnp.mean(y * y, axis=-1, keepdims=True) + EPS) * g_fin


def _ffn(x, mod, g, w_gu, w_down, g_final, final_norm):
    bx, r, d = x.shape
    hidden = w_down.shape[0]
    tm, tf = 512, 512
    nf = hidden // tf
    return pl.pallas_call(
        functools.partial(_ffn_kernel, final_norm=final_norm),
        grid=(bx, r // tm, nf),
        in_specs=[pl.BlockSpec((1, tm, d), lambda b, t, f: (b, t, 0)),
                  pl.BlockSpec((1, N_MOD, d), lambda b, t, f: (b, 0, 0)),
                  pl.BlockSpec((1, d), lambda b, t, f: (0, 0)),
                  pl.BlockSpec((d, tf), lambda b, t, f: (0, f)),
                  pl.BlockSpec((d, tf), lambda b, t, f: (0, nf + f)),
                  pl.BlockSpec((tf, d), lambda b, t, f: (f, 0)),
                  pl.BlockSpec((1, d), lambda b, t, f: (0, 0))],
        out_specs=pl.BlockSpec((1, tm, d), lambda b, t, f: (b, t, 0)),
        out_shape=jax.ShapeDtypeStruct(x.shape, F32),
        scratch_shapes=[pltpu.VMEM((tm, d), BF16)],
        compiler_params=_params(3),
        name="ffn",
    )(x, mod, g, w_gu, w_gu, w_down, g_final)


def _rope_tables(n, dim):
    half = dim // 4
    freqs = ROPE_THETA ** (-jnp.arange(half, dtype=F32) / half)
    pos = jnp.arange(n, dtype=jnp.int32)
    ang_r = (pos // GRID_W).astype(F32)[:, None] * freqs
    ang_c = (pos % GRID_W).astype(F32)[:, None] * freqs
    cos = jnp.concatenate([jnp.cos(ang_r)] * 2 + [jnp.cos(ang_c)] * 2, axis=1)
    sin = jnp.concatenate([-jnp.sin(ang_r), jnp.sin(ang_r), -jnp.sin(ang_c), jnp.sin(ang_c)], axis=1)
    reps = 128 // dim
    return jnp.tile(cos, (1, reps)), jnp.tile(sin, (1, reps))


def _relayout_w_in(w_in):
    depth, d, _ = w_in.shape
    sizes = (MLA_RANK, MLA_RANK, MLA_ROPE, SWA_HEADS * SWA_DIM, SWA_KV_HEADS * SWA_DIM, SWA_KV_HEADS * SWA_DIM,
             GLA_HEADS * GLA_DK, GLA_HEADS * GLA_DK, GLA_HEADS * GLA_DV, 2 * GLA_GATE_RANK, GLA_HEADS * GLA_DV)
    offs = [0]
    for s in sizes:
        offs.append(offs[-1] + s)
    w = w_in.astype(BF16)
    span = lambda i, j: w[:, :, offs[i]:offs[j]]
    pad = jnp.zeros((depth, d, P_GLA_Q - P_MISC - MLA_ROPE - 2 * GLA_GATE_RANK), BF16)
    out = jnp.concatenate([span(3, 5), span(0, 2), span(5, 6), span(2, 3), span(9, 10), pad,
                           span(6, 9), span(10, 11)], axis=-1)
    assert out.shape[-1] == P_WIDTH
    return out


def _relayout_mla(w_uq, w_ukv):
    depth, rk, _ = w_uq.shape
    uq = w_uq.reshape(depth, rk, MLA_HEADS, MLA_QK)
    uq = jnp.concatenate([uq[..., :MLA_NOPE].reshape(depth, rk, -1), uq[..., MLA_NOPE:].reshape(depth, rk, -1)], -1)
    ukv = w_ukv.reshape(depth, rk, MLA_HEADS, MLA_NOPE + MLA_V)
    ukv = jnp.concatenate([ukv[..., :MLA_NOPE].reshape(depth, rk, -1), ukv[..., MLA_NOPE:].reshape(depth, rk, -1)], -1)
    return uq.astype(BF16), ukv.astype(BF16)


def _relayout_gate(w_f, b_f, w_b, b_b):
    depth, rank, width = w_f.shape
    wg = jnp.zeros((depth, 128, 2 * width), F32)
    wg = wg.at[:, MISC_GATE_OFF:MISC_GATE_OFF + rank, :width].set(w_f)
    wg = wg.at[:, MISC_GATE_OFF + rank:MISC_GATE_OFF + 2 * rank, width:].set(w_b)
    bg = jnp.concatenate([b_f, b_b], axis=-1).reshape(depth, 1, 2 * width)
    return wg.astype(BF16), bg


def kernel(x, c, ctx, c_ctx, w_mod, b_mod, g_mix, g_ffn, w_in, g_mla_q, g_mla_kv, w_mla_uq, w_mla_ukv,
           swa_sink, w_gla_gate_f, b_gla_gate_f, w_gla_gate_b, b_gla_gate_b, g_gla_out, w_out, w_ffn_gu,
           w_ffn_down, g_final):
    B, N, D = x.shape
    C = ctx.shape[1]
    depth = w_mod.shape[0]

    cvec = jnp.concatenate([c, c_ctx[None, :], jnp.zeros((16 - B - 1, D), F32)], axis=0)
    mod = _modulation(cvec, w_mod, b_mod).reshape(depth, 16, N_MOD, D)

    w_in_p = _relayout_w_in(w_in)
    w_uq_p, w_ukv_p = _relayout_mla(w_mla_uq, w_mla_ukv)
    wg_p, bg_p = _relayout_gate(w_gla_gate_f, b_gla_gate_f, w_gla_gate_b, b_gla_gate_b)
    w_out_b = w_out.astype(BF16)
    w_gu_b = w_ffn_gu.astype(BF16)
    w_down_b = w_ffn_down.astype(BF16)

    cos_m, sin_m = _rope_tables(N, MLA_ROPE)
    cos_s, sin_s = _rope_tables(N, SWA_DIM)
    cos_id, sin_id = jnp.ones((C, 128), F32), jnp.zeros((C, 128), F32)
    state0 = jnp.zeros((B,) + GLA_STATE, F32)
    g_fin = g_final.reshape(1, D)

    xc = ctx.reshape(1, B * C, D)
    for l in range(depth):
        last = l == depth - 1
        mod_l, mod_c = mod[l, :B], mod[l, B:B + 1]
        g_mix_l, g_ffn_l = g_mix[l].reshape(1, D), g_ffn[l].reshape(1, D)
        g_q, g_kv = g_mla_q[l].reshape(1, -1), g_mla_kv[l].reshape(1, -1)
        g_out = g_gla_out[l].reshape(1, -1)
        sink = swa_sink[l].reshape(1, -1)

        proj_l = _in_projection(x, mod_l, g_mix_l, w_in_p[l])
        proj_c = _in_projection(xc, mod_c, g_mix_l, w_in_p[l]).reshape(B, C, P_WIDTH)

        q_l, k_l, v_l = _mla_prep(proj_l, cos_m, sin_m, g_q, g_kv, w_uq_p[l], w_ukv_p[l])
        q_c, k_c, v_c = _mla_prep(proj_c, cos_id, sin_id, g_q, g_kv, w_uq_p[l], w_ukv_p[l])
        mla_l = _mla_attention(q_l, k_l, v_l, k_c, v_c)
        swa_l = _swa_attention(proj_l, proj_c, cos_s, sin_s, sink)

        of_c, ob_c, s_f, s_b = _gla_scan(proj_c, wg_p[l], bg_p[l], state0, state0)
        of_l, ob_l, _, _ = _gla_scan(proj_l, wg_p[l], bg_p[l], s_f, s_b)

        x = _out_projection(x, mod_l, mla_l, swa_l, of_l, ob_l, proj_l, g_out, w_out_b[l])
        x = _ffn(x, mod_l, g_ffn_l, w_gu_b[l], w_down_b[l], g_fin, last)

        if not last:
            mla_c = _mla_ctx_attention(q_c, k_c, v_c)
            swa_c = _swa_ctx_attention(proj_c, sink)
            flat = lambda a: a.reshape(1, B * C, a.shape[-1])
            xc = _out_projection(xc, mod_c, flat(mla_c), flat(swa_c), flat(of_c), flat(ob_c), flat(proj_c),
                                 g_out, w_out_b[l])
            xc = _ffn(xc, mod_c, g_ffn_l, w_gu_b[l], w_down_b[l], g_fin, False)
    return x
```

```python
import functools

import jax
import jax.numpy as jnp
import numpy as np
from jax import lax
from jax.experimental import pallas as pl
from jax.experimental.pallas import tpu as pltpu

F32 = jnp.float32
BF16 = jnp.bfloat16

GRID_W = 64
EPS = 1e-6
ROPE_THETA = 10000.0
LOG2E = 1.4426950408889634

MLA_HEADS = 6
MLA_RANK = 512
MLA_NOPE = 128
MLA_ROPE = 64
MLA_V = 128
MLA_QK = MLA_NOPE + MLA_ROPE
MLA_VA = MLA_V + 16

SWA_HEADS = 6
SWA_KV_HEADS = 2
SWA_GROUP = SWA_HEADS // SWA_KV_HEADS
SWA_DIM = 128
SWA_BLOCK = 128

GLA_HEADS = 4
GLA_DK = 64
GLA_DV = 128
GLA_GATE_RANK = 16
GLA_TAU = 16.0
GLA_CHUNK = 64
GLA_STATE = (GLA_HEADS // 2, GLA_DV, 2 * GLA_DK)

N_MOD = 6

P_SWA_Q = 0
P_SWA_K = 768
P_CQ = 1024
P_CKV = 1536
P_SWA_V = 2048
P_MISC = 2304
P_GLA_Q = 2560
P_GLA_K = 2816
P_GLA_V = 3072
P_GLA_R = 3584
P_WIDTH = 4096
MISC_GATE_OFF = MLA_ROPE

VMEM_LIMIT = 56 * 1024 * 1024
ROW_CHUNK = 16


def _params(n_axes, flags=None):
    return pltpu.CompilerParams(dimension_semantics=("arbitrary",) * n_axes,
                                vmem_limit_bytes=VMEM_LIMIT, flags=flags)


def _silu(x):
    return x / (1.0 + jnp.exp(-x))


def _rms(x, g):
    ms = jnp.mean(x * x, axis=-1, keepdims=True)
    return x * lax.rsqrt(ms + EPS) * g


def _dot(a, b):
    return jnp.dot(a, b, preferred_element_type=F32)


def _dot_nt(a, b):
    return lax.dot_general(a, b, (((1,), (1,)), ((), ())), preferred_element_type=F32)


def _dot_tn(a, b):
    return lax.dot_general(a, b, (((0,), (0,)), ((), ())), preferred_element_type=F32)


def _mod_kernel(c_ref, w_ref, b_ref, o_ref):
    a = _silu(c_ref[...]).astype(BF16)
    o_ref[0] = _dot(a, w_ref[0].astype(BF16)) + b_ref[0]


def _modulation(cvec, w_mod, b_mod):
    depth, d, width = w_mod.shape
    rows = cvec.shape[0]
    tn = 1024
    return pl.pallas_call(
        _mod_kernel,
        grid=(depth, width // tn),
        in_specs=[pl.BlockSpec((rows, d), lambda l, j: (0, 0)),
                  pl.BlockSpec((1, d, tn), lambda l, j: (l, 0, j)),
                  pl.BlockSpec((1, 1, tn), lambda l, j: (l, 0, j))],
        out_specs=pl.BlockSpec((1, rows, tn), lambda l, j: (l, 0, j)),
        out_shape=jax.ShapeDtypeStruct((depth, rows, width), F32),
        compiler_params=_params(2),
        name="modulation",
    )(cvec, w_mod, b_mod.reshape(depth, 1, width))


def _inproj_kernel(x_ref, mod_ref, g_ref, w_ref, o_ref):
    h = _rms(x_ref[0], g_ref[...]) * (1.0 + mod_ref[0, 1:2, :]) + mod_ref[0, 0:1, :]
    o_ref[0] = _dot(h.astype(BF16), w_ref[...]).astype(o_ref.dtype)


def _in_projection(x, mod, g, w, layer):
    bx, r, d = x.shape
    tm, tn = 512, 2048
    return pl.pallas_call(
        _inproj_kernel,
        grid=(P_WIDTH // tn, bx, r // tm),
        in_specs=[pl.BlockSpec((1, tm, d), lambda j, b, t: (b, t, 0)),
                  pl.BlockSpec((1, N_MOD, d), lambda j, b, t: (b, 0, 0)),
                  pl.BlockSpec((1, d), lambda j, b, t: (0, 0)),
                  pl.BlockSpec((None, d, tn), lambda j, b, t: (layer, 0, j))],
        out_specs=pl.BlockSpec((1, tm, tn), lambda j, b, t: (b, t, j)),
        out_shape=jax.ShapeDtypeStruct((bx, r, P_WIDTH), BF16),
        compiler_params=_params(3),
        name="in_projection",
    )(x, mod, g, w)


def _rope(x, cos, sin, half):
    lane = lax.broadcasted_iota(jnp.int32, x.shape, 1)
    first = (lane % (2 * half)) < half
    width = x.shape[1]
    rot = jnp.where(first, pltpu.roll(x, width - half, 1), pltpu.roll(x, half, 1))
    return x * cos + rot * sin


def _mla_prep_kernel(cq_ref, ckv_ref, misc_ref, cos_ref, sin_ref, gq_ref, gkv_ref, wuq_ref, wukv_ref,
                     qt_ref, k_ref, vt_ref):
    scale = MLA_QK ** -0.5 * LOG2E
    cos, sin = cos_ref[...], sin_ref[...]
    half = MLA_ROPE // 4
    qf = _dot(_rms(cq_ref[0].astype(F32), gq_ref[...]).astype(BF16), wuq_ref[...])
    kvf = _dot(_rms(ckv_ref[0].astype(F32), gkv_ref[...]).astype(BF16), wukv_ref[...])
    kr = _rope(misc_ref[0].astype(F32), cos, sin, half)[:, :MLA_ROPE].astype(BF16)
    nope_w = MLA_HEADS * MLA_NOPE
    for pair in range(MLA_HEADS // 2):
        qr = _rope(qf[:, nope_w + 128 * pair: nope_w + 128 * (pair + 1)], cos, sin, half) * scale
        qrt = qr.astype(BF16).T
        for j in range(2):
            qt_ref[0, 2 * pair + j, MLA_NOPE:MLA_QK, :] = qrt[MLA_ROPE * j: MLA_ROPE * (j + 1), :]
    for h in range(MLA_HEADS):
        qt_ref[0, h, 0:MLA_NOPE, :] = (qf[:, MLA_NOPE * h: MLA_NOPE * (h + 1)] * scale).astype(BF16).T
        k_ref[0, h, :, 0:MLA_NOPE] = kvf[:, MLA_NOPE * h: MLA_NOPE * (h + 1)].astype(BF16)
        k_ref[0, h, :, MLA_NOPE:MLA_QK] = kr
        vt_ref[0, h, 0:MLA_V, :] = kvf[:, nope_w + MLA_V * h: nope_w + MLA_V * (h + 1)].astype(BF16).T
        vt_ref[0, h, MLA_V:, :] = jnp.ones((MLA_VA - MLA_V, vt_ref.shape[3]), BF16)


def _mla_prep(proj, cos, sin, g_q, g_kv, w_uq, w_ukv, layer):
    b, r, _ = proj.shape
    tm = min(r, 1024)
    rk = MLA_RANK
    const = lambda shape: pl.BlockSpec(shape, lambda i, t: (0,) * len(shape))
    stacked = lambda w: pl.BlockSpec((None,) + w.shape[1:], lambda i, t: (layer, 0, 0))
    return pl.pallas_call(
        _mla_prep_kernel,
        grid=(b, r // tm),
        in_specs=[pl.BlockSpec((1, tm, rk), lambda i, t: (i, t, P_CQ // rk)),
                  pl.BlockSpec((1, tm, rk), lambda i, t: (i, t, P_CKV // rk)),
                  pl.BlockSpec((1, tm, 128), lambda i, t: (i, t, P_MISC // 128)),
                  pl.BlockSpec((tm, 128), lambda i, t: (t, 0)),
                  pl.BlockSpec((tm, 128), lambda i, t: (t, 0)),
                  const((1, rk)), const((1, rk)),
                  stacked(w_uq), stacked(w_ukv)],
        out_specs=[pl.BlockSpec((1, MLA_HEADS, MLA_QK, tm), lambda i, t: (i, 0, 0, t)),
                   pl.BlockSpec((1, MLA_HEADS, tm, MLA_QK), lambda i, t: (i, 0, t, 0)),
                   pl.BlockSpec((1, MLA_HEADS, MLA_VA, tm), lambda i, t: (i, 0, 0, t))],
        out_shape=[jax.ShapeDtypeStruct((b, MLA_HEADS, MLA_QK, r), BF16),
                   jax.ShapeDtypeStruct((b, MLA_HEADS, r, MLA_QK), BF16),
                   jax.ShapeDtypeStruct((b, MLA_HEADS, MLA_VA, r), BF16)],
        compiler_params=_params(2),
        name="mla_prep",
    )(proj, proj, proj, cos, sin, g_q, g_kv, w_uq, w_ukv)


def _softmax_pv(s_parts, vt_parts):
    m = functools.reduce(jnp.maximum, [jnp.max(s, axis=0, keepdims=True) for s in s_parts])
    acc = sum(_dot(vt, jnp.exp2(s - m).astype(BF16)) for s, vt in zip(s_parts, vt_parts))
    return (acc[:MLA_V] / acc[MLA_V:MLA_V + 1]).T


def _mla_attn_kernel(qt_ref, kl_ref, kc_ref, vlt_ref, vct_ref, o_ref, s_scr, m_scr, *, tk):
    @pl.when(pl.program_id(0) == 0)
    def _():
        s_scr[...] = jnp.zeros_like(s_scr)
        m_scr[...] = jnp.zeros_like(m_scr)

    n_lat = kl_ref.shape[2] // tk
    qt = qt_ref[0, 0]
    tq = qt.shape[1]
    m_prev = m_scr[...]
    m_new = jnp.full((1, tq), -jnp.inf, F32)
    acc = jnp.zeros((MLA_VA, tq), F32)
    for j in range(n_lat + 1):
        rows = slice(j * tk, (j + 1) * tk)
        p = jnp.exp2(s_scr[rows, :] - m_prev).astype(BF16)
        k = kc_ref[0, 0] if j == n_lat else kl_ref[0, 0, rows, :]
        vt = vct_ref[0, 0] if j == n_lat else vlt_ref[0, 0, :, rows]
        s = _dot(k, qt)
        s_scr[rows, :] = s
        m_new = jnp.maximum(m_new, jnp.max(s, axis=0, keepdims=True))
        acc = acc + _dot(vt, p)
    m_scr[...] = m_new
    o_ref[0] = (acc[:MLA_V] / acc[MLA_V:MLA_V + 1]).T.astype(o_ref.dtype)


def _mla_attention(qt, k_l, vt_l, k_c, vt_c):
    b, h, dq, n = qt.shape
    c = k_c.shape[2]
    tq = min(n, 1024)
    nt = n // tq
    tiles = b * h * nt

    def cur(u):
        t = jnp.minimum(u, tiles - 1)
        return t // (h * nt), (t // nt) % h, t % nt

    def prev(u):
        t = jnp.maximum(u - 1, 0)
        return t // (h * nt), (t // nt) % h, t % nt

    assert n % c == 0
    return pl.pallas_call(
        functools.partial(_mla_attn_kernel, tk=c),
        grid=(tiles + 1,),
        in_specs=[pl.BlockSpec((1, 1, dq, tq), lambda u: (cur(u)[0], cur(u)[1], 0, cur(u)[2])),
                  pl.BlockSpec((1, 1, n, dq), lambda u: (cur(u)[0], cur(u)[1], 0, 0)),
                  pl.BlockSpec((1, 1, c, dq), lambda u: (cur(u)[0], cur(u)[1], 0, 0)),
                  pl.BlockSpec((1, 1, MLA_VA, n), lambda u: (prev(u)[0], prev(u)[1], 0, 0)),
                  pl.BlockSpec((1, 1, MLA_VA, c), lambda u: (prev(u)[0], prev(u)[1], 0, 0))],
        out_specs=pl.BlockSpec((1, tq, MLA_V), lambda u: (prev(u)[0], prev(u)[2], prev(u)[1])),
        out_shape=jax.ShapeDtypeStruct((b, n, h * MLA_V), BF16),
        scratch_shapes=[pltpu.VMEM((n + c, tq), F32), pltpu.VMEM((1, tq), F32)],
        compiler_params=_params(1),
        name="mla_attention",
    )(qt, k_l, k_c, vt_l, vt_c)


def _mla_ctx_kernel(qt_ref, k_ref, vt_ref, o_ref):
    o_ref[0] = _softmax_pv([_dot(k_ref[0, 0], qt_ref[0, 0])], [vt_ref[0, 0]]).astype(o_ref.dtype)


def _mla_ctx_attention(qt, k, vt):
    b, h, dq, c = qt.shape
    return pl.pallas_call(
        _mla_ctx_kernel,
        grid=(b, h),
        in_specs=[pl.BlockSpec((1, 1, dq, c), lambda i, j: (i, j, 0, 0)),
                  pl.BlockSpec((1, 1, c, dq), lambda i, j: (i, j, 0, 0)),
                  pl.BlockSpec((1, 1, MLA_VA, c), lambda i, j: (i, j, 0, 0))],
        out_specs=pl.BlockSpec((1, c, MLA_V), lambda i, j: (i, 0, j)),
        out_shape=jax.ShapeDtypeStruct((b, c, h * MLA_V), BF16),
        compiler_params=_params(2),
        name="mla_ctx_attention",
    )(qt, k, vt)


def _swa_ctx_kernel(sink_ref, q_ref, k_ref, v_ref, o_ref):
    q = (q_ref[0].astype(F32) * SWA_DIM ** -0.5).astype(BF16)
    s = _dot_nt(q, k_ref[0].astype(BF16))
    sink = sink_ref[0, pl.program_id(1)]
    m = jnp.maximum(jnp.max(s, axis=-1, keepdims=True), sink)
    p = jnp.exp(s - m)
    l = jnp.sum(p, axis=-1, keepdims=True) + jnp.exp(sink - m)
    o_ref[0] = (_dot(p.astype(BF16), v_ref[0].astype(BF16)) / l).astype(o_ref.dtype)


def _swa_ctx_attention(proj_c, sink):
    b, c, _ = proj_c.shape
    d = SWA_DIM
    return pl.pallas_call(
        _swa_ctx_kernel,
        grid=(b, SWA_HEADS),
        in_specs=[pl.BlockSpec(memory_space=pltpu.SMEM),
                  pl.BlockSpec((1, c, d), lambda i, j: (i, 0, P_SWA_Q // d + j)),
                  pl.BlockSpec((1, c, d), lambda i, j: (i, 0, P_SWA_K // d + j // SWA_GROUP)),
                  pl.BlockSpec((1, c, d), lambda i, j: (i, 0, P_SWA_V // d + j // SWA_GROUP))],
        out_specs=pl.BlockSpec((1, c, d), lambda i, j: (i, 0, j)),
        out_shape=jax.ShapeDtypeStruct((b, c, SWA_HEADS * d), BF16),
        compiler_params=_params(2),
        name="swa_ctx_attention",
    )(sink, proj_c, proj_c, proj_c)


def _swa_window_start(t, qb, nb):
    return pl.multiple_of(jnp.clip(t * qb - 1, 0, nb - (qb + 2)) * SWA_BLOCK, SWA_BLOCK)


def _swa_kernel(sink_ref, q_ref, k_ref, kc_ref, v_ref, vc_ref, cos_ref, sin_ref, o_ref, s_scr, m_scr, *,
                qb, tiles, nt):
    u = pl.program_id(0)

    @pl.when(u == 0)
    def _():
        s_scr[...] = jnp.zeros_like(s_scr)
        m_scr[...] = jnp.zeros_like(m_scr)

    blk, d = SWA_BLOCK, SWA_DIM
    half = d // 4
    nq = qb * blk
    nw = (qb + 2) * blk
    nb = k_ref.shape[1] // blk
    ck = kc_ref.shape[1]
    cur, prev = jnp.minimum(u, tiles - 1), jnp.maximum(u - 1, 0)
    g_cur, t_cur = (cur // nt) % SWA_KV_HEADS, cur % nt
    g_prev, t_prev = (prev // nt) % SWA_KV_HEADS, prev % nt
    q_start = pl.multiple_of(t_cur * nq, blk)
    k_start = _swa_window_start(t_cur, qb, nb)
    v_start = _swa_window_start(t_prev, qb, nb)

    def sink_row(g):
        return jnp.concatenate(
            [jnp.full((1, nq), sink_ref[0, g * SWA_GROUP + i] * LOG2E, F32) for i in range(SWA_GROUP)], axis=1)

    def values_t(v):
        return jnp.concatenate([v.astype(F32).T.astype(BF16), jnp.ones((16, v.shape[0]), BF16)], axis=0)

    scale = d ** -0.5 * LOG2E
    cos_q, sin_q = cos_ref[pl.ds(q_start, nq), :], sin_ref[pl.ds(q_start, nq), :]
    q = jnp.concatenate(
        [(_rope(q_ref[0, :, d * i: d * (i + 1)].astype(F32), cos_q, sin_q, half) * scale).astype(BF16)
         for i in range(SWA_GROUP)], axis=0)
    key = lax.broadcasted_iota(jnp.int32, (ck, nq), 0)
    tok = lax.broadcasted_iota(jnp.int32, (ck, nq), 1)
    m_prev = m_scr[...]
    m_new = sink_row(g_cur)
    acc = jnp.zeros((d + 16, SWA_GROUP * nq), F32)
    n_win = nw // ck
    for j in range(n_win + 1):
        rows = slice(j * ck, (j + 1) * ck)
        p = jnp.exp2(s_scr[rows, :] - m_prev).astype(BF16)
        if j < n_win:
            vt = values_t(v_ref[0, pl.ds(v_start + j * ck, ck), :])
            kr = pl.ds(k_start + j * ck, ck)
            k = _rope(k_ref[0, kr, :].astype(F32), cos_ref[kr, :], sin_ref[kr, :], half).astype(BF16)
            dist = key - tok + (k_start + j * ck - q_start)
            valid = (dist <= blk) & (dist >= -blk)
            s = _dot_nt(k, q)
            s = jnp.concatenate(
                [jnp.where(valid, s[:, nq * i: nq * (i + 1)], -jnp.inf) for i in range(SWA_GROUP)], axis=1)
        else:
            vt = values_t(vc_ref[0])
            s = _dot_nt(kc_ref[0].astype(BF16), q)
        s_scr[rows, :] = s
        m_new = jnp.maximum(m_new, jnp.max(s, axis=0, keepdims=True))
        acc = acc + _dot(vt, p)
    m_scr[...] = m_new
    out_t = acc[:d] / (acc[d:d + 1] + jnp.exp2(sink_row(g_prev) - m_prev))
    for i in range(SWA_GROUP):
        o_ref[0, :, d * i: d * (i + 1)] = out_t[:, nq * i: nq * (i + 1)].T.astype(o_ref.dtype)


def _swa_attention(proj_l, proj_c, cos, sin, sink):
    b, n, _ = proj_l.shape
    c = proj_c.shape[1]
    blk, d = SWA_BLOCK, SWA_DIM
    qb = 2
    nq = qb * blk
    nt = n // nq
    tiles = b * SWA_KV_HEADS * nt
    assert ((qb + 2) * blk) % c == 0 and n // blk >= qb + 2

    def cur(u):
        t = jnp.minimum(u, tiles - 1)
        return t // (SWA_KV_HEADS * nt), (t // nt) % SWA_KV_HEADS, t % nt

    def prev(u):
        t = jnp.maximum(u - 1, 0)
        return t // (SWA_KV_HEADS * nt), (t // nt) % SWA_KV_HEADS, t % nt

    kcol, vcol = P_SWA_K // d, P_SWA_V // d
    return pl.pallas_call(
        functools.partial(_swa_kernel, qb=qb, tiles=tiles, nt=nt),
        grid=(tiles + 1,),
        in_specs=[pl.BlockSpec(memory_space=pltpu.SMEM),
                  pl.BlockSpec((1, nq, SWA_GROUP * d), lambda u: (cur(u)[0], cur(u)[2], cur(u)[1])),
                  pl.BlockSpec((1, n, d), lambda u: (cur(u)[0], 0, kcol + cur(u)[1])),
                  pl.BlockSpec((1, c, d), lambda u: (cur(u)[0], 0, kcol + cur(u)[1])),
                  pl.BlockSpec((1, n, d), lambda u: (prev(u)[0], 0, vcol + prev(u)[1])),
                  pl.BlockSpec((1, c, d), lambda u: (prev(u)[0], 0, vcol + prev(u)[1])),
                  pl.BlockSpec((n, d), lambda u: (0, 0)),
                  pl.BlockSpec((n, d), lambda u: (0, 0))],
        out_specs=pl.BlockSpec((1, nq, SWA_GROUP * d), lambda u: (prev(u)[0], prev(u)[2], prev(u)[1])),
        out_shape=jax.ShapeDtypeStruct((b, n, SWA_HEADS * d), BF16),
        scratch_shapes=[pltpu.VMEM(((qb + 2) * blk + c, SWA_GROUP * nq), F32),
                        pltpu.VMEM((1, SWA_GROUP * nq), F32)],
        compiler_params=_params(1),
        name="swa_attention",
    )(sink, proj_l, proj_l, proj_c, proj_l, proj_c, cos, sin)


def _log_sigmoid(x):
    return jnp.minimum(x, 0.0) - jnp.log(1.0 + jnp.exp(-jnp.abs(x)))


def _gla_block_constants(rows):
    t = np.arange(rows)
    ti, tj = t[:, None], t[None, :]
    ci, cj = ti // GLA_CHUNK, tj // GLA_CHUNK
    ones, sel = [], []
    for reverse in (False, True):
        upto = (tj >= ti) if reverse else (tj <= ti)
        visible = (tj > ti) if reverse else (tj <= ti)
        earlier = (cj > ci) if reverse else (cj < ci)
        ones.append(np.concatenate([(ci == cj) & upto, upto], axis=0))
        sel.append(np.where((ci == cj) & visible, 1, np.where(earlier, 2, 0)))
    return jnp.asarray(np.stack(ones), BF16), jnp.asarray(np.stack(sel), jnp.int32)


def _gla_factors(q_ref, k_ref, misc_ref, i, wg, bg, ones, sel, reverse):
    rows = q_ref.shape[1]
    L = GLA_CHUNK
    assert rows == 4 * L
    width = GLA_HEADS * GLA_DK
    la = _log_sigmoid(_dot(misc_ref[i].astype(BF16), wg) + bg) * (1.0 / GLA_TAU)
    hi = la.astype(BF16)
    lo = (la - hi.astype(F32)).astype(BF16)
    sums = _dot(ones, jnp.concatenate([hi, lo], axis=1))
    sums = sums[:, :width] + sums[:, width:]
    b_in, b_abs = sums[:rows], sums[rows:]
    if reverse:
        g_mid, g_end = b_abs[rows // 2:rows // 2 + 1], b_abs[0:1]
    else:
        g_mid, g_end = b_abs[rows // 2 - 1:rows // 2], b_abs[rows - 1:rows]
    diag, cross = sel == 1, sel == 2
    q = q_ref[i].astype(F32) * GLA_DK ** -0.5
    k = k_ref[i].astype(F32)
    q_d, k_d = q * jnp.exp(b_in), k * jnp.exp(-b_in)
    q_x, k_x = q * jnp.exp(b_abs - g_mid), k * jnp.exp(g_mid - b_abs)
    q_s, k_s = q * jnp.exp(b_abs), k * jnp.exp(g_end - b_abs)
    decay = jnp.exp(g_end)
    return q_d, k_d, q_x, k_x, q_s, k_s, decay, diag, cross


def _gla_apply(factors, v_ref, state_ref, o_ref, i):
    q_d, k_d, q_x, k_x, q_s, k_s, decay, diag, cross = factors
    rows = q_d.shape[0]
    lane = lax.broadcasted_iota(jnp.int32, (rows, 2 * GLA_DK), 1)
    lane_s = lax.broadcasted_iota(jnp.int32, (GLA_DV, 2 * GLA_DK), 1)
    for pair in range(GLA_HEADS // 2):
        ps = slice(2 * GLA_DK * pair, 2 * GLA_DK * (pair + 1))
        kd_p, kx_p, ks_p = k_d[:, ps].astype(BF16), k_x[:, ps].astype(BF16), k_s[:, ps].astype(BF16)
        st = state_ref[i, pair]
        st_b = st.astype(BF16)
        ds = []
        for j in range(2):
            h = 2 * pair + j
            vs = slice(h * GLA_DV, (h + 1) * GLA_DV)
            mine = (lane // GLA_DK) == j
            qd_h = jnp.where(mine, q_d[:, ps], 0.0).astype(BF16)
            qx_h = jnp.where(mine, q_x[:, ps], 0.0).astype(BF16)
            qs_h = jnp.where(mine, q_s[:, ps], 0.0).astype(BF16)
            vh = v_ref[i, :, vs].astype(BF16)
            a = jnp.where(diag, _dot_nt(qd_h, kd_p), jnp.where(cross, _dot_nt(qx_h, kx_p), 0.0))
            o_ref[i, :, vs] = (_dot(a.astype(BF16), vh) + _dot_nt(qs_h, st_b)).astype(o_ref.dtype)
            ds.append(_dot_tn(vh, ks_p))
        state_ref[i, pair] = decay[:, ps] * st + jnp.where(lane_s < GLA_DK, ds[0], ds[1])


def _gla_kernel(qf_ref, kf_ref, vf_ref, mf_ref, qb_ref, kb_ref, vb_ref, mb_ref, wg_ref, bg_ref, ones_ref, sel_ref,
                sf0_ref, sb0_ref, of_ref, ob_ref, sf_ref, sb_ref, stf, stb):
    t = pl.program_id(1)

    @pl.when(t == 0)
    def _():
        stf[...] = sf0_ref[...]
        stb[...] = sb0_ref[...]

    width = GLA_HEADS * GLA_DK
    entries = range(qf_ref.shape[0])
    fwd = [_gla_factors(qf_ref, kf_ref, mf_ref, i, wg_ref[:, :width], bg_ref[:, :width], ones_ref[0], sel_ref[0],
                        False) for i in entries]
    bwd = [_gla_factors(qb_ref, kb_ref, mb_ref, i, wg_ref[:, width:], bg_ref[:, width:], ones_ref[1], sel_ref[1],
                        True) for i in entries]
    for i in entries:
        _gla_apply(fwd[i], vf_ref, stf, of_ref, i)
        _gla_apply(bwd[i], vb_ref, stb, ob_ref, i)

    @pl.when(t == pl.num_programs(1) - 1)
    def _():
        sf_ref[...] = stf[...]
        sb_ref[...] = stb[...]


def _gla_scan(proj, wg, bg, sf0, sb0, layer):
    b, r, _ = proj.shape
    tm = 256
    nt = r // tm
    kw, vw = GLA_HEADS * GLA_DK, GLA_HEADS * GLA_DV
    fwd = lambda t: t
    bwd = lambda t: nt - 1 - t

    nb = 2 if b % 2 == 0 else 1
    ones, sel = _gla_block_constants(tm)

    def specs(order):
        return [pl.BlockSpec((nb, tm, kw), lambda i, t: (i, order(t), P_GLA_Q // kw)),
                pl.BlockSpec((nb, tm, kw), lambda i, t: (i, order(t), P_GLA_K // kw)),
                pl.BlockSpec((nb, tm, vw), lambda i, t: (i, order(t), P_GLA_V // vw)),
                pl.BlockSpec((nb, tm, 128), lambda i, t: (i, order(t), P_MISC // 128))]

    state_spec = pl.BlockSpec((nb,) + GLA_STATE, lambda i, t: (i, 0, 0, 0))
    state_shape = jax.ShapeDtypeStruct((b,) + GLA_STATE, F32)
    return pl.pallas_call(
        _gla_kernel,
        grid=(b // nb, nt),
        in_specs=specs(fwd) + specs(bwd) + [
            pl.BlockSpec((None,) + wg.shape[1:], lambda i, t: (layer, 0, 0)),
            pl.BlockSpec((None,) + bg.shape[1:], lambda i, t: (layer, 0, 0)),
            pl.BlockSpec(ones.shape, lambda i, t: (0, 0, 0)),
            pl.BlockSpec(sel.shape, lambda i, t: (0, 0, 0)),
            state_spec, state_spec],
        out_specs=[pl.BlockSpec((nb, tm, vw), lambda i, t: (i, fwd(t), 0)),
                   pl.BlockSpec((nb, tm, vw), lambda i, t: (i, bwd(t), 0)),
                   state_spec, state_spec],
        out_shape=[jax.ShapeDtypeStruct((b, r, vw), BF16), jax.ShapeDtypeStruct((b, r, vw), BF16),
                   state_shape, state_shape],
        scratch_shapes=[pltpu.VMEM((nb,) + GLA_STATE, F32), pltpu.VMEM((nb,) + GLA_STATE, F32)],
        compiler_params=_params(2),
        name="gla_scan",
    )(proj, proj, proj, proj, proj, proj, proj, proj, wg, bg, ones, sel, sf0, sb0)


def _outproj_kernel(x_ref, mod_ref, mla_ref, swa_ref, of_ref, ob_ref, r_ref, g_ref, w_ref, o_ref):
    o = of_ref[0].astype(F32) + ob_ref[0].astype(F32)
    gate = _silu(r_ref[0].astype(F32))
    parts = []
    for h in range(GLA_HEADS):
        vs = slice(h * GLA_DV, (h + 1) * GLA_DV)
        parts.append((_rms(o[:, vs], g_ref[:, vs]) * gate[:, vs]).astype(BF16))
    gla = jnp.concatenate(parts, axis=1)
    a, b = mla_ref.shape[2], mla_ref.shape[2] + swa_ref.shape[2]
    mix = _dot(mla_ref[0], w_ref[0:a, :]) + _dot(swa_ref[0], w_ref[a:b, :]) + _dot(gla, w_ref[b:, :])
    o_ref[0] = x_ref[0] + mod_ref[0, 2:3, :] * mix


def _out_projection(x, mod, mla, swa, o_f, o_b, proj, g_out, w_out, layer):
    bx, r, d = x.shape
    tm = 512
    row = lambda width: pl.BlockSpec((1, tm, width), lambda b, t: (b, t, 0))
    gw = GLA_HEADS * GLA_DV
    return pl.pallas_call(
        _outproj_kernel,
        grid=(bx, r // tm),
        in_specs=[row(d),
                  pl.BlockSpec((1, N_MOD, d), lambda b, t: (b, 0, 0)),
                  row(mla.shape[2]), row(swa.shape[2]), row(gw), row(gw),
                  pl.BlockSpec((1, tm, gw), lambda b, t: (b, t, P_GLA_R // gw)),
                  pl.BlockSpec((1, gw), lambda b, t: (0, 0)),
                  pl.BlockSpec((None,) + w_out.shape[1:], lambda b, t: (layer, 0, 0))],
        out_specs=row(d),
        out_shape=jax.ShapeDtypeStruct(x.shape, F32),
        compiler_params=_params(2),
        name="out_projection",
    )(x, mod, mla, swa, o_f, o_b, proj, g_out, w_out)


def _ffn_kernel(x_ref, mod_ref, g_ref, wg_ref, wu_ref, wd_ref, gf_ref, o_ref, h_ref, *, final_norm):
    f = pl.program_id(2)
    row_chunks = [slice(r, r + ROW_CHUNK) for r in range(0, x_ref.shape[1], ROW_CHUNK)]
    chunk = (ROW_CHUNK, x_ref.shape[2])

    def gated_partial():
        h = h_ref[...]
        act = _silu(_dot(h, wg_ref[...])) * _dot(h, wu_ref[...])
        return mod_ref[0, 5:6, :] * _dot(act.astype(BF16), wd_ref[...])

    @pl.when(f == 0)
    def _():
        gain = jnp.broadcast_to(g_ref[...] * (1.0 + mod_ref[0, 4:5, :]), chunk)
        shift = jnp.broadcast_to(mod_ref[0, 3:4, :], chunk)
        for rows in row_chunks:
            x = x_ref[0, rows, :]
            rs = lax.rsqrt(jnp.mean(x * x, axis=-1, keepdims=True) + EPS)
            h_ref[rows, :] = (x * rs * gain + shift).astype(BF16)
        o_ref[0] = x_ref[0] + gated_partial()

    @pl.when(f > 0)
    def _():
        o_ref[0] += gated_partial()

    if final_norm:
        @pl.when(f == pl.num_programs(2) - 1)
        def _():
            g_fin = jnp.broadcast_to(gf_ref[...], chunk)
            for rows in row_chunks:
                y = o_ref[0, rows, :]
                o_ref[0, rows, :] = y * lax.rsqrt(jnp.mean(y * y, axis=-1, keepdims=True) + EPS) * g_fin


def _ffn(x, mod, g, w_gu, w_down, g_final, final_norm, layer):
    bx, r, d = x.shape
    hidden = w_down.shape[1]
    tm, tf = 512, 512
    nf = hidden // tf
    return pl.pallas_call(
        functools.partial(_ffn_kernel, final_norm=final_norm),
        grid=(bx, r // tm, nf),
        in_specs=[pl.BlockSpec((1, tm, d), lambda b, t, f: (b, t, 0)),
                  pl.BlockSpec((1, N_MOD, d), lambda b, t, f: (b, 0, 0)),
                  pl.BlockSpec((1, d), lambda b, t, f: (0, 0)),
                  pl.BlockSpec((None, d, tf), lambda b, t, f: (layer, 0, f)),
                  pl.BlockSpec((None, d, tf), lambda b, t, f: (layer, 0, nf + f)),
                  pl.BlockSpec((None, tf, d), lambda b, t, f: (layer, f, 0)),
                  pl.BlockSpec((1, d), lambda b, t, f: (0, 0))],
        out_specs=pl.BlockSpec((1, tm, d), lambda b, t, f: (b, t, 0)),
        out_shape=jax.ShapeDtypeStruct(x.shape, F32),
        scratch_shapes=[pltpu.VMEM((tm, d), BF16)],
        compiler_params=_params(3),
        name="ffn",
    )(x, mod, g, w_gu, w_gu, w_down, g_final)


def _rope_tables(n, dim):
    half = dim // 4
    freqs = ROPE_THETA ** (-jnp.arange(half, dtype=F32) / half)
    pos = jnp.arange(n, dtype=jnp.int32)
    ang_r = (pos // GRID_W).astype(F32)[:, None] * freqs
    ang_c = (pos % GRID_W).astype(F32)[:, None] * freqs
    cos = jnp.concatenate([jnp.cos(ang_r)] * 2 + [jnp.cos(ang_c)] * 2, axis=1)
    sin = jnp.concatenate([-jnp.sin(ang_r), jnp.sin(ang_r), -jnp.sin(ang_c), jnp.sin(ang_c)], axis=1)
    reps = 128 // dim
    return jnp.tile(cos, (1, reps)), jnp.tile(sin, (1, reps))


def _relayout_w_in(w_in):
    depth, d, _ = w_in.shape
    sizes = (MLA_RANK, MLA_RANK, MLA_ROPE, SWA_HEADS * SWA_DIM, SWA_KV_HEADS * SWA_DIM, SWA_KV_HEADS * SWA_DIM,
             GLA_HEADS * GLA_DK, GLA_HEADS * GLA_DK, GLA_HEADS * GLA_DV, 2 * GLA_GATE_RANK, GLA_HEADS * GLA_DV)
    offs = [0]
    for s in sizes:
        offs.append(offs[-1] + s)
    w = w_in.astype(BF16)
    span = lambda i, j: w[:, :, offs[i]:offs[j]]
    pad = jnp.zeros((depth, d, P_GLA_Q - P_MISC - MLA_ROPE - 2 * GLA_GATE_RANK), BF16)
    out = jnp.concatenate([span(3, 5), span(0, 2), span(5, 6), span(2, 3), span(9, 10), pad,
                           span(6, 9), span(10, 11)], axis=-1)
    assert out.shape[-1] == P_WIDTH
    return out


def _relayout_mla(w_uq, w_ukv):
    depth, rk, _ = w_uq.shape
    uq = w_uq.reshape(depth, rk, MLA_HEADS, MLA_QK)
    uq = jnp.concatenate([uq[..., :MLA_NOPE].reshape(depth, rk, -1), uq[..., MLA_NOPE:].reshape(depth, rk, -1)], -1)
    ukv = w_ukv.reshape(depth, rk, MLA_HEADS, MLA_NOPE + MLA_V)
    ukv = jnp.concatenate([ukv[..., :MLA_NOPE].reshape(depth, rk, -1), ukv[..., MLA_NOPE:].reshape(depth, rk, -1)], -1)
    return uq.astype(BF16), ukv.astype(BF16)


def _relayout_gate(w_f, b_f, w_b, b_b):
    depth, rank, width = w_f.shape
    wg = jnp.zeros((depth, 128, 2 * width), F32)
    wg = wg.at[:, MISC_GATE_OFF:MISC_GATE_OFF + rank, :width].set(w_f)
    wg = wg.at[:, MISC_GATE_OFF + rank:MISC_GATE_OFF + 2 * rank, width:].set(w_b)
    bg = jnp.concatenate([b_f, b_b], axis=-1).reshape(depth, 1, 2 * width)
    return wg.astype(BF16), bg


def kernel(x, c, ctx, c_ctx, w_mod, b_mod, g_mix, g_ffn, w_in, g_mla_q, g_mla_kv, w_mla_uq, w_mla_ukv,
           swa_sink, w_gla_gate_f, b_gla_gate_f, w_gla_gate_b, b_gla_gate_b, g_gla_out, w_out, w_ffn_gu,
           w_ffn_down, g_final):
    B, N, D = x.shape
    C = ctx.shape[1]
    depth = w_mod.shape[0]

    cvec = jnp.concatenate([c, c_ctx[None, :], jnp.zeros((16 - B - 1, D), F32)], axis=0)
    mod = _modulation(cvec, w_mod, b_mod).reshape(depth, 16, N_MOD, D)

    w_in_p = _relayout_w_in(w_in)
    w_uq_p, w_ukv_p = _relayout_mla(w_mla_uq, w_mla_ukv)
    wg_p, bg_p = _relayout_gate(w_gla_gate_f, b_gla_gate_f, w_gla_gate_b, b_gla_gate_b)
    w_out_b = w_out.astype(BF16)
    w_gu_b = w_ffn_gu.astype(BF16)
    w_down_b = w_ffn_down.astype(BF16)

    cos_m, sin_m = _rope_tables(N, MLA_ROPE)
    cos_s, sin_s = _rope_tables(N, SWA_DIM)
    cos_id, sin_id = jnp.ones((C, 128), F32), jnp.zeros((C, 128), F32)
    state0 = jnp.zeros((B,) + GLA_STATE, F32)
    g_fin = g_final.reshape(1, D)

    xc = ctx.reshape(1, B * C, D)
    for l in range(depth):
        last = l == depth - 1
        mod_l, mod_c = mod[l, :B], mod[l, B:B + 1]
        g_mix_l, g_ffn_l = g_mix[l].reshape(1, D), g_ffn[l].reshape(1, D)
        g_q, g_kv = g_mla_q[l].reshape(1, -1), g_mla_kv[l].reshape(1, -1)
        g_out = g_gla_out[l].reshape(1, -1)
        sink = swa_sink[l].reshape(1, -1)

        proj_l = _in_projection(x, mod_l, g_mix_l, w_in_p, l)
        proj_c = _in_projection(xc, mod_c, g_mix_l, w_in_p, l).reshape(B, C, P_WIDTH)

        q_l, k_l, v_l = _mla_prep(proj_l, cos_m, sin_m, g_q, g_kv, w_uq_p, w_ukv_p, l)
        q_c, k_c, v_c = _mla_prep(proj_c, cos_id, sin_id, g_q, g_kv, w_uq_p, w_ukv_p, l)
        mla_l = _mla_attention(q_l, k_l, v_l, k_c, v_c)
        swa_l = _swa_attention(proj_l, proj_c, cos_s, sin_s, sink)

        of_c, ob_c, s_f, s_b = _gla_scan(proj_c, wg_p, bg_p, state0, state0, l)
        of_l, ob_l, _, _ = _gla_scan(proj_l, wg_p, bg_p, s_f, s_b, l)

        x = _out_projection(x, mod_l, mla_l, swa_l, of_l, ob_l, proj_l, g_out, w_out_b, l)
        x = _ffn(x, mod_l, g_ffn_l, w_gu_b, w_down_b, g_fin, last, l)

        if not last:
            mla_c = _mla_ctx_attention(q_c, k_c, v_c)
            swa_c = _swa_ctx_attention(proj_c, sink)
            flat = lambda a: a.reshape(1, B * C, a.shape[-1])
            xc = _out_projection(xc, mod_c, flat(mla_c), flat(swa_c), flat(of_c), flat(ob_c), flat(proj_c),
                                 g_out, w_out_b, l)
            xc = _ffn(xc, mod_c, g_ffn_l, w_gu_b, w_down_b, g_fin, False, l)
    return x
```

```python
import functools

import jax
import jax.numpy as jnp
import numpy as np
from jax import lax
from jax.experimental import pallas as pl
from jax.experimental.pallas import tpu as pltpu

F32 = jnp.float32
BF16 = jnp.bfloat16

GRID_W = 64
EPS = 1e-6
ROPE_THETA = 10000.0
LOG2E = 1.4426950408889634

MLA_HEADS = 6
MLA_RANK = 512
MLA_NOPE = 128
MLA_ROPE = 64
MLA_V = 128
MLA_QK = MLA_NOPE + MLA_ROPE
MLA_VA = MLA_V + 16

SWA_HEADS = 6
SWA_KV_HEADS = 2
SWA_GROUP = SWA_HEADS // SWA_KV_HEADS
SWA_DIM = 128
SWA_BLOCK = 128

GLA_HEADS = 4
GLA_DK = 64
GLA_DV = 128
GLA_GATE_RANK = 16
GLA_TAU = 16.0
GLA_CHUNK = 64
GLA_STATE = (GLA_HEADS // 2, GLA_DV, 2 * GLA_DK)

N_MOD = 6

P_SWA_Q = 0
P_SWA_K = 768
P_CQ = 1024
P_CKV = 1536
P_SWA_V = 2048
P_MISC = 2304
P_GLA_Q = 2560
P_GLA_K = 2816
P_GLA_V = 3072
P_GLA_R = 3584
P_WIDTH = 4096
MISC_GATE_OFF = MLA_ROPE

VMEM_LIMIT = 56 * 1024 * 1024
ROW_CHUNK = 16


def _params(n_axes, flags=None):
    return pltpu.CompilerParams(dimension_semantics=("arbitrary",) * n_axes,
                                vmem_limit_bytes=VMEM_LIMIT, flags=flags)


def _silu(x):
    return x / (1.0 + jnp.exp(-x))


def _rms(x, g):
    ms = jnp.mean(x * x, axis=-1, keepdims=True)
    return x * lax.rsqrt(ms + EPS) * g


def _dot(a, b):
    return jnp.dot(a, b, preferred_element_type=F32)


def _dot_nt(a, b):
    return lax.dot_general(a, b, (((1,), (1,)), ((), ())), preferred_element_type=F32)


def _dot_tn(a, b):
    return lax.dot_general(a, b, (((0,), (0,)), ((), ())), preferred_element_type=F32)


def _mod_kernel(c_ref, w_ref, b_ref, o_ref):
    a = _silu(c_ref[...]).astype(BF16)
    o_ref[0] = _dot(a, w_ref[0].astype(BF16)) + b_ref[0]


def _modulation(cvec, w_mod, b_mod):
    depth, d, width = w_mod.shape
    rows = cvec.shape[0]
    tn = 1024
    return pl.pallas_call(
        _mod_kernel,
        grid=(depth, width // tn),
        in_specs=[pl.BlockSpec((rows, d), lambda l, j: (0, 0)),
                  pl.BlockSpec((1, d, tn), lambda l, j: (l, 0, j)),
                  pl.BlockSpec((1, 1, tn), lambda l, j: (l, 0, j))],
        out_specs=pl.BlockSpec((1, rows, tn), lambda l, j: (l, 0, j)),
        out_shape=jax.ShapeDtypeStruct((depth, rows, width), F32),
        compiler_params=_params(2),
        name="modulation",
    )(cvec, w_mod, b_mod.reshape(depth, 1, width))


def _inproj_kernel(x_ref, mod_ref, g_ref, w_ref, o_ref):
    h = _rms(x_ref[0], g_ref[...]) * (1.0 + mod_ref[0, 1:2, :]) + mod_ref[0, 0:1, :]
    o_ref[0] = _dot(h.astype(BF16), w_ref[...]).astype(o_ref.dtype)


def _in_projection(x, mod, g, w, layer):
    bx, r, d = x.shape
    tm, tn = 512, P_WIDTH
    return pl.pallas_call(
        _inproj_kernel,
        grid=(P_WIDTH // tn, bx, r // tm),
        in_specs=[pl.BlockSpec((1, tm, d), lambda j, b, t: (b, t, 0)),
                  pl.BlockSpec((1, N_MOD, d), lambda j, b, t: (b, 0, 0)),
                  pl.BlockSpec((1, d), lambda j, b, t: (0, 0)),
                  pl.BlockSpec((None, d, tn), lambda j, b, t: (layer, 0, j), pipeline_mode=pl.Buffered(1))],
        out_specs=pl.BlockSpec((1, tm, tn), lambda j, b, t: (b, t, j)),
        out_shape=jax.ShapeDtypeStruct((bx, r, P_WIDTH), BF16),
        compiler_params=_params(3),
        name="in_projection",
    )(x, mod, g, w)


def _rope(x, cos, sin, half):
    lane = lax.broadcasted_iota(jnp.int32, x.shape, 1)
    first = (lane % (2 * half)) < half
    width = x.shape[1]
    rot = jnp.where(first, pltpu.roll(x, width - half, 1), pltpu.roll(x, half, 1))
    return x * cos + rot * sin


def _mla_prep_kernel(cq_ref, ckv_ref, misc_ref, cos_ref, sin_ref, gq_ref, gkv_ref, wuq_ref, wukv_ref,
                     qt_ref, k_ref, vt_ref):
    scale = MLA_QK ** -0.5 * LOG2E
    cos, sin = cos_ref[...], sin_ref[...]
    half = MLA_ROPE // 4
    qf = _dot(_rms(cq_ref[0].astype(F32), gq_ref[...]).astype(BF16), wuq_ref[...])
    kvf = _dot(_rms(ckv_ref[0].astype(F32), gkv_ref[...]).astype(BF16), wukv_ref[...])
    kr = _rope(misc_ref[0].astype(F32), cos, sin, half)[:, :MLA_ROPE].astype(BF16)
    nope_w = MLA_HEADS * MLA_NOPE
    for pair in range(MLA_HEADS // 2):
        qr = _rope(qf[:, nope_w + 128 * pair: nope_w + 128 * (pair + 1)], cos, sin, half) * scale
        qrt = qr.astype(BF16).T
        for j in range(2):
            qt_ref[0, 2 * pair + j, MLA_NOPE:MLA_QK, :] = qrt[MLA_ROPE * j: MLA_ROPE * (j + 1), :]
    for h in range(MLA_HEADS):
        qt_ref[0, h, 0:MLA_NOPE, :] = (qf[:, MLA_NOPE * h: MLA_NOPE * (h + 1)] * scale).astype(BF16).T
        k_ref[0, h, :, 0:MLA_NOPE] = kvf[:, MLA_NOPE * h: MLA_NOPE * (h + 1)].astype(BF16)
        k_ref[0, h, :, MLA_NOPE:MLA_QK] = kr
        vt_ref[0, h, 0:MLA_V, :] = kvf[:, nope_w + MLA_V * h: nope_w + MLA_V * (h + 1)].astype(BF16).T
        vt_ref[0, h, MLA_V:, :] = jnp.ones((MLA_VA - MLA_V, vt_ref.shape[3]), BF16)


def _mla_prep(proj, cos, sin, g_q, g_kv, w_uq, w_ukv, layer):
    b, r, _ = proj.shape
    tm = min(r, 1024)
    rk = MLA_RANK
    const = lambda shape: pl.BlockSpec(shape, lambda i, t: (0,) * len(shape))
    stacked = lambda w: pl.BlockSpec((None,) + w.shape[1:], lambda i, t: (layer, 0, 0))
    return pl.pallas_call(
        _mla_prep_kernel,
        grid=(b, r // tm),
        in_specs=[pl.BlockSpec((1, tm, rk), lambda i, t: (i, t, P_CQ // rk)),
                  pl.BlockSpec((1, tm, rk), lambda i, t: (i, t, P_CKV // rk)),
                  pl.BlockSpec((1, tm, 128), lambda i, t: (i, t, P_MISC // 128)),
                  pl.BlockSpec((tm, 128), lambda i, t: (t, 0)),
                  pl.BlockSpec((tm, 128), lambda i, t: (t, 0)),
                  const((1, rk)), const((1, rk)),
                  stacked(w_uq), stacked(w_ukv)],
        out_specs=[pl.BlockSpec((1, MLA_HEADS, MLA_QK, tm), lambda i, t: (i, 0, 0, t)),
                   pl.BlockSpec((1, MLA_HEADS, tm, MLA_QK), lambda i, t: (i, 0, t, 0)),
                   pl.BlockSpec((1, MLA_HEADS, MLA_VA, tm), lambda i, t: (i, 0, 0, t))],
        out_shape=[jax.ShapeDtypeStruct((b, MLA_HEADS, MLA_QK, r), BF16),
                   jax.ShapeDtypeStruct((b, MLA_HEADS, r, MLA_QK), BF16),
                   jax.ShapeDtypeStruct((b, MLA_HEADS, MLA_VA, r), BF16)],
        compiler_params=_params(2),
        name="mla_prep",
    )(proj, proj, proj, cos, sin, g_q, g_kv, w_uq, w_ukv)


def _softmax_pv(s_parts, vt_parts):
    m = functools.reduce(jnp.maximum, [jnp.max(s, axis=0, keepdims=True) for s in s_parts])
    acc = sum(_dot(vt, jnp.exp2(s - m).astype(BF16)) for s, vt in zip(s_parts, vt_parts))
    return (acc[:MLA_V] / acc[MLA_V:MLA_V + 1]).T


def _mla_attn_kernel(qt_ref, kl_ref, kc_ref, vlt_ref, vct_ref, o_ref, s_scr, m_scr, *, tk):
    @pl.when(pl.program_id(0) == 0)
    def _():
        s_scr[...] = jnp.zeros_like(s_scr)
        m_scr[...] = jnp.zeros_like(m_scr)

    n_lat = kl_ref.shape[2] // tk
    qt = qt_ref[0, 0]
    tq = qt.shape[1]
    m_prev = m_scr[...]
    m_new = jnp.full((1, tq), -jnp.inf, F32)
    acc = jnp.zeros((MLA_VA, tq), F32)
    for j in range(n_lat + 1):
        rows = slice(j * tk, (j + 1) * tk)
        p = jnp.exp2(s_scr[rows, :] - m_prev).astype(BF16)
        k = kc_ref[0, 0] if j == n_lat else kl_ref[0, 0, rows, :]
        vt = vct_ref[0, 0] if j == n_lat else vlt_ref[0, 0, :, rows]
        s = _dot(k, qt)
        s_scr[rows, :] = s
        m_new = jnp.maximum(m_new, jnp.max(s, axis=0, keepdims=True))
        acc = acc + _dot(vt, p)
    m_scr[...] = m_new
    o_ref[0] = (acc[:MLA_V] / acc[MLA_V:MLA_V + 1]).T.astype(o_ref.dtype)


def _mla_attention(qt, k_l, vt_l, k_c, vt_c):
    b, h, dq, n = qt.shape
    c = k_c.shape[2]
    tq = min(n, 1024)
    nt = n // tq
    tiles = b * h * nt

    def cur(u):
        t = jnp.minimum(u, tiles - 1)
        return t // (h * nt), (t // nt) % h, t % nt

    def prev(u):
        t = jnp.maximum(u - 1, 0)
        return t // (h * nt), (t // nt) % h, t % nt

    assert n % c == 0
    return pl.pallas_call(
        functools.partial(_mla_attn_kernel, tk=c),
        grid=(tiles + 1,),
        in_specs=[pl.BlockSpec((1, 1, dq, tq), lambda u: (cur(u)[0], cur(u)[1], 0, cur(u)[2])),
                  pl.BlockSpec((1, 1, n, dq), lambda u: (cur(u)[0], cur(u)[1], 0, 0)),
                  pl.BlockSpec((1, 1, c, dq), lambda u: (cur(u)[0], cur(u)[1], 0, 0)),
                  pl.BlockSpec((1, 1, MLA_VA, n), lambda u: (prev(u)[0], prev(u)[1], 0, 0)),
                  pl.BlockSpec((1, 1, MLA_VA, c), lambda u: (prev(u)[0], prev(u)[1], 0, 0))],
        out_specs=pl.BlockSpec((1, tq, MLA_V), lambda u: (prev(u)[0], prev(u)[2], prev(u)[1])),
        out_shape=jax.ShapeDtypeStruct((b, n, h * MLA_V), BF16),
        scratch_shapes=[pltpu.VMEM((n + c, tq), F32), pltpu.VMEM((1, tq), F32)],
        compiler_params=_params(1),
        name="mla_attention",
    )(qt, k_l, k_c, vt_l, vt_c)


def _mla_ctx_kernel(qt_ref, k_ref, vt_ref, o_ref):
    o_ref[0] = _softmax_pv([_dot(k_ref[0, 0], qt_ref[0, 0])], [vt_ref[0, 0]]).astype(o_ref.dtype)


def _mla_ctx_attention(qt, k, vt):
    b, h, dq, c = qt.shape
    return pl.pallas_call(
        _mla_ctx_kernel,
        grid=(b, h),
        in_specs=[pl.BlockSpec((1, 1, dq, c), lambda i, j: (i, j, 0, 0)),
                  pl.BlockSpec((1, 1, c, dq), lambda i, j: (i, j, 0, 0)),
                  pl.BlockSpec((1, 1, MLA_VA, c), lambda i, j: (i, j, 0, 0))],
        out_specs=pl.BlockSpec((1, c, MLA_V), lambda i, j: (i, 0, j)),
        out_shape=jax.ShapeDtypeStruct((b, c, h * MLA_V), BF16),
        compiler_params=_params(2),
        name="mla_ctx_attention",
    )(qt, k, vt)


def _swa_ctx_kernel(sink_ref, q_ref, k_ref, v_ref, o_ref):
    q = (q_ref[0].astype(F32) * SWA_DIM ** -0.5).astype(BF16)
    s = _dot_nt(q, k_ref[0].astype(BF16))
    sink = sink_ref[0, pl.program_id(1)]
    m = jnp.maximum(jnp.max(s, axis=-1, keepdims=True), sink)
    p = jnp.exp(s - m)
    l = jnp.sum(p, axis=-1, keepdims=True) + jnp.exp(sink - m)
    o_ref[0] = (_dot(p.astype(BF16), v_ref[0].astype(BF16)) / l).astype(o_ref.dtype)


def _swa_ctx_attention(proj_c, sink):
    b, c, _ = proj_c.shape
    d = SWA_DIM
    return pl.pallas_call(
        _swa_ctx_kernel,
        grid=(b, SWA_HEADS),
        in_specs=[pl.BlockSpec(memory_space=pltpu.SMEM),
                  pl.BlockSpec((1, c, d), lambda i, j: (i, 0, P_SWA_Q // d + j)),
                  pl.BlockSpec((1, c, d), lambda i, j: (i, 0, P_SWA_K // d + j // SWA_GROUP)),
                  pl.BlockSpec((1, c, d), lambda i, j: (i, 0, P_SWA_V // d + j // SWA_GROUP))],
        out_specs=pl.BlockSpec((1, c, d), lambda i, j: (i, 0, j)),
        out_shape=jax.ShapeDtypeStruct((b, c, SWA_HEADS * d), BF16),
        compiler_params=_params(2),
        name="swa_ctx_attention",
    )(sink, proj_c, proj_c, proj_c)


def _swa_window_start(t, qb, nb):
    return pl.multiple_of(jnp.clip(t * qb - 1, 0, nb - (qb + 2)) * SWA_BLOCK, SWA_BLOCK)


def _swa_kernel(sink_ref, q_ref, k_ref, kc_ref, v_ref, vc_ref, cos_ref, sin_ref, o_ref, s_scr, m_scr, *,
                qb, tiles, nt):
    u = pl.program_id(0)

    @pl.when(u == 0)
    def _():
        s_scr[...] = jnp.zeros_like(s_scr)
        m_scr[...] = jnp.zeros_like(m_scr)

    blk, d = SWA_BLOCK, SWA_DIM
    half = d // 4
    nq = qb * blk
    nw = (qb + 2) * blk
    nb = k_ref.shape[1] // blk
    ck = kc_ref.shape[1]
    cur, prev = jnp.minimum(u, tiles - 1), jnp.maximum(u - 1, 0)
    t_cur, t_prev = cur % nt, prev % nt
    q_start = pl.multiple_of(t_cur * nq, blk)
    k_start = _swa_window_start(t_cur, qb, nb)
    v_start = _swa_window_start(t_prev, qb, nb)
    groups = range(SWA_KV_HEADS)

    def sink_row(g):
        return jnp.concatenate(
            [jnp.full((1, nq), sink_ref[0, g * SWA_GROUP + i] * LOG2E, F32) for i in range(SWA_GROUP)], axis=1)

    def values_t(v):
        return jnp.concatenate([v.astype(F32).T.astype(BF16), jnp.ones((16, v.shape[0]), BF16)], axis=0)

    scale = d ** -0.5 * LOG2E
    cos_q, sin_q = cos_ref[pl.ds(q_start, nq), :], sin_ref[pl.ds(q_start, nq), :]
    q = [jnp.concatenate(
        [(_rope(q_ref[0, :, d * h: d * (h + 1)].astype(F32), cos_q, sin_q, half) * scale).astype(BF16)
         for h in range(g * SWA_GROUP, (g + 1) * SWA_GROUP)], axis=0) for g in groups]
    key = lax.broadcasted_iota(jnp.int32, (ck, nq), 0)
    tok = lax.broadcasted_iota(jnp.int32, (ck, nq), 1)
    m_prev = [m_scr[g] for g in groups]
    m_new = [sink_row(g) for g in groups]
    acc = [jnp.zeros((d + 16, SWA_GROUP * nq), F32) for g in groups]
    n_win = nw // ck
    for j in range(n_win + 1):
        rows = slice(j * ck, (j + 1) * ck)
        if j < n_win:
            kr = pl.ds(k_start + j * ck, ck)
            cos_k, sin_k = cos_ref[kr, :], sin_ref[kr, :]
            dist = key - tok + (k_start + j * ck - q_start)
            valid = (dist <= blk) & (dist >= -blk)
        for g in groups:
            cols = slice(d * g, d * (g + 1))
            p = jnp.exp2(s_scr[g, rows, :] - m_prev[g]).astype(BF16)
            if j < n_win:
                vt = values_t(v_ref[0, pl.ds(v_start + j * ck, ck), cols])
                k = _rope(k_ref[0, kr, cols].astype(F32), cos_k, sin_k, half).astype(BF16)
                s = _dot_nt(k, q[g])
                s = jnp.concatenate(
                    [jnp.where(valid, s[:, nq * i: nq * (i + 1)], -jnp.inf) for i in range(SWA_GROUP)], axis=1)
            else:
                vt = values_t(vc_ref[0, :, cols])
                s = _dot_nt(kc_ref[0, :, cols].astype(BF16), q[g])
            s_scr[g, rows, :] = s
            m_new[g] = jnp.maximum(m_new[g], jnp.max(s, axis=0, keepdims=True))
            acc[g] = acc[g] + _dot(vt, p)
    for g in groups:
        m_scr[g] = m_new[g]
        out_t = acc[g][:d] / (acc[g][d:d + 1] + jnp.exp2(sink_row(g) - m_prev[g]))
        for i in range(SWA_GROUP):
            h = g * SWA_GROUP + i
            o_ref[0, :, d * h: d * (h + 1)] = out_t[:, nq * i: nq * (i + 1)].T.astype(o_ref.dtype)


def _swa_attention(proj_l, proj_c, cos, sin, sink):
    b, n, _ = proj_l.shape
    c = proj_c.shape[1]
    blk, d = SWA_BLOCK, SWA_DIM
    qb = 2
    nq = qb * blk
    nt = n // nq
    tiles = b * nt
    assert ((qb + 2) * blk) % c == 0 and n // blk >= qb + 2
    cur = lambda u: jnp.minimum(u, tiles - 1)
    prev = lambda u: jnp.maximum(u - 1, 0)
    qw, kw = SWA_HEADS * d, SWA_KV_HEADS * d
    return pl.pallas_call(
        functools.partial(_swa_kernel, qb=qb, tiles=tiles, nt=nt),
        grid=(tiles + 1,),
        in_specs=[pl.BlockSpec(memory_space=pltpu.SMEM),
                  pl.BlockSpec((1, nq, qw), lambda u: (cur(u) // nt, cur(u) % nt, P_SWA_Q // qw)),
                  pl.BlockSpec((1, n, kw), lambda u: (cur(u) // nt, 0, P_SWA_K // kw)),
                  pl.BlockSpec((1, c, kw), lambda u: (cur(u) // nt, 0, P_SWA_K // kw)),
                  pl.BlockSpec((1, n, kw), lambda u: (prev(u) // nt, 0, P_SWA_V // kw)),
                  pl.BlockSpec((1, c, kw), lambda u: (prev(u) // nt, 0, P_SWA_V // kw)),
                  pl.BlockSpec((n, d), lambda u: (0, 0)),
                  pl.BlockSpec((n, d), lambda u: (0, 0))],
        out_specs=pl.BlockSpec((1, nq, qw), lambda u: (prev(u) // nt, prev(u) % nt, 0)),
        out_shape=jax.ShapeDtypeStruct((b, n, qw), BF16),
        scratch_shapes=[pltpu.VMEM((SWA_KV_HEADS, (qb + 2) * blk + c, SWA_GROUP * nq), F32),
                        pltpu.VMEM((SWA_KV_HEADS, 1, SWA_GROUP * nq), F32)],
        compiler_params=_params(1),
        name="swa_attention",
    )(sink, proj_l, proj_l, proj_c, proj_l, proj_c, cos, sin)


def _log_sigmoid(x):
    return jnp.minimum(x, 0.0) - jnp.log(1.0 + jnp.exp(-jnp.abs(x)))


def _gla_block_constants(rows):
    t = np.arange(rows)
    ti, tj = t[:, None], t[None, :]
    ci, cj = ti // GLA_CHUNK, tj // GLA_CHUNK
    ones, sel = [], []
    for reverse in (False, True):
        upto = (tj >= ti) if reverse else (tj <= ti)
        visible = (tj > ti) if reverse else (tj <= ti)
        earlier = (cj > ci) if reverse else (cj < ci)
        ones.append(np.concatenate([(ci == cj) & upto, upto], axis=0))
        sel.append(np.where((ci == cj) & visible, 1, np.where(earlier, 2, 0)))
    return jnp.asarray(np.stack(ones), BF16), jnp.asarray(np.stack(sel), jnp.int32)


def _gla_factors(q_ref, k_ref, misc_ref, i, wg, bg, ones, sel, reverse):
    rows = q_ref.shape[1]
    L = GLA_CHUNK
    assert rows == 4 * L
    width = GLA_HEADS * GLA_DK
    la = _log_sigmoid(_dot(misc_ref[i].astype(BF16), wg) + bg) * (1.0 / GLA_TAU)
    hi = la.astype(BF16)
    lo = (la - hi.astype(F32)).astype(BF16)
    sums = _dot(ones, jnp.concatenate([hi, lo], axis=1))
    sums = sums[:, :width] + sums[:, width:]
    b_in, b_abs = sums[:rows], sums[rows:]
    if reverse:
        g_mid, g_end = b_abs[rows // 2:rows // 2 + 1], b_abs[0:1]
    else:
        g_mid, g_end = b_abs[rows // 2 - 1:rows // 2], b_abs[rows - 1:rows]
    diag, cross = sel == 1, sel == 2
    q = q_ref[i].astype(F32) * GLA_DK ** -0.5
    k = k_ref[i].astype(F32)
    q_d, k_d = q * jnp.exp(b_in), k * jnp.exp(-b_in)
    q_x, k_x = q * jnp.exp(b_abs - g_mid), k * jnp.exp(g_mid - b_abs)
    q_s, k_s = q * jnp.exp(b_abs), k * jnp.exp(g_end - b_abs)
    decay = jnp.exp(g_end)
    return q_d, k_d, q_x, k_x, q_s, k_s, decay, diag, cross


def _gla_apply(factors, v_ref, state_ref, o_ref, i):
    q_d, k_d, q_x, k_x, q_s, k_s, decay, diag, cross = factors
    rows = q_d.shape[0]
    lane = lax.broadcasted_iota(jnp.int32, (rows, 2 * GLA_DK), 1)
    lane_s = lax.broadcasted_iota(jnp.int32, (GLA_DV, 2 * GLA_DK), 1)
    for pair in range(GLA_HEADS // 2):
        ps = slice(2 * GLA_DK * pair, 2 * GLA_DK * (pair + 1))
        kd_p, kx_p, ks_p = k_d[:, ps].astype(BF16), k_x[:, ps].astype(BF16), k_s[:, ps].astype(BF16)
        st = state_ref[i, pair]
        st_b = st.astype(BF16)
        ds = []
        for j in range(2):
            h = 2 * pair + j
            vs = slice(h * GLA_DV, (h + 1) * GLA_DV)
            mine = (lane // GLA_DK) == j
            qd_h = jnp.where(mine, q_d[:, ps], 0.0).astype(BF16)
            qx_h = jnp.where(mine, q_x[:, ps], 0.0).astype(BF16)
            qs_h = jnp.where(mine, q_s[:, ps], 0.0).astype(BF16)
            vh = v_ref[i, :, vs].astype(BF16)
            a = jnp.where(diag, _dot_nt(qd_h, kd_p), jnp.where(cross, _dot_nt(qx_h, kx_p), 0.0))
            o_ref[i, :, vs] = (_dot(a.astype(BF16), vh) + _dot_nt(qs_h, st_b)).astype(o_ref.dtype)
            ds.append(_dot_tn(vh, ks_p))
        state_ref[i, pair] = decay[:, ps] * st + jnp.where(lane_s < GLA_DK, ds[0], ds[1])


def _gla_kernel(qf_ref, kf_ref, vf_ref, mf_ref, qb_ref, kb_ref, vb_ref, mb_ref, wg_ref, bg_ref, ones_ref, sel_ref,
                sf0_ref, sb0_ref, of_ref, ob_ref, sf_ref, sb_ref, stf, stb):
    t = pl.program_id(1)

    @pl.when(t == 0)
    def _():
        stf[...] = sf0_ref[...]
        stb[...] = sb0_ref[...]

    width = GLA_HEADS * GLA_DK
    entries = range(qf_ref.shape[0])
    fwd = [_gla_factors(qf_ref, kf_ref, mf_ref, i, wg_ref[:, :width], bg_ref[:, :width], ones_ref[0], sel_ref[0],
                        False) for i in entries]
    bwd = [_gla_factors(qb_ref, kb_ref, mb_ref, i, wg_ref[:, width:], bg_ref[:, width:], ones_ref[1], sel_ref[1],
                        True) for i in entries]
    for i in entries:
        _gla_apply(fwd[i], vf_ref, stf, of_ref, i)
        _gla_apply(bwd[i], vb_ref, stb, ob_ref, i)

    @pl.when(t == pl.num_programs(1) - 1)
    def _():
        sf_ref[...] = stf[...]
        sb_ref[...] = stb[...]


def _gla_scan(proj, wg, bg, sf0, sb0, layer):
    b, r, _ = proj.shape
    tm = 256
    nt = r // tm
    kw, vw = GLA_HEADS * GLA_DK, GLA_HEADS * GLA_DV
    fwd = lambda t: t
    bwd = lambda t: nt - 1 - t

    nb = 2 if b % 2 == 0 else 1
    ones, sel = _gla_block_constants(tm)

    def specs(order):
        return [pl.BlockSpec((nb, tm, kw), lambda i, t: (i, order(t), P_GLA_Q // kw)),
                pl.BlockSpec((nb, tm, kw), lambda i, t: (i, order(t), P_GLA_K // kw)),
                pl.BlockSpec((nb, tm, vw), lambda i, t: (i, order(t), P_GLA_V // vw)),
                pl.BlockSpec((nb, tm, 128), lambda i, t: (i, order(t), P_MISC // 128))]

    state_spec = pl.BlockSpec((nb,) + GLA_STATE, lambda i, t: (i, 0, 0, 0))
    state_shape = jax.ShapeDtypeStruct((b,) + GLA_STATE, F32)
    return pl.pallas_call(
        _gla_kernel,
        grid=(b // nb, nt),
        in_specs=specs(fwd) + specs(bwd) + [
            pl.BlockSpec((None,) + wg.shape[1:], lambda i, t: (layer, 0, 0)),
            pl.BlockSpec((None,) + bg.shape[1:], lambda i, t: (layer, 0, 0)),
            pl.BlockSpec(ones.shape, lambda i, t: (0, 0, 0)),
            pl.BlockSpec(sel.shape, lambda i, t: (0, 0, 0)),
            state_spec, state_spec],
        out_specs=[pl.BlockSpec((nb, tm, vw), lambda i, t: (i, fwd(t), 0)),
                   pl.BlockSpec((nb, tm, vw), lambda i, t: (i, bwd(t), 0)),
                   state_spec, state_spec],
        out_shape=[jax.ShapeDtypeStruct((b, r, vw), BF16), jax.ShapeDtypeStruct((b, r, vw), BF16),
                   state_shape, state_shape],
        scratch_shapes=[pltpu.VMEM((nb,) + GLA_STATE, F32), pltpu.VMEM((nb,) + GLA_STATE, F32)],
        compiler_params=_params(2),
        name="gla_scan",
    )(proj, proj, proj, proj, proj, proj, proj, proj, wg, bg, ones, sel, sf0, sb0)


def _outproj_kernel(x_ref, mod_ref, mla_ref, swa_ref, of_ref, ob_ref, r_ref, g_ref, w_ref, o_ref):
    o = of_ref[0].astype(F32) + ob_ref[0].astype(F32)
    gate = _silu(r_ref[0].astype(F32))
    parts = []
    for h in range(GLA_HEADS):
        vs = slice(h * GLA_DV, (h + 1) * GLA_DV)
        parts.append((_rms(o[:, vs], g_ref[:, vs]) * gate[:, vs]).astype(BF16))
    gla = jnp.concatenate(parts, axis=1)
    a, b = mla_ref.shape[2], mla_ref.shape[2] + swa_ref.shape[2]
    mix = _dot(mla_ref[0], w_ref[0:a, :]) + _dot(swa_ref[0], w_ref[a:b, :]) + _dot(gla, w_ref[b:, :])
    o_ref[0] = x_ref[0] + mod_ref[0, 2:3, :] * mix


def _out_projection(x, mod, mla, swa, o_f, o_b, proj, g_out, w_out, layer):
    bx, r, d = x.shape
    tm = 512
    row = lambda width: pl.BlockSpec((1, tm, width), lambda b, t: (b, t, 0))
    gw = GLA_HEADS * GLA_DV
    return pl.pallas_call(
        _outproj_kernel,
        grid=(bx, r // tm),
        in_specs=[row(d),
                  pl.BlockSpec((1, N_MOD, d), lambda b, t: (b, 0, 0)),
                  row(mla.shape[2]), row(swa.shape[2]), row(gw), row(gw),
                  pl.BlockSpec((1, tm, gw), lambda b, t: (b, t, P_GLA_R // gw)),
                  pl.BlockSpec((1, gw), lambda b, t: (0, 0)),
                  pl.BlockSpec((None,) + w_out.shape[1:], lambda b, t: (layer, 0, 0))],
        out_specs=row(d),
        out_shape=jax.ShapeDtypeStruct(x.shape, F32),
        compiler_params=_params(2),
        name="out_projection",
    )(x, mod, mla, swa, o_f, o_b, proj, g_out, w_out)


def _ffn_kernel(x_ref, mod_ref, g_ref, wg_ref, wu_ref, wd_ref, gf_ref, o_ref, h_ref, *, final_norm):
    f = pl.program_id(2)
    row_chunks = [slice(r, r + ROW_CHUNK) for r in range(0, x_ref.shape[1], ROW_CHUNK)]
    chunk = (ROW_CHUNK, x_ref.shape[2])

    def gated_partial():
        h = h_ref[...]
        act = _silu(_dot(h, wg_ref[...])) * _dot(h, wu_ref[...])
        return mod_ref[0, 5:6, :] * _dot(act.astype(BF16), wd_ref[...])

    @pl.when(f == 0)
    def _():
        gain = jnp.broadcast_to(g_ref[...] * (1.0 + mod_ref[0, 4:5, :]), chunk)
        shift = jnp.broadcast_to(mod_ref[0, 3:4, :], chunk)
        for rows in row_chunks:
            x = x_ref[0, rows, :]
            rs = lax.rsqrt(jnp.mean(x * x, axis=-1, keepdims=True) + EPS)
            h_ref[rows, :] = (x * rs * gain + shift).astype(BF16)
        o_ref[0] = x_ref[0] + gated_partial()

    @pl.when(f > 0)
    def _():
        o_ref[0] += gated_partial()

    if final_norm:
        @pl.when(f == pl.num_programs(2) - 1)
        def _():
            g_fin = jnp.broadcast_to(gf_ref[...], chunk)
            for rows in row_chunks:
                y = o_ref[0, rows, :]
                o_ref[0, rows, :] = y * lax.rsqrt(jnp.mean(y * y, axis=-1, keepdims=True) + EPS) * g_fin


def _ffn(x, mod, g, w_gu, w_down, g_final, final_norm, layer):
    bx, r, d = x.shape
    hidden = w_down.shape[1]
    tm, tf = 512, 512
    nf = hidden // tf
    return pl.pallas_call(
        functools.partial(_ffn_kernel, final_norm=final_norm),
        grid=(bx, r // tm, nf),
        in_specs=[pl.BlockSpec((1, tm, d), lambda b, t, f: (b, t, 0)),
                  pl.BlockSpec((1, N_MOD, d), lambda b, t, f: (b, 0, 0)),
                  pl.BlockSpec((1, d), lambda b, t, f: (0, 0)),
                  pl.BlockSpec((None, d, tf), lambda b, t, f: (layer, 0, f)),
                  pl.BlockSpec((None, d, tf), lambda b, t, f: (layer, 0, nf + f)),
                  pl.BlockSpec((None, tf, d), lambda b, t, f: (layer, f, 0)),
                  pl.BlockSpec((1, d), lambda b, t, f: (0, 0))],
        out_specs=pl.BlockSpec((1, tm, d), lambda b, t, f: (b, t, 0)),
        out_shape=jax.ShapeDtypeStruct(x.shape, F32),
        scratch_shapes=[pltpu.VMEM((tm, d), BF16)],
        compiler_params=_params(3),
        name="ffn",
    )(x, mod, g, w_gu, w_gu, w_down, g_final)


def _rope_tables(n, dim):
    half = dim // 4
    freqs = ROPE_THETA ** (-jnp.arange(half, dtype=F32) / half)
    pos = jnp.arange(n, dtype=jnp.int32)
    ang_r = (pos // GRID_W).astype(F32)[:, None] * freqs
    ang_c = (pos % GRID_W).astype(F32)[:, None] * freqs
    cos = jnp.concatenate([jnp.cos(ang_r)] * 2 + [jnp.cos(ang_c)] * 2, axis=1)
    sin = jnp.concatenate([-jnp.sin(ang_r), jnp.sin(ang_r), -jnp.sin(ang_c), jnp.sin(ang_c)], axis=1)
    reps = 128 // dim
    return jnp.tile(cos, (1, reps)), jnp.tile(sin, (1, reps))


def _relayout_w_in(w_in):
    depth, d, _ = w_in.shape
    sizes = (MLA_RANK, MLA_RANK, MLA_ROPE, SWA_HEADS * SWA_DIM, SWA_KV_HEADS * SWA_DIM, SWA_KV_HEADS * SWA_DIM,
             GLA_HEADS * GLA_DK, GLA_HEADS * GLA_DK, GLA_HEADS * GLA_DV, 2 * GLA_GATE_RANK, GLA_HEADS * GLA_DV)
    offs = [0]
    for s in sizes:
        offs.append(offs[-1] + s)
    span = lambda i, j: w_in[:, :, offs[i]:offs[j]].astype(BF16)
    pad = jnp.zeros((depth, d, P_GLA_Q - P_MISC - MLA_ROPE - 2 * GLA_GATE_RANK), BF16)
    out = jnp.concatenate([span(3, 5), span(0, 2), span(5, 6), span(2, 3), span(9, 10), pad,
                           span(6, 9), span(10, 11)], axis=-1)
    assert out.shape[-1] == P_WIDTH
    return out


def _relayout_mla(w_uq, w_ukv):
    depth, rk, _ = w_uq.shape
    uq = w_uq.reshape(depth, rk, MLA_HEADS, MLA_QK)
    uq = jnp.concatenate([uq[..., :MLA_NOPE].reshape(depth, rk, -1), uq[..., MLA_NOPE:].reshape(depth, rk, -1)], -1)
    ukv = w_ukv.reshape(depth, rk, MLA_HEADS, MLA_NOPE + MLA_V)
    ukv = jnp.concatenate([ukv[..., :MLA_NOPE].reshape(depth, rk, -1), ukv[..., MLA_NOPE:].reshape(depth, rk, -1)], -1)
    return uq.astype(BF16), ukv.astype(BF16)


def _relayout_gate(w_f, b_f, w_b, b_b):
    depth, rank, width = w_f.shape
    wg = jnp.zeros((depth, 128, 2 * width), F32)
    wg = wg.at[:, MISC_GATE_OFF:MISC_GATE_OFF + rank, :width].set(w_f)
    wg = wg.at[:, MISC_GATE_OFF + rank:MISC_GATE_OFF + 2 * rank, width:].set(w_b)
    bg = jnp.concatenate([b_f, b_b], axis=-1).reshape(depth, 1, 2 * width)
    return wg.astype(BF16), bg


def kernel(x, c, ctx, c_ctx, w_mod, b_mod, g_mix, g_ffn, w_in, g_mla_q, g_mla_kv, w_mla_uq, w_mla_ukv,
           swa_sink, w_gla_gate_f, b_gla_gate_f, w_gla_gate_b, b_gla_gate_b, g_gla_out, w_out, w_ffn_gu,
           w_ffn_down, g_final):
    B, N, D = x.shape
    C = ctx.shape[1]
    depth = w_mod.shape[0]

    cvec = jnp.concatenate([c, c_ctx[None, :], jnp.zeros((16 - B - 1, D), F32)], axis=0)
    mod = _modulation(cvec, w_mod, b_mod).reshape(depth, 16, N_MOD, D)

    w_in_p = _relayout_w_in(w_in)
    w_uq_p, w_ukv_p = _relayout_mla(w_mla_uq, w_mla_ukv)
    wg_p, bg_p = _relayout_gate(w_gla_gate_f, b_gla_gate_f, w_gla_gate_b, b_gla_gate_b)
    w_out_b = w_out.astype(BF16)
    w_gu_b = w_ffn_gu.astype(BF16)
    w_down_b = w_ffn_down.astype(BF16)

    cos_m, sin_m = _rope_tables(N, MLA_ROPE)
    cos_s, sin_s = _rope_tables(N, SWA_DIM)
    cos_id, sin_id = jnp.ones((C, 128), F32), jnp.zeros((C, 128), F32)
    state0 = jnp.zeros((B,) + GLA_STATE, F32)
    g_fin = g_final.reshape(1, D)

    xc = ctx.reshape(1, B * C, D)
    for l in range(depth):
        last = l == depth - 1
        mod_l, mod_c = mod[l, :B], mod[l, B:B + 1]
        g_mix_l, g_ffn_l = g_mix[l].reshape(1, D), g_ffn[l].reshape(1, D)
        g_q, g_kv = g_mla_q[l].reshape(1, -1), g_mla_kv[l].reshape(1, -1)
        g_out = g_gla_out[l].reshape(1, -1)
        sink = swa_sink[l].reshape(1, -1)

        proj_l = _in_projection(x, mod_l, g_mix_l, w_in_p, l)
        proj_c = _in_projection(xc, mod_c, g_mix_l, w_in_p, l).reshape(B, C, P_WIDTH)

        q_l, k_l, v_l = _mla_prep(proj_l, cos_m, sin_m, g_q, g_kv, w_uq_p, w_ukv_p, l)
        q_c, k_c, v_c = _mla_prep(proj_c, cos_id, sin_id, g_q, g_kv, w_uq_p, w_ukv_p, l)
        mla_l = _mla_attention(q_l, k_l, v_l, k_c, v_c)
        swa_l = _swa_attention(proj_l, proj_c, cos_s, sin_s, sink)

        of_c, ob_c, s_f, s_b = _gla_scan(proj_c, wg_p, bg_p, state0, state0, l)
        of_l, ob_l, _, _ = _gla_scan(proj_l, wg_p, bg_p, s_f, s_b, l)

        x = _out_projection(x, mod_l, mla_l, swa_l, of_l, ob_l, proj_l, g_out, w_out_b, l)
        x = _ffn(x, mod_l, g_ffn_l, w_gu_b, w_down_b, g_fin, last, l)

        if not last:
            mla_c = _mla_ctx_attention(q_c, k_c, v_c)
            swa_c = _swa_ctx_attention(proj_c, sink)
            flat = lambda a: a.reshape(1, B * C, a.shape[-1])
            xc = _out_projection(xc, mod_c, flat(mla_c), flat(swa_c), flat(of_c), flat(ob_c), flat(proj_c),
                                 g_out, w_out_b, l)
            xc = _ffn(xc, mod_c, g_ffn_l, w_gu_b, w_down_b, g_fin, False, l)
    return x
```

```python
import functools

import jax
import jax.numpy as jnp
import numpy as np
from jax import lax
from jax.experimental import pallas as pl
from jax.experimental.pallas import tpu as pltpu

F32 = jnp.float32
BF16 = jnp.bfloat16

GRID_W = 64
EPS = 1e-6
ROPE_THETA = 10000.0
LOG2E = 1.4426950408889634

MLA_HEADS = 6
MLA_RANK = 512
MLA_NOPE = 128
MLA_ROPE = 64
MLA_V = 128
MLA_QK = MLA_NOPE + MLA_ROPE
MLA_VA = MLA_V + 16

SWA_HEADS = 6
SWA_KV_HEADS = 2
SWA_GROUP = SWA_HEADS // SWA_KV_HEADS
SWA_DIM = 128
SWA_BLOCK = 128

GLA_HEADS = 4
GLA_DK = 64
GLA_DV = 128
GLA_GATE_RANK = 16
GLA_TAU = 16.0
GLA_CHUNK = 64
GLA_STATE = (GLA_HEADS // 2, GLA_DV, 2 * GLA_DK)

N_MOD = 6

P_SWA_Q = 0
P_SWA_K = 768
P_CQ = 1024
P_CKV = 1536
P_SWA_V = 2048
P_MISC = 2304
P_GLA_Q = 2560
P_GLA_K = 2816
P_GLA_V = 3072
P_GLA_R = 3584
P_WIDTH = 4096
MISC_GATE_OFF = MLA_ROPE

VMEM_LIMIT = 56 * 1024 * 1024
ROW_CHUNK = 16


def _params(n_axes, flags=None):
    return pltpu.CompilerParams(dimension_semantics=("arbitrary",) * n_axes,
                                vmem_limit_bytes=VMEM_LIMIT, flags=flags)


def _silu(x):
    return x / (1.0 + jnp.exp(-x))


def _rms(x, g):
    ms = jnp.mean(x * x, axis=-1, keepdims=True)
    return x * lax.rsqrt(ms + EPS) * g


def _dot(a, b):
    return jnp.dot(a, b, preferred_element_type=F32)


def _dot_nt(a, b):
    return lax.dot_general(a, b, (((1,), (1,)), ((), ())), preferred_element_type=F32)


def _dot_tn(a, b):
    return lax.dot_general(a, b, (((0,), (0,)), ((), ())), preferred_element_type=F32)


def _mod_kernel(c_ref, w_ref, b_ref, o_ref):
    a = _silu(c_ref[...]).astype(BF16)
    o_ref[0] = _dot(a, w_ref[0].astype(BF16)) + b_ref[0]


def _modulation(cvec, w_mod, b_mod):
    depth, d, width = w_mod.shape
    rows = cvec.shape[0]
    tn = 1024
    return pl.pallas_call(
        _mod_kernel,
        grid=(depth, width // tn),
        in_specs=[pl.BlockSpec((rows, d), lambda l, j: (0, 0)),
                  pl.BlockSpec((1, d, tn), lambda l, j: (l, 0, j)),
                  pl.BlockSpec((1, 1, tn), lambda l, j: (l, 0, j))],
        out_specs=pl.BlockSpec((1, rows, tn), lambda l, j: (l, 0, j)),
        out_shape=jax.ShapeDtypeStruct((depth, rows, width), F32),
        compiler_params=_params(2),
        name="modulation",
    )(cvec, w_mod, b_mod.reshape(depth, 1, width))


def _inproj_kernel(x_ref, mod_ref, g_ref, w_ref, o_ref):
    h = _rms(x_ref[0], g_ref[...]) * (1.0 + mod_ref[0, 1:2, :]) + mod_ref[0, 0:1, :]
    o_ref[0] = _dot(h.astype(BF16), w_ref[...]).astype(o_ref.dtype)


def _in_projection(x, mod, g, w, layer):
    bx, r, d = x.shape
    tm, tn = 512, P_WIDTH
    return pl.pallas_call(
        _inproj_kernel,
        grid=(P_WIDTH // tn, bx, r // tm),
        in_specs=[pl.BlockSpec((1, tm, d), lambda j, b, t: (b, t, 0)),
                  pl.BlockSpec((1, N_MOD, d), lambda j, b, t: (b, 0, 0)),
                  pl.BlockSpec((1, d), lambda j, b, t: (0, 0)),
                  pl.BlockSpec((None, d, tn), lambda j, b, t: (layer, 0, j), pipeline_mode=pl.Buffered(1))],
        out_specs=pl.BlockSpec((1, tm, tn), lambda j, b, t: (b, t, j)),
        out_shape=jax.ShapeDtypeStruct((bx, r, P_WIDTH), BF16),
        compiler_params=_params(3),
        name="in_projection",
    )(x, mod, g, w)


def _rope(x, cos, sin, half):
    lane = lax.broadcasted_iota(jnp.int32, x.shape, 1)
    first = (lane % (2 * half)) < half
    width = x.shape[1]
    rot = jnp.where(first, pltpu.roll(x, width - half, 1), pltpu.roll(x, half, 1))
    return x * cos + rot * sin


def _mla_prep_kernel(cq_ref, ckv_ref, misc_ref, cos_ref, sin_ref, gq_ref, gkv_ref, wuq_ref, wukv_ref,
                     qt_ref, k_ref, vt_ref):
    scale = MLA_QK ** -0.5 * LOG2E
    cos, sin = cos_ref[...], sin_ref[...]
    half = MLA_ROPE // 4
    qf = _dot(_rms(cq_ref[0].astype(F32), gq_ref[...]).astype(BF16), wuq_ref[...])
    kvf = _dot(_rms(ckv_ref[0].astype(F32), gkv_ref[...]).astype(BF16), wukv_ref[...])
    kr = _rope(misc_ref[0].astype(F32), cos, sin, half)[:, :MLA_ROPE].astype(BF16)
    nope_w = MLA_HEADS * MLA_NOPE
    for pair in range(MLA_HEADS // 2):
        qr = _rope(qf[:, nope_w + 128 * pair: nope_w + 128 * (pair + 1)], cos, sin, half) * scale
        qrt = qr.astype(BF16).T
        for j in range(2):
            qt_ref[0, 2 * pair + j, MLA_NOPE:MLA_QK, :] = qrt[MLA_ROPE * j: MLA_ROPE * (j + 1), :]
    for h in range(MLA_HEADS):
        qt_ref[0, h, 0:MLA_NOPE, :] = (qf[:, MLA_NOPE * h: MLA_NOPE * (h + 1)] * scale).astype(BF16).T
        k_ref[0, h, :, 0:MLA_NOPE] = kvf[:, MLA_NOPE * h: MLA_NOPE * (h + 1)].astype(BF16)
        k_ref[0, h, :, MLA_NOPE:MLA_QK] = kr
        vt_ref[0, h, 0:MLA_V, :] = kvf[:, nope_w + MLA_V * h: nope_w + MLA_V * (h + 1)].astype(BF16).T
        vt_ref[0, h, MLA_V:, :] = jnp.ones((MLA_VA - MLA_V, vt_ref.shape[3]), BF16)


def _mla_prep(proj, cos, sin, g_q, g_kv, w_uq, w_ukv, layer):
    b, r, _ = proj.shape
    tm = min(r, 1024)
    rk = MLA_RANK
    const = lambda shape: pl.BlockSpec(shape, lambda i, t: (0,) * len(shape))
    stacked = lambda w: pl.BlockSpec((None,) + w.shape[1:], lambda i, t: (layer, 0, 0))
    return pl.pallas_call(
        _mla_prep_kernel,
        grid=(b, r // tm),
        in_specs=[pl.BlockSpec((1, tm, rk), lambda i, t: (i, t, P_CQ // rk)),
                  pl.BlockSpec((1, tm, rk), lambda i, t: (i, t, P_CKV // rk)),
                  pl.BlockSpec((1, tm, 128), lambda i, t: (i, t, P_MISC // 128)),
                  pl.BlockSpec((tm, 128), lambda i, t: (t, 0)),
                  pl.BlockSpec((tm, 128), lambda i, t: (t, 0)),
                  const((1, rk)), const((1, rk)),
                  stacked(w_uq), stacked(w_ukv)],
        out_specs=[pl.BlockSpec((1, MLA_HEADS, MLA_QK, tm), lambda i, t: (i, 0, 0, t)),
                   pl.BlockSpec((1, MLA_HEADS, tm, MLA_QK), lambda i, t: (i, 0, t, 0)),
                   pl.BlockSpec((1, MLA_HEADS, MLA_VA, tm), lambda i, t: (i, 0, 0, t))],
        out_shape=[jax.ShapeDtypeStruct((b, MLA_HEADS, MLA_QK, r), BF16),
                   jax.ShapeDtypeStruct((b, MLA_HEADS, r, MLA_QK), BF16),
                   jax.ShapeDtypeStruct((b, MLA_HEADS, MLA_VA, r), BF16)],
        compiler_params=_params(2),
        name="mla_prep",
    )(proj, proj, proj, cos, sin, g_q, g_kv, w_uq, w_ukv)


def _softmax_pv(s_parts, vt_parts):
    m = functools.reduce(jnp.maximum, [jnp.max(s, axis=0, keepdims=True) for s in s_parts])
    acc = sum(_dot(vt, jnp.exp2(s - m).astype(BF16)) for s, vt in zip(s_parts, vt_parts))
    return (acc[:MLA_V] / acc[MLA_V:MLA_V + 1]).T


def _mla_attn_kernel(qt_ref, kl_ref, kc_ref, vlt_ref, vct_ref, o_ref, s_scr, m_scr, *, tk):
    @pl.when(pl.program_id(0) == 0)
    def _():
        s_scr[...] = jnp.zeros_like(s_scr)
        m_scr[...] = jnp.zeros_like(m_scr)

    n_lat = kl_ref.shape[2] // tk
    qt = qt_ref[0, 0]
    tq = qt.shape[1]
    m_prev = m_scr[...]
    m_new = jnp.full((1, tq), -jnp.inf, F32)
    acc = jnp.zeros((MLA_VA, tq), F32)
    for j in range(n_lat + 1):
        rows = slice(j * tk, (j + 1) * tk)
        p = jnp.exp2(s_scr[rows, :] - m_prev).astype(BF16)
        k = kc_ref[0, 0] if j == n_lat else kl_ref[0, 0, rows, :]
        vt = vct_ref[0, 0] if j == n_lat else vlt_ref[0, 0, :, rows]
        s = _dot(k, qt)
        s_scr[rows, :] = s
        m_new = jnp.maximum(m_new, jnp.max(s, axis=0, keepdims=True))
        acc = acc + _dot(vt, p)
    m_scr[...] = m_new
    o_ref[0] = (acc[:MLA_V] / acc[MLA_V:MLA_V + 1]).T.astype(o_ref.dtype)


def _mla_attention(qt, k_l, vt_l, k_c, vt_c):
    b, h, dq, n = qt.shape
    c = k_c.shape[2]
    tq = min(n, 1024)
    nt = n // tq
    tiles = b * h * nt

    def cur(u):
        t = jnp.minimum(u, tiles - 1)
        return t // (h * nt), (t // nt) % h, t % nt

    def prev(u):
        t = jnp.maximum(u - 1, 0)
        return t // (h * nt), (t // nt) % h, t % nt

    assert n % c == 0
    return pl.pallas_call(
        functools.partial(_mla_attn_kernel, tk=c),
        grid=(tiles + 1,),
        in_specs=[pl.BlockSpec((1, 1, dq, tq), lambda u: (cur(u)[0], cur(u)[1], 0, cur(u)[2])),
                  pl.BlockSpec((1, 1, n, dq), lambda u: (cur(u)[0], cur(u)[1], 0, 0)),
                  pl.BlockSpec((1, 1, c, dq), lambda u: (cur(u)[0], cur(u)[1], 0, 0)),
                  pl.BlockSpec((1, 1, MLA_VA, n), lambda u: (prev(u)[0], prev(u)[1], 0, 0)),
                  pl.BlockSpec((1, 1, MLA_VA, c), lambda u: (prev(u)[0], prev(u)[1], 0, 0))],
        out_specs=pl.BlockSpec((1, tq, MLA_V), lambda u: (prev(u)[0], prev(u)[2], prev(u)[1])),
        out_shape=jax.ShapeDtypeStruct((b, n, h * MLA_V), BF16),
        scratch_shapes=[pltpu.VMEM((n + c, tq), F32), pltpu.VMEM((1, tq), F32)],
        compiler_params=_params(1),
        name="mla_attention",
    )(qt, k_l, k_c, vt_l, vt_c)


def _mla_ctx_kernel(qt_ref, k_ref, vt_ref, o_ref):
    o_ref[0] = _softmax_pv([_dot(k_ref[0, 0], qt_ref[0, 0])], [vt_ref[0, 0]]).astype(o_ref.dtype)


def _mla_ctx_attention(qt, k, vt):
    b, h, dq, c = qt.shape
    return pl.pallas_call(
        _mla_ctx_kernel,
        grid=(b, h),
        in_specs=[pl.BlockSpec((1, 1, dq, c), lambda i, j: (i, j, 0, 0)),
                  pl.BlockSpec((1, 1, c, dq), lambda i, j: (i, j, 0, 0)),
                  pl.BlockSpec((1, 1, MLA_VA, c), lambda i, j: (i, j, 0, 0))],
        out_specs=pl.BlockSpec((1, c, MLA_V), lambda i, j: (i, 0, j)),
        out_shape=jax.ShapeDtypeStruct((b, c, h * MLA_V), BF16),
        compiler_params=_params(2),
        name="mla_ctx_attention",
    )(qt, k, vt)


def _swa_ctx_kernel(sink_ref, q_ref, k_ref, v_ref, o_ref):
    q = (q_ref[0].astype(F32) * SWA_DIM ** -0.5).astype(BF16)
    s = _dot_nt(q, k_ref[0].astype(BF16))
    sink = sink_ref[0, pl.program_id(1)]
    m = jnp.maximum(jnp.max(s, axis=-1, keepdims=True), sink)
    p = jnp.exp(s - m)
    l = jnp.sum(p, axis=-1, keepdims=True) + jnp.exp(sink - m)
    o_ref[0] = (_dot(p.astype(BF16), v_ref[0].astype(BF16)) / l).astype(o_ref.dtype)


def _swa_ctx_attention(proj_c, sink):
    b, c, _ = proj_c.shape
    d = SWA_DIM
    return pl.pallas_call(
        _swa_ctx_kernel,
        grid=(b, SWA_HEADS),
        in_specs=[pl.BlockSpec(memory_space=pltpu.SMEM),
                  pl.BlockSpec((1, c, d), lambda i, j: (i, 0, P_SWA_Q // d + j)),
                  pl.BlockSpec((1, c, d), lambda i, j: (i, 0, P_SWA_K // d + j // SWA_GROUP)),
                  pl.BlockSpec((1, c, d), lambda i, j: (i, 0, P_SWA_V // d + j // SWA_GROUP))],
        out_specs=pl.BlockSpec((1, c, d), lambda i, j: (i, 0, j)),
        out_shape=jax.ShapeDtypeStruct((b, c, SWA_HEADS * d), BF16),
        compiler_params=_params(2),
        name="swa_ctx_attention",
    )(sink, proj_c, proj_c, proj_c)


def _swa_window_start(t, qb, nb):
    return pl.multiple_of(jnp.clip(t * qb - 1, 0, nb - (qb + 2)) * SWA_BLOCK, SWA_BLOCK)


def _swa_kernel(sink_ref, q_ref, k_ref, kc_ref, v_ref, vc_ref, cos_ref, sin_ref, o_ref, s_scr, m_scr, *,
                qb, tiles, nt):
    u = pl.program_id(0)

    @pl.when(u == 0)
    def _():
        s_scr[...] = jnp.zeros_like(s_scr)
        m_scr[...] = jnp.zeros_like(m_scr)

    blk, d = SWA_BLOCK, SWA_DIM
    half = d // 4
    nq = qb * blk
    nw = (qb + 2) * blk
    nb = k_ref.shape[1] // blk
    ck = kc_ref.shape[1]
    cur, prev = jnp.minimum(u, tiles - 1), jnp.maximum(u - 1, 0)
    t_cur, t_prev = cur % nt, prev % nt
    q_start = pl.multiple_of(t_cur * nq, blk)
    k_start = _swa_window_start(t_cur, qb, nb)
    v_start = _swa_window_start(t_prev, qb, nb)
    groups = range(SWA_KV_HEADS)

    def sink_row(g):
        return jnp.concatenate(
            [jnp.full((1, nq), sink_ref[0, g * SWA_GROUP + i] * LOG2E, F32) for i in range(SWA_GROUP)], axis=1)

    def values_t(v):
        return jnp.concatenate([v.astype(F32).T.astype(BF16), jnp.ones((16, v.shape[0]), BF16)], axis=0)

    scale = d ** -0.5 * LOG2E
    cos_q, sin_q = cos_ref[pl.ds(q_start, nq), :], sin_ref[pl.ds(q_start, nq), :]
    q = [jnp.concatenate(
        [(_rope(q_ref[0, :, d * h: d * (h + 1)].astype(F32), cos_q, sin_q, half) * scale).astype(BF16)
         for h in range(g * SWA_GROUP, (g + 1) * SWA_GROUP)], axis=0) for g in groups]
    key = lax.broadcasted_iota(jnp.int32, (ck, nq), 0)
    tok = lax.broadcasted_iota(jnp.int32, (ck, nq), 1)
    m_prev = [m_scr[g] for g in groups]
    m_new = [sink_row(g) for g in groups]
    acc = [jnp.zeros((d + 16, SWA_GROUP * nq), F32) for g in groups]
    n_win = nw // ck
    for j in range(n_win + 1):
        rows = slice(j * ck, (j + 1) * ck)
        if j < n_win:
            kr = pl.ds(k_start + j * ck, ck)
            cos_k, sin_k = cos_ref[kr, :], sin_ref[kr, :]
            dist = key - tok + (k_start + j * ck - q_start)
            valid = (dist <= blk) & (dist >= -blk)
        for g in groups:
            cols = slice(d * g, d * (g + 1))
            p = jnp.exp2(s_scr[g, rows, :] - m_prev[g]).astype(BF16)
            if j < n_win:
                vt = values_t(v_ref[0, pl.ds(v_start + j * ck, ck), cols])
                k = _rope(k_ref[0, kr, cols].astype(F32), cos_k, sin_k, half).astype(BF16)
                s = _dot_nt(k, q[g])
                s = jnp.concatenate(
                    [jnp.where(valid, s[:, nq * i: nq * (i + 1)], -jnp.inf) for i in range(SWA_GROUP)], axis=1)
            else:
                vt = values_t(vc_ref[0, :, cols])
                s = _dot_nt(kc_ref[0, :, cols].astype(BF16), q[g])
            s_scr[g, rows, :] = s
            m_new[g] = jnp.maximum(m_new[g], jnp.max(s, axis=0, keepdims=True))
            acc[g] = acc[g] + _dot(vt, p)
    for g in groups:
        m_scr[g] = m_new[g]
        out_t = acc[g][:d] / (acc[g][d:d + 1] + jnp.exp2(sink_row(g) - m_prev[g]))
        for i in range(SWA_GROUP):
            h = g * SWA_GROUP + i
            o_ref[0, :, d * h: d * (h + 1)] = out_t[:, nq * i: nq * (i + 1)].T.astype(o_ref.dtype)


def _swa_attention(proj_l, proj_c, cos, sin, sink):
    b, n, _ = proj_l.shape
    c = proj_c.shape[1]
    blk, d = SWA_BLOCK, SWA_DIM
    qb = 2
    nq = qb * blk
    nt = n // nq
    tiles = b * nt
    assert ((qb + 2) * blk) % c == 0 and n // blk >= qb + 2
    cur = lambda u: jnp.minimum(u, tiles - 1)
    prev = lambda u: jnp.maximum(u - 1, 0)
    qw, kw = SWA_HEADS * d, SWA_KV_HEADS * d
    return pl.pallas_call(
        functools.partial(_swa_kernel, qb=qb, tiles=tiles, nt=nt),
        grid=(tiles + 1,),
        in_specs=[pl.BlockSpec(memory_space=pltpu.SMEM),
                  pl.BlockSpec((1, nq, qw), lambda u: (cur(u) // nt, cur(u) % nt, P_SWA_Q // qw)),
                  pl.BlockSpec((1, n, kw), lambda u: (cur(u) // nt, 0, P_SWA_K // kw)),
                  pl.BlockSpec((1, c, kw), lambda u: (cur(u) // nt, 0, P_SWA_K // kw)),
                  pl.BlockSpec((1, n, kw), lambda u: (prev(u) // nt, 0, P_SWA_V // kw)),
                  pl.BlockSpec((1, c, kw), lambda u: (prev(u) // nt, 0, P_SWA_V // kw)),
                  pl.BlockSpec((n, d), lambda u: (0, 0)),
                  pl.BlockSpec((n, d), lambda u: (0, 0))],
        out_specs=pl.BlockSpec((1, nq, qw), lambda u: (prev(u) // nt, prev(u) % nt, 0)),
        out_shape=jax.ShapeDtypeStruct((b, n, qw), BF16),
        scratch_shapes=[pltpu.VMEM((SWA_KV_HEADS, (qb + 2) * blk + c, SWA_GROUP * nq), F32),
                        pltpu.VMEM((SWA_KV_HEADS, 1, SWA_GROUP * nq), F32)],
        compiler_params=_params(1),
        name="swa_attention",
    )(sink, proj_l, proj_l, proj_c, proj_l, proj_c, cos, sin)


def _log_sigmoid(x):
    return jnp.minimum(x, 0.0) - jnp.log(1.0 + jnp.exp(-jnp.abs(x)))


def _gla_block_constants(rows):
    t = np.arange(rows)
    ti, tj = t[:, None], t[None, :]
    ci, cj = ti // GLA_CHUNK, tj // GLA_CHUNK
    ones, sel = [], []
    for reverse in (False, True):
        upto = (tj >= ti) if reverse else (tj <= ti)
        visible = (tj > ti) if reverse else (tj <= ti)
        earlier = (cj > ci) if reverse else (cj < ci)
        ones.append(np.concatenate([(ci == cj) & upto, upto], axis=0))
        sel.append(np.where((ci == cj) & visible, 1, np.where(earlier, 2, 0)))
    return jnp.asarray(np.stack(ones), BF16), jnp.asarray(np.stack(sel), jnp.int32)


def _gla_factors(q_ref, k_ref, misc_ref, i, wg, bg, ones, sel, reverse):
    rows = q_ref.shape[1]
    L = GLA_CHUNK
    assert rows == 4 * L
    width = GLA_HEADS * GLA_DK
    la = _log_sigmoid(_dot(misc_ref[i].astype(BF16), wg) + bg) * (1.0 / GLA_TAU)
    hi = la.astype(BF16)
    lo = (la - hi.astype(F32)).astype(BF16)
    sums = _dot(ones, jnp.concatenate([hi, lo], axis=1))
    sums = sums[:, :width] + sums[:, width:]
    b_in, b_abs = sums[:rows], sums[rows:]
    if reverse:
        g_mid, g_end = b_abs[rows // 2:rows // 2 + 1], b_abs[0:1]
    else:
        g_mid, g_end = b_abs[rows // 2 - 1:rows // 2], b_abs[rows - 1:rows]
    diag, cross = sel == 1, sel == 2
    q = q_ref[i].astype(F32) * GLA_DK ** -0.5
    k = k_ref[i].astype(F32)
    q_d, k_d = q * jnp.exp(b_in), k * jnp.exp(-b_in)
    q_x, k_x = q * jnp.exp(b_abs - g_mid), k * jnp.exp(g_mid - b_abs)
    q_s, k_s = q * jnp.exp(b_abs), k * jnp.exp(g_end - b_abs)
    decay = jnp.exp(g_end)
    return q_d, k_d, q_x, k_x, q_s, k_s, decay, diag, cross


def _gla_apply(factors, v_ref, state_ref, o_ref, i):
    q_d, k_d, q_x, k_x, q_s, k_s, decay, diag, cross = factors
    rows = q_d.shape[0]
    lane = lax.broadcasted_iota(jnp.int32, (rows, 2 * GLA_DK), 1)
    lane_s = lax.broadcasted_iota(jnp.int32, (GLA_DV, 2 * GLA_DK), 1)
    for pair in range(GLA_HEADS // 2):
        ps = slice(2 * GLA_DK * pair, 2 * GLA_DK * (pair + 1))
        kd_p, kx_p, ks_p = k_d[:, ps].astype(BF16), k_x[:, ps].astype(BF16), k_s[:, ps].astype(BF16)
        st = state_ref[i, pair]
        st_b = st.astype(BF16)
        ds = []
        for j in range(2):
            h = 2 * pair + j
            vs = slice(h * GLA_DV, (h + 1) * GLA_DV)
            mine = (lane // GLA_DK) == j
            qd_h = jnp.where(mine, q_d[:, ps], 0.0).astype(BF16)
            qx_h = jnp.where(mine, q_x[:, ps], 0.0).astype(BF16)
            qs_h = jnp.where(mine, q_s[:, ps], 0.0).astype(BF16)
            vh = v_ref[i, :, vs].astype(BF16)
            a = jnp.where(diag, _dot_nt(qd_h, kd_p), jnp.where(cross, _dot_nt(qx_h, kx_p), 0.0))
            o_ref[i, :, vs] = (_dot(a.astype(BF16), vh) + _dot_nt(qs_h, st_b)).astype(o_ref.dtype)
            ds.append(_dot_tn(vh, ks_p))
        state_ref[i, pair] = decay[:, ps] * st + jnp.where(lane_s < GLA_DK, ds[0], ds[1])


def _gla_kernel(qf_ref, kf_ref, vf_ref, mf_ref, qb_ref, kb_ref, vb_ref, mb_ref, wg_ref, bg_ref, ones_ref, sel_ref,
                sf0_ref, sb0_ref, of_ref, ob_ref, sf_ref, sb_ref, stf, stb):
    t = pl.program_id(1)

    @pl.when(t == 0)
    def _():
        stf[...] = sf0_ref[...]
        stb[...] = sb0_ref[...]

    width = GLA_HEADS * GLA_DK
    entries = range(qf_ref.shape[0])
    fwd = [_gla_factors(qf_ref, kf_ref, mf_ref, i, wg_ref[:, :width], bg_ref[:, :width], ones_ref[0], sel_ref[0],
                        False) for i in entries]
    bwd = [_gla_factors(qb_ref, kb_ref, mb_ref, i, wg_ref[:, width:], bg_ref[:, width:], ones_ref[1], sel_ref[1],
                        True) for i in entries]
    for i in entries:
        _gla_apply(fwd[i], vf_ref, stf, of_ref, i)
        _gla_apply(bwd[i], vb_ref, stb, ob_ref, i)

    @pl.when(t == pl.num_programs(1) - 1)
    def _():
        sf_ref[...] = stf[...]
        sb_ref[...] = stb[...]


def _gla_scan(proj, wg, bg, sf0, sb0, layer):
    b, r, _ = proj.shape
    tm = 256
    nt = r // tm
    kw, vw = GLA_HEADS * GLA_DK, GLA_HEADS * GLA_DV
    fwd = lambda t: t
    bwd = lambda t: nt - 1 - t

    nb = 2 if b % 2 == 0 else 1
    ones, sel = _gla_block_constants(tm)

    def specs(order):
        return [pl.BlockSpec((nb, tm, kw), lambda i, t: (i, order(t), P_GLA_Q // kw)),
                pl.BlockSpec((nb, tm, kw), lambda i, t: (i, order(t), P_GLA_K // kw)),
                pl.BlockSpec((nb, tm, vw), lambda i, t: (i, order(t), P_GLA_V // vw)),
                pl.BlockSpec((nb, tm, 128), lambda i, t: (i, order(t), P_MISC // 128))]

    state_spec = pl.BlockSpec((nb,) + GLA_STATE, lambda i, t: (i, 0, 0, 0))
    state_shape = jax.ShapeDtypeStruct((b,) + GLA_STATE, F32)
    return pl.pallas_call(
        _gla_kernel,
        grid=(b // nb, nt),
        in_specs=specs(fwd) + specs(bwd) + [
            pl.BlockSpec((None,) + wg.shape[1:], lambda i, t: (layer, 0, 0)),
            pl.BlockSpec((None,) + bg.shape[1:], lambda i, t: (layer, 0, 0)),
            pl.BlockSpec(ones.shape, lambda i, t: (0, 0, 0)),
            pl.BlockSpec(sel.shape, lambda i, t: (0, 0, 0)),
            state_spec, state_spec],
        out_specs=[pl.BlockSpec((nb, tm, vw), lambda i, t: (i, fwd(t), 0)),
                   pl.BlockSpec((nb, tm, vw), lambda i, t: (i, bwd(t), 0)),
                   state_spec, state_spec],
        out_shape=[jax.ShapeDtypeStruct((b, r, vw), BF16), jax.ShapeDtypeStruct((b, r, vw), BF16),
                   state_shape, state_shape],
        scratch_shapes=[pltpu.VMEM((nb,) + GLA_STATE, F32), pltpu.VMEM((nb,) + GLA_STATE, F32)],
        compiler_params=_params(2),
        name="gla_scan",
    )(proj, proj, proj, proj, proj, proj, proj, proj, wg, bg, ones, sel, sf0, sb0)


def _outproj_kernel(x_ref, mod_ref, mla_ref, swa_ref, of_ref, ob_ref, r_ref, g_ref, w_ref, o_ref):
    o = of_ref[0].astype(F32) + ob_ref[0].astype(F32)
    gate = _silu(r_ref[0].astype(F32))
    parts = []
    for h in range(GLA_HEADS):
        vs = slice(h * GLA_DV, (h + 1) * GLA_DV)
        parts.append((_rms(o[:, vs], g_ref[:, vs]) * gate[:, vs]).astype(BF16))
    gla = jnp.concatenate(parts, axis=1)
    a, b = mla_ref.shape[2], mla_ref.shape[2] + swa_ref.shape[2]
    mix = _dot(mla_ref[0], w_ref[0:a, :]) + _dot(swa_ref[0], w_ref[a:b, :]) + _dot(gla, w_ref[b:, :])
    o_ref[0] = x_ref[0] + mod_ref[0, 2:3, :] * mix


def _out_projection(x, mod, mla, swa, o_f, o_b, proj, g_out, w_out, layer):
    bx, r, d = x.shape
    tm = 512
    row = lambda width: pl.BlockSpec((1, tm, width), lambda b, t: (b, t, 0))
    gw = GLA_HEADS * GLA_DV
    return pl.pallas_call(
        _outproj_kernel,
        grid=(bx, r // tm),
        in_specs=[row(d),
                  pl.BlockSpec((1, N_MOD, d), lambda b, t: (b, 0, 0)),
                  row(mla.shape[2]), row(swa.shape[2]), row(gw), row(gw),
                  pl.BlockSpec((1, tm, gw), lambda b, t: (b, t, P_GLA_R // gw)),
                  pl.BlockSpec((1, gw), lambda b, t: (0, 0)),
                  pl.BlockSpec((None,) + w_out.shape[1:], lambda b, t: (layer, 0, 0))],
        out_specs=row(d),
        out_shape=jax.ShapeDtypeStruct(x.shape, F32),
        compiler_params=_params(2),
        name="out_projection",
    )(x, mod, mla, swa, o_f, o_b, proj, g_out, w_out)


def _ffn_kernel(x_ref, mod_ref, g_ref, wg_ref, wu_ref, wd_ref, gf_ref, o_ref, h_ref, *, final_norm):
    f = pl.program_id(2)
    row_chunks = [slice(r, r + ROW_CHUNK) for r in range(0, x_ref.shape[1], ROW_CHUNK)]
    chunk = (ROW_CHUNK, x_ref.shape[2])

    def gated_partial():
        h = h_ref[...]
        act = _silu(_dot(h, wg_ref[...])) * _dot(h, wu_ref[...])
        return mod_ref[0, 5:6, :] * _dot(act.astype(BF16), wd_ref[...])

    @pl.when(f == 0)
    def _():
        gain = jnp.broadcast_to(g_ref[...] * (1.0 + mod_ref[0, 4:5, :]), chunk)
        shift = jnp.broadcast_to(mod_ref[0, 3:4, :], chunk)
        for rows in row_chunks:
            x = x_ref[0, rows, :]
            rs = lax.rsqrt(jnp.mean(x * x, axis=-1, keepdims=True) + EPS)
            h_ref[rows, :] = (x * rs * gain + shift).astype(BF16)
        o_ref[0] = x_ref[0] + gated_partial()

    @pl.when(f > 0)
    def _():
        o_ref[0] += gated_partial()

    if final_norm:
        @pl.when(f == pl.num_programs(2) - 1)
        def _():
            g_fin = jnp.broadcast_to(gf_ref[...], chunk)
            for rows in row_chunks:
                y = o_ref[0, rows, :]
                o_ref[0, rows, :] = y * lax.rsqrt(jnp.mean(y * y, axis=-1, keepdims=True) + EPS) * g_fin


def _ffn(x, mod, g, w_gu, w_down, g_final, final_norm, layer):
    bx, r, d = x.shape
    hidden = w_down.shape[1]
    tm, tf = min(r, 1024), 512
    nf = hidden // tf
    return pl.pallas_call(
        functools.partial(_ffn_kernel, final_norm=final_norm),
        grid=(bx, r // tm, nf),
        in_specs=[pl.BlockSpec((1, tm, d), lambda b, t, f: (b, t, 0)),
                  pl.BlockSpec((1, N_MOD, d), lambda b, t, f: (b, 0, 0)),
                  pl.BlockSpec((1, d), lambda b, t, f: (0, 0)),
                  pl.BlockSpec((None, d, tf), lambda b, t, f: (layer, 0, f)),
                  pl.BlockSpec((None, d, tf), lambda b, t, f: (layer, 0, nf + f)),
                  pl.BlockSpec((None, tf, d), lambda b, t, f: (layer, f, 0)),
                  pl.BlockSpec((1, d), lambda b, t, f: (0, 0))],
        out_specs=pl.BlockSpec((1, tm, d), lambda b, t, f: (b, t, 0)),
        out_shape=jax.ShapeDtypeStruct(x.shape, F32),
        scratch_shapes=[pltpu.VMEM((tm, d), BF16)],
        compiler_params=_params(3),
        name="ffn",
    )(x, mod, g, w_gu, w_gu, w_down, g_final)


def _rope_tables(n, dim):
    half = dim // 4
    freqs = ROPE_THETA ** (-jnp.arange(half, dtype=F32) / half)
    pos = jnp.arange(n, dtype=jnp.int32)
    ang_r = (pos // GRID_W).astype(F32)[:, None] * freqs
    ang_c = (pos % GRID_W).astype(F32)[:, None] * freqs
    cos = jnp.concatenate([jnp.cos(ang_r)] * 2 + [jnp.cos(ang_c)] * 2, axis=1)
    sin = jnp.concatenate([-jnp.sin(ang_r), jnp.sin(ang_r), -jnp.sin(ang_c), jnp.sin(ang_c)], axis=1)
    reps = 128 // dim
    return jnp.tile(cos, (1, reps)), jnp.tile(sin, (1, reps))


def _relayout_w_in(w_in):
    depth, d, _ = w_in.shape
    sizes = (MLA_RANK, MLA_RANK, MLA_ROPE, SWA_HEADS * SWA_DIM, SWA_KV_HEADS * SWA_DIM, SWA_KV_HEADS * SWA_DIM,
             GLA_HEADS * GLA_DK, GLA_HEADS * GLA_DK, GLA_HEADS * GLA_DV, 2 * GLA_GATE_RANK, GLA_HEADS * GLA_DV)
    offs = [0]
    for s in sizes:
        offs.append(offs[-1] + s)
    span = lambda i, j: w_in[:, :, offs[i]:offs[j]].astype(BF16)
    pad = jnp.zeros((depth, d, P_GLA_Q - P_MISC - MLA_ROPE - 2 * GLA_GATE_RANK), BF16)
    out = jnp.concatenate([span(3, 5), span(0, 2), span(5, 6), span(2, 3), span(9, 10), pad,
                           span(6, 9), span(10, 11)], axis=-1)
    assert out.shape[-1] == P_WIDTH
    return out


def _relayout_mla(w_uq, w_ukv):
    depth, rk, _ = w_uq.shape
    uq = w_uq.reshape(depth, rk, MLA_HEADS, MLA_QK)
    uq = jnp.concatenate([uq[..., :MLA_NOPE].reshape(depth, rk, -1), uq[..., MLA_NOPE:].reshape(depth, rk, -1)], -1)
    ukv = w_ukv.reshape(depth, rk, MLA_HEADS, MLA_NOPE + MLA_V)
    ukv = jnp.concatenate([ukv[..., :MLA_NOPE].reshape(depth, rk, -1), ukv[..., MLA_NOPE:].reshape(depth, rk, -1)], -1)
    return uq.astype(BF16), ukv.astype(BF16)


def _relayout_gate(w_f, b_f, w_b, b_b):
    depth, rank, width = w_f.shape
    wg = jnp.zeros((depth, 128, 2 * width), F32)
    wg = wg.at[:, MISC_GATE_OFF:MISC_GATE_OFF + rank, :width].set(w_f)
    wg = wg.at[:, MISC_GATE_OFF + rank:MISC_GATE_OFF + 2 * rank, width:].set(w_b)
    bg = jnp.concatenate([b_f, b_b], axis=-1).reshape(depth, 1, 2 * width)
    return wg.astype(BF16), bg


def kernel(x, c, ctx, c_ctx, w_mod, b_mod, g_mix, g_ffn, w_in, g_mla_q, g_mla_kv, w_mla_uq, w_mla_ukv,
           swa_sink, w_gla_gate_f, b_gla_gate_f, w_gla_gate_b, b_gla_gate_b, g_gla_out, w_out, w_ffn_gu,
           w_ffn_down, g_final):
    B, N, D = x.shape
    C = ctx.shape[1]
    depth = w_mod.shape[0]

    cvec = jnp.concatenate([c, c_ctx[None, :], jnp.zeros((16 - B - 1, D), F32)], axis=0)
    mod = _modulation(cvec, w_mod, b_mod).reshape(depth, 16, N_MOD, D)

    w_in_p = _relayout_w_in(w_in)
    w_uq_p, w_ukv_p = _relayout_mla(w_mla_uq, w_mla_ukv)
    wg_p, bg_p = _relayout_gate(w_gla_gate_f, b_gla_gate_f, w_gla_gate_b, b_gla_gate_b)
    w_out_b = w_out.astype(BF16)
    w_gu_b = w_ffn_gu.astype(BF16)
    w_down_b = w_ffn_down.astype(BF16)

    cos_m, sin_m = _rope_tables(N, MLA_ROPE)
    cos_s, sin_s = _rope_tables(N, SWA_DIM)
    cos_id, sin_id = jnp.ones((C, 128), F32), jnp.zeros((C, 128), F32)
    state0 = jnp.zeros((B,) + GLA_STATE, F32)
    g_fin = g_final.reshape(1, D)

    xc = ctx.reshape(1, B * C, D)
    for l in range(depth):
        last = l == depth - 1
        mod_l, mod_c = mod[l, :B], mod[l, B:B + 1]
        g_mix_l, g_ffn_l = g_mix[l].reshape(1, D), g_ffn[l].reshape(1, D)
        g_q, g_kv = g_mla_q[l].reshape(1, -1), g_mla_kv[l].reshape(1, -1)
        g_out = g_gla_out[l].reshape(1, -1)
        sink = swa_sink[l].reshape(1, -1)

        proj_l = _in_projection(x, mod_l, g_mix_l, w_in_p, l)
        proj_c = _in_projection(xc, mod_c, g_mix_l, w_in_p, l).reshape(B, C, P_WIDTH)

        q_l, k_l, v_l = _mla_prep(proj_l, cos_m, sin_m, g_q, g_kv, w_uq_p, w_ukv_p, l)
        q_c, k_c, v_c = _mla_prep(proj_c, cos_id, sin_id, g_q, g_kv, w_uq_p, w_ukv_p, l)
        mla_l = _mla_attention(q_l, k_l, v_l, k_c, v_c)
        swa_l = _swa_attention(proj_l, proj_c, cos_s, sin_s, sink)

        of_c, ob_c, s_f, s_b = _gla_scan(proj_c, wg_p, bg_p, state0, state0, l)
        of_l, ob_l, _, _ = _gla_scan(proj_l, wg_p, bg_p, s_f, s_b, l)

        x = _out_projection(x, mod_l, mla_l, swa_l, of_l, ob_l, proj_l, g_out, w_out_b, l)
        x = _ffn(x, mod_l, g_ffn_l, w_gu_b, w_down_b, g_fin, last, l)

        if not last:
            mla_c = _mla_ctx_attention(q_c, k_c, v_c)
            swa_c = _swa_ctx_attention(proj_c, sink)
            flat = lambda a: a.reshape(1, B * C, a.shape[-1])
            xc = _out_projection(xc, mod_c, flat(mla_c), flat(swa_c), flat(of_c), flat(ob_c), flat(proj_c),
                                 g_out, w_out_b, l)
            xc = _ffn(xc, mod_c, g_ffn_l, w_gu_b, w_down_b, g_fin, False, l)
    return x
```

```python
import functools

import jax
import jax.numpy as jnp
import numpy as np
from jax import lax
from jax.experimental import pallas as pl
from jax.experimental.pallas import tpu as pltpu

F32 = jnp.float32
BF16 = jnp.bfloat16

GRID_W = 64
EPS = 1e-6
ROPE_THETA = 10000.0
LOG2E = 1.4426950408889634

MLA_HEADS = 6
MLA_RANK = 512
MLA_NOPE = 128
MLA_ROPE = 64
MLA_V = 128
MLA_QK = MLA_NOPE + MLA_ROPE
MLA_VA = MLA_V + 16

SWA_HEADS = 6
SWA_KV_HEADS = 2
SWA_GROUP = SWA_HEADS // SWA_KV_HEADS
SWA_DIM = 128
SWA_BLOCK = 128

GLA_HEADS = 4
GLA_DK = 64
GLA_DV = 128
GLA_GATE_RANK = 16
GLA_TAU = 16.0
GLA_CHUNK = 64
GLA_STATE = (GLA_HEADS // 2, GLA_DV, 2 * GLA_DK)

N_MOD = 6

P_SWA_Q = 0
P_SWA_K = 768
P_CQ = 1024
P_CKV = 1536
P_SWA_V = 2048
P_MISC = 2304
P_GLA_Q = 2560
P_GLA_K = 2816
P_GLA_V = 3072
P_GLA_R = 3584
P_WIDTH = 4096
MISC_GATE_OFF = MLA_ROPE

VMEM_LIMIT = 56 * 1024 * 1024
ROW_CHUNK = 16


def _params(n_axes, flags=None):
    return pltpu.CompilerParams(dimension_semantics=("arbitrary",) * n_axes,
                                vmem_limit_bytes=VMEM_LIMIT, flags=flags)


def _silu(x):
    return x / (1.0 + jnp.exp(-x))


def _rms(x, g):
    ms = jnp.mean(x * x, axis=-1, keepdims=True)
    return x * lax.rsqrt(ms + EPS) * g


def _dot(a, b):
    return jnp.dot(a, b, preferred_element_type=F32)


def _dot_nt(a, b):
    return lax.dot_general(a, b, (((1,), (1,)), ((), ())), preferred_element_type=F32)


def _dot_tn(a, b):
    return lax.dot_general(a, b, (((0,), (0,)), ((), ())), preferred_element_type=F32)


def _mod_kernel(c_ref, w_ref, b_ref, o_ref):
    a = _silu(c_ref[...]).astype(BF16)
    o_ref[0] = _dot(a, w_ref[0].astype(BF16)) + b_ref[0]


def _modulation(cvec, w_mod, b_mod):
    depth, d, width = w_mod.shape
    rows = cvec.shape[0]
    tn = 1024
    return pl.pallas_call(
        _mod_kernel,
        grid=(depth, width // tn),
        in_specs=[pl.BlockSpec((rows, d), lambda l, j: (0, 0)),
                  pl.BlockSpec((1, d, tn), lambda l, j: (l, 0, j)),
                  pl.BlockSpec((1, 1, tn), lambda l, j: (l, 0, j))],
        out_specs=pl.BlockSpec((1, rows, tn), lambda l, j: (l, 0, j)),
        out_shape=jax.ShapeDtypeStruct((depth, rows, width), F32),
        compiler_params=_params(2),
        name="modulation",
    )(cvec, w_mod, b_mod.reshape(depth, 1, width))


def _inproj_kernel(x_ref, mod_ref, g_ref, w_ref, o_ref):
    h = _rms(x_ref[0], g_ref[...]) * (1.0 + mod_ref[0, 1:2, :]) + mod_ref[0, 0:1, :]
    o_ref[0] = _dot(h.astype(BF16), w_ref[...]).astype(o_ref.dtype)


def _in_projection(x, mod, g, w, layer):
    bx, r, d = x.shape
    tm, tn = 512, P_WIDTH
    return pl.pallas_call(
        _inproj_kernel,
        grid=(P_WIDTH // tn, bx, r // tm),
        in_specs=[pl.BlockSpec((1, tm, d), lambda j, b, t: (b, t, 0)),
                  pl.BlockSpec((1, N_MOD, d), lambda j, b, t: (b, 0, 0)),
                  pl.BlockSpec((1, d), lambda j, b, t: (0, 0)),
                  pl.BlockSpec((None, d, tn), lambda j, b, t: (layer, 0, j), pipeline_mode=pl.Buffered(1))],
        out_specs=pl.BlockSpec((1, tm, tn), lambda j, b, t: (b, t, j)),
        out_shape=jax.ShapeDtypeStruct((bx, r, P_WIDTH), BF16),
        compiler_params=_params(3),
        name="in_projection",
    )(x, mod, g, w)


def _rope(x, cos, sin, half):
    lane = lax.broadcasted_iota(jnp.int32, x.shape, 1)
    first = (lane % (2 * half)) < half
    width = x.shape[1]
    rot = jnp.where(first, pltpu.roll(x, width - half, 1), pltpu.roll(x, half, 1))
    return x * cos + rot * sin


def _mla_prep_kernel(cq_ref, ckv_ref, misc_ref, cos_ref, sin_ref, gq_ref, gkv_ref, wuq_ref, wukv_ref,
                     qt_ref, k_ref, vt_ref):
    scale = MLA_QK ** -0.5 * LOG2E
    cos, sin = cos_ref[...], sin_ref[...]
    half = MLA_ROPE // 4
    qf = _dot(_rms(cq_ref[0].astype(F32), gq_ref[...]).astype(BF16), wuq_ref[...])
    kvf = _dot(_rms(ckv_ref[0].astype(F32), gkv_ref[...]).astype(BF16), wukv_ref[...])
    kr = _rope(misc_ref[0].astype(F32), cos, sin, half)[:, :MLA_ROPE].astype(BF16)
    nope_w = MLA_HEADS * MLA_NOPE
    for pair in range(MLA_HEADS // 2):
        qr = _rope(qf[:, nope_w + 128 * pair: nope_w + 128 * (pair + 1)], cos, sin, half) * scale
        qrt = qr.astype(BF16).T
        for j in range(2):
            qt_ref[0, 2 * pair + j, MLA_NOPE:MLA_QK, :] = qrt[MLA_ROPE * j: MLA_ROPE * (j + 1), :]
    for h in range(MLA_HEADS):
        qt_ref[0, h, 0:MLA_NOPE, :] = (qf[:, MLA_NOPE * h: MLA_NOPE * (h + 1)] * scale).astype(BF16).T
        k_ref[0, h, :, 0:MLA_NOPE] = kvf[:, MLA_NOPE * h: MLA_NOPE * (h + 1)].astype(BF16)
        k_ref[0, h, :, MLA_NOPE:MLA_QK] = kr
        vt_ref[0, h, 0:MLA_V, :] = kvf[:, nope_w + MLA_V * h: nope_w + MLA_V * (h + 1)].astype(BF16).T
        vt_ref[0, h, MLA_V:, :] = jnp.ones((MLA_VA - MLA_V, vt_ref.shape[3]), BF16)


def _mla_prep(proj, cos, sin, g_q, g_kv, w_uq, w_ukv, layer):
    b, r, _ = proj.shape
    tm = min(r, 1024)
    rk = MLA_RANK
    const = lambda shape: pl.BlockSpec(shape, lambda i, t: (0,) * len(shape))
    stacked = lambda w: pl.BlockSpec((None,) + w.shape[1:], lambda i, t: (layer, 0, 0))
    return pl.pallas_call(
        _mla_prep_kernel,
        grid=(b, r // tm),
        in_specs=[pl.BlockSpec((1, tm, rk), lambda i, t: (i, t, P_CQ // rk)),
                  pl.BlockSpec((1, tm, rk), lambda i, t: (i, t, P_CKV // rk)),
                  pl.BlockSpec((1, tm, 128), lambda i, t: (i, t, P_MISC // 128)),
                  pl.BlockSpec((tm, 128), lambda i, t: (t, 0)),
                  pl.BlockSpec((tm, 128), lambda i, t: (t, 0)),
                  const((1, rk)), const((1, rk)),
                  stacked(w_uq), stacked(w_ukv)],
        out_specs=[pl.BlockSpec((1, MLA_HEADS, MLA_QK, tm), lambda i, t: (i, 0, 0, t)),
                   pl.BlockSpec((1, MLA_HEADS, tm, MLA_QK), lambda i, t: (i, 0, t, 0)),
                   pl.BlockSpec((1, MLA_HEADS, MLA_VA, tm), lambda i, t: (i, 0, 0, t))],
        out_shape=[jax.ShapeDtypeStruct((b, MLA_HEADS, MLA_QK, r), BF16),
                   jax.ShapeDtypeStruct((b, MLA_HEADS, r, MLA_QK), BF16),
                   jax.ShapeDtypeStruct((b, MLA_HEADS, MLA_VA, r), BF16)],
        compiler_params=_params(2),
        name="mla_prep",
    )(proj, proj, proj, cos, sin, g_q, g_kv, w_uq, w_ukv)


def _softmax_pv(s_parts, vt_parts):
    m = functools.reduce(jnp.maximum, [jnp.max(s, axis=0, keepdims=True) for s in s_parts])
    acc = sum(_dot(vt, jnp.exp2(s - m).astype(BF16)) for s, vt in zip(s_parts, vt_parts))
    return (acc[:MLA_V] / acc[MLA_V:MLA_V + 1]).T


def _mla_attn_kernel(qt_ref, kl_ref, kc_ref, vlt_ref, vct_ref, o_ref, s_scr, m_scr, *, tk):
    @pl.when(pl.program_id(0) == 0)
    def _():
        s_scr[...] = jnp.zeros_like(s_scr)
        m_scr[...] = jnp.zeros_like(m_scr)

    n_lat = kl_ref.shape[2] // tk
    qt = qt_ref[0, 0]
    tq = qt.shape[1]
    m_prev = m_scr[...]
    m_new = jnp.full((1, tq), -jnp.inf, F32)
    acc = jnp.zeros((MLA_VA, tq), F32)
    for j in range(n_lat + 1):
        rows = slice(j * tk, (j + 1) * tk)
        p = jnp.exp2(s_scr[rows, :] - m_prev).astype(BF16)
        k = kc_ref[0, 0] if j == n_lat else kl_ref[0, 0, rows, :]
        vt = vct_ref[0, 0] if j == n_lat else vlt_ref[0, 0, :, rows]
        s = _dot(k, qt)
        s_scr[rows, :] = s
        m_new = jnp.maximum(m_new, jnp.max(s, axis=0, keepdims=True))
        acc = acc + _dot(vt, p)
    m_scr[...] = m_new
    o_ref[0] = (acc[:MLA_V] / acc[MLA_V:MLA_V + 1]).T.astype(o_ref.dtype)


def _mla_attention(qt, k_l, vt_l, k_c, vt_c):
    b, h, dq, n = qt.shape
    c = k_c.shape[2]
    tq = min(n, 1024)
    nt = n // tq
    tiles = b * h * nt

    def cur(u):
        t = jnp.minimum(u, tiles - 1)
        return t // (h * nt), (t // nt) % h, t % nt

    def prev(u):
        t = jnp.maximum(u - 1, 0)
        return t // (h * nt), (t // nt) % h, t % nt

    assert n % c == 0
    return pl.pallas_call(
        functools.partial(_mla_attn_kernel, tk=c),
        grid=(tiles + 1,),
        in_specs=[pl.BlockSpec((1, 1, dq, tq), lambda u: (cur(u)[0], cur(u)[1], 0, cur(u)[2])),
                  pl.BlockSpec((1, 1, n, dq), lambda u: (cur(u)[0], cur(u)[1], 0, 0)),
                  pl.BlockSpec((1, 1, c, dq), lambda u: (cur(u)[0], cur(u)[1], 0, 0)),
                  pl.BlockSpec((1, 1, MLA_VA, n), lambda u: (prev(u)[0], prev(u)[1], 0, 0)),
                  pl.BlockSpec((1, 1, MLA_VA, c), lambda u: (prev(u)[0], prev(u)[1], 0, 0))],
        out_specs=pl.BlockSpec((1, tq, MLA_V), lambda u: (prev(u)[0], prev(u)[2], prev(u)[1])),
        out_shape=jax.ShapeDtypeStruct((b, n, h * MLA_V), BF16),
        scratch_shapes=[pltpu.VMEM((n + c, tq), F32), pltpu.VMEM((1, tq), F32)],
        compiler_params=_params(1),
        name="mla_attention",
    )(qt, k_l, k_c, vt_l, vt_c)


def _mla_ctx_kernel(qt_ref, k_ref, vt_ref, o_ref):
    o_ref[0] = _softmax_pv([_dot(k_ref[0, 0], qt_ref[0, 0])], [vt_ref[0, 0]]).astype(o_ref.dtype)


def _mla_ctx_attention(qt, k, vt):
    b, h, dq, c = qt.shape
    return pl.pallas_call(
        _mla_ctx_kernel,
        grid=(b, h),
        in_specs=[pl.BlockSpec((1, 1, dq, c), lambda i, j: (i, j, 0, 0)),
                  pl.BlockSpec((1, 1, c, dq), lambda i, j: (i, j, 0, 0)),
                  pl.BlockSpec((1, 1, MLA_VA, c), lambda i, j: (i, j, 0, 0))],
        out_specs=pl.BlockSpec((1, c, MLA_V), lambda i, j: (i, 0, j)),
        out_shape=jax.ShapeDtypeStruct((b, c, h * MLA_V), BF16),
        compiler_params=_params(2),
        name="mla_ctx_attention",
    )(qt, k, vt)


def _swa_ctx_kernel(sink_ref, q_ref, k_ref, v_ref, o_ref):
    q = (q_ref[0].astype(F32) * SWA_DIM ** -0.5).astype(BF16)
    s = _dot_nt(q, k_ref[0].astype(BF16))
    sink = sink_ref[0, pl.program_id(1)]
    m = jnp.maximum(jnp.max(s, axis=-1, keepdims=True), sink)
    p = jnp.exp(s - m)
    l = jnp.sum(p, axis=-1, keepdims=True) + jnp.exp(sink - m)
    o_ref[0] = (_dot(p.astype(BF16), v_ref[0].astype(BF16)) / l).astype(o_ref.dtype)


def _swa_ctx_attention(proj_c, sink):
    b, c, _ = proj_c.shape
    d = SWA_DIM
    return pl.pallas_call(
        _swa_ctx_kernel,
        grid=(b, SWA_HEADS),
        in_specs=[pl.BlockSpec(memory_space=pltpu.SMEM),
                  pl.BlockSpec((1, c, d), lambda i, j: (i, 0, P_SWA_Q // d + j)),
                  pl.BlockSpec((1, c, d), lambda i, j: (i, 0, P_SWA_K // d + j // SWA_GROUP)),
                  pl.BlockSpec((1, c, d), lambda i, j: (i, 0, P_SWA_V // d + j // SWA_GROUP))],
        out_specs=pl.BlockSpec((1, c, d), lambda i, j: (i, 0, j)),
        out_shape=jax.ShapeDtypeStruct((b, c, SWA_HEADS * d), BF16),
        compiler_params=_params(2),
        name="swa_ctx_attention",
    )(sink, proj_c, proj_c, proj_c)


def _swa_window_start(t, qb, nb):
    return pl.multiple_of(jnp.clip(t * qb - 1, 0, nb - (qb + 2)) * SWA_BLOCK, SWA_BLOCK)


def _swa_kernel(sink_ref, q_ref, k_ref, kc_ref, v_ref, vc_ref, cos_ref, sin_ref, o_ref, s_scr, m_scr, *,
                qb, tiles, nt):
    u = pl.program_id(0)

    @pl.when(u == 0)
    def _():
        s_scr[...] = jnp.zeros_like(s_scr)
        m_scr[...] = jnp.zeros_like(m_scr)

    blk, d = SWA_BLOCK, SWA_DIM
    half = d // 4
    nq = qb * blk
    nw = (qb + 2) * blk
    nb = k_ref.shape[1] // blk
    ck = kc_ref.shape[1]
    cur, prev = jnp.minimum(u, tiles - 1), jnp.maximum(u - 1, 0)
    t_cur, t_prev = cur % nt, prev % nt
    q_start = pl.multiple_of(t_cur * nq, blk)
    k_start = _swa_window_start(t_cur, qb, nb)
    v_start = _swa_window_start(t_prev, qb, nb)
    groups = range(SWA_KV_HEADS)

    def sink_row(g):
        return jnp.concatenate(
            [jnp.full((1, nq), sink_ref[0, g * SWA_GROUP + i] * LOG2E, F32) for i in range(SWA_GROUP)], axis=1)

    def values_t(v):
        return jnp.concatenate([v.astype(F32).T.astype(BF16), jnp.ones((16, v.shape[0]), BF16)], axis=0)

    scale = d ** -0.5 * LOG2E
    cos_q, sin_q = cos_ref[pl.ds(q_start, nq), :], sin_ref[pl.ds(q_start, nq), :]
    q = [jnp.concatenate(
        [(_rope(q_ref[0, :, d * h: d * (h + 1)].astype(F32), cos_q, sin_q, half) * scale).astype(BF16)
         for h in range(g * SWA_GROUP, (g + 1) * SWA_GROUP)], axis=0) for g in groups]
    key = lax.broadcasted_iota(jnp.int32, (ck, nq), 0)
    tok = lax.broadcasted_iota(jnp.int32, (ck, nq), 1)
    m_prev = [m_scr[g] for g in groups]
    m_new = [sink_row(g) for g in groups]
    acc = [jnp.zeros((d + 16, SWA_GROUP * nq), F32) for g in groups]
    n_win = nw // ck
    for j in range(n_win + 1):
        rows = slice(j * ck, (j + 1) * ck)
        if j < n_win:
            kr = pl.ds(k_start + j * ck, ck)
            cos_k, sin_k = cos_ref[kr, :], sin_ref[kr, :]
            dist = key - tok + (k_start + j * ck - q_start)
            valid = (dist <= blk) & (dist >= -blk)
        for g in groups:
            cols = slice(d * g, d * (g + 1))
            p = jnp.exp2(s_scr[g, rows, :] - m_prev[g]).astype(BF16)
            if j < n_win:
                vt = values_t(v_ref[0, pl.ds(v_start + j * ck, ck), cols])
                k = _rope(k_ref[0, kr, cols].astype(F32), cos_k, sin_k, half).astype(BF16)
                s = _dot_nt(k, q[g])
                s = jnp.concatenate(
                    [jnp.where(valid, s[:, nq * i: nq * (i + 1)], -jnp.inf) for i in range(SWA_GROUP)], axis=1)
            else:
                vt = values_t(vc_ref[0, :, cols])
                s = _dot_nt(kc_ref[0, :, cols].astype(BF16), q[g])
            s_scr[g, rows, :] = s
            m_new[g] = jnp.maximum(m_new[g], jnp.max(s, axis=0, keepdims=True))
            acc[g] = acc[g] + _dot(vt, p)
    for g in groups:
        m_scr[g] = m_new[g]
        out_t = acc[g][:d] / (acc[g][d:d + 1] + jnp.exp2(sink_row(g) - m_prev[g]))
        for i in range(SWA_GROUP):
            h = g * SWA_GROUP + i
            o_ref[0, :, d * h: d * (h + 1)] = out_t[:, nq * i: nq * (i + 1)].T.astype(o_ref.dtype)


def _swa_attention(proj_l, proj_c, cos, sin, sink):
    b, n, _ = proj_l.shape
    c = proj_c.shape[1]
    blk, d = SWA_BLOCK, SWA_DIM
    qb = 2
    nq = qb * blk
    nt = n // nq
    tiles = b * nt
    assert ((qb + 2) * blk) % c == 0 and n // blk >= qb + 2
    cur = lambda u: jnp.minimum(u, tiles - 1)
    prev = lambda u: jnp.maximum(u - 1, 0)
    qw, kw = SWA_HEADS * d, SWA_KV_HEADS * d
    return pl.pallas_call(
        functools.partial(_swa_kernel, qb=qb, tiles=tiles, nt=nt),
        grid=(tiles + 1,),
        in_specs=[pl.BlockSpec(memory_space=pltpu.SMEM),
                  pl.BlockSpec((1, nq, qw), lambda u: (cur(u) // nt, cur(u) % nt, P_SWA_Q // qw)),
                  pl.BlockSpec((1, n, kw), lambda u: (cur(u) // nt, 0, P_SWA_K // kw)),
                  pl.BlockSpec((1, c, kw), lambda u: (cur(u) // nt, 0, P_SWA_K // kw)),
                  pl.BlockSpec((1, n, kw), lambda u: (prev(u) // nt, 0, P_SWA_V // kw)),
                  pl.BlockSpec((1, c, kw), lambda u: (prev(u) // nt, 0, P_SWA_V // kw)),
                  pl.BlockSpec((n, d), lambda u: (0, 0)),
                  pl.BlockSpec((n, d), lambda u: (0, 0))],
        out_specs=pl.BlockSpec((1, nq, qw), lambda u: (prev(u) // nt, prev(u) % nt, 0)),
        out_shape=jax.ShapeDtypeStruct((b, n, qw), BF16),
        scratch_shapes=[pltpu.VMEM((SWA_KV_HEADS, (qb + 2) * blk + c, SWA_GROUP * nq), F32),
                        pltpu.VMEM((SWA_KV_HEADS, 1, SWA_GROUP * nq), F32)],
        compiler_params=_params(1),
        name="swa_attention",
    )(sink, proj_l, proj_l, proj_c, proj_l, proj_c, cos, sin)


def _log_sigmoid(x):
    return jnp.minimum(x, 0.0) - jnp.log(1.0 + jnp.exp(-jnp.abs(x)))


def _gla_block_constants(rows):
    t = np.arange(rows)
    ti, tj = t[:, None], t[None, :]
    ci, cj = ti // GLA_CHUNK, tj // GLA_CHUNK
    ones, sel = [], []
    for reverse in (False, True):
        upto = (tj >= ti) if reverse else (tj <= ti)
        visible = (tj > ti) if reverse else (tj <= ti)
        earlier = (cj > ci) if reverse else (cj < ci)
        ones.append((ci == cj) & upto)
        sel.append(np.where((ci == cj) & visible, 1, np.where(earlier, 2, 0)))
    return jnp.asarray(np.stack(ones), BF16), jnp.asarray(np.stack(sel), jnp.int32)


def _gla_factors(q_ref, k_ref, misc_ref, i, wg, bg, ones, sel, reverse):
    rows = q_ref.shape[1]
    L = GLA_CHUNK
    assert rows == 4 * L
    width = GLA_HEADS * GLA_DK
    la = _log_sigmoid(_dot(misc_ref[i].astype(BF16), wg) + bg) * (1.0 / GLA_TAU)
    hi = la.astype(BF16)
    lo = (la - hi.astype(F32)).astype(BF16)
    sums = _dot(ones, jnp.concatenate([hi, lo], axis=1))
    b_in = sums[:, :width] + sums[:, width:]
    chunks = [b_in[c * L:(c + 1) * L] for c in range(rows // L)]
    order = list(reversed(range(len(chunks)))) if reverse else list(range(len(chunks)))
    offset = None
    for c in order:
        total = chunks[c][0:1] if reverse else chunks[c][L - 1:L]
        if offset is not None:
            chunks[c] = chunks[c] + offset
        offset = total if offset is None else offset + total
    b_abs = jnp.concatenate(chunks, axis=0)
    if reverse:
        g_mid, g_end = b_abs[rows // 2:rows // 2 + 1], b_abs[0:1]
    else:
        g_mid, g_end = b_abs[rows // 2 - 1:rows // 2], b_abs[rows - 1:rows]
    diag, cross = sel == 1, sel == 2
    q = q_ref[i].astype(F32) * GLA_DK ** -0.5
    k = k_ref[i].astype(F32)
    q_d, k_d = q * jnp.exp(b_in), k * jnp.exp(-b_in)
    q_x, k_x = q * jnp.exp(b_abs - g_mid), k * jnp.exp(g_mid - b_abs)
    q_s, k_s = q * jnp.exp(b_abs), k * jnp.exp(g_end - b_abs)
    decay = jnp.exp(g_end)
    return q_d, k_d, q_x, k_x, q_s, k_s, decay, diag, cross


def _gla_apply(factors, v_ref, state_ref, o_ref, i):
    q_d, k_d, q_x, k_x, q_s, k_s, decay, diag, cross = factors
    rows = q_d.shape[0]
    lane = lax.broadcasted_iota(jnp.int32, (rows, 2 * GLA_DK), 1)
    lane_s = lax.broadcasted_iota(jnp.int32, (GLA_DV, 2 * GLA_DK), 1)
    for pair in range(GLA_HEADS // 2):
        ps = slice(2 * GLA_DK * pair, 2 * GLA_DK * (pair + 1))
        kd_p, kx_p, ks_p = k_d[:, ps].astype(BF16), k_x[:, ps].astype(BF16), k_s[:, ps].astype(BF16)
        st = state_ref[i, pair]
        st_b = st.astype(BF16)
        ds = []
        for j in range(2):
            h = 2 * pair + j
            vs = slice(h * GLA_DV, (h + 1) * GLA_DV)
            mine = (lane // GLA_DK) == j
            qd_h = jnp.where(mine, q_d[:, ps], 0.0).astype(BF16)
            qx_h = jnp.where(mine, q_x[:, ps], 0.0).astype(BF16)
            qs_h = jnp.where(mine, q_s[:, ps], 0.0).astype(BF16)
            vh = v_ref[i, :, vs].astype(BF16)
            a = jnp.where(diag, _dot_nt(qd_h, kd_p), jnp.where(cross, _dot_nt(qx_h, kx_p), 0.0))
            o_ref[i, :, vs] = (_dot(a.astype(BF16), vh) + _dot_nt(qs_h, st_b)).astype(o_ref.dtype)
            ds.append(_dot_tn(vh, ks_p))
        state_ref[i, pair] = decay[:, ps] * st + jnp.where(lane_s < GLA_DK, ds[0], ds[1])


def _gla_kernel(qf_ref, kf_ref, vf_ref, mf_ref, qb_ref, kb_ref, vb_ref, mb_ref, wg_ref, bg_ref, ones_ref, sel_ref,
                sf0_ref, sb0_ref, of_ref, ob_ref, sf_ref, sb_ref, stf, stb):
    t = pl.program_id(1)

    @pl.when(t == 0)
    def _():
        stf[...] = sf0_ref[...]
        stb[...] = sb0_ref[...]

    width = GLA_HEADS * GLA_DK
    entries = range(qf_ref.shape[0])
    fwd = [_gla_factors(qf_ref, kf_ref, mf_ref, i, wg_ref[:, :width], bg_ref[:, :width], ones_ref[0], sel_ref[0],
                        False) for i in entries]
    bwd = [_gla_factors(qb_ref, kb_ref, mb_ref, i, wg_ref[:, width:], bg_ref[:, width:], ones_ref[1], sel_ref[1],
                        True) for i in entries]
    for i in entries:
        _gla_apply(fwd[i], vf_ref, stf, of_ref, i)
        _gla_apply(bwd[i], vb_ref, stb, ob_ref, i)

    @pl.when(t == pl.num_programs(1) - 1)
    def _():
        sf_ref[...] = stf[...]
        sb_ref[...] = stb[...]


def _gla_scan(proj, wg, bg, sf0, sb0, layer):
    b, r, _ = proj.shape
    tm = 256
    nt = r // tm
    kw, vw = GLA_HEADS * GLA_DK, GLA_HEADS * GLA_DV
    fwd = lambda t: t
    bwd = lambda t: nt - 1 - t

    nb = next(k for k in (4, 2, 1) if b % k == 0)
    ones, sel = _gla_block_constants(tm)

    def specs(order):
        return [pl.BlockSpec((nb, tm, kw), lambda i, t: (i, order(t), P_GLA_Q // kw)),
                pl.BlockSpec((nb, tm, kw), lambda i, t: (i, order(t), P_GLA_K // kw)),
                pl.BlockSpec((nb, tm, vw), lambda i, t: (i, order(t), P_GLA_V // vw)),
                pl.BlockSpec((nb, tm, 128), lambda i, t: (i, order(t), P_MISC // 128))]

    state_spec = pl.BlockSpec((nb,) + GLA_STATE, lambda i, t: (i, 0, 0, 0))
    state_shape = jax.ShapeDtypeStruct((b,) + GLA_STATE, F32)
    return pl.pallas_call(
        _gla_kernel,
        grid=(b // nb, nt),
        in_specs=specs(fwd) + specs(bwd) + [
            pl.BlockSpec((None,) + wg.shape[1:], lambda i, t: (layer, 0, 0)),
            pl.BlockSpec((None,) + bg.shape[1:], lambda i, t: (layer, 0, 0)),
            pl.BlockSpec(ones.shape, lambda i, t: (0, 0, 0)),
            pl.BlockSpec(sel.shape, lambda i, t: (0, 0, 0)),
            state_spec, state_spec],
        out_specs=[pl.BlockSpec((nb, tm, vw), lambda i, t: (i, fwd(t), 0)),
                   pl.BlockSpec((nb, tm, vw), lambda i, t: (i, bwd(t), 0)),
                   state_spec, state_spec],
        out_shape=[jax.ShapeDtypeStruct((b, r, vw), BF16), jax.ShapeDtypeStruct((b, r, vw), BF16),
                   state_shape, state_shape],
        scratch_shapes=[pltpu.VMEM((nb,) + GLA_STATE, F32), pltpu.VMEM((nb,) + GLA_STATE, F32)],
        compiler_params=_params(2),
        name="gla_scan",
    )(proj, proj, proj, proj, proj, proj, proj, proj, wg, bg, ones, sel, sf0, sb0)


def _outproj_kernel(x_ref, mod_ref, mla_ref, swa_ref, of_ref, ob_ref, r_ref, g_ref, w_ref, o_ref):
    o = of_ref[0].astype(F32) + ob_ref[0].astype(F32)
    gate = _silu(r_ref[0].astype(F32))
    parts = []
    for h in range(GLA_HEADS):
        vs = slice(h * GLA_DV, (h + 1) * GLA_DV)
        parts.append((_rms(o[:, vs], g_ref[:, vs]) * gate[:, vs]).astype(BF16))
    mix = _dot(jnp.concatenate([mla_ref[0], swa_ref[0]] + parts, axis=1), w_ref[...])
    o_ref[0] = x_ref[0] + mod_ref[0, 2:3, :] * mix


def _out_projection(x, mod, mla, swa, o_f, o_b, proj, g_out, w_out, layer):
    bx, r, d = x.shape
    tm = 512
    row = lambda width: pl.BlockSpec((1, tm, width), lambda b, t: (b, t, 0))
    gw = GLA_HEADS * GLA_DV
    return pl.pallas_call(
        _outproj_kernel,
        grid=(bx, r // tm),
        in_specs=[row(d),
                  pl.BlockSpec((1, N_MOD, d), lambda b, t: (b, 0, 0)),
                  row(mla.shape[2]), row(swa.shape[2]), row(gw), row(gw),
                  pl.BlockSpec((1, tm, gw), lambda b, t: (b, t, P_GLA_R // gw)),
                  pl.BlockSpec((1, gw), lambda b, t: (0, 0)),
                  pl.BlockSpec((None,) + w_out.shape[1:], lambda b, t: (layer, 0, 0))],
        out_specs=row(d),
        out_shape=jax.ShapeDtypeStruct(x.shape, F32),
        compiler_params=_params(2),
        name="out_projection",
    )(x, mod, mla, swa, o_f, o_b, proj, g_out, w_out)


def _ffn_kernel(x_ref, mod_ref, g_ref, wg_ref, wu_ref, wd_ref, gf_ref, o_ref, h_ref, *, final_norm):
    f = pl.program_id(2)
    row_chunks = [slice(r, r + ROW_CHUNK) for r in range(0, x_ref.shape[1], ROW_CHUNK)]
    chunk = (ROW_CHUNK, x_ref.shape[2])

    def gated_partial():
        h = h_ref[...]
        act = _silu(_dot(h, wg_ref[...])) * _dot(h, wu_ref[...])
        return mod_ref[0, 5:6, :] * _dot(act.astype(BF16), wd_ref[...])

    @pl.when(f == 0)
    def _():
        gain = jnp.broadcast_to(g_ref[...] * (1.0 + mod_ref[0, 4:5, :]), chunk)
        shift = jnp.broadcast_to(mod_ref[0, 3:4, :], chunk)
        for rows in row_chunks:
            x = x_ref[0, rows, :]
            rs = lax.rsqrt(jnp.mean(x * x, axis=-1, keepdims=True) + EPS)
            h_ref[rows, :] = (x * rs * gain + shift).astype(BF16)
        o_ref[0] = x_ref[0] + gated_partial()

    @pl.when(f > 0)
    def _():
        o_ref[0] += gated_partial()

    if final_norm:
        @pl.when(f == pl.num_programs(2) - 1)
        def _():
            g_fin = jnp.broadcast_to(gf_ref[...], chunk)
            for rows in row_chunks:
                y = o_ref[0, rows, :]
                o_ref[0, rows, :] = y * lax.rsqrt(jnp.mean(y * y, axis=-1, keepdims=True) + EPS) * g_fin


def _ffn(x, mod, g, w_gu, w_down, g_final, final_norm, layer):
    bx, r, d = x.shape
    hidden = w_down.shape[1]
    tm, tf = min(r, 1024), 512
    nf = hidden // tf
    return pl.pallas_call(
        functools.partial(_ffn_kernel, final_norm=final_norm),
        grid=(bx, r // tm, nf),
        in_specs=[pl.BlockSpec((1, tm, d), lambda b, t, f: (b, t, 0)),
                  pl.BlockSpec((1, N_MOD, d), lambda b, t, f: (b, 0, 0)),
                  pl.BlockSpec((1, d), lambda b, t, f: (0, 0)),
                  pl.BlockSpec((None, d, tf), lambda b, t, f: (layer, 0, f)),
                  pl.BlockSpec((None, d, tf), lambda b, t, f: (layer, 0, nf + f)),
                  pl.BlockSpec((None, tf, d), lambda b, t, f: (layer, f, 0)),
                  pl.BlockSpec((1, d), lambda b, t, f: (0, 0))],
        out_specs=pl.BlockSpec((1, tm, d), lambda b, t, f: (b, t, 0)),
        out_shape=jax.ShapeDtypeStruct(x.shape, F32),
        scratch_shapes=[pltpu.VMEM((tm, d), BF16)],
        compiler_params=_params(3),
        name="ffn",
    )(x, mod, g, w_gu, w_gu, w_down, g_final)


def _rope_tables(n, dim):
    half = dim // 4
    freqs = ROPE_THETA ** (-jnp.arange(half, dtype=F32) / half)
    pos = jnp.arange(n, dtype=jnp.int32)
    ang_r = (pos // GRID_W).astype(F32)[:, None] * freqs
    ang_c = (pos % GRID_W).astype(F32)[:, None] * freqs
    cos = jnp.concatenate([jnp.cos(ang_r)] * 2 + [jnp.cos(ang_c)] * 2, axis=1)
    sin = jnp.concatenate([-jnp.sin(ang_r), jnp.sin(ang_r), -jnp.sin(ang_c), jnp.sin(ang_c)], axis=1)
    reps = 128 // dim
    return jnp.tile(cos, (1, reps)), jnp.tile(sin, (1, reps))


def _relayout_w_in(w_in):
    depth, d, _ = w_in.shape
    sizes = (MLA_RANK, MLA_RANK, MLA_ROPE, SWA_HEADS * SWA_DIM, SWA_KV_HEADS * SWA_DIM, SWA_KV_HEADS * SWA_DIM,
             GLA_HEADS * GLA_DK, GLA_HEADS * GLA_DK, GLA_HEADS * GLA_DV, 2 * GLA_GATE_RANK, GLA_HEADS * GLA_DV)
    offs = [0]
    for s in sizes:
        offs.append(offs[-1] + s)
    span = lambda i, j: w_in[:, :, offs[i]:offs[j]].astype(BF16)
    pad = jnp.zeros((depth, d, P_GLA_Q - P_MISC - MLA_ROPE - 2 * GLA_GATE_RANK), BF16)
    out = jnp.concatenate([span(3, 5), span(0, 2), span(5, 6), span(2, 3), span(9, 10), pad,
                           span(6, 9), span(10, 11)], axis=-1)
    assert out.shape[-1] == P_WIDTH
    return out


def _relayout_mla(w_uq, w_ukv):
    depth, rk, _ = w_uq.shape
    uq = w_uq.reshape(depth, rk, MLA_HEADS, MLA_QK)
    uq = jnp.concatenate([uq[..., :MLA_NOPE].reshape(depth, rk, -1), uq[..., MLA_NOPE:].reshape(depth, rk, -1)], -1)
    ukv = w_ukv.reshape(depth, rk, MLA_HEADS, MLA_NOPE + MLA_V)
    ukv = jnp.concatenate([ukv[..., :MLA_NOPE].reshape(depth, rk, -1), ukv[..., MLA_NOPE:].reshape(depth, rk, -1)], -1)
    return uq.astype(BF16), ukv.astype(BF16)


def _relayout_gate(w_f, b_f, w_b, b_b):
    depth, rank, width = w_f.shape
    wg = jnp.zeros((depth, 128, 2 * width), F32)
    wg = wg.at[:, MISC_GATE_OFF:MISC_GATE_OFF + rank, :width].set(w_f)
    wg = wg.at[:, MISC_GATE_OFF + rank:MISC_GATE_OFF + 2 * rank, width:].set(w_b)
    bg = jnp.concatenate([b_f, b_b], axis=-1).reshape(depth, 1, 2 * width)
    return wg.astype(BF16), bg


def kernel(x, c, ctx, c_ctx, w_mod, b_mod, g_mix, g_ffn, w_in, g_mla_q, g_mla_kv, w_mla_uq, w_mla_ukv,
           swa_sink, w_gla_gate_f, b_gla_gate_f, w_gla_gate_b, b_gla_gate_b, g_gla_out, w_out, w_ffn_gu,
           w_ffn_down, g_final):
    B, N, D = x.shape
    C = ctx.shape[1]
    depth = w_mod.shape[0]

    cvec = jnp.concatenate([c, c_ctx[None, :], jnp.zeros((16 - B - 1, D), F32)], axis=0)
    mod = _modulation(cvec, w_mod, b_mod).reshape(depth, 16, N_MOD, D)

    w_in_p = _relayout_w_in(w_in)
    w_uq_p, w_ukv_p = _relayout_mla(w_mla_uq, w_mla_ukv)
    wg_p, bg_p = _relayout_gate(w_gla_gate_f, b_gla_gate_f, w_gla_gate_b, b_gla_gate_b)
    w_out_b = w_out.astype(BF16)
    w_gu_b = w_ffn_gu.astype(BF16)
    w_down_b = w_ffn_down.astype(BF16)

    cos_m, sin_m = _rope_tables(N, MLA_ROPE)
    cos_s, sin_s = _rope_tables(N, SWA_DIM)
    cos_id, sin_id = jnp.ones((C, 128), F32), jnp.zeros((C, 128), F32)
    state0 = jnp.zeros((B,) + GLA_STATE, F32)
    g_fin = g_final.reshape(1, D)

    xc = ctx.reshape(1, B * C, D)
    for l in range(depth):
        last = l == depth - 1
        mod_l, mod_c = mod[l, :B], mod[l, B:B + 1]
        g_mix_l, g_ffn_l = g_mix[l].reshape(1, D), g_ffn[l].reshape(1, D)
        g_q, g_kv = g_mla_q[l].reshape(1, -1), g_mla_kv[l].reshape(1, -1)
        g_out = g_gla_out[l].reshape(1, -1)
        sink = swa_sink[l].reshape(1, -1)

        proj_l = _in_projection(x, mod_l, g_mix_l, w_in_p, l)
        proj_c = _in_projection(xc, mod_c, g_mix_l, w_in_p, l).reshape(B, C, P_WIDTH)

        q_l, k_l, v_l = _mla_prep(proj_l, cos_m, sin_m, g_q, g_kv, w_uq_p, w_ukv_p, l)
        q_c, k_c, v_c = _mla_prep(proj_c, cos_id, sin_id, g_q, g_kv, w_uq_p, w_ukv_p, l)
        mla_l = _mla_attention(q_l, k_l, v_l, k_c, v_c)
        swa_l = _swa_attention(proj_l, proj_c, cos_s, sin_s, sink)

        of_c, ob_c, s_f, s_b = _gla_scan(proj_c, wg_p, bg_p, state0, state0, l)
        of_l, ob_l, _, _ = _gla_scan(proj_l, wg_p, bg_p, s_f, s_b, l)

        x = _out_projection(x, mod_l, mla_l, swa_l, of_l, ob_l, proj_l, g_out, w_out_b, l)
        x = _ffn(x, mod_l, g_ffn_l, w_gu_b, w_down_b, g_fin, last, l)

        if not last:
            mla_c = _mla_ctx_attention(q_c, k_c, v_c)
            swa_c = _swa_ctx_attention(proj_c, sink)
            flat = lambda a: a.reshape(1, B * C, a.shape[-1])
            xc = _out_projection(xc, mod_c, flat(mla_c), flat(swa_c), flat(of_c), flat(ob_c), flat(proj_c),
                                 g_out, w_out_b, l)
            xc = _ffn(xc, mod_c, g_ffn_l, w_gu_b, w_down_b, g_fin, False, l)
    return x
```

```python
import functools

import jax
import jax.numpy as jnp
import numpy as np
from jax import lax
from jax.experimental import pallas as pl
from jax.experimental.pallas import tpu as pltpu

F32 = jnp.float32
BF16 = jnp.bfloat16

GRID_W = 64
EPS = 1e-6
ROPE_THETA = 10000.0
LOG2E = 1.4426950408889634

MLA_HEADS = 6
MLA_RANK = 512
MLA_NOPE = 128
MLA_ROPE = 64
MLA_V = 128
MLA_QK = MLA_NOPE + MLA_ROPE
MLA_VA = MLA_V + 16

SWA_HEADS = 6
SWA_KV_HEADS = 2
SWA_GROUP = SWA_HEADS // SWA_KV_HEADS
SWA_DIM = 128
SWA_BLOCK = 128

GLA_HEADS = 4
GLA_DK = 64
GLA_DV = 128
GLA_GATE_RANK = 16
GLA_TAU = 16.0
GLA_CHUNK = 64
GLA_STATE = (GLA_HEADS // 2, GLA_DV, 2 * GLA_DK)

N_MOD = 6

P_SWA_Q = 0
P_SWA_K = 768
P_CQ = 1024
P_CKV = 1536
P_SWA_V = 2048
P_MISC = 2304
P_GLA_Q = 2560
P_GLA_K = 2816
P_GLA_V = 3072
P_GLA_R = 3584
P_WIDTH = 4096
MISC_GATE_OFF = MLA_ROPE

VMEM_LIMIT = 56 * 1024 * 1024
ROW_CHUNK = 16


def _params(n_axes, flags=None):
    return pltpu.CompilerParams(dimension_semantics=("arbitrary",) * n_axes,
                                vmem_limit_bytes=VMEM_LIMIT, flags=flags)


def _silu(x):
    return x / (1.0 + jnp.exp(-x))


def _rms(x, g):
    ms = jnp.mean(x * x, axis=-1, keepdims=True)
    return x * lax.rsqrt(ms + EPS) * g


def _dot(a, b):
    return jnp.dot(a, b, preferred_element_type=F32)


def _dot_nt(a, b):
    return lax.dot_general(a, b, (((1,), (1,)), ((), ())), preferred_element_type=F32)


def _dot_tn(a, b):
    return lax.dot_general(a, b, (((0,), (0,)), ((), ())), preferred_element_type=F32)


def _mod_kernel(c_ref, w_ref, b_ref, o_ref):
    a = _silu(c_ref[...]).astype(BF16)
    o_ref[0] = _dot(a, w_ref[0].astype(BF16)) + b_ref[0]


def _modulation(cvec, w_mod, b_mod):
    depth, d, width = w_mod.shape
    rows = cvec.shape[0]
    tn = 1024
    return pl.pallas_call(
        _mod_kernel,
        grid=(depth, width // tn),
        in_specs=[pl.BlockSpec((rows, d), lambda l, j: (0, 0)),
                  pl.BlockSpec((1, d, tn), lambda l, j: (l, 0, j)),
                  pl.BlockSpec((1, 1, tn), lambda l, j: (l, 0, j))],
        out_specs=pl.BlockSpec((1, rows, tn), lambda l, j: (l, 0, j)),
        out_shape=jax.ShapeDtypeStruct((depth, rows, width), F32),
        compiler_params=_params(2),
        name="modulation",
    )(cvec, w_mod, b_mod.reshape(depth, 1, width))


def _inproj_kernel(x_ref, mod_ref, g_ref, w_ref, o_ref):
    h = _rms(x_ref[0], g_ref[...]) * (1.0 + mod_ref[0, 1:2, :]) + mod_ref[0, 0:1, :]
    o_ref[0] = _dot(h.astype(BF16), w_ref[...]).astype(o_ref.dtype)


def _in_projection(x, mod, g, w, layer):
    bx, r, d = x.shape
    tm, tn = 512, P_WIDTH
    return pl.pallas_call(
        _inproj_kernel,
        grid=(P_WIDTH // tn, bx, r // tm),
        in_specs=[pl.BlockSpec((1, tm, d), lambda j, b, t: (b, t, 0)),
                  pl.BlockSpec((1, N_MOD, d), lambda j, b, t: (b, 0, 0)),
                  pl.BlockSpec((1, d), lambda j, b, t: (0, 0)),
                  pl.BlockSpec((None, d, tn), lambda j, b, t: (layer, 0, j), pipeline_mode=pl.Buffered(1))],
        out_specs=pl.BlockSpec((1, tm, tn), lambda j, b, t: (b, t, j)),
        out_shape=jax.ShapeDtypeStruct((bx, r, P_WIDTH), BF16),
        compiler_params=_params(3),
        name="in_projection",
    )(x, mod, g, w)


def _rope(x, cos, sin, half):
    lane = lax.broadcasted_iota(jnp.int32, x.shape, 1)
    first = (lane % (2 * half)) < half
    width = x.shape[1]
    rot = jnp.where(first, pltpu.roll(x, width - half, 1), pltpu.roll(x, half, 1))
    return x * cos + rot * sin


def _mla_prep_kernel(cq_ref, ckv_ref, misc_ref, cos_ref, sin_ref, gq_ref, gkv_ref, wuq_ref, wukv_ref,
                     qt_ref, k_ref, vt_ref):
    scale = MLA_QK ** -0.5 * LOG2E
    cos, sin = cos_ref[...], sin_ref[...]
    half = MLA_ROPE // 4
    qf = _dot(_rms(cq_ref[0].astype(F32), gq_ref[...]).astype(BF16), wuq_ref[...])
    kvf = _dot(_rms(ckv_ref[0].astype(F32), gkv_ref[...]).astype(BF16), wukv_ref[...])
    kr = _rope(misc_ref[0].astype(F32), cos, sin, half)[:, :MLA_ROPE].astype(BF16)
    nope_w = MLA_HEADS * MLA_NOPE
    for pair in range(MLA_HEADS // 2):
        qr = _rope(qf[:, nope_w + 128 * pair: nope_w + 128 * (pair + 1)], cos, sin, half) * scale
        qrt = qr.astype(BF16).T
        for j in range(2):
            qt_ref[0, 2 * pair + j, MLA_NOPE:MLA_QK, :] = qrt[MLA_ROPE * j: MLA_ROPE * (j + 1), :]
    for h in range(MLA_HEADS):
        qt_ref[0, h, 0:MLA_NOPE, :] = (qf[:, MLA_NOPE * h: MLA_NOPE * (h + 1)] * scale).astype(BF16).T
        k_ref[0, h, :, 0:MLA_NOPE] = kvf[:, MLA_NOPE * h: MLA_NOPE * (h + 1)].astype(BF16)
        k_ref[0, h, :, MLA_NOPE:MLA_QK] = kr
        vt_ref[0, h, 0:MLA_V, :] = kvf[:, nope_w + MLA_V * h: nope_w + MLA_V * (h + 1)].astype(BF16).T
        vt_ref[0, h, MLA_V:, :] = jnp.ones((MLA_VA - MLA_V, vt_ref.shape[3]), BF16)


def _mla_prep(proj, cos, sin, g_q, g_kv, w_uq, w_ukv, layer):
    b, r, _ = proj.shape
    tm = min(r, 1024)
    rk = MLA_RANK
    const = lambda shape: pl.BlockSpec(shape, lambda i, t: (0,) * len(shape))
    stacked = lambda w: pl.BlockSpec((None,) + w.shape[1:], lambda i, t: (layer, 0, 0))
    return pl.pallas_call(
        _mla_prep_kernel,
        grid=(b, r // tm),
        in_specs=[pl.BlockSpec((1, tm, rk), lambda i, t: (i, t, P_CQ // rk)),
                  pl.BlockSpec((1, tm, rk), lambda i, t: (i, t, P_CKV // rk)),
                  pl.BlockSpec((1, tm, 128), lambda i, t: (i, t, P_MISC // 128)),
                  pl.BlockSpec((tm, 128), lambda i, t: (t, 0)),
                  pl.BlockSpec((tm, 128), lambda i, t: (t, 0)),
                  const((1, rk)), const((1, rk)),
                  stacked(w_uq), stacked(w_ukv)],
        out_specs=[pl.BlockSpec((1, MLA_HEADS, MLA_QK, tm), lambda i, t: (i, 0, 0, t)),
                   pl.BlockSpec((1, MLA_HEADS, tm, MLA_QK), lambda i, t: (i, 0, t, 0)),
                   pl.BlockSpec((1, MLA_HEADS, MLA_VA, tm), lambda i, t: (i, 0, 0, t))],
        out_shape=[jax.ShapeDtypeStruct((b, MLA_HEADS, MLA_QK, r), BF16),
                   jax.ShapeDtypeStruct((b, MLA_HEADS, r, MLA_QK), BF16),
                   jax.ShapeDtypeStruct((b, MLA_HEADS, MLA_VA, r), BF16)],
        compiler_params=_params(2),
        name="mla_prep",
    )(proj, proj, proj, cos, sin, g_q, g_kv, w_uq, w_ukv)


def _softmax_pv(s_parts, vt_parts):
    m = functools.reduce(jnp.maximum, [jnp.max(s, axis=0, keepdims=True) for s in s_parts])
    acc = sum(_dot(vt, jnp.exp2(s - m).astype(BF16)) for s, vt in zip(s_parts, vt_parts))
    return (acc[:MLA_V] / acc[MLA_V:MLA_V + 1]).T


def _mla_attn_kernel(qt_ref, kl_ref, kc_ref, vlt_ref, vct_ref, o_ref, s_scr, m_scr, *, tk):
    @pl.when(pl.program_id(0) == 0)
    def _():
        s_scr[...] = jnp.zeros_like(s_scr)
        m_scr[...] = jnp.zeros_like(m_scr)

    n_lat = kl_ref.shape[2] // tk
    qt = qt_ref[0, 0]
    tq = qt.shape[1]
    m_prev = m_scr[...]
    m_new = jnp.full((1, tq), -jnp.inf, F32)
    acc = jnp.zeros((MLA_VA, tq), F32)
    for j in range(n_lat + 1):
        rows = slice(j * tk, (j + 1) * tk)
        p = jnp.exp2(s_scr[rows, :] - m_prev).astype(BF16)
        k = kc_ref[0, 0] if j == n_lat else kl_ref[0, 0, rows, :]
        vt = vct_ref[0, 0] if j == n_lat else vlt_ref[0, 0, :, rows]
        s = _dot(k, qt)
        s_scr[rows, :] = s
        m_new = jnp.maximum(m_new, jnp.max(s, axis=0, keepdims=True))
        acc = acc + _dot(vt, p)
    m_scr[...] = m_new
    o_ref[0] = (acc[:MLA_V] / acc[MLA_V:MLA_V + 1]).T.astype(o_ref.dtype)


def _mla_attention(qt, k_l, vt_l, k_c, vt_c):
    b, h, dq, n = qt.shape
    c = k_c.shape[2]
    tq = min(n, 1024)
    nt = n // tq
    tiles = b * h * nt

    def cur(u):
        t = jnp.minimum(u, tiles - 1)
        return t // (h * nt), (t // nt) % h, t % nt

    def prev(u):
        t = jnp.maximum(u - 1, 0)
        return t // (h * nt), (t // nt) % h, t % nt

    assert n % c == 0
    return pl.pallas_call(
        functools.partial(_mla_attn_kernel, tk=c),
        grid=(tiles + 1,),
        in_specs=[pl.BlockSpec((1, 1, dq, tq), lambda u: (cur(u)[0], cur(u)[1], 0, cur(u)[2])),
                  pl.BlockSpec((1, 1, n, dq), lambda u: (cur(u)[0], cur(u)[1], 0, 0)),
                  pl.BlockSpec((1, 1, c, dq), lambda u: (cur(u)[0], cur(u)[1], 0, 0)),
                  pl.BlockSpec((1, 1, MLA_VA, n), lambda u: (prev(u)[0], prev(u)[1], 0, 0)),
                  pl.BlockSpec((1, 1, MLA_VA, c), lambda u: (prev(u)[0], prev(u)[1], 0, 0))],
        out_specs=pl.BlockSpec((1, tq, MLA_V), lambda u: (prev(u)[0], prev(u)[2], prev(u)[1])),
        out_shape=jax.ShapeDtypeStruct((b, n, h * MLA_V), BF16),
        scratch_shapes=[pltpu.VMEM((n + c, tq), F32), pltpu.VMEM((1, tq), F32)],
        compiler_params=_params(1),
        name="mla_attention",
    )(qt, k_l, k_c, vt_l, vt_c)


def _mla_ctx_kernel(qt_ref, k_ref, vt_ref, o_ref):
    o_ref[0] = _softmax_pv([_dot(k_ref[0, 0], qt_ref[0, 0])], [vt_ref[0, 0]]).astype(o_ref.dtype)


def _mla_ctx_attention(qt, k, vt):
    b, h, dq, c = qt.shape
    return pl.pallas_call(
        _mla_ctx_kernel,
        grid=(b, h),
        in_specs=[pl.BlockSpec((1, 1, dq, c), lambda i, j: (i, j, 0, 0)),
                  pl.BlockSpec((1, 1, c, dq), lambda i, j: (i, j, 0, 0)),
                  pl.BlockSpec((1, 1, MLA_VA, c), lambda i, j: (i, j, 0, 0))],
        out_specs=pl.BlockSpec((1, c, MLA_V), lambda i, j: (i, 0, j)),
        out_shape=jax.ShapeDtypeStruct((b, c, h * MLA_V), BF16),
        compiler_params=_params(2),
        name="mla_ctx_attention",
    )(qt, k, vt)


def _swa_ctx_kernel(sink_ref, q_ref, k_ref, v_ref, o_ref):
    q = (q_ref[0].astype(F32) * SWA_DIM ** -0.5).astype(BF16)
    s = _dot_nt(q, k_ref[0].astype(BF16))
    sink = sink_ref[0, pl.program_id(1)]
    m = jnp.maximum(jnp.max(s, axis=-1, keepdims=True), sink)
    p = jnp.exp(s - m)
    l = jnp.sum(p, axis=-1, keepdims=True) + jnp.exp(sink - m)
    o_ref[0] = (_dot(p.astype(BF16), v_ref[0].astype(BF16)) / l).astype(o_ref.dtype)


def _swa_ctx_attention(proj_c, sink):
    b, c, _ = proj_c.shape
    d = SWA_DIM
    return pl.pallas_call(
        _swa_ctx_kernel,
        grid=(b, SWA_HEADS),
        in_specs=[pl.BlockSpec(memory_space=pltpu.SMEM),
                  pl.BlockSpec((1, c, d), lambda i, j: (i, 0, P_SWA_Q // d + j)),
                  pl.BlockSpec((1, c, d), lambda i, j: (i, 0, P_SWA_K // d + j // SWA_GROUP)),
                  pl.BlockSpec((1, c, d), lambda i, j: (i, 0, P_SWA_V // d + j // SWA_GROUP))],
        out_specs=pl.BlockSpec((1, c, d), lambda i, j: (i, 0, j)),
        out_shape=jax.ShapeDtypeStruct((b, c, SWA_HEADS * d), BF16),
        compiler_params=_params(2),
        name="swa_ctx_attention",
    )(sink, proj_c, proj_c, proj_c)


def _swa_window_start(t, qb, nb):
    return pl.multiple_of(jnp.clip(t * qb - 1, 0, nb - (qb + 2)) * SWA_BLOCK, SWA_BLOCK)


def _swa_kernel(sink_ref, q_ref, k_ref, kc_ref, v_ref, vc_ref, cos_ref, sin_ref, o_ref, s_scr, m_scr, *,
                qb, tiles, nt):
    u = pl.program_id(0)

    @pl.when(u == 0)
    def _():
        s_scr[...] = jnp.zeros_like(s_scr)
        m_scr[...] = jnp.zeros_like(m_scr)

    blk, d = SWA_BLOCK, SWA_DIM
    half = d // 4
    nq = qb * blk
    nw = (qb + 2) * blk
    nb = k_ref.shape[1] // blk
    ck = kc_ref.shape[1]
    cur, prev = jnp.minimum(u, tiles - 1), jnp.maximum(u - 1, 0)
    t_cur, t_prev = cur % nt, prev % nt
    q_start = pl.multiple_of(t_cur * nq, blk)
    k_start = _swa_window_start(t_cur, qb, nb)
    v_start = _swa_window_start(t_prev, qb, nb)
    groups = range(SWA_KV_HEADS)

    def sink_row(g):
        return jnp.concatenate(
            [jnp.full((1, nq), sink_ref[0, g * SWA_GROUP + i] * LOG2E, F32) for i in range(SWA_GROUP)], axis=1)

    def values_t(v):
        return jnp.concatenate([v.astype(F32).T.astype(BF16), jnp.ones((16, v.shape[0]), BF16)], axis=0)

    scale = d ** -0.5 * LOG2E
    cos_q, sin_q = cos_ref[pl.ds(q_start, nq), :], sin_ref[pl.ds(q_start, nq), :]
    q = [jnp.concatenate(
        [(_rope(q_ref[0, :, d * h: d * (h + 1)].astype(F32), cos_q, sin_q, half) * scale).astype(BF16)
         for h in range(g * SWA_GROUP, (g + 1) * SWA_GROUP)], axis=0) for g in groups]
    key = lax.broadcasted_iota(jnp.int32, (ck, nq), 0)
    tok = lax.broadcasted_iota(jnp.int32, (ck, nq), 1)
    m_prev = [m_scr[g] for g in groups]
    m_new = [sink_row(g) for g in groups]
    acc = [jnp.zeros((d + 16, SWA_GROUP * nq), F32) for g in groups]
    n_win = nw // ck
    for j in range(n_win + 1):
        rows = slice(j * ck, (j + 1) * ck)
        if j < n_win:
            kr = pl.ds(k_start + j * ck, ck)
            cos_k, sin_k = cos_ref[kr, :], sin_ref[kr, :]
            dist = key - tok + (k_start + j * ck - q_start)
            valid = (dist <= blk) & (dist >= -blk)
        for g in groups:
            cols = slice(d * g, d * (g + 1))
            p = jnp.exp2(s_scr[g, rows, :] - m_prev[g]).astype(BF16)
            if j < n_win:
                vt = values_t(v_ref[0, pl.ds(v_start + j * ck, ck), cols])
                k = _rope(k_ref[0, kr, cols].astype(F32), cos_k, sin_k, half).astype(BF16)
                s = _dot_nt(k, q[g])
                s = jnp.concatenate(
                    [jnp.where(valid, s[:, nq * i: nq * (i + 1)], -jnp.inf) for i in range(SWA_GROUP)], axis=1)
            else:
                vt = values_t(vc_ref[0, :, cols])
                s = _dot_nt(kc_ref[0, :, cols].astype(BF16), q[g])
            s_scr[g, rows, :] = s
            m_new[g] = jnp.maximum(m_new[g], jnp.max(s, axis=0, keepdims=True))
            acc[g] = acc[g] + _dot(vt, p)
    for g in groups:
        m_scr[g] = m_new[g]
        out_t = acc[g][:d] / (acc[g][d:d + 1] + jnp.exp2(sink_row(g) - m_prev[g]))
        for i in range(SWA_GROUP):
            h = g * SWA_GROUP + i
            o_ref[0, :, d * h: d * (h + 1)] = out_t[:, nq * i: nq * (i + 1)].T.astype(o_ref.dtype)


def _swa_attention(proj_l, proj_c, cos, sin, sink):
    b, n, _ = proj_l.shape
    c = proj_c.shape[1]
    blk, d = SWA_BLOCK, SWA_DIM
    qb = 2
    nq = qb * blk
    nt = n // nq
    tiles = b * nt
    assert ((qb + 2) * blk) % c == 0 and n // blk >= qb + 2
    cur = lambda u: jnp.minimum(u, tiles - 1)
    prev = lambda u: jnp.maximum(u - 1, 0)
    qw, kw = SWA_HEADS * d, SWA_KV_HEADS * d
    return pl.pallas_call(
        functools.partial(_swa_kernel, qb=qb, tiles=tiles, nt=nt),
        grid=(tiles + 1,),
        in_specs=[pl.BlockSpec(memory_space=pltpu.SMEM),
                  pl.BlockSpec((1, nq, qw), lambda u: (cur(u) // nt, cur(u) % nt, P_SWA_Q // qw)),
                  pl.BlockSpec((1, n, kw), lambda u: (cur(u) // nt, 0, P_SWA_K // kw)),
                  pl.BlockSpec((1, c, kw), lambda u: (cur(u) // nt, 0, P_SWA_K // kw)),
                  pl.BlockSpec((1, n, kw), lambda u: (prev(u) // nt, 0, P_SWA_V // kw)),
                  pl.BlockSpec((1, c, kw), lambda u: (prev(u) // nt, 0, P_SWA_V // kw)),
                  pl.BlockSpec((n, d), lambda u: (0, 0)),
                  pl.BlockSpec((n, d), lambda u: (0, 0))],
        out_specs=pl.BlockSpec((1, nq, qw), lambda u: (prev(u) // nt, prev(u) % nt, 0)),
        out_shape=jax.ShapeDtypeStruct((b, n, qw), BF16),
        scratch_shapes=[pltpu.VMEM((SWA_KV_HEADS, (qb + 2) * blk + c, SWA_GROUP * nq), F32),
                        pltpu.VMEM((SWA_KV_HEADS, 1, SWA_GROUP * nq), F32)],
        compiler_params=_params(1),
        name="swa_attention",
    )(sink, proj_l, proj_l, proj_c, proj_l, proj_c, cos, sin)


def _log_sigmoid(x):
    return jnp.minimum(x, 0.0) - jnp.log(1.0 + jnp.exp(-jnp.abs(x)))


def _gla_block_constants(rows):
    t = np.arange(rows)
    ti, tj = t[:, None], t[None, :]
    ci, cj = ti // GLA_CHUNK, tj // GLA_CHUNK
    ones, sel = [], []
    for reverse in (False, True):
        upto = (tj >= ti) if reverse else (tj <= ti)
        visible = (tj > ti) if reverse else (tj <= ti)
        earlier = (cj > ci) if reverse else (cj < ci)
        ones.append((ci == cj) & upto)
        sel.append(np.where((ci == cj) & visible, 1, np.where(earlier, 2, 0)))
    return jnp.asarray(np.stack(ones), BF16), jnp.asarray(np.stack(sel), jnp.int32)


def _gla_factors(q_ref, k_ref, misc_ref, i, wg, bg, ones, sel, reverse):
    rows = q_ref.shape[1]
    L = GLA_CHUNK
    assert rows == 4 * L
    width = GLA_HEADS * GLA_DK
    la = _log_sigmoid(_dot(misc_ref[i].astype(BF16), wg) + bg) * (1.0 / GLA_TAU)
    hi = la.astype(BF16)
    lo = (la - hi.astype(F32)).astype(BF16)
    sums = _dot(ones, jnp.concatenate([hi, lo], axis=1))
    b_in = sums[:, :width] + sums[:, width:]
    chunks = [b_in[c * L:(c + 1) * L] for c in range(rows // L)]
    order = list(reversed(range(len(chunks)))) if reverse else list(range(len(chunks)))
    offset = None
    for c in order:
        total = chunks[c][0:1] if reverse else chunks[c][L - 1:L]
        if offset is not None:
            chunks[c] = chunks[c] + offset
        offset = total if offset is None else offset + total
    b_abs = jnp.concatenate(chunks, axis=0)
    if reverse:
        g_mid, g_end = b_abs[rows // 2:rows // 2 + 1], b_abs[0:1]
    else:
        g_mid, g_end = b_abs[rows // 2 - 1:rows // 2], b_abs[rows - 1:rows]
    diag, cross = sel == 1, sel == 2
    q = q_ref[i].astype(F32) * GLA_DK ** -0.5
    k = k_ref[i].astype(F32)
    q_d, k_d = q * jnp.exp(b_in), k * jnp.exp(-b_in)
    q_x, k_x = q * jnp.exp(b_abs - g_mid), k * jnp.exp(g_mid - b_abs)
    q_s, k_s = q * jnp.exp(b_abs), k * jnp.exp(g_end - b_abs)
    decay = jnp.exp(g_end)
    return q_d, k_d, q_x, k_x, q_s, k_s, decay, diag, cross


def _gla_apply(factors, v_ref, state_ref, o_ref, i):
    q_d, k_d, q_x, k_x, q_s, k_s, decay, diag, cross = factors
    rows = q_d.shape[0]
    lane = lax.broadcasted_iota(jnp.int32, (rows, 2 * GLA_DK), 1)
    lane_s = lax.broadcasted_iota(jnp.int32, (GLA_DV, 2 * GLA_DK), 1)
    for pair in range(GLA_HEADS // 2):
        ps = slice(2 * GLA_DK * pair, 2 * GLA_DK * (pair + 1))
        kd_p, kx_p, ks_p = k_d[:, ps].astype(BF16), k_x[:, ps].astype(BF16), k_s[:, ps].astype(BF16)
        st = state_ref[i, pair]
        st_b = st.astype(BF16)
        ds = []
        for j in range(2):
            h = 2 * pair + j
            vs = slice(h * GLA_DV, (h + 1) * GLA_DV)
            mine = (lane // GLA_DK) == j
            qd_h = jnp.where(mine, q_d[:, ps], 0.0).astype(BF16)
            qx_h = jnp.where(mine, q_x[:, ps], 0.0).astype(BF16)
            qs_h = jnp.where(mine, q_s[:, ps], 0.0).astype(BF16)
            vh = v_ref[i, :, vs].astype(BF16)
            a = jnp.where(diag, _dot_nt(qd_h, kd_p), jnp.where(cross, _dot_nt(qx_h, kx_p), 0.0))
            o_ref[i, :, vs] = (_dot(a.astype(BF16), vh) + _dot_nt(qs_h, st_b)).astype(o_ref.dtype)
            ds.append(_dot_tn(vh, ks_p))
        state_ref[i, pair] = decay[:, ps] * st + jnp.where(lane_s < GLA_DK, ds[0], ds[1])


def _gla_kernel(qf_ref, kf_ref, vf_ref, mf_ref, qb_ref, kb_ref, vb_ref, mb_ref, wg_ref, bg_ref, ones_ref, sel_ref,
                sf0_ref, sb0_ref, of_ref, ob_ref, sf_ref, sb_ref, stf, stb):
    t = pl.program_id(1)

    @pl.when(t == 0)
    def _():
        stf[...] = sf0_ref[...]
        stb[...] = sb0_ref[...]

    width = GLA_HEADS * GLA_DK
    entries = range(qf_ref.shape[0])
    fwd = [_gla_factors(qf_ref, kf_ref, mf_ref, i, wg_ref[:, :width], bg_ref[:, :width], ones_ref[0], sel_ref[0],
                        False) for i in entries]
    bwd = [_gla_factors(qb_ref, kb_ref, mb_ref, i, wg_ref[:, width:], bg_ref[:, width:], ones_ref[1], sel_ref[1],
                        True) for i in entries]
    for i in entries:
        _gla_apply(fwd[i], vf_ref, stf, of_ref, i)
        _gla_apply(bwd[i], vb_ref, stb, ob_ref, i)

    @pl.when(t == pl.num_programs(1) - 1)
    def _():
        sf_ref[...] = stf[...]
        sb_ref[...] = stb[...]


def _gla_scan(proj, wg, bg, sf0, sb0, layer):
    b, r, _ = proj.shape
    tm = 256
    nt = r // tm
    kw, vw = GLA_HEADS * GLA_DK, GLA_HEADS * GLA_DV
    fwd = lambda t: t
    bwd = lambda t: nt - 1 - t

    nb = next(k for k in (4, 2, 1) if b % k == 0)
    ones, sel = _gla_block_constants(tm)

    def specs(order):
        return [pl.BlockSpec((nb, tm, kw), lambda i, t: (i, order(t), P_GLA_Q // kw)),
                pl.BlockSpec((nb, tm, kw), lambda i, t: (i, order(t), P_GLA_K // kw)),
                pl.BlockSpec((nb, tm, vw), lambda i, t: (i, order(t), P_GLA_V // vw)),
                pl.BlockSpec((nb, tm, 128), lambda i, t: (i, order(t), P_MISC // 128))]

    state_spec = pl.BlockSpec((nb,) + GLA_STATE, lambda i, t: (i, 0, 0, 0))
    state_shape = jax.ShapeDtypeStruct((b,) + GLA_STATE, F32)
    return pl.pallas_call(
        _gla_kernel,
        grid=(b // nb, nt),
        in_specs=specs(fwd) + specs(bwd) + [
            pl.BlockSpec((None,) + wg.shape[1:], lambda i, t: (layer, 0, 0)),
            pl.BlockSpec((None,) + bg.shape[1:], lambda i, t: (layer, 0, 0)),
            pl.BlockSpec(ones.shape, lambda i, t: (0, 0, 0)),
            pl.BlockSpec(sel.shape, lambda i, t: (0, 0, 0)),
            state_spec, state_spec],
        out_specs=[pl.BlockSpec((nb, tm, vw), lambda i, t: (i, fwd(t), 0)),
                   pl.BlockSpec((nb, tm, vw), lambda i, t: (i, bwd(t), 0)),
                   state_spec, state_spec],
        out_shape=[jax.ShapeDtypeStruct((b, r, vw), BF16), jax.ShapeDtypeStruct((b, r, vw), BF16),
                   state_shape, state_shape],
        scratch_shapes=[pltpu.VMEM((nb,) + GLA_STATE, F32), pltpu.VMEM((nb,) + GLA_STATE, F32)],
        compiler_params=_params(2),
        name="gla_scan",
    )(proj, proj, proj, proj, proj, proj, proj, proj, wg, bg, ones, sel, sf0, sb0)


def _outproj_kernel(x_ref, mod_ref, mla_ref, swa_ref, of_ref, ob_ref, r_ref, g_ref, w_ref, o_ref):
    o = of_ref[0].astype(F32) + ob_ref[0].astype(F32)
    gate = _silu(r_ref[0].astype(F32))
    parts = []
    for h in range(GLA_HEADS):
        vs = slice(h * GLA_DV, (h + 1) * GLA_DV)
        parts.append((_rms(o[:, vs], g_ref[:, vs]) * gate[:, vs]).astype(BF16))
    mix = _dot(jnp.concatenate([mla_ref[0], swa_ref[0]] + parts, axis=1), w_ref[...])
    o_ref[0] = x_ref[0] + mod_ref[0, 2:3, :] * mix


def _out_projection(x, mod, mla, swa, o_f, o_b, proj, g_out, w_out, layer):
    bx, r, d = x.shape
    tm = 512
    row = lambda width: pl.BlockSpec((1, tm, width), lambda b, t: (b, t, 0))
    gw = GLA_HEADS * GLA_DV
    return pl.pallas_call(
        _outproj_kernel,
        grid=(bx, r // tm),
        in_specs=[row(d),
                  pl.BlockSpec((1, N_MOD, d), lambda b, t: (b, 0, 0)),
                  row(mla.shape[2]), row(swa.shape[2]), row(gw), row(gw),
                  pl.BlockSpec((1, tm, gw), lambda b, t: (b, t, P_GLA_R // gw)),
                  pl.BlockSpec((1, gw), lambda b, t: (0, 0)),
                  pl.BlockSpec((None,) + w_out.shape[1:], lambda b, t: (layer, 0, 0))],
        out_specs=row(d),
        out_shape=jax.ShapeDtypeStruct(x.shape, F32),
        compiler_params=_params(2),
        name="out_projection",
    )(x, mod, mla, swa, o_f, o_b, proj, g_out, w_out)


def _ffn_kernel(x_ref, mod_ref, g_ref, wg_ref, wu_ref, wd_ref, gf_ref, o_ref, h_ref, *, final_norm):
    f = pl.program_id(2)
    row_chunks = [slice(r, r + ROW_CHUNK) for r in range(0, x_ref.shape[1], ROW_CHUNK)]
    chunk = (ROW_CHUNK, x_ref.shape[2])

    def gated_partial():
        h = h_ref[...]
        act = _silu(_dot(h, wg_ref[...])) * _dot(h, wu_ref[...])
        return mod_ref[0, 5:6, :] * _dot(act.astype(BF16), wd_ref[...])

    @pl.when(f == 0)
    def _():
        gain = jnp.broadcast_to(g_ref[...] * (1.0 + mod_ref[0, 4:5, :]), chunk)
        shift = jnp.broadcast_to(mod_ref[0, 3:4, :], chunk)
        for rows in row_chunks:
            x = x_ref[0, rows, :]
            rs = lax.rsqrt(jnp.mean(x * x, axis=-1, keepdims=True) + EPS)
            h_ref[rows, :] = (x * rs * gain + shift).astype(BF16)
        o_ref[0] = x_ref[0] + gated_partial()

    @pl.when(f > 0)
    def _():
        o_ref[0] += gated_partial()

    if final_norm:
        @pl.when(f == pl.num_programs(2) - 1)
        def _():
            g_fin = jnp.broadcast_to(gf_ref[...], chunk)
            for rows in row_chunks:
                y = o_ref[0, rows, :]
                o_ref[0, rows, :] = y * lax.rsqrt(jnp.mean(y * y, axis=-1, keepdims=True) + EPS) * g_fin


def _ffn(x, mod, g, w_gu, w_down, g_final, final_norm, layer):
    bx, r, d = x.shape
    hidden = w_down.shape[1]
    tm, tf = min(r, 1024), 512
    nf = hidden // tf
    return pl.pallas_call(
        functools.partial(_ffn_kernel, final_norm=final_norm),
        grid=(bx, r // tm, nf),
        in_specs=[pl.BlockSpec((1, tm, d), lambda b, t, f: (b, t, 0)),
                  pl.BlockSpec((1, N_MOD, d), lambda b, t, f: (b, 0, 0)),
                  pl.BlockSpec((1, d), lambda b, t, f: (0, 0)),
                  pl.BlockSpec((None, d, tf), lambda b, t, f: (layer, 0, f)),
                  pl.BlockSpec((None, d, tf), lambda b, t, f: (layer, 0, nf + f)),
                  pl.BlockSpec((None, tf, d), lambda b, t, f: (layer, f, 0)),
                  pl.BlockSpec((1, d), lambda b, t, f: (0, 0))],
        out_specs=pl.BlockSpec((1, tm, d), lambda b, t, f: (b, t, 0)),
        out_shape=jax.ShapeDtypeStruct(x.shape, F32),
        scratch_shapes=[pltpu.VMEM((tm, d), BF16)],
        compiler_params=_params(3),
        name="ffn",
    )(x, mod, g, w_gu, w_gu, w_down, g_final)


def _rope_tables(n, dim):
    half = dim // 4
    freqs = ROPE_THETA ** (-jnp.arange(half, dtype=F32) / half)
    pos = jnp.arange(n, dtype=jnp.int32)
    ang_r = (pos // GRID_W).astype(F32)[:, None] * freqs
    ang_c = (pos % GRID_W).astype(F32)[:, None] * freqs
    cos = jnp.concatenate([jnp.cos(ang_r)] * 2 + [jnp.cos(ang_c)] * 2, axis=1)
    sin = jnp.concatenate([-jnp.sin(ang_r), jnp.sin(ang_r), -jnp.sin(ang_c), jnp.sin(ang_c)], axis=1)
    reps = 128 // dim
    return jnp.tile(cos, (1, reps)), jnp.tile(sin, (1, reps))


def _cast_kernel(w_ref, o_ref):
    o_ref[...] = w_ref[...].astype(o_ref.dtype)


def _to_bf16(w):
    depth, rows, cols = w.shape
    tr = max(16, min(rows, (8 * 1024 * 1024) // (4 * cols) // 16 * 16))
    while rows % tr:
        tr -= 16
    return pl.pallas_call(
        _cast_kernel,
        grid=(depth, rows // tr),
        in_specs=[pl.BlockSpec((None, tr, cols), lambda l, t: (l, t, 0))],
        out_specs=pl.BlockSpec((None, tr, cols), lambda l, t: (l, t, 0)),
        out_shape=jax.ShapeDtypeStruct(w.shape, BF16),
        compiler_params=_params(2),
        name="weight_cast",
    )(w)


def _relayout_w_in(w_in):
    depth, d, _ = w_in.shape
    sizes = (MLA_RANK, MLA_RANK, MLA_ROPE, SWA_HEADS * SWA_DIM, SWA_KV_HEADS * SWA_DIM, SWA_KV_HEADS * SWA_DIM,
             GLA_HEADS * GLA_DK, GLA_HEADS * GLA_DK, GLA_HEADS * GLA_DV, 2 * GLA_GATE_RANK, GLA_HEADS * GLA_DV)
    offs = [0]
    for s in sizes:
        offs.append(offs[-1] + s)
    runs = [(3, 5), (0, 2), (5, 6), (2, 3), (9, 10), P_GLA_Q - P_MISC - MLA_ROPE - 2 * GLA_GATE_RANK, (6, 9), (10, 11)]

    def body(w_ref, o_ref):
        dst = 0
        for run in runs:
            if isinstance(run, int):
                o_ref[:, dst:dst + run] = jnp.zeros((o_ref.shape[0], run), o_ref.dtype)
                dst += run
            else:
                src, width = offs[run[0]], offs[run[1]] - offs[run[0]]
                o_ref[:, dst:dst + width] = w_ref[:, src:src + width].astype(o_ref.dtype)
                dst += width
        assert dst == P_WIDTH

    tr = 256
    return pl.pallas_call(
        body,
        grid=(depth, d // tr),
        in_specs=[pl.BlockSpec((None, tr, w_in.shape[2]), lambda l, t: (l, t, 0))],
        out_specs=pl.BlockSpec((None, tr, P_WIDTH), lambda l, t: (l, t, 0)),
        out_shape=jax.ShapeDtypeStruct((depth, d, P_WIDTH), BF16),
        compiler_params=_params(2),
        name="w_in_relayout",
    )(w_in)


def _relayout_mla(w_uq, w_ukv):
    depth, rk, _ = w_uq.shape
    uq = w_uq.reshape(depth, rk, MLA_HEADS, MLA_QK)
    uq = jnp.concatenate([uq[..., :MLA_NOPE].reshape(depth, rk, -1), uq[..., MLA_NOPE:].reshape(depth, rk, -1)], -1)
    ukv = w_ukv.reshape(depth, rk, MLA_HEADS, MLA_NOPE + MLA_V)
    ukv = jnp.concatenate([ukv[..., :MLA_NOPE].reshape(depth, rk, -1), ukv[..., MLA_NOPE:].reshape(depth, rk, -1)], -1)
    return uq.astype(BF16), ukv.astype(BF16)


def _relayout_gate(w_f, b_f, w_b, b_b):
    depth, rank, width = w_f.shape
    wg = jnp.zeros((depth, 128, 2 * width), F32)
    wg = wg.at[:, MISC_GATE_OFF:MISC_GATE_OFF + rank, :width].set(w_f)
    wg = wg.at[:, MISC_GATE_OFF + rank:MISC_GATE_OFF + 2 * rank, width:].set(w_b)
    bg = jnp.concatenate([b_f, b_b], axis=-1).reshape(depth, 1, 2 * width)
    return wg.astype(BF16), bg


def kernel(x, c, ctx, c_ctx, w_mod, b_mod, g_mix, g_ffn, w_in, g_mla_q, g_mla_kv, w_mla_uq, w_mla_ukv,
           swa_sink, w_gla_gate_f, b_gla_gate_f, w_gla_gate_b, b_gla_gate_b, g_gla_out, w_out, w_ffn_gu,
           w_ffn_down, g_final):
    B, N, D = x.shape
    C = ctx.shape[1]
    depth = w_mod.shape[0]

    cvec = jnp.concatenate([c, c_ctx[None, :], jnp.zeros((16 - B - 1, D), F32)], axis=0)
    mod = _modulation(cvec, w_mod, b_mod).reshape(depth, 16, N_MOD, D)

    w_in_p = _relayout_w_in(w_in)
    w_uq_p, w_ukv_p = _relayout_mla(w_mla_uq, w_mla_ukv)
    wg_p, bg_p = _relayout_gate(w_gla_gate_f, b_gla_gate_f, w_gla_gate_b, b_gla_gate_b)
    w_out_b = _to_bf16(w_out)
    w_gu_b = _to_bf16(w_ffn_gu)
    w_down_b = _to_bf16(w_ffn_down)

    cos_m, sin_m = _rope_tables(N, MLA_ROPE)
    cos_s, sin_s = _rope_tables(N, SWA_DIM)
    cos_id, sin_id = jnp.ones((C, 128), F32), jnp.zeros((C, 128), F32)
    state0 = jnp.zeros((B,) + GLA_STATE, F32)
    g_fin = g_final.reshape(1, D)

    xc = ctx.reshape(1, B * C, D)
    for l in range(depth):
        last = l == depth - 1
        mod_l, mod_c = mod[l, :B], mod[l, B:B + 1]
        g_mix_l, g_ffn_l = g_mix[l].reshape(1, D), g_ffn[l].reshape(1, D)
        g_q, g_kv = g_mla_q[l].reshape(1, -1), g_mla_kv[l].reshape(1, -1)
        g_out = g_gla_out[l].reshape(1, -1)
        sink = swa_sink[l].reshape(1, -1)

        proj_l = _in_projection(x, mod_l, g_mix_l, w_in_p, l)
        proj_c = _in_projection(xc, mod_c, g_mix_l, w_in_p, l).reshape(B, C, P_WIDTH)

        q_l, k_l, v_l = _mla_prep(proj_l, cos_m, sin_m, g_q, g_kv, w_uq_p, w_ukv_p, l)
        q_c, k_c, v_c = _mla_prep(proj_c, cos_id, sin_id, g_q, g_kv, w_uq_p, w_ukv_p, l)
        mla_l = _mla_attention(q_l, k_l, v_l, k_c, v_c)
        swa_l = _swa_attention(proj_l, proj_c, cos_s, sin_s, sink)

        of_c, ob_c, s_f, s_b = _gla_scan(proj_c, wg_p, bg_p, state0, state0, l)
        of_l, ob_l, _, _ = _gla_scan(proj_l, wg_p, bg_p, s_f, s_b, l)

        x = _out_projection(x, mod_l, mla_l, swa_l, of_l, ob_l, proj_l, g_out, w_out_b, l)
        x = _ffn(x, mod_l, g_ffn_l, w_gu_b, w_down_b, g_fin, last, l)

        if not last:
            mla_c = _mla_ctx_attention(q_c, k_c, v_c)
            swa_c = _swa_ctx_attention(proj_c, sink)
            flat = lambda a: a.reshape(1, B * C, a.shape[-1])
            xc = _out_projection(xc, mod_c, flat(mla_c), flat(swa_c), flat(of_c), flat(ob_c), flat(proj_c),
                                 g_out, w_out_b, l)
            xc = _ffn(xc, mod_c, g_ffn_l, w_gu_b, w_down_b, g_fin, False, l)
    return x
```

```python
import functools

import jax
import jax.numpy as jnp
import numpy as np
from jax import lax
from jax.experimental import pallas as pl
from jax.experimental.pallas import tpu as pltpu

F32 = jnp.float32
BF16 = jnp.bfloat16

GRID_W = 64
EPS = 1e-6
ROPE_THETA = 10000.0
LOG2E = 1.4426950408889634

MLA_HEADS = 6
MLA_RANK = 512
MLA_NOPE = 128
MLA_ROPE = 64
MLA_V = 128
MLA_QK = MLA_NOPE + MLA_ROPE
MLA_VA = MLA_V + 16

SWA_HEADS = 6
SWA_KV_HEADS = 2
SWA_GROUP = SWA_HEADS // SWA_KV_HEADS
SWA_DIM = 128
SWA_BLOCK = 128

GLA_HEADS = 4
GLA_DK = 64
GLA_DV = 128
GLA_GATE_RANK = 16
GLA_TAU = 16.0
GLA_CHUNK = 64
GLA_STATE = (GLA_HEADS // 2, GLA_DV, 2 * GLA_DK)

N_MOD = 6

P_SWA_Q = 0
P_SWA_K = 768
P_CQ = 1024
P_CKV = 1536
P_SWA_V = 2048
P_MISC = 2304
P_GLA_Q = 2560
P_GLA_K = 2816
P_GLA_V = 3072
P_GLA_R = 3584
P_WIDTH = 4096
MISC_GATE_OFF = MLA_ROPE

VMEM_LIMIT = 56 * 1024 * 1024
ROW_CHUNK = 16


def _params(n_axes, flags=None):
    return pltpu.CompilerParams(dimension_semantics=("arbitrary",) * n_axes,
                                vmem_limit_bytes=VMEM_LIMIT, flags=flags)


def _silu(x):
    return x / (1.0 + jnp.exp(-x))


def _rms(x, g):
    ms = jnp.mean(x * x, axis=-1, keepdims=True)
    return x * lax.rsqrt(ms + EPS) * g


def _dot(a, b):
    return jnp.dot(a, b, preferred_element_type=F32)


def _dot_nt(a, b):
    return lax.dot_general(a, b, (((1,), (1,)), ((), ())), preferred_element_type=F32)


def _dot_tn(a, b):
    return lax.dot_general(a, b, (((0,), (0,)), ((), ())), preferred_element_type=F32)


def _mod_kernel(c_ref, w_ref, b_ref, o_ref):
    a = _silu(c_ref[...]).astype(BF16)
    o_ref[0] = _dot(a, w_ref[0].astype(BF16)) + b_ref[0]


def _modulation(cvec, w_mod, b_mod):
    depth, d, width = w_mod.shape
    rows = cvec.shape[0]
    tn = 1024
    return pl.pallas_call(
        _mod_kernel,
        grid=(depth, width // tn),
        in_specs=[pl.BlockSpec((rows, d), lambda l, j: (0, 0)),
                  pl.BlockSpec((1, d, tn), lambda l, j: (l, 0, j)),
                  pl.BlockSpec((1, 1, tn), lambda l, j: (l, 0, j))],
        out_specs=pl.BlockSpec((1, rows, tn), lambda l, j: (l, 0, j)),
        out_shape=jax.ShapeDtypeStruct((depth, rows, width), F32),
        compiler_params=_params(2),
        name="modulation",
    )(cvec, w_mod, b_mod.reshape(depth, 1, width))


def _inproj_kernel(x_ref, mod_ref, g_ref, w_ref, o_ref):
    h = _rms(x_ref[0], g_ref[...]) * (1.0 + mod_ref[0, 1:2, :]) + mod_ref[0, 0:1, :]
    o_ref[0] = _dot(h.astype(BF16), w_ref[...]).astype(o_ref.dtype)


def _in_projection(x, mod, g, w, layer):
    bx, r, d = x.shape
    tm, tn = 512, P_WIDTH
    return pl.pallas_call(
        _inproj_kernel,
        grid=(P_WIDTH // tn, bx, r // tm),
        in_specs=[pl.BlockSpec((1, tm, d), lambda j, b, t: (b, t, 0)),
                  pl.BlockSpec((1, N_MOD, d), lambda j, b, t: (b, 0, 0)),
                  pl.BlockSpec((1, d), lambda j, b, t: (0, 0)),
                  pl.BlockSpec((None, d, tn), lambda j, b, t: (layer, 0, j), pipeline_mode=pl.Buffered(1))],
        out_specs=pl.BlockSpec((1, tm, tn), lambda j, b, t: (b, t, j)),
        out_shape=jax.ShapeDtypeStruct((bx, r, P_WIDTH), BF16),
        compiler_params=_params(3),
        name="in_projection",
    )(x, mod, g, w)


def _rope(x, cos, sin, half):
    lane = lax.broadcasted_iota(jnp.int32, x.shape, 1)
    first = (lane % (2 * half)) < half
    width = x.shape[1]
    rot = jnp.where(first, pltpu.roll(x, width - half, 1), pltpu.roll(x, half, 1))
    return x * cos + rot * sin


def _mla_prep_kernel(cq_ref, ckv_ref, misc_ref, cos_ref, sin_ref, gq_ref, gkv_ref, wuq_ref, wukv_ref,
                     qt_ref, k_ref, vt_ref):
    scale = MLA_QK ** -0.5 * LOG2E
    cos, sin = cos_ref[...], sin_ref[...]
    half = MLA_ROPE // 4
    qf = _dot(_rms(cq_ref[0].astype(F32), gq_ref[...]).astype(BF16), wuq_ref[...])
    kvf = _dot(_rms(ckv_ref[0].astype(F32), gkv_ref[...]).astype(BF16), wukv_ref[...])
    kr = _rope(misc_ref[0].astype(F32), cos, sin, half)[:, :MLA_ROPE].astype(BF16)
    nope_w = MLA_HEADS * MLA_NOPE
    for pair in range(MLA_HEADS // 2):
        qr = _rope(qf[:, nope_w + 128 * pair: nope_w + 128 * (pair + 1)], cos, sin, half) * scale
        qrt = qr.astype(BF16).T
        for j in range(2):
            qt_ref[0, 2 * pair + j, MLA_NOPE:MLA_QK, :] = qrt[MLA_ROPE * j: MLA_ROPE * (j + 1), :]
    for h in range(MLA_HEADS):
        qt_ref[0, h, 0:MLA_NOPE, :] = (qf[:, MLA_NOPE * h: MLA_NOPE * (h + 1)] * scale).astype(BF16).T
        k_ref[0, h, :, 0:MLA_NOPE] = kvf[:, MLA_NOPE * h: MLA_NOPE * (h + 1)].astype(BF16)
        k_ref[0, h, :, MLA_NOPE:MLA_QK] = kr
        vt_ref[0, h, 0:MLA_V, :] = kvf[:, nope_w + MLA_V * h: nope_w + MLA_V * (h + 1)].astype(BF16).T
        vt_ref[0, h, MLA_V:, :] = jnp.ones((MLA_VA - MLA_V, vt_ref.shape[3]), BF16)


def _mla_prep(proj, cos, sin, g_q, g_kv, w_uq, w_ukv, layer):
    b, r, _ = proj.shape
    tm = min(r, 1024)
    rk = MLA_RANK
    const = lambda shape: pl.BlockSpec(shape, lambda i, t: (0,) * len(shape))
    stacked = lambda w: pl.BlockSpec((None,) + w.shape[1:], lambda i, t: (layer, 0, 0))
    return pl.pallas_call(
        _mla_prep_kernel,
        grid=(b, r // tm),
        in_specs=[pl.BlockSpec((1, tm, rk), lambda i, t: (i, t, P_CQ // rk)),
                  pl.BlockSpec((1, tm, rk), lambda i, t: (i, t, P_CKV // rk)),
                  pl.BlockSpec((1, tm, 128), lambda i, t: (i, t, P_MISC // 128)),
                  pl.BlockSpec((tm, 128), lambda i, t: (t, 0)),
                  pl.BlockSpec((tm, 128), lambda i, t: (t, 0)),
                  const((1, rk)), const((1, rk)),
                  stacked(w_uq), stacked(w_ukv)],
        out_specs=[pl.BlockSpec((1, MLA_HEADS, MLA_QK, tm), lambda i, t: (i, 0, 0, t)),
                   pl.BlockSpec((1, MLA_HEADS, tm, MLA_QK), lambda i, t: (i, 0, t, 0)),
                   pl.BlockSpec((1, MLA_HEADS, MLA_VA, tm), lambda i, t: (i, 0, 0, t))],
        out_shape=[jax.ShapeDtypeStruct((b, MLA_HEADS, MLA_QK, r), BF16),
                   jax.ShapeDtypeStruct((b, MLA_HEADS, r, MLA_QK), BF16),
                   jax.ShapeDtypeStruct((b, MLA_HEADS, MLA_VA, r), BF16)],
        compiler_params=_params(2),
        name="mla_prep",
    )(proj, proj, proj, cos, sin, g_q, g_kv, w_uq, w_ukv)


def _softmax_pv(s_parts, vt_parts):
    m = functools.reduce(jnp.maximum, [jnp.max(s, axis=0, keepdims=True) for s in s_parts])
    acc = sum(_dot(vt, jnp.exp2(s - m).astype(BF16)) for s, vt in zip(s_parts, vt_parts))
    return (acc[:MLA_V] / acc[MLA_V:MLA_V + 1]).T


def _mla_attn_kernel(qt_ref, kl_ref, kc_ref, vlt_ref, vct_ref, o_ref, s_scr, m_scr, *, tk):
    @pl.when(pl.program_id(0) == 0)
    def _():
        s_scr[...] = jnp.zeros_like(s_scr)
        m_scr[...] = jnp.zeros_like(m_scr)

    n_lat = kl_ref.shape[2] // tk
    qt = qt_ref[0, 0]
    tq = qt.shape[1]
    m_prev = m_scr[...]
    m_new = jnp.full((1, tq), -jnp.inf, F32)
    acc = jnp.zeros((MLA_VA, tq), F32)
    for j in range(n_lat + 1):
        rows = slice(j * tk, (j + 1) * tk)
        p = jnp.exp2(s_scr[rows, :] - m_prev).astype(BF16)
        k = kc_ref[0, 0] if j == n_lat else kl_ref[0, 0, rows, :]
        vt = vct_ref[0, 0] if j == n_lat else vlt_ref[0, 0, :, rows]
        s = _dot(k, qt)
        s_scr[rows, :] = s
        m_new = jnp.maximum(m_new, jnp.max(s, axis=0, keepdims=True))
        acc = acc + _dot(vt, p)
    m_scr[...] = m_new
    o_ref[0] = (acc[:MLA_V] / acc[MLA_V:MLA_V + 1]).T.astype(o_ref.dtype)


def _mla_attention(qt, k_l, vt_l, k_c, vt_c):
    b, h, dq, n = qt.shape
    c = k_c.shape[2]
    tq = min(n, 1024)
    nt = n // tq
    tiles = b * h * nt

    def cur(u):
        t = jnp.minimum(u, tiles - 1)
        return t // (h * nt), (t // nt) % h, t % nt

    def prev(u):
        t = jnp.maximum(u - 1, 0)
        return t // (h * nt), (t // nt) % h, t % nt

    assert n % c == 0
    return pl.pallas_call(
        functools.partial(_mla_attn_kernel, tk=c),
        grid=(tiles + 1,),
        in_specs=[pl.BlockSpec((1, 1, dq, tq), lambda u: (cur(u)[0], cur(u)[1], 0, cur(u)[2])),
                  pl.BlockSpec((1, 1, n, dq), lambda u: (cur(u)[0], cur(u)[1], 0, 0)),
                  pl.BlockSpec((1, 1, c, dq), lambda u: (cur(u)[0], cur(u)[1], 0, 0)),
                  pl.BlockSpec((1, 1, MLA_VA, n), lambda u: (prev(u)[0], prev(u)[1], 0, 0)),
                  pl.BlockSpec((1, 1, MLA_VA, c), lambda u: (prev(u)[0], prev(u)[1], 0, 0))],
        out_specs=pl.BlockSpec((1, None, tq, MLA_V), lambda u: (prev(u)[0], prev(u)[1], prev(u)[2], 0)),
        out_shape=jax.ShapeDtypeStruct((b, h, n, MLA_V), BF16),
        scratch_shapes=[pltpu.VMEM((n + c, tq), F32), pltpu.VMEM((1, tq), F32)],
        compiler_params=_params(1),
        name="mla_attention",
    )(qt, k_l, k_c, vt_l, vt_c)


def _mla_ctx_kernel(qt_ref, k_ref, vt_ref, o_ref):
    o_ref[0] = _softmax_pv([_dot(k_ref[0, 0], qt_ref[0, 0])], [vt_ref[0, 0]]).astype(o_ref.dtype)


def _mla_ctx_attention(qt, k, vt):
    b, h, dq, c = qt.shape
    return pl.pallas_call(
        _mla_ctx_kernel,
        grid=(b, h),
        in_specs=[pl.BlockSpec((1, 1, dq, c), lambda i, j: (i, j, 0, 0)),
                  pl.BlockSpec((1, 1, c, dq), lambda i, j: (i, j, 0, 0)),
                  pl.BlockSpec((1, 1, MLA_VA, c), lambda i, j: (i, j, 0, 0))],
        out_specs=pl.BlockSpec((1, c, MLA_V), lambda i, j: (i, 0, j)),
        out_shape=jax.ShapeDtypeStruct((b, c, h * MLA_V), BF16),
        compiler_params=_params(2),
        name="mla_ctx_attention",
    )(qt, k, vt)


def _swa_ctx_kernel(sink_ref, q_ref, k_ref, v_ref, o_ref):
    q = (q_ref[0].astype(F32) * SWA_DIM ** -0.5).astype(BF16)
    s = _dot_nt(q, k_ref[0].astype(BF16))
    sink = sink_ref[0, pl.program_id(1)]
    m = jnp.maximum(jnp.max(s, axis=-1, keepdims=True), sink)
    p = jnp.exp(s - m)
    l = jnp.sum(p, axis=-1, keepdims=True) + jnp.exp(sink - m)
    o_ref[0] = (_dot(p.astype(BF16), v_ref[0].astype(BF16)) / l).astype(o_ref.dtype)


def _swa_ctx_attention(proj_c, sink):
    b, c, _ = proj_c.shape
    d = SWA_DIM
    return pl.pallas_call(
        _swa_ctx_kernel,
        grid=(b, SWA_HEADS),
        in_specs=[pl.BlockSpec(memory_space=pltpu.SMEM),
                  pl.BlockSpec((1, c, d), lambda i, j: (i, 0, P_SWA_Q // d + j)),
                  pl.BlockSpec((1, c, d), lambda i, j: (i, 0, P_SWA_K // d + j // SWA_GROUP)),
                  pl.BlockSpec((1, c, d), lambda i, j: (i, 0, P_SWA_V // d + j // SWA_GROUP))],
        out_specs=pl.BlockSpec((1, c, d), lambda i, j: (i, 0, j)),
        out_shape=jax.ShapeDtypeStruct((b, c, SWA_HEADS * d), BF16),
        compiler_params=_params(2),
        name="swa_ctx_attention",
    )(sink, proj_c, proj_c, proj_c)


def _swa_window_start(t, qb, nb):
    return pl.multiple_of(jnp.clip(t * qb - 1, 0, nb - (qb + 2)) * SWA_BLOCK, SWA_BLOCK)


def _swa_kernel(sink_ref, q_ref, k_ref, kc_ref, v_ref, vc_ref, cos_ref, sin_ref, o_ref, s_scr, m_scr, *,
                qb, tiles, nt):
    u = pl.program_id(0)

    @pl.when(u == 0)
    def _():
        s_scr[...] = jnp.zeros_like(s_scr)
        m_scr[...] = jnp.zeros_like(m_scr)

    blk, d = SWA_BLOCK, SWA_DIM
    half = d // 4
    nq = qb * blk
    nw = (qb + 2) * blk
    nb = k_ref.shape[1] // blk
    ck = kc_ref.shape[1]
    cur, prev = jnp.minimum(u, tiles - 1), jnp.maximum(u - 1, 0)
    t_cur, t_prev = cur % nt, prev % nt
    q_start = pl.multiple_of(t_cur * nq, blk)
    k_start = _swa_window_start(t_cur, qb, nb)
    v_start = _swa_window_start(t_prev, qb, nb)
    groups = range(SWA_KV_HEADS)

    def sink_row(g):
        return jnp.concatenate(
            [jnp.full((1, nq), sink_ref[0, g * SWA_GROUP + i] * LOG2E, F32) for i in range(SWA_GROUP)], axis=1)

    def values_t(v):
        return jnp.concatenate([v.astype(F32).T.astype(BF16), jnp.ones((16, v.shape[0]), BF16)], axis=0)

    scale = d ** -0.5 * LOG2E
    cos_q, sin_q = cos_ref[pl.ds(q_start, nq), :], sin_ref[pl.ds(q_start, nq), :]
    q = [jnp.concatenate(
        [(_rope(q_ref[0, :, d * h: d * (h + 1)].astype(F32), cos_q, sin_q, half) * scale).astype(BF16)
         for h in range(g * SWA_GROUP, (g + 1) * SWA_GROUP)], axis=0) for g in groups]
    key = lax.broadcasted_iota(jnp.int32, (ck, nq), 0)
    tok = lax.broadcasted_iota(jnp.int32, (ck, nq), 1)
    m_prev = [m_scr[g] for g in groups]
    m_new = [sink_row(g) for g in groups]
    acc = [jnp.zeros((d + 16, SWA_GROUP * nq), F32) for g in groups]
    n_win = nw // ck
    for j in range(n_win + 1):
        rows = slice(j * ck, (j + 1) * ck)
        if j < n_win:
            kr = pl.ds(k_start + j * ck, ck)
            cos_k, sin_k = cos_ref[kr, :], sin_ref[kr, :]
            dist = key - tok + (k_start + j * ck - q_start)
            valid = (dist <= blk) & (dist >= -blk)
        for g in groups:
            cols = slice(d * g, d * (g + 1))
            p = jnp.exp2(s_scr[g, rows, :] - m_prev[g]).astype(BF16)
            if j < n_win:
                vt = values_t(v_ref[0, pl.ds(v_start + j * ck, ck), cols])
                k = _rope(k_ref[0, kr, cols].astype(F32), cos_k, sin_k, half).astype(BF16)
                s = _dot_nt(k, q[g])
                s = jnp.concatenate(
                    [jnp.where(valid, s[:, nq * i: nq * (i + 1)], -jnp.inf) for i in range(SWA_GROUP)], axis=1)
            else:
                vt = values_t(vc_ref[0, :, cols])
                s = _dot_nt(kc_ref[0, :, cols].astype(BF16), q[g])
            s_scr[g, rows, :] = s
            m_new[g] = jnp.maximum(m_new[g], jnp.max(s, axis=0, keepdims=True))
            acc[g] = acc[g] + _dot(vt, p)
    for g in groups:
        m_scr[g] = m_new[g]
        out_t = acc[g][:d] / (acc[g][d:d + 1] + jnp.exp2(sink_row(g) - m_prev[g]))
        for i in range(SWA_GROUP):
            h = g * SWA_GROUP + i
            o_ref[0, :, d * h: d * (h + 1)] = out_t[:, nq * i: nq * (i + 1)].T.astype(o_ref.dtype)


def _swa_attention(proj_l, proj_c, cos, sin, sink):
    b, n, _ = proj_l.shape
    c = proj_c.shape[1]
    blk, d = SWA_BLOCK, SWA_DIM
    qb = 2
    nq = qb * blk
    nt = n // nq
    tiles = b * nt
    assert ((qb + 2) * blk) % c == 0 and n // blk >= qb + 2
    cur = lambda u: jnp.minimum(u, tiles - 1)
    prev = lambda u: jnp.maximum(u - 1, 0)
    qw, kw = SWA_HEADS * d, SWA_KV_HEADS * d
    return pl.pallas_call(
        functools.partial(_swa_kernel, qb=qb, tiles=tiles, nt=nt),
        grid=(tiles + 1,),
        in_specs=[pl.BlockSpec(memory_space=pltpu.SMEM),
                  pl.BlockSpec((1, nq, qw), lambda u: (cur(u) // nt, cur(u) % nt, P_SWA_Q // qw)),
                  pl.BlockSpec((1, n, kw), lambda u: (cur(u) // nt, 0, P_SWA_K // kw)),
                  pl.BlockSpec((1, c, kw), lambda u: (cur(u) // nt, 0, P_SWA_K // kw)),
                  pl.BlockSpec((1, n, kw), lambda u: (prev(u) // nt, 0, P_SWA_V // kw)),
                  pl.BlockSpec((1, c, kw), lambda u: (prev(u) // nt, 0, P_SWA_V // kw)),
                  pl.BlockSpec((n, d), lambda u: (0, 0)),
                  pl.BlockSpec((n, d), lambda u: (0, 0))],
        out_specs=pl.BlockSpec((1, nq, qw), lambda u: (prev(u) // nt, prev(u) % nt, 0)),
        out_shape=jax.ShapeDtypeStruct((b, n, qw), BF16),
        scratch_shapes=[pltpu.VMEM((SWA_KV_HEADS, (qb + 2) * blk + c, SWA_GROUP * nq), F32),
                        pltpu.VMEM((SWA_KV_HEADS, 1, SWA_GROUP * nq), F32)],
        compiler_params=_params(1),
        name="swa_attention",
    )(sink, proj_l, proj_l, proj_c, proj_l, proj_c, cos, sin)


def _log_sigmoid(x):
    return jnp.minimum(x, 0.0) - jnp.log(1.0 + jnp.exp(-jnp.abs(x)))


def _gla_block_constants(rows):
    t = np.arange(rows)
    ti, tj = t[:, None], t[None, :]
    ci, cj = ti // GLA_CHUNK, tj // GLA_CHUNK
    ones, sel = [], []
    for reverse in (False, True):
        upto = (tj >= ti) if reverse else (tj <= ti)
        visible = (tj > ti) if reverse else (tj <= ti)
        earlier = (cj > ci) if reverse else (cj < ci)
        ones.append((ci == cj) & upto)
        sel.append(np.where((ci == cj) & visible, 1, np.where(earlier, 2, 0)))
    return jnp.asarray(np.stack(ones), BF16), jnp.asarray(np.stack(sel), jnp.int32)


def _gla_factors(q_ref, k_ref, misc_ref, i, wg, bg, ones, sel, reverse):
    rows = q_ref.shape[1]
    L = GLA_CHUNK
    assert rows == 4 * L
    width = GLA_HEADS * GLA_DK
    la = _log_sigmoid(_dot(misc_ref[i].astype(BF16), wg) + bg) * (1.0 / GLA_TAU)
    hi = la.astype(BF16)
    lo = (la - hi.astype(F32)).astype(BF16)
    sums = _dot(ones, jnp.concatenate([hi, lo], axis=1))
    b_in = sums[:, :width] + sums[:, width:]
    chunks = [b_in[c * L:(c + 1) * L] for c in range(rows // L)]
    order = list(reversed(range(len(chunks)))) if reverse else list(range(len(chunks)))
    offset = None
    for c in order:
        total = chunks[c][0:1] if reverse else chunks[c][L - 1:L]
        if offset is not None:
            chunks[c] = chunks[c] + offset
        offset = total if offset is None else offset + total
    b_abs = jnp.concatenate(chunks, axis=0)
    if reverse:
        g_mid, g_end = b_abs[rows // 2:rows // 2 + 1], b_abs[0:1]
    else:
        g_mid, g_end = b_abs[rows // 2 - 1:rows // 2], b_abs[rows - 1:rows]
    diag, cross = sel == 1, sel == 2
    q = q_ref[i].astype(F32) * GLA_DK ** -0.5
    k = k_ref[i].astype(F32)
    q_d, k_d = q * jnp.exp(b_in), k * jnp.exp(-b_in)
    q_x, k_x = q * jnp.exp(b_abs - g_mid), k * jnp.exp(g_mid - b_abs)
    q_s, k_s = q * jnp.exp(b_abs), k * jnp.exp(g_end - b_abs)
    decay = jnp.exp(g_end)
    return q_d, k_d, q_x, k_x, q_s, k_s, decay, diag, cross


def _gla_apply(factors, v_ref, state_ref, o_ref, i):
    q_d, k_d, q_x, k_x, q_s, k_s, decay, diag, cross = factors
    rows = q_d.shape[0]
    lane = lax.broadcasted_iota(jnp.int32, (rows, 2 * GLA_DK), 1)
    lane_s = lax.broadcasted_iota(jnp.int32, (GLA_DV, 2 * GLA_DK), 1)
    for pair in range(GLA_HEADS // 2):
        ps = slice(2 * GLA_DK * pair, 2 * GLA_DK * (pair + 1))
        kd_p, kx_p, ks_p = k_d[:, ps].astype(BF16), k_x[:, ps].astype(BF16), k_s[:, ps].astype(BF16)
        st = state_ref[i, pair]
        st_b = st.astype(BF16)
        ds = []
        for j in range(2):
            h = 2 * pair + j
            vs = slice(h * GLA_DV, (h + 1) * GLA_DV)
            mine = (lane // GLA_DK) == j
            qd_h = jnp.where(mine, q_d[:, ps], 0.0).astype(BF16)
            qx_h = jnp.where(mine, q_x[:, ps], 0.0).astype(BF16)
            qs_h = jnp.where(mine, q_s[:, ps], 0.0).astype(BF16)
            vh = v_ref[i, :, vs].astype(BF16)
            a = jnp.where(diag, _dot_nt(qd_h, kd_p), jnp.where(cross, _dot_nt(qx_h, kx_p), 0.0))
            o_ref[i, :, vs] = (_dot(a.astype(BF16), vh) + _dot_nt(qs_h, st_b)).astype(o_ref.dtype)
            ds.append(_dot_tn(vh, ks_p))
        state_ref[i, pair] = decay[:, ps] * st + jnp.where(lane_s < GLA_DK, ds[0], ds[1])


def _gla_kernel(qf_ref, kf_ref, vf_ref, mf_ref, qb_ref, kb_ref, vb_ref, mb_ref, wg_ref, bg_ref, ones_ref, sel_ref,
                sf0_ref, sb0_ref, of_ref, ob_ref, sf_ref, sb_ref, stf, stb):
    t = pl.program_id(1)

    @pl.when(t == 0)
    def _():
        stf[...] = sf0_ref[...]
        stb[...] = sb0_ref[...]

    width = GLA_HEADS * GLA_DK
    entries = range(qf_ref.shape[0])
    fwd = [_gla_factors(qf_ref, kf_ref, mf_ref, i, wg_ref[:, :width], bg_ref[:, :width], ones_ref[0], sel_ref[0],
                        False) for i in entries]
    bwd = [_gla_factors(qb_ref, kb_ref, mb_ref, i, wg_ref[:, width:], bg_ref[:, width:], ones_ref[1], sel_ref[1],
                        True) for i in entries]
    for i in entries:
        _gla_apply(fwd[i], vf_ref, stf, of_ref, i)
        _gla_apply(bwd[i], vb_ref, stb, ob_ref, i)

    @pl.when(t == pl.num_programs(1) - 1)
    def _():
        sf_ref[...] = stf[...]
        sb_ref[...] = stb[...]


def _gla_scan(proj, wg, bg, sf0, sb0, layer):
    b, r, _ = proj.shape
    tm = 256
    nt = r // tm
    kw, vw = GLA_HEADS * GLA_DK, GLA_HEADS * GLA_DV
    fwd = lambda t: t
    bwd = lambda t: nt - 1 - t

    nb = next(k for k in (4, 2, 1) if b % k == 0)
    ones, sel = _gla_block_constants(tm)

    def specs(order):
        return [pl.BlockSpec((nb, tm, kw), lambda i, t: (i, order(t), P_GLA_Q // kw)),
                pl.BlockSpec((nb, tm, kw), lambda i, t: (i, order(t), P_GLA_K // kw)),
                pl.BlockSpec((nb, tm, vw), lambda i, t: (i, order(t), P_GLA_V // vw)),
                pl.BlockSpec((nb, tm, 128), lambda i, t: (i, order(t), P_MISC // 128))]

    state_spec = pl.BlockSpec((nb,) + GLA_STATE, lambda i, t: (i, 0, 0, 0))
    state_shape = jax.ShapeDtypeStruct((b,) + GLA_STATE, F32)
    return pl.pallas_call(
        _gla_kernel,
        grid=(b // nb, nt),
        in_specs=specs(fwd) + specs(bwd) + [
            pl.BlockSpec((None,) + wg.shape[1:], lambda i, t: (layer, 0, 0)),
            pl.BlockSpec((None,) + bg.shape[1:], lambda i, t: (layer, 0, 0)),
            pl.BlockSpec(ones.shape, lambda i, t: (0, 0, 0)),
            pl.BlockSpec(sel.shape, lambda i, t: (0, 0, 0)),
            state_spec, state_spec],
        out_specs=[pl.BlockSpec((nb, tm, vw), lambda i, t: (i, fwd(t), 0)),
                   pl.BlockSpec((nb, tm, vw), lambda i, t: (i, bwd(t), 0)),
                   state_spec, state_spec],
        out_shape=[jax.ShapeDtypeStruct((b, r, vw), BF16), jax.ShapeDtypeStruct((b, r, vw), BF16),
                   state_shape, state_shape],
        scratch_shapes=[pltpu.VMEM((nb,) + GLA_STATE, F32), pltpu.VMEM((nb,) + GLA_STATE, F32)],
        compiler_params=_params(2),
        name="gla_scan",
    )(proj, proj, proj, proj, proj, proj, proj, proj, wg, bg, ones, sel, sf0, sb0)


def _outproj_kernel(x_ref, mod_ref, mla_ref, swa_ref, of_ref, ob_ref, r_ref, g_ref, w_ref, o_ref):
    o = of_ref[0].astype(F32) + ob_ref[0].astype(F32)
    gate = _silu(r_ref[0].astype(F32))
    parts = []
    for h in range(GLA_HEADS):
        vs = slice(h * GLA_DV, (h + 1) * GLA_DV)
        parts.append((_rms(o[:, vs], g_ref[:, vs]) * gate[:, vs]).astype(BF16))
    mla = [mla_ref[0]] if len(mla_ref.shape) == 3 else [mla_ref[0, h] for h in range(mla_ref.shape[1])]
    mix = _dot(jnp.concatenate(mla + [swa_ref[0]] + parts, axis=1), w_ref[...])
    o_ref[0] = x_ref[0] + mod_ref[0, 2:3, :] * mix


def _out_projection(x, mod, mla, swa, o_f, o_b, proj, g_out, w_out, layer):
    bx, r, d = x.shape
    tm = 512
    row = lambda width: pl.BlockSpec((1, tm, width), lambda b, t: (b, t, 0))
    gw = GLA_HEADS * GLA_DV
    return pl.pallas_call(
        _outproj_kernel,
        grid=(bx, r // tm),
        in_specs=[row(d),
                  pl.BlockSpec((1, N_MOD, d), lambda b, t: (b, 0, 0)),
                  row(mla.shape[2]) if mla.ndim == 3 else
                  pl.BlockSpec((1, mla.shape[1], tm, mla.shape[3]), lambda b, t: (b, 0, t, 0)),
                  row(swa.shape[2]), row(gw), row(gw),
                  pl.BlockSpec((1, tm, gw), lambda b, t: (b, t, P_GLA_R // gw)),
                  pl.BlockSpec((1, gw), lambda b, t: (0, 0)),
                  pl.BlockSpec((None,) + w_out.shape[1:], lambda b, t: (layer, 0, 0))],
        out_specs=row(d),
        out_shape=jax.ShapeDtypeStruct(x.shape, F32),
        compiler_params=_params(2),
        name="out_projection",
    )(x, mod, mla, swa, o_f, o_b, proj, g_out, w_out)


def _ffn_kernel(x_ref, mod_ref, g_ref, wg_ref, wu_ref, wd_ref, gf_ref, o_ref, h_ref, *, final_norm):
    f = pl.program_id(2)
    row_chunks = [slice(r, r + ROW_CHUNK) for r in range(0, x_ref.shape[1], ROW_CHUNK)]
    chunk = (ROW_CHUNK, x_ref.shape[2])

    def gated_partial():
        h = h_ref[...]
        act = _silu(_dot(h, wg_ref[...])) * _dot(h, wu_ref[...])
        return mod_ref[0, 5:6, :] * _dot(act.astype(BF16), wd_ref[...])

    @pl.when(f == 0)
    def _():
        gain = jnp.broadcast_to(g_ref[...] * (1.0 + mod_ref[0, 4:5, :]), chunk)
        shift = jnp.broadcast_to(mod_ref[0, 3:4, :], chunk)
        for rows in row_chunks:
            x = x_ref[0, rows, :]
            rs = lax.rsqrt(jnp.mean(x * x, axis=-1, keepdims=True) + EPS)
            h_ref[rows, :] = (x * rs * gain + shift).astype(BF16)
        o_ref[0] = x_ref[0] + gated_partial()

    @pl.when(f > 0)
    def _():
        o_ref[0] += gated_partial()

    if final_norm:
        @pl.when(f == pl.num_programs(2) - 1)
        def _():
            g_fin = jnp.broadcast_to(gf_ref[...], chunk)
            for rows in row_chunks:
                y = o_ref[0, rows, :]
                o_ref[0, rows, :] = y * lax.rsqrt(jnp.mean(y * y, axis=-1, keepdims=True) + EPS) * g_fin


def _ffn(x, mod, g, w_gu, w_down, g_final, final_norm, layer):
    bx, r, d = x.shape
    hidden = w_down.shape[1]
    tm, tf = min(r, 1024), 512
    nf = hidden // tf
    return pl.pallas_call(
        functools.partial(_ffn_kernel, final_norm=final_norm),
        grid=(bx, r // tm, nf),
        in_specs=[pl.BlockSpec((1, tm, d), lambda b, t, f: (b, t, 0)),
                  pl.BlockSpec((1, N_MOD, d), lambda b, t, f: (b, 0, 0)),
                  pl.BlockSpec((1, d), lambda b, t, f: (0, 0)),
                  pl.BlockSpec((None, d, tf), lambda b, t, f: (layer, 0, f)),
                  pl.BlockSpec((None, d, tf), lambda b, t, f: (layer, 0, nf + f)),
                  pl.BlockSpec((None, tf, d), lambda b, t, f: (layer, f, 0)),
                  pl.BlockSpec((1, d), lambda b, t, f: (0, 0))],
        out_specs=pl.BlockSpec((1, tm, d), lambda b, t, f: (b, t, 0)),
        out_shape=jax.ShapeDtypeStruct(x.shape, F32),
        scratch_shapes=[pltpu.VMEM((tm, d), BF16)],
        compiler_params=_params(3),
        name="ffn",
    )(x, mod, g, w_gu, w_gu, w_down, g_final)


def _rope_tables(n, dim):
    half = dim // 4
    freqs = ROPE_THETA ** (-jnp.arange(half, dtype=F32) / half)
    pos = jnp.arange(n, dtype=jnp.int32)
    ang_r = (pos // GRID_W).astype(F32)[:, None] * freqs
    ang_c = (pos % GRID_W).astype(F32)[:, None] * freqs
    cos = jnp.concatenate([jnp.cos(ang_r)] * 2 + [jnp.cos(ang_c)] * 2, axis=1)
    sin = jnp.concatenate([-jnp.sin(ang_r), jnp.sin(ang_r), -jnp.sin(ang_c), jnp.sin(ang_c)], axis=1)
    reps = 128 // dim
    return jnp.tile(cos, (1, reps)), jnp.tile(sin, (1, reps))


def _cast_kernel(w_ref, o_ref):
    o_ref[...] = w_ref[...].astype(o_ref.dtype)


def _to_bf16(w):
    depth, rows, cols = w.shape
    tr = max(16, min(rows, (8 * 1024 * 1024) // (4 * cols) // 16 * 16))
    while rows % tr:
        tr -= 16
    return pl.pallas_call(
        _cast_kernel,
        grid=(depth, rows // tr),
        in_specs=[pl.BlockSpec((None, tr, cols), lambda l, t: (l, t, 0))],
        out_specs=pl.BlockSpec((None, tr, cols), lambda l, t: (l, t, 0)),
        out_shape=jax.ShapeDtypeStruct(w.shape, BF16),
        compiler_params=_params(2),
        name="weight_cast",
    )(w)


def _relayout_w_in(w_in):
    depth, d, _ = w_in.shape
    sizes = (MLA_RANK, MLA_RANK, MLA_ROPE, SWA_HEADS * SWA_DIM, SWA_KV_HEADS * SWA_DIM, SWA_KV_HEADS * SWA_DIM,
             GLA_HEADS * GLA_DK, GLA_HEADS * GLA_DK, GLA_HEADS * GLA_DV, 2 * GLA_GATE_RANK, GLA_HEADS * GLA_DV)
    offs = [0]
    for s in sizes:
        offs.append(offs[-1] + s)
    runs = [(3, 5), (0, 2), (5, 6), (2, 3), (9, 10), P_GLA_Q - P_MISC - MLA_ROPE - 2 * GLA_GATE_RANK, (6, 9), (10, 11)]

    def body(w_ref, o_ref):
        dst = 0
        for run in runs:
            if isinstance(run, int):
                o_ref[:, dst:dst + run] = jnp.zeros((o_ref.shape[0], run), o_ref.dtype)
                dst += run
            else:
                src, width = offs[run[0]], offs[run[1]] - offs[run[0]]
                o_ref[:, dst:dst + width] = w_ref[:, src:src + width].astype(o_ref.dtype)
                dst += width
        assert dst == P_WIDTH

    tr = 256
    return pl.pallas_call(
        body,
        grid=(depth, d // tr),
        in_specs=[pl.BlockSpec((None, tr, w_in.shape[2]), lambda l, t: (l, t, 0))],
        out_specs=pl.BlockSpec((None, tr, P_WIDTH), lambda l, t: (l, t, 0)),
        out_shape=jax.ShapeDtypeStruct((depth, d, P_WIDTH), BF16),
        compiler_params=_params(2),
        name="w_in_relayout",
    )(w_in)


def _relayout_mla(w_uq, w_ukv):
    depth, rk, _ = w_uq.shape
    uq = w_uq.reshape(depth, rk, MLA_HEADS, MLA_QK)
    uq = jnp.concatenate([uq[..., :MLA_NOPE].reshape(depth, rk, -1), uq[..., MLA_NOPE:].reshape(depth, rk, -1)], -1)
    ukv = w_ukv.reshape(depth, rk, MLA_HEADS, MLA_NOPE + MLA_V)
    ukv = jnp.concatenate([ukv[..., :MLA_NOPE].reshape(depth, rk, -1), ukv[..., MLA_NOPE:].reshape(depth, rk, -1)], -1)
    return uq.astype(BF16), ukv.astype(BF16)


def _relayout_gate(w_f, b_f, w_b, b_b):
    depth, rank, width = w_f.shape
    wg = jnp.zeros((depth, 128, 2 * width), F32)
    wg = wg.at[:, MISC_GATE_OFF:MISC_GATE_OFF + rank, :width].set(w_f)
    wg = wg.at[:, MISC_GATE_OFF + rank:MISC_GATE_OFF + 2 * rank, width:].set(w_b)
    bg = jnp.concatenate([b_f, b_b], axis=-1).reshape(depth, 1, 2 * width)
    return wg.astype(BF16), bg


def kernel(x, c, ctx, c_ctx, w_mod, b_mod, g_mix, g_ffn, w_in, g_mla_q, g_mla_kv, w_mla_uq, w_mla_ukv,
           swa_sink, w_gla_gate_f, b_gla_gate_f, w_gla_gate_b, b_gla_gate_b, g_gla_out, w_out, w_ffn_gu,
           w_ffn_down, g_final):
    B, N, D = x.shape
    C = ctx.shape[1]
    depth = w_mod.shape[0]

    cvec = jnp.concatenate([c, c_ctx[None, :], jnp.zeros((16 - B - 1, D), F32)], axis=0)
    mod = _modulation(cvec, w_mod, b_mod).reshape(depth, 16, N_MOD, D)

    w_in_p = _relayout_w_in(w_in)
    w_uq_p, w_ukv_p = _relayout_mla(w_mla_uq, w_mla_ukv)
    wg_p, bg_p = _relayout_gate(w_gla_gate_f, b_gla_gate_f, w_gla_gate_b, b_gla_gate_b)
    w_out_b = _to_bf16(w_out)
    w_gu_b = _to_bf16(w_ffn_gu)
    w_down_b = _to_bf16(w_ffn_down)

    cos_m, sin_m = _rope_tables(N, MLA_ROPE)
    cos_s, sin_s = _rope_tables(N, SWA_DIM)
    cos_id, sin_id = jnp.ones((C, 128), F32), jnp.zeros((C, 128), F32)
    state0 = jnp.zeros((B,) + GLA_STATE, F32)
    g_fin = g_final.reshape(1, D)

    xc = ctx.reshape(1, B * C, D)
    for l in range(depth):
        last = l == depth - 1
        mod_l, mod_c = mod[l, :B], mod[l, B:B + 1]
        g_mix_l, g_ffn_l = g_mix[l].reshape(1, D), g_ffn[l].reshape(1, D)
        g_q, g_kv = g_mla_q[l].reshape(1, -1), g_mla_kv[l].reshape(1, -1)
        g_out = g_gla_out[l].reshape(1, -1)
        sink = swa_sink[l].reshape(1, -1)

        proj_l = _in_projection(x, mod_l, g_mix_l, w_in_p, l)
        proj_c = _in_projection(xc, mod_c, g_mix_l, w_in_p, l).reshape(B, C, P_WIDTH)

        q_l, k_l, v_l = _mla_prep(proj_l, cos_m, sin_m, g_q, g_kv, w_uq_p, w_ukv_p, l)
        q_c, k_c, v_c = _mla_prep(proj_c, cos_id, sin_id, g_q, g_kv, w_uq_p, w_ukv_p, l)
        mla_l = _mla_attention(q_l, k_l, v_l, k_c, v_c)
        swa_l = _swa_attention(proj_l, proj_c, cos_s, sin_s, sink)

        of_c, ob_c, s_f, s_b = _gla_scan(proj_c, wg_p, bg_p, state0, state0, l)
        of_l, ob_l, _, _ = _gla_scan(proj_l, wg_p, bg_p, s_f, s_b, l)

        x = _out_projection(x, mod_l, mla_l, swa_l, of_l, ob_l, proj_l, g_out, w_out_b, l)
        x = _ffn(x, mod_l, g_ffn_l, w_gu_b, w_down_b, g_fin, last, l)

        if not last:
            mla_c = _mla_ctx_attention(q_c, k_c, v_c)
            swa_c = _swa_ctx_attention(proj_c, sink)
            flat = lambda a: a.reshape(1, B * C, a.shape[-1])
            xc = _out_projection(xc, mod_c, flat(mla_c), flat(swa_c), flat(of_c), flat(ob_c), flat(proj_c),
                                 g_out, w_out_b, l)
            xc = _ffn(xc, mod_c, g_ffn_l, w_gu_b, w_down_b, g_fin, False, l)
    return x
```

```python
import functools

import jax
import jax.numpy as jnp
import numpy as np
from jax import lax
from jax.experimental import pallas as pl
from jax.experimental.pallas import tpu as pltpu

F32 = jnp.float32
BF16 = jnp.bfloat16

GRID_W = 64
EPS = 1e-6
ROPE_THETA = 10000.0
LOG2E = 1.4426950408889634

MLA_HEADS = 6
MLA_RANK = 512
MLA_NOPE = 128
MLA_ROPE = 64
MLA_V = 128
MLA_QK = MLA_NOPE + MLA_ROPE
MLA_VA = MLA_V + 16

SWA_HEADS = 6
SWA_KV_HEADS = 2
SWA_GROUP = SWA_HEADS // SWA_KV_HEADS
SWA_DIM = 128
SWA_BLOCK = 128

GLA_HEADS = 4
GLA_DK = 64
GLA_DV = 128
GLA_GATE_RANK = 16
GLA_TAU = 16.0
GLA_CHUNK = 64
GLA_STATE = (GLA_HEADS // 2, GLA_DV, 2 * GLA_DK)

N_MOD = 6

P_SWA_Q = 0
P_SWA_K = 768
P_CQ = 1024
P_CKV = 1536
P_SWA_V = 2048
P_MISC = 2304
P_GLA_Q = 2560
P_GLA_K = 2816
P_GLA_V = 3072
P_GLA_R = 3584
P_WIDTH = 4096
MISC_GATE_OFF = MLA_ROPE

VMEM_LIMIT = 56 * 1024 * 1024
ROW_CHUNK = 16


def _params(n_axes, flags=None):
    return pltpu.CompilerParams(dimension_semantics=("arbitrary",) * n_axes,
                                vmem_limit_bytes=VMEM_LIMIT, flags=flags)


def _silu(x):
    return x / (1.0 + jnp.exp(-x))


def _rms(x, g):
    ms = jnp.mean(x * x, axis=-1, keepdims=True)
    return x * lax.rsqrt(ms + EPS) * g


def _dot(a, b):
    return jnp.dot(a, b, preferred_element_type=F32)


def _dot_nt(a, b):
    return lax.dot_general(a, b, (((1,), (1,)), ((), ())), preferred_element_type=F32)


def _dot_tn(a, b):
    return lax.dot_general(a, b, (((0,), (0,)), ((), ())), preferred_element_type=F32)


def _mod_kernel(c_ref, w_ref, b_ref, o_ref):
    a = _silu(c_ref[...]).astype(BF16)
    o_ref[0] = _dot(a, w_ref[0].astype(BF16)) + b_ref[0]


def _modulation(cvec, w_mod, b_mod):
    depth, d, width = w_mod.shape
    rows = cvec.shape[0]
    tn = 1024
    return pl.pallas_call(
        _mod_kernel,
        grid=(depth, width // tn),
        in_specs=[pl.BlockSpec((rows, d), lambda l, j: (0, 0)),
                  pl.BlockSpec((1, d, tn), lambda l, j: (l, 0, j)),
                  pl.BlockSpec((1, 1, tn), lambda l, j: (l, 0, j))],
        out_specs=pl.BlockSpec((1, rows, tn), lambda l, j: (l, 0, j)),
        out_shape=jax.ShapeDtypeStruct((depth, rows, width), F32),
        compiler_params=_params(2),
        name="modulation",
    )(cvec, w_mod, b_mod.reshape(depth, 1, width))


def _inproj_kernel(x_ref, mod_ref, g_ref, w_ref, o_ref):
    h = _rms(x_ref[0], g_ref[...]) * (1.0 + mod_ref[0, 1:2, :]) + mod_ref[0, 0:1, :]
    o_ref[0] = _dot(h.astype(BF16), w_ref[...]).astype(o_ref.dtype)


def _in_projection(x, mod, g, w, layer):
    bx, r, d = x.shape
    tm, tn = 512, P_WIDTH
    return pl.pallas_call(
        _inproj_kernel,
        grid=(P_WIDTH // tn, bx, r // tm),
        in_specs=[pl.BlockSpec((1, tm, d), lambda j, b, t: (b, t, 0)),
                  pl.BlockSpec((1, N_MOD, d), lambda j, b, t: (b, 0, 0)),
                  pl.BlockSpec((1, d), lambda j, b, t: (0, 0)),
                  pl.BlockSpec((None, d, tn), lambda j, b, t: (layer, 0, j), pipeline_mode=pl.Buffered(1))],
        out_specs=pl.BlockSpec((1, tm, tn), lambda j, b, t: (b, t, j)),
        out_shape=jax.ShapeDtypeStruct((bx, r, P_WIDTH), BF16),
        compiler_params=_params(3),
        name="in_projection",
    )(x, mod, g, w)


def _rope(x, cos, sin, half):
    lane = lax.broadcasted_iota(jnp.int32, x.shape, 1)
    first = (lane % (2 * half)) < half
    width = x.shape[1]
    rot = jnp.where(first, pltpu.roll(x, width - half, 1), pltpu.roll(x, half, 1))
    return x * cos + rot * sin


def _mla_prep_kernel(cq_ref, ckv_ref, misc_ref, cos_ref, sin_ref, gq_ref, gkv_ref, wuq_ref, wukv_ref,
                     qt_ref, k_ref, vt_ref):
    scale = MLA_QK ** -0.5 * LOG2E
    cos, sin = cos_ref[...], sin_ref[...]
    half = MLA_ROPE // 4
    qf = _dot(_rms(cq_ref[0].astype(F32), gq_ref[...]).astype(BF16), wuq_ref[...])
    kvf = _dot(_rms(ckv_ref[0].astype(F32), gkv_ref[...]).astype(BF16), wukv_ref[...])
    kr = _rope(misc_ref[0].astype(F32), cos, sin, half)[:, :MLA_ROPE].astype(BF16)
    nope_w = MLA_HEADS * MLA_NOPE
    for pair in range(MLA_HEADS // 2):
        qr = _rope(qf[:, nope_w + 128 * pair: nope_w + 128 * (pair + 1)], cos, sin, half) * scale
        qrt = qr.astype(BF16).T
        for j in range(2):
            qt_ref[0, 2 * pair + j, MLA_NOPE:MLA_QK, :] = qrt[MLA_ROPE * j: MLA_ROPE * (j + 1), :]
    for h in range(MLA_HEADS):
        qt_ref[0, h, 0:MLA_NOPE, :] = (qf[:, MLA_NOPE * h: MLA_NOPE * (h + 1)] * scale).astype(BF16).T
        k_ref[0, h, :, 0:MLA_NOPE] = kvf[:, MLA_NOPE * h: MLA_NOPE * (h + 1)].astype(BF16)
        k_ref[0, h, :, MLA_NOPE:MLA_QK] = kr
        vt_ref[0, h, 0:MLA_V, :] = kvf[:, nope_w + MLA_V * h: nope_w + MLA_V * (h + 1)].astype(BF16).T
        vt_ref[0, h, MLA_V:, :] = jnp.ones((MLA_VA - MLA_V, vt_ref.shape[3]), BF16)


def _mla_prep(proj, cos, sin, g_q, g_kv, w_uq, w_ukv, layer):
    b, r, _ = proj.shape
    tm = min(r, 1024)
    rk = MLA_RANK
    const = lambda shape: pl.BlockSpec(shape, lambda i, t: (0,) * len(shape))
    stacked = lambda w: pl.BlockSpec((None,) + w.shape[1:], lambda i, t: (layer, 0, 0))
    return pl.pallas_call(
        _mla_prep_kernel,
        grid=(b, r // tm),
        in_specs=[pl.BlockSpec((1, tm, rk), lambda i, t: (i, t, P_CQ // rk)),
                  pl.BlockSpec((1, tm, rk), lambda i, t: (i, t, P_CKV // rk)),
                  pl.BlockSpec((1, tm, 128), lambda i, t: (i, t, P_MISC // 128)),
                  pl.BlockSpec((tm, 128), lambda i, t: (t, 0)),
                  pl.BlockSpec((tm, 128), lambda i, t: (t, 0)),
                  const((1, rk)), const((1, rk)),
                  stacked(w_uq), stacked(w_ukv)],
        out_specs=[pl.BlockSpec((1, MLA_HEADS, MLA_QK, tm), lambda i, t: (i, 0, 0, t)),
                   pl.BlockSpec((1, MLA_HEADS, tm, MLA_QK), lambda i, t: (i, 0, t, 0)),
                   pl.BlockSpec((1, MLA_HEADS, MLA_VA, tm), lambda i, t: (i, 0, 0, t))],
        out_shape=[jax.ShapeDtypeStruct((b, MLA_HEADS, MLA_QK, r), BF16),
                   jax.ShapeDtypeStruct((b, MLA_HEADS, r, MLA_QK), BF16),
                   jax.ShapeDtypeStruct((b, MLA_HEADS, MLA_VA, r), BF16)],
        compiler_params=_params(2),
        name="mla_prep",
    )(proj, proj, proj, cos, sin, g_q, g_kv, w_uq, w_ukv)


def _softmax_pv(s_parts, vt_parts):
    m = functools.reduce(jnp.maximum, [jnp.max(s, axis=0, keepdims=True) for s in s_parts])
    acc = sum(_dot(vt, jnp.exp2(s - m).astype(BF16)) for s, vt in zip(s_parts, vt_parts))
    return (acc[:MLA_V] / acc[MLA_V:MLA_V + 1]).T


def _mla_attn_kernel(qt_ref, kl_ref, kc_ref, vlt_ref, vct_ref, o_ref, s_scr, m_scr, *, tk):
    @pl.when(pl.program_id(0) == 0)
    def _():
        s_scr[...] = jnp.zeros_like(s_scr)
        m_scr[...] = jnp.zeros_like(m_scr)

    n_lat = kl_ref.shape[2] // tk
    qt = qt_ref[0, 0]
    tq = qt.shape[1]
    m_prev = m_scr[...]
    m_new = jnp.full((1, tq), -jnp.inf, F32)
    acc = jnp.zeros((MLA_VA, tq), F32)
    for j in range(n_lat + 1):
        rows = slice(j * tk, (j + 1) * tk)
        p = jnp.exp2(s_scr[rows, :] - m_prev).astype(BF16)
        k = kc_ref[0, 0] if j == n_lat else kl_ref[0, 0, rows, :]
        vt = vct_ref[0, 0] if j == n_lat else vlt_ref[0, 0, :, rows]
        s = _dot(k, qt)
        s_scr[rows, :] = s
        m_new = jnp.maximum(m_new, jnp.max(s, axis=0, keepdims=True))
        acc = acc + _dot(vt, p)
    m_scr[...] = m_new
    o_ref[0] = (acc[:MLA_V] / acc[MLA_V:MLA_V + 1]).T.astype(o_ref.dtype)


def _mla_attention(qt, k_l, vt_l, k_c, vt_c):
    b, h, dq, n = qt.shape
    c = k_c.shape[2]
    tq = min(n, 1024)
    nt = n // tq
    tiles = b * h * nt

    def cur(u):
        t = jnp.minimum(u, tiles - 1)
        return t // (h * nt), (t // nt) % h, t % nt

    def prev(u):
        t = jnp.maximum(u - 1, 0)
        return t // (h * nt), (t // nt) % h, t % nt

    assert n % c == 0
    return pl.pallas_call(
        functools.partial(_mla_attn_kernel, tk=c),
        grid=(tiles + 1,),
        in_specs=[pl.BlockSpec((1, 1, dq, tq), lambda u: (cur(u)[0], cur(u)[1], 0, cur(u)[2])),
                  pl.BlockSpec((1, 1, n, dq), lambda u: (cur(u)[0], cur(u)[1], 0, 0)),
                  pl.BlockSpec((1, 1, c, dq), lambda u: (cur(u)[0], cur(u)[1], 0, 0)),
                  pl.BlockSpec((1, 1, MLA_VA, n), lambda u: (prev(u)[0], prev(u)[1], 0, 0)),
                  pl.BlockSpec((1, 1, MLA_VA, c), lambda u: (prev(u)[0], prev(u)[1], 0, 0))],
        out_specs=pl.BlockSpec((1, tq, MLA_V), lambda u: (prev(u)[0], prev(u)[2], prev(u)[1])),
        out_shape=jax.ShapeDtypeStruct((b, n, h * MLA_V), BF16),
        scratch_shapes=[pltpu.VMEM((n + c, tq), F32), pltpu.VMEM((1, tq), F32)],
        compiler_params=_params(1),
        name="mla_attention",
    )(qt, k_l, k_c, vt_l, vt_c)


def _mla_ctx_kernel(qt_ref, k_ref, vt_ref, o_ref):
    for h in range(qt_ref.shape[1]):
        o_ref[0, :, MLA_V * h: MLA_V * (h + 1)] = _softmax_pv(
            [_dot(k_ref[0, h], qt_ref[0, h])], [vt_ref[0, h]]).astype(o_ref.dtype)


def _mla_ctx_attention(qt, k, vt):
    b, h, dq, c = qt.shape
    return pl.pallas_call(
        _mla_ctx_kernel,
        grid=(b,),
        in_specs=[pl.BlockSpec((1, h, dq, c), lambda i: (i, 0, 0, 0)),
                  pl.BlockSpec((1, h, c, dq), lambda i: (i, 0, 0, 0)),
                  pl.BlockSpec((1, h, MLA_VA, c), lambda i: (i, 0, 0, 0))],
        out_specs=pl.BlockSpec((1, c, h * MLA_V), lambda i: (i, 0, 0)),
        out_shape=jax.ShapeDtypeStruct((b, c, h * MLA_V), BF16),
        compiler_params=_params(1),
        name="mla_ctx_attention",
    )(qt, k, vt)


def _swa_ctx_kernel(sink_ref, q_ref, k_ref, v_ref, o_ref):
    d = SWA_DIM
    for h in range(SWA_HEADS):
        g = h // SWA_GROUP
        q = (q_ref[0, :, d * h: d * (h + 1)].astype(F32) * d ** -0.5).astype(BF16)
        s = _dot_nt(q, k_ref[0, :, d * g: d * (g + 1)].astype(BF16))
        sink = sink_ref[0, h]
        m = jnp.maximum(jnp.max(s, axis=-1, keepdims=True), sink)
        p = jnp.exp(s - m)
        l = jnp.sum(p, axis=-1, keepdims=True) + jnp.exp(sink - m)
        o = _dot(p.astype(BF16), v_ref[0, :, d * g: d * (g + 1)].astype(BF16)) / l
        o_ref[0, :, d * h: d * (h + 1)] = o.astype(o_ref.dtype)


def _swa_ctx_attention(proj_c, sink):
    b, c, _ = proj_c.shape
    qw, kw = SWA_HEADS * SWA_DIM, SWA_KV_HEADS * SWA_DIM
    return pl.pallas_call(
        _swa_ctx_kernel,
        grid=(b,),
        in_specs=[pl.BlockSpec(memory_space=pltpu.SMEM),
                  pl.BlockSpec((1, c, qw), lambda i: (i, 0, P_SWA_Q // qw)),
                  pl.BlockSpec((1, c, kw), lambda i: (i, 0, P_SWA_K // kw)),
                  pl.BlockSpec((1, c, kw), lambda i: (i, 0, P_SWA_V // kw))],
        out_specs=pl.BlockSpec((1, c, qw), lambda i: (i, 0, 0)),
        out_shape=jax.ShapeDtypeStruct((b, c, qw), BF16),
        compiler_params=_params(1),
        name="swa_ctx_attention",
    )(sink, proj_c, proj_c, proj_c)


def _swa_window_start(t, qb, nb):
    return pl.multiple_of(jnp.clip(t * qb - 1, 0, nb - (qb + 2)) * SWA_BLOCK, SWA_BLOCK)


def _swa_kernel(sink_ref, q_ref, k_ref, kc_ref, v_ref, vc_ref, cos_ref, sin_ref, o_ref, s_scr, m_scr, *,
                qb, tiles, nt):
    u = pl.program_id(0)

    @pl.when(u == 0)
    def _():
        s_scr[...] = jnp.zeros_like(s_scr)
        m_scr[...] = jnp.zeros_like(m_scr)

    blk, d = SWA_BLOCK, SWA_DIM
    half = d // 4
    nq = qb * blk
    nw = (qb + 2) * blk
    nb = k_ref.shape[1] // blk
    ck = kc_ref.shape[1]
    cur, prev = jnp.minimum(u, tiles - 1), jnp.maximum(u - 1, 0)
    t_cur, t_prev = cur % nt, prev % nt
    q_start = pl.multiple_of(t_cur * nq, blk)
    k_start = _swa_window_start(t_cur, qb, nb)
    v_start = _swa_window_start(t_prev, qb, nb)
    groups = range(SWA_KV_HEADS)

    def sink_row(g):
        return jnp.concatenate(
            [jnp.full((1, nq), sink_ref[0, g * SWA_GROUP + i] * LOG2E, F32) for i in range(SWA_GROUP)], axis=1)

    def values_t(v):
        return jnp.concatenate([v.astype(F32).T.astype(BF16), jnp.ones((16, v.shape[0]), BF16)], axis=0)

    scale = d ** -0.5 * LOG2E
    cos_q, sin_q = cos_ref[pl.ds(q_start, nq), :], sin_ref[pl.ds(q_start, nq), :]
    q = [jnp.concatenate(
        [(_rope(q_ref[0, :, d * h: d * (h + 1)].astype(F32), cos_q, sin_q, half) * scale).astype(BF16)
         for h in range(g * SWA_GROUP, (g + 1) * SWA_GROUP)], axis=0) for g in groups]
    key = lax.broadcasted_iota(jnp.int32, (ck, nq), 0)
    tok = lax.broadcasted_iota(jnp.int32, (ck, nq), 1)
    m_prev = [m_scr[g] for g in groups]
    m_new = [sink_row(g) for g in groups]
    acc = [jnp.zeros((d + 16, SWA_GROUP * nq), F32) for g in groups]
    n_win = nw // ck
    for j in range(n_win + 1):
        rows = slice(j * ck, (j + 1) * ck)
        if j < n_win:
            kr = pl.ds(k_start + j * ck, ck)
            cos_k, sin_k = cos_ref[kr, :], sin_ref[kr, :]
            dist = key - tok + (k_start + j * ck - q_start)
            valid = (dist <= blk) & (dist >= -blk)
        for g in groups:
            cols = slice(d * g, d * (g + 1))
            p = jnp.exp2(s_scr[g, rows, :] - m_prev[g]).astype(BF16)
            if j < n_win:
                vt = values_t(v_ref[0, pl.ds(v_start + j * ck, ck), cols])
                k = _rope(k_ref[0, kr, cols].astype(F32), cos_k, sin_k, half).astype(BF16)
                s = _dot_nt(k, q[g])
                s = jnp.concatenate(
                    [jnp.where(valid, s[:, nq * i: nq * (i + 1)], -jnp.inf) for i in range(SWA_GROUP)], axis=1)
            else:
                vt = values_t(vc_ref[0, :, cols])
                s = _dot_nt(kc_ref[0, :, cols].astype(BF16), q[g])
            s_scr[g, rows, :] = s
            m_new[g] = jnp.maximum(m_new[g], jnp.max(s, axis=0, keepdims=True))
            acc[g] = acc[g] + _dot(vt, p)
    for g in groups:
        m_scr[g] = m_new[g]
        out_t = acc[g][:d] / (acc[g][d:d + 1] + jnp.exp2(sink_row(g) - m_prev[g]))
        for i in range(SWA_GROUP):
            h = g * SWA_GROUP + i
            o_ref[0, :, d * h: d * (h + 1)] = out_t[:, nq * i: nq * (i + 1)].T.astype(o_ref.dtype)


def _swa_attention(proj_l, proj_c, cos, sin, sink):
    b, n, _ = proj_l.shape
    c = proj_c.shape[1]
    blk, d = SWA_BLOCK, SWA_DIM
    qb = 2
    nq = qb * blk
    nt = n // nq
    tiles = b * nt
    assert ((qb + 2) * blk) % c == 0 and n // blk >= qb + 2
    cur = lambda u: jnp.minimum(u, tiles - 1)
    prev = lambda u: jnp.maximum(u - 1, 0)
    qw, kw = SWA_HEADS * d, SWA_KV_HEADS * d
    return pl.pallas_call(
        functools.partial(_swa_kernel, qb=qb, tiles=tiles, nt=nt),
        grid=(tiles + 1,),
        in_specs=[pl.BlockSpec(memory_space=pltpu.SMEM),
                  pl.BlockSpec((1, nq, qw), lambda u: (cur(u) // nt, cur(u) % nt, P_SWA_Q // qw)),
                  pl.BlockSpec((1, n, kw), lambda u: (cur(u) // nt, 0, P_SWA_K // kw)),
                  pl.BlockSpec((1, c, kw), lambda u: (cur(u) // nt, 0, P_SWA_K // kw)),
                  pl.BlockSpec((1, n, kw), lambda u: (prev(u) // nt, 0, P_SWA_V // kw)),
                  pl.BlockSpec((1, c, kw), lambda u: (prev(u) // nt, 0, P_SWA_V // kw)),
                  pl.BlockSpec((n, d), lambda u: (0, 0)),
                  pl.BlockSpec((n, d), lambda u: (0, 0))],
        out_specs=pl.BlockSpec((1, nq, qw), lambda u: (prev(u) // nt, prev(u) % nt, 0)),
        out_shape=jax.ShapeDtypeStruct((b, n, qw), BF16),
        scratch_shapes=[pltpu.VMEM((SWA_KV_HEADS, (qb + 2) * blk + c, SWA_GROUP * nq), F32),
                        pltpu.VMEM((SWA_KV_HEADS, 1, SWA_GROUP * nq), F32)],
        compiler_params=_params(1),
        name="swa_attention",
    )(sink, proj_l, proj_l, proj_c, proj_l, proj_c, cos, sin)


def _log_sigmoid(x):
    return jnp.minimum(x, 0.0) - jnp.log(1.0 + jnp.exp(-jnp.abs(x)))


def _gla_block_constants(rows):
    t = np.arange(rows)
    ti, tj = t[:, None], t[None, :]
    ci, cj = ti // GLA_CHUNK, tj // GLA_CHUNK
    ones, sel = [], []
    for reverse in (False, True):
        upto = (tj >= ti) if reverse else (tj <= ti)
        visible = (tj > ti) if reverse else (tj <= ti)
        earlier = (cj > ci) if reverse else (cj < ci)
        ones.append((ci == cj) & upto)
        sel.append(np.where((ci == cj) & visible, 1, np.where(earlier, 2, 0)))
    return jnp.asarray(np.stack(ones), BF16), jnp.asarray(np.stack(sel), jnp.int32)


def _gla_factors(q_ref, k_ref, misc_ref, i, wg, bg, ones, sel, reverse):
    rows = q_ref.shape[1]
    L = GLA_CHUNK
    assert rows == 4 * L
    width = GLA_HEADS * GLA_DK
    la = _log_sigmoid(_dot(misc_ref[i].astype(BF16), wg) + bg) * (1.0 / GLA_TAU)
    hi = la.astype(BF16)
    lo = (la - hi.astype(F32)).astype(BF16)
    sums = _dot(ones, jnp.concatenate([hi, lo], axis=1))
    b_in = sums[:, :width] + sums[:, width:]
    chunks = [b_in[c * L:(c + 1) * L] for c in range(rows // L)]
    order = list(reversed(range(len(chunks)))) if reverse else list(range(len(chunks)))
    offset = None
    for c in order:
        total = chunks[c][0:1] if reverse else chunks[c][L - 1:L]
        if offset is not None:
            chunks[c] = chunks[c] + offset
        offset = total if offset is None else offset + total
    b_abs = jnp.concatenate(chunks, axis=0)
    if reverse:
        g_mid, g_end = b_abs[rows // 2:rows // 2 + 1], b_abs[0:1]
    else:
        g_mid, g_end = b_abs[rows // 2 - 1:rows // 2], b_abs[rows - 1:rows]
    diag, cross = sel == 1, sel == 2
    q = q_ref[i].astype(F32) * GLA_DK ** -0.5
    k = k_ref[i].astype(F32)
    q_d, k_d = q * jnp.exp(b_in), k * jnp.exp(-b_in)
    q_x, k_x = q * jnp.exp(b_abs - g_mid), k * jnp.exp(g_mid - b_abs)
    q_s, k_s = q * jnp.exp(b_abs), k * jnp.exp(g_end - b_abs)
    decay = jnp.exp(g_end)
    return q_d, k_d, q_x, k_x, q_s, k_s, decay, diag, cross


def _gla_apply(factors, v_ref, state_ref, o_ref, i):
    q_d, k_d, q_x, k_x, q_s, k_s, decay, diag, cross = factors
    rows = q_d.shape[0]
    lane = lax.broadcasted_iota(jnp.int32, (rows, 2 * GLA_DK), 1)
    lane_s = lax.broadcasted_iota(jnp.int32, (GLA_DV, 2 * GLA_DK), 1)
    for pair in range(GLA_HEADS // 2):
        ps = slice(2 * GLA_DK * pair, 2 * GLA_DK * (pair + 1))
        kd_p, kx_p, ks_p = k_d[:, ps].astype(BF16), k_x[:, ps].astype(BF16), k_s[:, ps].astype(BF16)
        st = state_ref[i, pair]
        st_b = st.astype(BF16)
        ds = []
        for j in range(2):
            h = 2 * pair + j
            vs = slice(h * GLA_DV, (h + 1) * GLA_DV)
            mine = (lane // GLA_DK) == j
            qd_h = jnp.where(mine, q_d[:, ps], 0.0).astype(BF16)
            qx_h = jnp.where(mine, q_x[:, ps], 0.0).astype(BF16)
            qs_h = jnp.where(mine, q_s[:, ps], 0.0).astype(BF16)
            vh = v_ref[i, :, vs].astype(BF16)
            a = jnp.where(diag, _dot_nt(qd_h, kd_p), jnp.where(cross, _dot_nt(qx_h, kx_p), 0.0))
            o_ref[i, :, vs] = (_dot(a.astype(BF16), vh) + _dot_nt(qs_h, st_b)).astype(o_ref.dtype)
            ds.append(_dot_tn(vh, ks_p))
        state_ref[i, pair] = decay[:, ps] * st + jnp.where(lane_s < GLA_DK, ds[0], ds[1])


def _gla_kernel(qf_ref, kf_ref, vf_ref, mf_ref, qb_ref, kb_ref, vb_ref, mb_ref, wg_ref, bg_ref, ones_ref, sel_ref,
                sf0_ref, sb0_ref, of_ref, ob_ref, sf_ref, sb_ref, stf, stb):
    t = pl.program_id(1)

    @pl.when(t == 0)
    def _():
        stf[...] = sf0_ref[...]
        stb[...] = sb0_ref[...]

    width = GLA_HEADS * GLA_DK
    entries = range(qf_ref.shape[0])
    fwd = [_gla_factors(qf_ref, kf_ref, mf_ref, i, wg_ref[:, :width], bg_ref[:, :width], ones_ref[0], sel_ref[0],
                        False) for i in entries]
    bwd = [_gla_factors(qb_ref, kb_ref, mb_ref, i, wg_ref[:, width:], bg_ref[:, width:], ones_ref[1], sel_ref[1],
                        True) for i in entries]
    for i in entries:
        _gla_apply(fwd[i], vf_ref, stf, of_ref, i)
        _gla_apply(bwd[i], vb_ref, stb, ob_ref, i)

    @pl.when(t == pl.num_programs(1) - 1)
    def _():
        sf_ref[...] = stf[...]
        sb_ref[...] = stb[...]


def _gla_scan(proj, wg, bg, sf0, sb0, layer):
    b, r, _ = proj.shape
    tm = 256
    nt = r // tm
    kw, vw = GLA_HEADS * GLA_DK, GLA_HEADS * GLA_DV
    fwd = lambda t: t
    bwd = lambda t: nt - 1 - t

    nb = next(k for k in (4, 2, 1) if b % k == 0)
    ones, sel = _gla_block_constants(tm)

    def specs(order):
        return [pl.BlockSpec((nb, tm, kw), lambda i, t: (i, order(t), P_GLA_Q // kw)),
                pl.BlockSpec((nb, tm, kw), lambda i, t: (i, order(t), P_GLA_K // kw)),
                pl.BlockSpec((nb, tm, vw), lambda i, t: (i, order(t), P_GLA_V // vw)),
                pl.BlockSpec((nb, tm, 128), lambda i, t: (i, order(t), P_MISC // 128))]

    state_spec = pl.BlockSpec((nb,) + GLA_STATE, lambda i, t: (i, 0, 0, 0))
    state_shape = jax.ShapeDtypeStruct((b,) + GLA_STATE, F32)
    return pl.pallas_call(
        _gla_kernel,
        grid=(b // nb, nt),
        in_specs=specs(fwd) + specs(bwd) + [
            pl.BlockSpec((None,) + wg.shape[1:], lambda i, t: (layer, 0, 0)),
            pl.BlockSpec((None,) + bg.shape[1:], lambda i, t: (layer, 0, 0)),
            pl.BlockSpec(ones.shape, lambda i, t: (0, 0, 0)),
            pl.BlockSpec(sel.shape, lambda i, t: (0, 0, 0)),
            state_spec, state_spec],
        out_specs=[pl.BlockSpec((nb, tm, vw), lambda i, t: (i, fwd(t), 0)),
                   pl.BlockSpec((nb, tm, vw), lambda i, t: (i, bwd(t), 0)),
                   state_spec, state_spec],
        out_shape=[jax.ShapeDtypeStruct((b, r, vw), BF16), jax.ShapeDtypeStruct((b, r, vw), BF16),
                   state_shape, state_shape],
        scratch_shapes=[pltpu.VMEM((nb,) + GLA_STATE, F32), pltpu.VMEM((nb,) + GLA_STATE, F32)],
        compiler_params=_params(2),
        name="gla_scan",
    )(proj, proj, proj, proj, proj, proj, proj, proj, wg, bg, ones, sel, sf0, sb0)


def _outproj_kernel(x_ref, mod_ref, mla_ref, swa_ref, of_ref, ob_ref, r_ref, g_ref, w_ref, o_ref):
    o = of_ref[0].astype(F32) + ob_ref[0].astype(F32)
    gate = _silu(r_ref[0].astype(F32))
    parts = []
    for h in range(GLA_HEADS):
        vs = slice(h * GLA_DV, (h + 1) * GLA_DV)
        parts.append((_rms(o[:, vs], g_ref[:, vs]) * gate[:, vs]).astype(BF16))
    mix = _dot(jnp.concatenate([mla_ref[0], swa_ref[0]] + parts, axis=1), w_ref[...])
    o_ref[0] = x_ref[0] + mod_ref[0, 2:3, :] * mix


def _out_projection(x, mod, mla, swa, o_f, o_b, proj, g_out, w_out, layer):
    bx, r, d = x.shape
    tm = 512
    row = lambda width: pl.BlockSpec((1, tm, width), lambda b, t: (b, t, 0))
    gw = GLA_HEADS * GLA_DV
    return pl.pallas_call(
        _outproj_kernel,
        grid=(bx, r // tm),
        in_specs=[row(d),
                  pl.BlockSpec((1, N_MOD, d), lambda b, t: (b, 0, 0)),
                  row(mla.shape[2]), row(swa.shape[2]), row(gw), row(gw),
                  pl.BlockSpec((1, tm, gw), lambda b, t: (b, t, P_GLA_R // gw)),
                  pl.BlockSpec((1, gw), lambda b, t: (0, 0)),
                  pl.BlockSpec((None,) + w_out.shape[1:], lambda b, t: (layer, 0, 0))],
        out_specs=row(d),
        out_shape=jax.ShapeDtypeStruct(x.shape, F32),
        compiler_params=_params(2),
        name="out_projection",
    )(x, mod, mla, swa, o_f, o_b, proj, g_out, w_out)


def _ffn_kernel(x_ref, mod_ref, g_ref, wg_ref, wu_ref, wd_ref, gf_ref, o_ref, h_ref, *, final_norm):
    f = pl.program_id(2)
    row_chunks = [slice(r, r + ROW_CHUNK) for r in range(0, x_ref.shape[1], ROW_CHUNK)]
    chunk = (ROW_CHUNK, x_ref.shape[2])

    def gated_partial():
        h = h_ref[...]
        act = _silu(_dot(h, wg_ref[...])) * _dot(h, wu_ref[...])
        return mod_ref[0, 5:6, :] * _dot(act.astype(BF16), wd_ref[...])

    @pl.when(f == 0)
    def _():
        gain = jnp.broadcast_to(g_ref[...] * (1.0 + mod_ref[0, 4:5, :]), chunk)
        shift = jnp.broadcast_to(mod_ref[0, 3:4, :], chunk)
        for rows in row_chunks:
            x = x_ref[0, rows, :]
            rs = lax.rsqrt(jnp.mean(x * x, axis=-1, keepdims=True) + EPS)
            h_ref[rows, :] = (x * rs * gain + shift).astype(BF16)
        o_ref[0] = x_ref[0] + gated_partial()

    @pl.when(f > 0)
    def _():
        o_ref[0] += gated_partial()

    if final_norm:
        @pl.when(f == pl.num_programs(2) - 1)
        def _():
            g_fin = jnp.broadcast_to(gf_ref[...], chunk)
            for rows in row_chunks:
                y = o_ref[0, rows, :]
                o_ref[0, rows, :] = y * lax.rsqrt(jnp.mean(y * y, axis=-1, keepdims=True) + EPS) * g_fin


def _ffn(x, mod, g, w_gu, w_down, g_final, final_norm, layer):
    bx, r, d = x.shape
    hidden = w_down.shape[1]
    tm, tf = min(r, 1024), 512
    nf = hidden // tf
    return pl.pallas_call(
        functools.partial(_ffn_kernel, final_norm=final_norm),
        grid=(bx, r // tm, nf),
        in_specs=[pl.BlockSpec((1, tm, d), lambda b, t, f: (b, t, 0)),
                  pl.BlockSpec((1, N_MOD, d), lambda b, t, f: (b, 0, 0)),
                  pl.BlockSpec((1, d), lambda b, t, f: (0, 0)),
                  pl.BlockSpec((None, d, tf), lambda b, t, f: (layer, 0, f)),
                  pl.BlockSpec((None, d, tf), lambda b, t, f: (layer, 0, nf + f)),
                  pl.BlockSpec((None, tf, d), lambda b, t, f: (layer, f, 0)),
                  pl.BlockSpec((1, d), lambda b, t, f: (0, 0))],
        out_specs=pl.BlockSpec((1, tm, d), lambda b, t, f: (b, t, 0)),
        out_shape=jax.ShapeDtypeStruct(x.shape, F32),
        scratch_shapes=[pltpu.VMEM((tm, d), BF16)],
        compiler_params=_params(3),
        name="ffn",
    )(x, mod, g, w_gu, w_gu, w_down, g_final)


def _rope_tables(n, dim):
    half = dim // 4
    freqs = ROPE_THETA ** (-jnp.arange(half, dtype=F32) / half)
    pos = jnp.arange(n, dtype=jnp.int32)
    ang_r = (pos // GRID_W).astype(F32)[:, None] * freqs
    ang_c = (pos % GRID_W).astype(F32)[:, None] * freqs
    cos = jnp.concatenate([jnp.cos(ang_r)] * 2 + [jnp.cos(ang_c)] * 2, axis=1)
    sin = jnp.concatenate([-jnp.sin(ang_r), jnp.sin(ang_r), -jnp.sin(ang_c), jnp.sin(ang_c)], axis=1)
    reps = 128 // dim
    return jnp.tile(cos, (1, reps)), jnp.tile(sin, (1, reps))


def _cast_kernel(w_ref, o_ref):
    o_ref[...] = w_ref[...].astype(o_ref.dtype)


def _to_bf16(w):
    depth, rows, cols = w.shape
    tr = max(16, min(rows, (8 * 1024 * 1024) // (4 * cols) // 16 * 16))
    while rows % tr:
        tr -= 16
    return pl.pallas_call(
        _cast_kernel,
        grid=(depth, rows // tr),
        in_specs=[pl.BlockSpec((None, tr, cols), lambda l, t: (l, t, 0))],
        out_specs=pl.BlockSpec((None, tr, cols), lambda l, t: (l, t, 0)),
        out_shape=jax.ShapeDtypeStruct(w.shape, BF16),
        compiler_params=_params(2),
        name="weight_cast",
    )(w)


def _relayout_w_in(w_in):
    depth, d, _ = w_in.shape
    sizes = (MLA_RANK, MLA_RANK, MLA_ROPE, SWA_HEADS * SWA_DIM, SWA_KV_HEADS * SWA_DIM, SWA_KV_HEADS * SWA_DIM,
             GLA_HEADS * GLA_DK, GLA_HEADS * GLA_DK, GLA_HEADS * GLA_DV, 2 * GLA_GATE_RANK, GLA_HEADS * GLA_DV)
    offs = [0]
    for s in sizes:
        offs.append(offs[-1] + s)
    runs = [(3, 5), (0, 2), (5, 6), (2, 3), (9, 10), P_GLA_Q - P_MISC - MLA_ROPE - 2 * GLA_GATE_RANK, (6, 9), (10, 11)]

    def body(w_ref, o_ref):
        dst = 0
        for run in runs:
            if isinstance(run, int):
                o_ref[:, dst:dst + run] = jnp.zeros((o_ref.shape[0], run), o_ref.dtype)
                dst += run
            else:
                src, width = offs[run[0]], offs[run[1]] - offs[run[0]]
                o_ref[:, dst:dst + width] = w_ref[:, src:src + width].astype(o_ref.dtype)
                dst += width
        assert dst == P_WIDTH

    tr = 256
    return pl.pallas_call(
        body,
        grid=(depth, d // tr),
        in_specs=[pl.BlockSpec((None, tr, w_in.shape[2]), lambda l, t: (l, t, 0))],
        out_specs=pl.BlockSpec((None, tr, P_WIDTH), lambda l, t: (l, t, 0)),
        out_shape=jax.ShapeDtypeStruct((depth, d, P_WIDTH), BF16),
        compiler_params=_params(2),
        name="w_in_relayout",
    )(w_in)


def _relayout_mla(w_uq, w_ukv):
    depth, rk, _ = w_uq.shape
    uq = w_uq.reshape(depth, rk, MLA_HEADS, MLA_QK)
    uq = jnp.concatenate([uq[..., :MLA_NOPE].reshape(depth, rk, -1), uq[..., MLA_NOPE:].reshape(depth, rk, -1)], -1)
    ukv = w_ukv.reshape(depth, rk, MLA_HEADS, MLA_NOPE + MLA_V)
    ukv = jnp.concatenate([ukv[..., :MLA_NOPE].reshape(depth, rk, -1), ukv[..., MLA_NOPE:].reshape(depth, rk, -1)], -1)
    return uq.astype(BF16), ukv.astype(BF16)


def _relayout_gate(w_f, b_f, w_b, b_b):
    depth, rank, width = w_f.shape
    wg = jnp.zeros((depth, 128, 2 * width), F32)
    wg = wg.at[:, MISC_GATE_OFF:MISC_GATE_OFF + rank, :width].set(w_f)
    wg = wg.at[:, MISC_GATE_OFF + rank:MISC_GATE_OFF + 2 * rank, width:].set(w_b)
    bg = jnp.concatenate([b_f, b_b], axis=-1).reshape(depth, 1, 2 * width)
    return wg.astype(BF16), bg


def kernel(x, c, ctx, c_ctx, w_mod, b_mod, g_mix, g_ffn, w_in, g_mla_q, g_mla_kv, w_mla_uq, w_mla_ukv,
           swa_sink, w_gla_gate_f, b_gla_gate_f, w_gla_gate_b, b_gla_gate_b, g_gla_out, w_out, w_ffn_gu,
           w_ffn_down, g_final):
    B, N, D = x.shape
    C = ctx.shape[1]
    depth = w_mod.shape[0]

    cvec = jnp.concatenate([c, c_ctx[None, :], jnp.zeros((16 - B - 1, D), F32)], axis=0)
    mod = _modulation(cvec, w_mod, b_mod).reshape(depth, 16, N_MOD, D)

    w_in_p = _relayout_w_in(w_in)
    w_uq_p, w_ukv_p = _relayout_mla(w_mla_uq, w_mla_ukv)
    wg_p, bg_p = _relayout_gate(w_gla_gate_f, b_gla_gate_f, w_gla_gate_b, b_gla_gate_b)
    w_out_b = _to_bf16(w_out)
    w_gu_b = _to_bf16(w_ffn_gu)
    w_down_b = _to_bf16(w_ffn_down)

    cos_m, sin_m = _rope_tables(N, MLA_ROPE)
    cos_s, sin_s = _rope_tables(N, SWA_DIM)
    cos_id, sin_id = jnp.ones((C, 128), F32), jnp.zeros((C, 128), F32)
    state0 = jnp.zeros((B,) + GLA_STATE, F32)
    g_fin = g_final.reshape(1, D)

    xc = ctx.reshape(1, B * C, D)
    for l in range(depth):
        last = l == depth - 1
        mod_l, mod_c = mod[l, :B], mod[l, B:B + 1]
        g_mix_l, g_ffn_l = g_mix[l].reshape(1, D), g_ffn[l].reshape(1, D)
        g_q, g_kv = g_mla_q[l].reshape(1, -1), g_mla_kv[l].reshape(1, -1)
        g_out = g_gla_out[l].reshape(1, -1)
        sink = swa_sink[l].reshape(1, -1)

        proj_l = _in_projection(x, mod_l, g_mix_l, w_in_p, l)
        proj_c = _in_projection(xc, mod_c, g_mix_l, w_in_p, l).reshape(B, C, P_WIDTH)

        q_l, k_l, v_l = _mla_prep(proj_l, cos_m, sin_m, g_q, g_kv, w_uq_p, w_ukv_p, l)
        q_c, k_c, v_c = _mla_prep(proj_c, cos_id, sin_id, g_q, g_kv, w_uq_p, w_ukv_p, l)
        mla_l = _mla_attention(q_l, k_l, v_l, k_c, v_c)
        swa_l = _swa_attention(proj_l, proj_c, cos_s, sin_s, sink)

        of_c, ob_c, s_f, s_b = _gla_scan(proj_c, wg_p, bg_p, state0, state0, l)
        of_l, ob_l, _, _ = _gla_scan(proj_l, wg_p, bg_p, s_f, s_b, l)

        x = _out_projection(x, mod_l, mla_l, swa_l, of_l, ob_l, proj_l, g_out, w_out_b, l)
        x = _ffn(x, mod_l, g_ffn_l, w_gu_b, w_down_b, g_fin, last, l)

        if not last:
            mla_c = _mla_ctx_attention(q_c, k_c, v_c)
            swa_c = _swa_ctx_attention(proj_c, sink)
            flat = lambda a: a.reshape(1, B * C, a.shape[-1])
            xc = _out_projection(xc, mod_c, flat(mla_c), flat(swa_c), flat(of_c), flat(ob_c), flat(proj_c),
                                 g_out, w_out_b, l)
            xc = _ffn(xc, mod_c, g_ffn_l, w_gu_b, w_down_b, g_fin, False, l)
    return x
```

```python
import functools

import jax
import jax.numpy as jnp
import numpy as np
from jax import lax
from jax.experimental import pallas as pl
from jax.experimental.pallas import tpu as pltpu

F32 = jnp.float32
BF16 = jnp.bfloat16

GRID_W = 64
EPS = 1e-6
ROPE_THETA = 10000.0
LOG2E = 1.4426950408889634

MLA_HEADS = 6
MLA_RANK = 512
MLA_NOPE = 128
MLA_ROPE = 64
MLA_V = 128
MLA_QK = MLA_NOPE + MLA_ROPE
MLA_VA = MLA_V + 16

SWA_HEADS = 6
SWA_KV_HEADS = 2
SWA_GROUP = SWA_HEADS // SWA_KV_HEADS
SWA_DIM = 128
SWA_BLOCK = 128

GLA_HEADS = 4
GLA_DK = 64
GLA_DV = 128
GLA_GATE_RANK = 16
GLA_TAU = 16.0
GLA_CHUNK = 64
GLA_STATE = (GLA_HEADS // 2, GLA_DV, 2 * GLA_DK)

N_MOD = 6

P_SWA_Q = 0
P_SWA_K = 768
P_CQ = 1024
P_CKV = 1536
P_SWA_V = 2048
P_MISC = 2304
P_GLA_Q = 2560
P_GLA_K = 2816
P_GLA_V = 3072
P_GLA_R = 3584
P_WIDTH = 4096
MISC_GATE_OFF = MLA_ROPE

VMEM_LIMIT = 56 * 1024 * 1024
ROW_CHUNK = 16


def _params(n_axes, flags=None):
    return pltpu.CompilerParams(dimension_semantics=("arbitrary",) * n_axes,
                                vmem_limit_bytes=VMEM_LIMIT, flags=flags)


def _silu(x):
    return x / (1.0 + jnp.exp(-x))


def _rms(x, g):
    ms = jnp.mean(x * x, axis=-1, keepdims=True)
    return x * lax.rsqrt(ms + EPS) * g


def _dot(a, b):
    return jnp.dot(a, b, preferred_element_type=F32)


def _dot_nt(a, b):
    return lax.dot_general(a, b, (((1,), (1,)), ((), ())), preferred_element_type=F32)


def _dot_tn(a, b):
    return lax.dot_general(a, b, (((0,), (0,)), ((), ())), preferred_element_type=F32)


def _mod_kernel(c_ref, w_ref, b_ref, o_ref):
    a = _silu(c_ref[...]).astype(BF16)
    o_ref[0] = _dot(a, w_ref[0].astype(BF16)) + b_ref[0]


def _modulation(cvec, w_mod, b_mod):
    depth, d, width = w_mod.shape
    rows = cvec.shape[0]
    tn = 1024
    return pl.pallas_call(
        _mod_kernel,
        grid=(depth, width // tn),
        in_specs=[pl.BlockSpec((rows, d), lambda l, j: (0, 0)),
                  pl.BlockSpec((1, d, tn), lambda l, j: (l, 0, j)),
                  pl.BlockSpec((1, 1, tn), lambda l, j: (l, 0, j))],
        out_specs=pl.BlockSpec((1, rows, tn), lambda l, j: (l, 0, j)),
        out_shape=jax.ShapeDtypeStruct((depth, rows, width), F32),
        compiler_params=_params(2),
        name="modulation",
    )(cvec, w_mod, b_mod.reshape(depth, 1, width))


def _inproj_kernel(x_ref, mod_ref, g_ref, w_ref, o_ref):
    h = _rms(x_ref[0], g_ref[...]) * (1.0 + mod_ref[0, 1:2, :]) + mod_ref[0, 0:1, :]
    o_ref[0] = _dot(h.astype(BF16), w_ref[...]).astype(o_ref.dtype)


def _in_projection(x, mod, g, w, layer):
    bx, r, d = x.shape
    tm, tn = 512, P_WIDTH
    return pl.pallas_call(
        _inproj_kernel,
        grid=(P_WIDTH // tn, bx, r // tm),
        in_specs=[pl.BlockSpec((1, tm, d), lambda j, b, t: (b, t, 0)),
                  pl.BlockSpec((1, N_MOD, d), lambda j, b, t: (b, 0, 0)),
                  pl.BlockSpec((1, d), lambda j, b, t: (0, 0)),
                  pl.BlockSpec((None, d, tn), lambda j, b, t: (layer, 0, j), pipeline_mode=pl.Buffered(1))],
        out_specs=pl.BlockSpec((1, tm, tn), lambda j, b, t: (b, t, j)),
        out_shape=jax.ShapeDtypeStruct((bx, r, P_WIDTH), BF16),
        compiler_params=_params(3),
        name="in_projection",
    )(x, mod, g, w)


def _rope(x, cos, sin, half):
    lane = lax.broadcasted_iota(jnp.int32, x.shape, 1)
    first = (lane % (2 * half)) < half
    width = x.shape[1]
    rot = jnp.where(first, pltpu.roll(x, width - half, 1), pltpu.roll(x, half, 1))
    return x * cos + rot * sin


def _mla_prep_kernel(cq_ref, ckv_ref, misc_ref, cos_ref, sin_ref, gq_ref, gkv_ref, wuq_ref, wukv_ref,
                     qt_ref, k_ref, vt_ref):
    scale = MLA_QK ** -0.5 * LOG2E
    cos, sin = cos_ref[...], sin_ref[...]
    half = MLA_ROPE // 4
    qf = _dot(_rms(cq_ref[0].astype(F32), gq_ref[...]).astype(BF16), wuq_ref[...])
    kvf = _dot(_rms(ckv_ref[0].astype(F32), gkv_ref[...]).astype(BF16), wukv_ref[...])
    kr = _rope(misc_ref[0].astype(F32), cos, sin, half)[:, :MLA_ROPE].astype(BF16)
    nope_w = MLA_HEADS * MLA_NOPE
    for pair in range(MLA_HEADS // 2):
        qr = _rope(qf[:, nope_w + 128 * pair: nope_w + 128 * (pair + 1)], cos, sin, half) * scale
        qrt = qr.astype(BF16).T
        for j in range(2):
            qt_ref[0, 2 * pair + j, MLA_NOPE:MLA_QK, :] = qrt[MLA_ROPE * j: MLA_ROPE * (j + 1), :]
    for h in range(MLA_HEADS):
        qt_ref[0, h, 0:MLA_NOPE, :] = (qf[:, MLA_NOPE * h: MLA_NOPE * (h + 1)] * scale).astype(BF16).T
        k_ref[0, h, :, 0:MLA_NOPE] = kvf[:, MLA_NOPE * h: MLA_NOPE * (h + 1)].astype(BF16)
        k_ref[0, h, :, MLA_NOPE:MLA_QK] = kr
        vt_ref[0, h, 0:MLA_V, :] = kvf[:, nope_w + MLA_V * h: nope_w + MLA_V * (h + 1)].astype(BF16).T
        vt_ref[0, h, MLA_V:, :] = jnp.ones((MLA_VA - MLA_V, vt_ref.shape[3]), BF16)


def _mla_prep(proj, cos, sin, g_q, g_kv, w_uq, w_ukv, layer):
    b, r, _ = proj.shape
    tm = min(r, 1024)
    rk = MLA_RANK
    const = lambda shape: pl.BlockSpec(shape, lambda i, t: (0,) * len(shape))
    stacked = lambda w: pl.BlockSpec((None,) + w.shape[1:], lambda i, t: (layer, 0, 0))
    return pl.pallas_call(
        _mla_prep_kernel,
        grid=(b, r // tm),
        in_specs=[pl.BlockSpec((1, tm, rk), lambda i, t: (i, t, P_CQ // rk)),
                  pl.BlockSpec((1, tm, rk), lambda i, t: (i, t, P_CKV // rk)),
                  pl.BlockSpec((1, tm, 128), lambda i, t: (i, t, P_MISC // 128)),
                  pl.BlockSpec((tm, 128), lambda i, t: (t, 0)),
                  pl.BlockSpec((tm, 128), lambda i, t: (t, 0)),
                  const((1, rk)), const((1, rk)),
                  stacked(w_uq), stacked(w_ukv)],
        out_specs=[pl.BlockSpec((1, MLA_HEADS, MLA_QK, tm), lambda i, t: (i, 0, 0, t)),
                   pl.BlockSpec((1, MLA_HEADS, tm, MLA_QK), lambda i, t: (i, 0, t, 0)),
                   pl.BlockSpec((1, MLA_HEADS, MLA_VA, tm), lambda i, t: (i, 0, 0, t))],
        out_shape=[jax.ShapeDtypeStruct((b, MLA_HEADS, MLA_QK, r), BF16),
                   jax.ShapeDtypeStruct((b, MLA_HEADS, r, MLA_QK), BF16),
                   jax.ShapeDtypeStruct((b, MLA_HEADS, MLA_VA, r), BF16)],
        compiler_params=_params(2),
        name="mla_prep",
    )(proj, proj, proj, cos, sin, g_q, g_kv, w_uq, w_ukv)


def _softmax_pv(s_parts, vt_parts):
    m = functools.reduce(jnp.maximum, [jnp.max(s, axis=0, keepdims=True) for s in s_parts])
    acc = sum(_dot(vt, jnp.exp2(s - m).astype(BF16)) for s, vt in zip(s_parts, vt_parts))
    return (acc[:MLA_V] / acc[MLA_V:MLA_V + 1]).T


def _mla_attn_kernel(qt_ref, kl_ref, kc_ref, vlt_ref, vct_ref, o_ref, s_scr, m_scr, *, tk):
    @pl.when(pl.program_id(0) == 0)
    def _():
        s_scr[...] = jnp.zeros_like(s_scr)
        m_scr[...] = jnp.zeros_like(m_scr)

    n_lat = kl_ref.shape[2] // tk
    qt = qt_ref[0, 0]
    tq = qt.shape[1]
    m_prev = m_scr[...]
    m_new = jnp.full((1, tq), -jnp.inf, F32)
    acc = jnp.zeros((MLA_VA, tq), F32)
    for j in range(n_lat + 1):
        rows = slice(j * tk, (j + 1) * tk)
        p = jnp.exp2(s_scr[rows, :] - m_prev).astype(BF16)
        k = kc_ref[0, 0] if j == n_lat else kl_ref[0, 0, rows, :]
        vt = vct_ref[0, 0] if j == n_lat else vlt_ref[0, 0, :, rows]
        s = _dot(k, qt)
        s_scr[rows, :] = s
        m_new = jnp.maximum(m_new, jnp.max(s, axis=0, keepdims=True))
        acc = acc + _dot(vt, p)
    m_scr[...] = m_new
    o_ref[0] = (acc[:MLA_V] / acc[MLA_V:MLA_V + 1]).T.astype(o_ref.dtype)


def _mla_attention(qt, k_l, vt_l, k_c, vt_c):
    b, h, dq, n = qt.shape
    c = k_c.shape[2]
    tq = min(n, 1024)
    nt = n // tq
    tiles = b * h * nt

    def cur(u):
        t = jnp.minimum(u, tiles - 1)
        return t // (h * nt), (t // nt) % h, t % nt

    def prev(u):
        t = jnp.maximum(u - 1, 0)
        return t // (h * nt), (t // nt) % h, t % nt

    assert n % c == 0
    return pl.pallas_call(
        functools.partial(_mla_attn_kernel, tk=c),
        grid=(tiles + 1,),
        in_specs=[pl.BlockSpec((1, 1, dq, tq), lambda u: (cur(u)[0], cur(u)[1], 0, cur(u)[2])),
                  pl.BlockSpec((1, 1, n, dq), lambda u: (cur(u)[0], cur(u)[1], 0, 0)),
                  pl.BlockSpec((1, 1, c, dq), lambda u: (cur(u)[0], cur(u)[1], 0, 0)),
                  pl.BlockSpec((1, 1, MLA_VA, n), lambda u: (prev(u)[0], prev(u)[1], 0, 0)),
                  pl.BlockSpec((1, 1, MLA_VA, c), lambda u: (prev(u)[0], prev(u)[1], 0, 0))],
        out_specs=pl.BlockSpec((1, tq, MLA_V), lambda u: (prev(u)[0], prev(u)[2], prev(u)[1])),
        out_shape=jax.ShapeDtypeStruct((b, n, h * MLA_V), BF16),
        scratch_shapes=[pltpu.VMEM((n + c, tq), F32), pltpu.VMEM((1, tq), F32)],
        compiler_params=_params(1),
        name="mla_attention",
    )(qt, k_l, k_c, vt_l, vt_c)


def _mla_ctx_kernel(qt_ref, k_ref, vt_ref, o_ref):
    for h in range(qt_ref.shape[1]):
        o_ref[0, :, MLA_V * h: MLA_V * (h + 1)] = _softmax_pv(
            [_dot(k_ref[0, h], qt_ref[0, h])], [vt_ref[0, h]]).astype(o_ref.dtype)


def _mla_ctx_attention(qt, k, vt):
    b, h, dq, c = qt.shape
    return pl.pallas_call(
        _mla_ctx_kernel,
        grid=(b,),
        in_specs=[pl.BlockSpec((1, h, dq, c), lambda i: (i, 0, 0, 0)),
                  pl.BlockSpec((1, h, c, dq), lambda i: (i, 0, 0, 0)),
                  pl.BlockSpec((1, h, MLA_VA, c), lambda i: (i, 0, 0, 0))],
        out_specs=pl.BlockSpec((1, c, h * MLA_V), lambda i: (i, 0, 0)),
        out_shape=jax.ShapeDtypeStruct((b, c, h * MLA_V), BF16),
        compiler_params=_params(1),
        name="mla_ctx_attention",
    )(qt, k, vt)


def _swa_ctx_kernel(sink_ref, q_ref, k_ref, v_ref, o_ref):
    d = SWA_DIM
    for h in range(SWA_HEADS):
        g = h // SWA_GROUP
        q = (q_ref[0, :, d * h: d * (h + 1)].astype(F32) * d ** -0.5).astype(BF16)
        s = _dot_nt(q, k_ref[0, :, d * g: d * (g + 1)].astype(BF16))
        sink = sink_ref[0, h]
        m = jnp.maximum(jnp.max(s, axis=-1, keepdims=True), sink)
        p = jnp.exp(s - m)
        l = jnp.sum(p, axis=-1, keepdims=True) + jnp.exp(sink - m)
        o = _dot(p.astype(BF16), v_ref[0, :, d * g: d * (g + 1)].astype(BF16)) / l
        o_ref[0, :, d * h: d * (h + 1)] = o.astype(o_ref.dtype)


def _swa_ctx_attention(proj_c, sink):
    b, c, _ = proj_c.shape
    qw, kw = SWA_HEADS * SWA_DIM, SWA_KV_HEADS * SWA_DIM
    return pl.pallas_call(
        _swa_ctx_kernel,
        grid=(b,),
        in_specs=[pl.BlockSpec(memory_space=pltpu.SMEM),
                  pl.BlockSpec((1, c, qw), lambda i: (i, 0, P_SWA_Q // qw)),
                  pl.BlockSpec((1, c, kw), lambda i: (i, 0, P_SWA_K // kw)),
                  pl.BlockSpec((1, c, kw), lambda i: (i, 0, P_SWA_V // kw))],
        out_specs=pl.BlockSpec((1, c, qw), lambda i: (i, 0, 0)),
        out_shape=jax.ShapeDtypeStruct((b, c, qw), BF16),
        compiler_params=_params(1),
        name="swa_ctx_attention",
    )(sink, proj_c, proj_c, proj_c)


def _swa_window_start(t, qb, nb):
    return pl.multiple_of(jnp.clip(t * qb - 1, 0, nb - (qb + 2)) * SWA_BLOCK, SWA_BLOCK)


def _swa_kernel(sink_ref, q_ref, k_ref, kc_ref, v_ref, vc_ref, cos_ref, sin_ref, o_ref, s_scr, m_scr, *,
                qb, tiles, nt):
    u = pl.program_id(0)

    @pl.when(u == 0)
    def _():
        s_scr[...] = jnp.zeros_like(s_scr)
        m_scr[...] = jnp.zeros_like(m_scr)

    blk, d = SWA_BLOCK, SWA_DIM
    half = d // 4
    nq = qb * blk
    nw = (qb + 2) * blk
    nb = k_ref.shape[1] // blk
    ck = kc_ref.shape[1]
    cur, prev = jnp.minimum(u, tiles - 1), jnp.maximum(u - 1, 0)
    t_cur, t_prev = cur % nt, prev % nt
    q_start = pl.multiple_of(t_cur * nq, blk)
    k_start = _swa_window_start(t_cur, qb, nb)
    v_start = _swa_window_start(t_prev, qb, nb)
    groups = range(SWA_KV_HEADS)

    def sink_row(g):
        return jnp.concatenate(
            [jnp.full((1, nq), sink_ref[0, g * SWA_GROUP + i] * LOG2E, F32) for i in range(SWA_GROUP)], axis=1)

    def values_t(v):
        return jnp.concatenate([v.astype(F32).T.astype(BF16), jnp.ones((16, v.shape[0]), BF16)], axis=0)

    scale = d ** -0.5 * LOG2E
    cos_q, sin_q = cos_ref[pl.ds(q_start, nq), :], sin_ref[pl.ds(q_start, nq), :]
    q = [jnp.concatenate(
        [(_rope(q_ref[0, :, d * h: d * (h + 1)].astype(F32), cos_q, sin_q, half) * scale).astype(BF16)
         for h in range(g * SWA_GROUP, (g + 1) * SWA_GROUP)], axis=0) for g in groups]
    key = lax.broadcasted_iota(jnp.int32, (ck, nq), 0)
    tok = lax.broadcasted_iota(jnp.int32, (ck, nq), 1)
    m_prev = [m_scr[g] for g in groups]
    m_new = [sink_row(g) for g in groups]
    acc = [jnp.zeros((d + 16, SWA_GROUP * nq), F32) for g in groups]
    n_win = nw // ck
    for j in range(n_win + 1):
        rows = slice(j * ck, (j + 1) * ck)
        if j < n_win:
            kr = pl.ds(k_start + j * ck, ck)
            cos_k, sin_k = cos_ref[kr, :], sin_ref[kr, :]
            dist = key - tok + (k_start + j * ck - q_start)
            valid = (dist <= blk) & (dist >= -blk)
        for g in groups:
            cols = slice(d * g, d * (g + 1))
            p = jnp.exp2(s_scr[g, rows, :] - m_prev[g]).astype(BF16)
            if j < n_win:
                vt = values_t(v_ref[0, pl.ds(v_start + j * ck, ck), cols])
                k = _rope(k_ref[0, kr, cols].astype(F32), cos_k, sin_k, half).astype(BF16)
                s = _dot_nt(k, q[g])
                s = jnp.concatenate(
                    [jnp.where(valid, s[:, nq * i: nq * (i + 1)], -jnp.inf) for i in range(SWA_GROUP)], axis=1)
            else:
                vt = values_t(vc_ref[0, :, cols])
                s = _dot_nt(kc_ref[0, :, cols].astype(BF16), q[g])
            s_scr[g, rows, :] = s
            m_new[g] = jnp.maximum(m_new[g], jnp.max(s, axis=0, keepdims=True))
            acc[g] = acc[g] + _dot(vt, p)
    for g in groups:
        m_scr[g] = m_new[g]
        out_t = acc[g][:d] / (acc[g][d:d + 1] + jnp.exp2(sink_row(g) - m_prev[g]))
        for i in range(SWA_GROUP):
            h = g * SWA_GROUP + i
            o_ref[0, :, d * h: d * (h + 1)] = out_t[:, nq * i: nq * (i + 1)].T.astype(o_ref.dtype)


def _swa_attention(proj_l, proj_c, cos, sin, sink):
    b, n, _ = proj_l.shape
    c = proj_c.shape[1]
    blk, d = SWA_BLOCK, SWA_DIM
    qb = 2
    nq = qb * blk
    nt = n // nq
    tiles = b * nt
    assert ((qb + 2) * blk) % c == 0 and n // blk >= qb + 2
    cur = lambda u: jnp.minimum(u, tiles - 1)
    prev = lambda u: jnp.maximum(u - 1, 0)
    qw, kw = SWA_HEADS * d, SWA_KV_HEADS * d
    return pl.pallas_call(
        functools.partial(_swa_kernel, qb=qb, tiles=tiles, nt=nt),
        grid=(tiles + 1,),
        in_specs=[pl.BlockSpec(memory_space=pltpu.SMEM),
                  pl.BlockSpec((1, nq, qw), lambda u: (cur(u) // nt, cur(u) % nt, P_SWA_Q // qw)),
                  pl.BlockSpec((1, n, kw), lambda u: (cur(u) // nt, 0, P_SWA_K // kw)),
                  pl.BlockSpec((1, c, kw), lambda u: (cur(u) // nt, 0, P_SWA_K // kw)),
                  pl.BlockSpec((1, n, kw), lambda u: (prev(u) // nt, 0, P_SWA_V // kw)),
                  pl.BlockSpec((1, c, kw), lambda u: (prev(u) // nt, 0, P_SWA_V // kw)),
                  pl.BlockSpec((n, d), lambda u: (0, 0)),
                  pl.BlockSpec((n, d), lambda u: (0, 0))],
        out_specs=pl.BlockSpec((1, nq, qw), lambda u: (prev(u) // nt, prev(u) % nt, 0)),
        out_shape=jax.ShapeDtypeStruct((b, n, qw), BF16),
        scratch_shapes=[pltpu.VMEM((SWA_KV_HEADS, (qb + 2) * blk + c, SWA_GROUP * nq), F32),
                        pltpu.VMEM((SWA_KV_HEADS, 1, SWA_GROUP * nq), F32)],
        compiler_params=_params(1),
        name="swa_attention",
    )(sink, proj_l, proj_l, proj_c, proj_l, proj_c, cos, sin)


def _log_sigmoid(x):
    return jnp.minimum(x, 0.0) - jnp.log(1.0 + jnp.exp(-jnp.abs(x)))


def _gla_block_constants(rows):
    t = np.arange(rows)
    ti, tj = t[:, None], t[None, :]
    ci, cj = ti // GLA_CHUNK, tj // GLA_CHUNK
    ones, sel = [], []
    for reverse in (False, True):
        upto = (tj >= ti) if reverse else (tj <= ti)
        visible = (tj > ti) if reverse else (tj <= ti)
        earlier = (cj > ci) if reverse else (cj < ci)
        ones.append((ci == cj) & upto)
        sel.append(np.where((ci == cj) & visible, 1, np.where(earlier, 2, 0)))
    return jnp.asarray(np.stack(ones), BF16), jnp.asarray(np.stack(sel), jnp.int32)


def _gla_factors(q_ref, k_ref, misc_ref, i, wg, bg, ones, sel, reverse):
    rows = q_ref.shape[1]
    L = GLA_CHUNK
    assert rows == 4 * L
    width = GLA_HEADS * GLA_DK
    la = _log_sigmoid(_dot(misc_ref[i].astype(BF16), wg) + bg) * (1.0 / GLA_TAU)
    hi = la.astype(BF16)
    lo = (la - hi.astype(F32)).astype(BF16)
    sums = _dot(ones, jnp.concatenate([hi, lo], axis=1))
    b_in = sums[:, :width] + sums[:, width:]
    chunks = [b_in[c * L:(c + 1) * L] for c in range(rows // L)]
    order = list(reversed(range(len(chunks)))) if reverse else list(range(len(chunks)))
    offset = None
    for c in order:
        total = chunks[c][0:1] if reverse else chunks[c][L - 1:L]
        if offset is not None:
            chunks[c] = chunks[c] + offset
        offset = total if offset is None else offset + total
    b_abs = jnp.concatenate(chunks, axis=0)
    if reverse:
        g_mid, g_end = b_abs[rows // 2:rows // 2 + 1], b_abs[0:1]
    else:
        g_mid, g_end = b_abs[rows // 2 - 1:rows // 2], b_abs[rows - 1:rows]
    diag, cross = sel == 1, sel == 2
    q = q_ref[i].astype(F32) * GLA_DK ** -0.5
    k = k_ref[i].astype(F32)
    q_d, k_d = q * jnp.exp(b_in), k * jnp.exp(-b_in)
    q_x, k_x = q * jnp.exp(b_abs - g_mid), k * jnp.exp(g_mid - b_abs)
    q_s, k_s = q * jnp.exp(b_abs), k * jnp.exp(g_end - b_abs)
    decay = jnp.exp(g_end)
    return q_d, k_d, q_x, k_x, q_s, k_s, decay, diag, cross


def _gla_apply(factors, v_ref, state_ref, o_ref, i):
    q_d, k_d, q_x, k_x, q_s, k_s, decay, diag, cross = factors
    rows = q_d.shape[0]
    lane = lax.broadcasted_iota(jnp.int32, (rows, 2 * GLA_DK), 1)
    lane_s = lax.broadcasted_iota(jnp.int32, (GLA_DV, 2 * GLA_DK), 1)
    for pair in range(GLA_HEADS // 2):
        ps = slice(2 * GLA_DK * pair, 2 * GLA_DK * (pair + 1))
        kd_p, kx_p, ks_p = k_d[:, ps].astype(BF16), k_x[:, ps].astype(BF16), k_s[:, ps].astype(BF16)
        st = state_ref[i, pair]
        st_b = st.astype(BF16)
        ds = []
        for j in range(2):
            h = 2 * pair + j
            vs = slice(h * GLA_DV, (h + 1) * GLA_DV)
            mine = (lane // GLA_DK) == j
            qd_h = jnp.where(mine, q_d[:, ps], 0.0).astype(BF16)
            qx_h = jnp.where(mine, q_x[:, ps], 0.0).astype(BF16)
            qs_h = jnp.where(mine, q_s[:, ps], 0.0).astype(BF16)
            vh = v_ref[i, :, vs].astype(BF16)
            a = jnp.where(diag, _dot_nt(qd_h, kd_p), jnp.where(cross, _dot_nt(qx_h, kx_p), 0.0))
            o_ref[i, :, vs] = (_dot(a.astype(BF16), vh) + _dot_nt(qs_h, st_b)).astype(o_ref.dtype)
            ds.append(_dot_tn(vh, ks_p))
        state_ref[i, pair] = decay[:, ps] * st + jnp.where(lane_s < GLA_DK, ds[0], ds[1])


def _gla_kernel(qf_ref, kf_ref, vf_ref, mf_ref, qb_ref, kb_ref, vb_ref, mb_ref, wg_ref, bg_ref, ones_ref, sel_ref,
                sf0_ref, sb0_ref, of_ref, ob_ref, sf_ref, sb_ref, stf, stb):
    t = pl.program_id(1)

    @pl.when(t == 0)
    def _():
        stf[...] = sf0_ref[...]
        stb[...] = sb0_ref[...]

    width = GLA_HEADS * GLA_DK
    entries = range(qf_ref.shape[0])
    fwd = [_gla_factors(qf_ref, kf_ref, mf_ref, i, wg_ref[:, :width], bg_ref[:, :width], ones_ref[0], sel_ref[0],
                        False) for i in entries]
    bwd = [_gla_factors(qb_ref, kb_ref, mb_ref, i, wg_ref[:, width:], bg_ref[:, width:], ones_ref[1], sel_ref[1],
                        True) for i in entries]
    for i in entries:
        _gla_apply(fwd[i], vf_ref, stf, of_ref, i)
        _gla_apply(bwd[i], vb_ref, stb, ob_ref, i)

    @pl.when(t == pl.num_programs(1) - 1)
    def _():
        sf_ref[...] = stf[...]
        sb_ref[...] = stb[...]


def _gla_scan(proj, wg, bg, sf0, sb0, layer):
    b, r, _ = proj.shape
    tm = 256
    nt = r // tm
    kw, vw = GLA_HEADS * GLA_DK, GLA_HEADS * GLA_DV
    fwd = lambda t: t
    bwd = lambda t: nt - 1 - t

    nb = next(k for k in (4, 2, 1) if b % k == 0)
    ones, sel = _gla_block_constants(tm)

    def specs(order):
        return [pl.BlockSpec((nb, tm, kw), lambda i, t: (i, order(t), P_GLA_Q // kw)),
                pl.BlockSpec((nb, tm, kw), lambda i, t: (i, order(t), P_GLA_K // kw)),
                pl.BlockSpec((nb, tm, vw), lambda i, t: (i, order(t), P_GLA_V // vw)),
                pl.BlockSpec((nb, tm, 128), lambda i, t: (i, order(t), P_MISC // 128))]

    state_spec = pl.BlockSpec((nb,) + GLA_STATE, lambda i, t: (i, 0, 0, 0))
    state_shape = jax.ShapeDtypeStruct((b,) + GLA_STATE, F32)
    return pl.pallas_call(
        _gla_kernel,
        grid=(b // nb, nt),
        in_specs=specs(fwd) + specs(bwd) + [
            pl.BlockSpec((None,) + wg.shape[1:], lambda i, t: (layer, 0, 0)),
            pl.BlockSpec((None,) + bg.shape[1:], lambda i, t: (layer, 0, 0)),
            pl.BlockSpec(ones.shape, lambda i, t: (0, 0, 0)),
            pl.BlockSpec(sel.shape, lambda i, t: (0, 0, 0)),
            state_spec, state_spec],
        out_specs=[pl.BlockSpec((nb, tm, vw), lambda i, t: (i, fwd(t), 0)),
                   pl.BlockSpec((nb, tm, vw), lambda i, t: (i, bwd(t), 0)),
                   state_spec, state_spec],
        out_shape=[jax.ShapeDtypeStruct((b, r, vw), BF16), jax.ShapeDtypeStruct((b, r, vw), BF16),
                   state_shape, state_shape],
        scratch_shapes=[pltpu.VMEM((nb,) + GLA_STATE, F32), pltpu.VMEM((nb,) + GLA_STATE, F32)],
        compiler_params=_params(2),
        name="gla_scan",
    )(proj, proj, proj, proj, proj, proj, proj, proj, wg, bg, ones, sel, sf0, sb0)


def _outproj_kernel(x_ref, mod_ref, mla_ref, swa_ref, of_ref, ob_ref, r_ref, g_ref, w_ref, o_ref):
    o = of_ref[0].astype(F32) + ob_ref[0].astype(F32)
    gate = _silu(r_ref[0].astype(F32))
    parts = []
    for h in range(GLA_HEADS):
        vs = slice(h * GLA_DV, (h + 1) * GLA_DV)
        parts.append((_rms(o[:, vs], g_ref[:, vs]) * gate[:, vs]).astype(BF16))
    mix = _dot(jnp.concatenate([mla_ref[0], swa_ref[0]] + parts, axis=1), w_ref[...])
    o_ref[0] = x_ref[0] + mod_ref[0, 2:3, :] * mix


def _out_projection(x, mod, mla, swa, o_f, o_b, proj, g_out, w_out, layer):
    bx, r, d = x.shape
    tm = 512
    row = lambda width: pl.BlockSpec((1, tm, width), lambda b, t: (b, t, 0))
    gw = GLA_HEADS * GLA_DV
    return pl.pallas_call(
        _outproj_kernel,
        grid=(bx, r // tm),
        in_specs=[row(d),
                  pl.BlockSpec((1, N_MOD, d), lambda b, t: (b, 0, 0)),
                  row(mla.shape[2]), row(swa.shape[2]), row(gw), row(gw),
                  pl.BlockSpec((1, tm, gw), lambda b, t: (b, t, P_GLA_R // gw)),
                  pl.BlockSpec((1, gw), lambda b, t: (0, 0)),
                  pl.BlockSpec((None,) + w_out.shape[1:], lambda b, t: (layer, 0, 0))],
        out_specs=row(d),
        out_shape=jax.ShapeDtypeStruct(x.shape, F32),
        compiler_params=_params(2),
        name="out_projection",
    )(x, mod, mla, swa, o_f, o_b, proj, g_out, w_out)


def _ffn_kernel(x_ref, mod_ref, g_ref, wg_ref, wu_ref, wd_ref, gf_ref, o_ref, h_ref, *, final_norm):
    f = pl.program_id(2)
    row_chunks = [slice(r, r + ROW_CHUNK) for r in range(0, x_ref.shape[1], ROW_CHUNK)]
    chunk = (ROW_CHUNK, x_ref.shape[2])

    def gated_partial():
        h = h_ref[...]
        act = _silu(_dot(h, wg_ref[...])) * _dot(h, wu_ref[...])
        return mod_ref[0, 5:6, :] * _dot(act.astype(BF16), wd_ref[...])

    @pl.when(f == 0)
    def _():
        gain = jnp.broadcast_to(g_ref[...] * (1.0 + mod_ref[0, 4:5, :]), chunk)
        shift = jnp.broadcast_to(mod_ref[0, 3:4, :], chunk)
        for rows in row_chunks:
            x = x_ref[0, rows, :]
            rs = lax.rsqrt(jnp.mean(x * x, axis=-1, keepdims=True) + EPS)
            h_ref[rows, :] = (x * rs * gain + shift).astype(BF16)
        o_ref[0] = x_ref[0] + gated_partial()

    @pl.when(f > 0)
    def _():
        o_ref[0] += gated_partial()

    if final_norm:
        @pl.when(f == pl.num_programs(2) - 1)
        def _():
            g_fin = jnp.broadcast_to(gf_ref[...], chunk)
            for rows in row_chunks:
                y = o_ref[0, rows, :]
                o_ref[0, rows, :] = y * lax.rsqrt(jnp.mean(y * y, axis=-1, keepdims=True) + EPS) * g_fin


def _ffn(x, mod, g, w_gu, w_down, g_final, final_norm, layer):
    bx, r, d = x.shape
    hidden = w_down.shape[1]
    tm, tf = min(r, 1024), 512
    nf = hidden // tf
    return pl.pallas_call(
        functools.partial(_ffn_kernel, final_norm=final_norm),
        grid=(bx, r // tm, nf),
        in_specs=[pl.BlockSpec((1, tm, d), lambda b, t, f: (b, t, 0)),
                  pl.BlockSpec((1, N_MOD, d), lambda b, t, f: (b, 0, 0)),
                  pl.BlockSpec((1, d), lambda b, t, f: (0, 0)),
                  pl.BlockSpec((None, d, tf), lambda b, t, f: (layer, 0, f)),
                  pl.BlockSpec((None, d, tf), lambda b, t, f: (layer, 0, nf + f)),
                  pl.BlockSpec((None, tf, d), lambda b, t, f: (layer, f, 0)),
                  pl.BlockSpec((1, d), lambda b, t, f: (0, 0))],
        out_specs=pl.BlockSpec((1, tm, d), lambda b, t, f: (b, t, 0)),
        out_shape=jax.ShapeDtypeStruct(x.shape, F32),
        scratch_shapes=[pltpu.VMEM((tm, d), BF16)],
        compiler_params=_params(3),
        name="ffn",
    )(x, mod, g, w_gu, w_gu, w_down, g_final)


def _rope_tables(n, dim):
    half = dim // 4
    freqs = ROPE_THETA ** (-jnp.arange(half, dtype=F32) / half)
    pos = jnp.arange(n, dtype=jnp.int32)
    ang_r = (pos // GRID_W).astype(F32)[:, None] * freqs
    ang_c = (pos % GRID_W).astype(F32)[:, None] * freqs
    cos = jnp.concatenate([jnp.cos(ang_r)] * 2 + [jnp.cos(ang_c)] * 2, axis=1)
    sin = jnp.concatenate([-jnp.sin(ang_r), jnp.sin(ang_r), -jnp.sin(ang_c), jnp.sin(ang_c)], axis=1)
    reps = 128 // dim
    return jnp.tile(cos, (1, reps)), jnp.tile(sin, (1, reps))


def _cast_kernel(w_ref, o_ref):
    o_ref[...] = w_ref[...].astype(o_ref.dtype)


def _to_bf16(w):
    depth, rows, cols = w.shape
    tr = max(16, min(rows, (8 * 1024 * 1024) // (4 * cols) // 16 * 16))
    while rows % tr:
        tr -= 16
    return pl.pallas_call(
        _cast_kernel,
        grid=(depth, rows // tr),
        in_specs=[pl.BlockSpec((None, tr, cols), lambda l, t: (l, t, 0))],
        out_specs=pl.BlockSpec((None, tr, cols), lambda l, t: (l, t, 0)),
        out_shape=jax.ShapeDtypeStruct(w.shape, BF16),
        compiler_params=_params(2),
        name="weight_cast",
    )(w)


def _relayout_w_in(w_in):
    depth, d, _ = w_in.shape
    sizes = (MLA_RANK, MLA_RANK, MLA_ROPE, SWA_HEADS * SWA_DIM, SWA_KV_HEADS * SWA_DIM, SWA_KV_HEADS * SWA_DIM,
             GLA_HEADS * GLA_DK, GLA_HEADS * GLA_DK, GLA_HEADS * GLA_DV, 2 * GLA_GATE_RANK, GLA_HEADS * GLA_DV)
    offs = [0]
    for s in sizes:
        offs.append(offs[-1] + s)
    runs = [(3, 5), (0, 2), (5, 6), (2, 3), (9, 10), P_GLA_Q - P_MISC - MLA_ROPE - 2 * GLA_GATE_RANK, (6, 9), (10, 11)]

    def body(w_ref, o_ref):
        dst = 0
        for run in runs:
            if isinstance(run, int):
                o_ref[:, dst:dst + run] = jnp.zeros((o_ref.shape[0], run), o_ref.dtype)
                dst += run
            else:
                src, width = offs[run[0]], offs[run[1]] - offs[run[0]]
                o_ref[:, dst:dst + width] = w_ref[:, src:src + width].astype(o_ref.dtype)
                dst += width
        assert dst == P_WIDTH

    tr = 256
    return pl.pallas_call(
        body,
        grid=(depth * d // tr,),
        in_specs=[pl.BlockSpec((tr, w_in.shape[2]), lambda t: (t, 0))],
        out_specs=pl.BlockSpec((tr, P_WIDTH), lambda t: (t, 0)),
        out_shape=jax.ShapeDtypeStruct((depth * d, P_WIDTH), BF16),
        compiler_params=_params(1),
        name="w_in_relayout",
    )(w_in.reshape(depth * d, w_in.shape[2])).reshape(depth, d, P_WIDTH)


def _relayout_mla(w_uq, w_ukv):
    depth, rk, _ = w_uq.shape
    uq = w_uq.reshape(depth, rk, MLA_HEADS, MLA_QK)
    uq = jnp.concatenate([uq[..., :MLA_NOPE].reshape(depth, rk, -1), uq[..., MLA_NOPE:].reshape(depth, rk, -1)], -1)
    ukv = w_ukv.reshape(depth, rk, MLA_HEADS, MLA_NOPE + MLA_V)
    ukv = jnp.concatenate([ukv[..., :MLA_NOPE].reshape(depth, rk, -1), ukv[..., MLA_NOPE:].reshape(depth, rk, -1)], -1)
    return uq.astype(BF16), ukv.astype(BF16)


def _relayout_gate(w_f, b_f, w_b, b_b):
    depth, rank, width = w_f.shape
    wg = jnp.zeros((depth, 128, 2 * width), F32)
    wg = wg.at[:, MISC_GATE_OFF:MISC_GATE_OFF + rank, :width].set(w_f)
    wg = wg.at[:, MISC_GATE_OFF + rank:MISC_GATE_OFF + 2 * rank, width:].set(w_b)
    bg = jnp.concatenate([b_f, b_b], axis=-1).reshape(depth, 1, 2 * width)
    return wg.astype(BF16), bg


def kernel(x, c, ctx, c_ctx, w_mod, b_mod, g_mix, g_ffn, w_in, g_mla_q, g_mla_kv, w_mla_uq, w_mla_ukv,
           swa_sink, w_gla_gate_f, b_gla_gate_f, w_gla_gate_b, b_gla_gate_b, g_gla_out, w_out, w_ffn_gu,
           w_ffn_down, g_final):
    B, N, D = x.shape
    C = ctx.shape[1]
    depth = w_mod.shape[0]

    cvec = jnp.concatenate([c, c_ctx[None, :], jnp.zeros((16 - B - 1, D), F32)], axis=0)
    mod = _modulation(cvec, w_mod, b_mod).reshape(depth, 16, N_MOD, D)

    w_in_p = _relayout_w_in(w_in)
    w_uq_p, w_ukv_p = _relayout_mla(w_mla_uq, w_mla_ukv)
    wg_p, bg_p = _relayout_gate(w_gla_gate_f, b_gla_gate_f, w_gla_gate_b, b_gla_gate_b)
    w_out_b = _to_bf16(w_out)
    w_gu_b = _to_bf16(w_ffn_gu)
    w_down_b = _to_bf16(w_ffn_down)

    cos_m, sin_m = _rope_tables(N, MLA_ROPE)
    cos_s, sin_s = _rope_tables(N, SWA_DIM)
    cos_id, sin_id = jnp.ones((C, 128), F32), jnp.zeros((C, 128), F32)
    state0 = jnp.zeros((B,) + GLA_STATE, F32)
    g_fin = g_final.reshape(1, D)

    xc = ctx.reshape(1, B * C, D)
    for l in range(depth):
        last = l == depth - 1
        mod_l, mod_c = mod[l, :B], mod[l, B:B + 1]
        g_mix_l, g_ffn_l = g_mix[l].reshape(1, D), g_ffn[l].reshape(1, D)
        g_q, g_kv = g_mla_q[l].reshape(1, -1), g_mla_kv[l].reshape(1, -1)
        g_out = g_gla_out[l].reshape(1, -1)
        sink = swa_sink[l].reshape(1, -1)

        proj_l = _in_projection(x, mod_l, g_mix_l, w_in_p, l)
        proj_c = _in_projection(xc, mod_c, g_mix_l, w_in_p, l).reshape(B, C, P_WIDTH)

        q_l, k_l, v_l = _mla_prep(proj_l, cos_m, sin_m, g_q, g_kv, w_uq_p, w_ukv_p, l)
        q_c, k_c, v_c = _mla_prep(proj_c, cos_id, sin_id, g_q, g_kv, w_uq_p, w_ukv_p, l)
        mla_l = _mla_attention(q_l, k_l, v_l, k_c, v_c)
        swa_l = _swa_attention(proj_l, proj_c, cos_s, sin_s, sink)

        of_c, ob_c, s_f, s_b = _gla_scan(proj_c, wg_p, bg_p, state0, state0, l)
        of_l, ob_l, _, _ = _gla_scan(proj_l, wg_p, bg_p, s_f, s_b, l)

        x = _out_projection(x, mod_l, mla_l, swa_l, of_l, ob_l, proj_l, g_out, w_out_b, l)
        x = _ffn(x, mod_l, g_ffn_l, w_gu_b, w_down_b, g_fin, last, l)

        if not last:
            mla_c = _mla_ctx_attention(q_c, k_c, v_c)
            swa_c = _swa_ctx_attention(proj_c, sink)
            flat = lambda a: a.reshape(1, B * C, a.shape[-1])
            xc = _out_projection(xc, mod_c, flat(mla_c), flat(swa_c), flat(of_c), flat(ob_c), flat(proj_c),
                                 g_out, w_out_b, l)
            xc = _ffn(xc, mod_c, g_ffn_l, w_gu_b, w_down_b, g_fin, False, l)
    return x
```

```python
import functools

import jax
import jax.numpy as jnp
import numpy as np
from jax import lax
from jax.experimental import pallas as pl
from jax.experimental.pallas import tpu as pltpu

F32 = jnp.float32
BF16 = jnp.bfloat16

GRID_W = 64
EPS = 1e-6
ROPE_THETA = 10000.0
LOG2E = 1.4426950408889634

MLA_HEADS = 6
MLA_RANK = 512
MLA_NOPE = 128
MLA_ROPE = 64
MLA_V = 128
MLA_QK = MLA_NOPE + MLA_ROPE
MLA_VA = MLA_V + 16

SWA_HEADS = 6
SWA_KV_HEADS = 2
SWA_GROUP = SWA_HEADS // SWA_KV_HEADS
SWA_DIM = 128
SWA_BLOCK = 128

GLA_HEADS = 4
GLA_DK = 64
GLA_DV = 128
GLA_GATE_RANK = 16
GLA_TAU = 16.0
GLA_CHUNK = 64
GLA_STATE = (GLA_HEADS // 2, GLA_DV, 2 * GLA_DK)

N_MOD = 6

P_SWA_Q = 0
P_SWA_K = 768
P_CQ = 1024
P_CKV = 1536
P_SWA_V = 2048
P_MISC = 2304
P_GLA_Q = 2560
P_GLA_K = 2816
P_GLA_V = 3072
P_GLA_R = 3584
P_WIDTH = 4096
MISC_GATE_OFF = MLA_ROPE

VMEM_LIMIT = 56 * 1024 * 1024
ROW_CHUNK = 16


def _params(n_axes, flags=None):
    return pltpu.CompilerParams(dimension_semantics=("arbitrary",) * n_axes,
                                vmem_limit_bytes=VMEM_LIMIT, flags=flags)


def _silu(x):
    return x / (1.0 + jnp.exp(-x))


def _rms(x, g):
    ms = jnp.mean(x * x, axis=-1, keepdims=True)
    return x * lax.rsqrt(ms + EPS) * g


def _dot(a, b):
    return jnp.dot(a, b, preferred_element_type=F32)


def _dot_nt(a, b):
    return lax.dot_general(a, b, (((1,), (1,)), ((), ())), preferred_element_type=F32)


def _dot_tn(a, b):
    return lax.dot_general(a, b, (((0,), (0,)), ((), ())), preferred_element_type=F32)


def _mod_kernel(c_ref, w_ref, b_ref, o_ref):
    a = _silu(c_ref[...]).astype(BF16)
    o_ref[0] = _dot(a, w_ref[0].astype(BF16)) + b_ref[0]


def _modulation(cvec, w_mod, b_mod):
    depth, d, width = w_mod.shape
    rows = cvec.shape[0]
    tn = 1024
    return pl.pallas_call(
        _mod_kernel,
        grid=(depth, width // tn),
        in_specs=[pl.BlockSpec((rows, d), lambda l, j: (0, 0)),
                  pl.BlockSpec((1, d, tn), lambda l, j: (l, 0, j)),
                  pl.BlockSpec((1, 1, tn), lambda l, j: (l, 0, j))],
        out_specs=pl.BlockSpec((1, rows, tn), lambda l, j: (l, 0, j)),
        out_shape=jax.ShapeDtypeStruct((depth, rows, width), F32),
        compiler_params=_params(2),
        name="modulation",
    )(cvec, w_mod, b_mod.reshape(depth, 1, width))


def _inproj_kernel(x_ref, mod_ref, g_ref, w_ref, o_ref):
    h = _rms(x_ref[0], g_ref[...]) * (1.0 + mod_ref[0, 1:2, :]) + mod_ref[0, 0:1, :]
    o_ref[0] = _dot(h.astype(BF16), w_ref[...]).astype(o_ref.dtype)


def _in_projection(x, mod, g, w, layer):
    bx, r, d = x.shape
    tm, tn = 512, P_WIDTH
    return pl.pallas_call(
        _inproj_kernel,
        grid=(P_WIDTH // tn, bx, r // tm),
        in_specs=[pl.BlockSpec((1, tm, d), lambda j, b, t: (b, t, 0)),
                  pl.BlockSpec((1, N_MOD, d), lambda j, b, t: (b, 0, 0)),
                  pl.BlockSpec((1, d), lambda j, b, t: (0, 0)),
                  pl.BlockSpec((None, d, tn), lambda j, b, t: (layer, 0, j), pipeline_mode=pl.Buffered(1))],
        out_specs=pl.BlockSpec((1, tm, tn), lambda j, b, t: (b, t, j)),
        out_shape=jax.ShapeDtypeStruct((bx, r, P_WIDTH), BF16),
        compiler_params=_params(3),
        name="in_projection",
    )(x, mod, g, w)


def _rope(x, cos, sin, half):
    lane = lax.broadcasted_iota(jnp.int32, x.shape, 1)
    first = (lane % (2 * half)) < half
    width = x.shape[1]
    rot = jnp.where(first, pltpu.roll(x, width - half, 1), pltpu.roll(x, half, 1))
    return x * cos + rot * sin


def _mla_prep_kernel(cq_ref, ckv_ref, misc_ref, cos_ref, sin_ref, gq_ref, gkv_ref, wuq_ref, wukv_ref,
                     qt_ref, k_ref, vt_ref):
    scale = MLA_QK ** -0.5 * LOG2E
    cos, sin = cos_ref[...], sin_ref[...]
    half = MLA_ROPE // 4
    qf = _dot(_rms(cq_ref[0].astype(F32), gq_ref[...]).astype(BF16), wuq_ref[...])
    kvf = _dot(_rms(ckv_ref[0].astype(F32), gkv_ref[...]).astype(BF16), wukv_ref[...])
    kr = _rope(misc_ref[0].astype(F32), cos, sin, half)[:, :MLA_ROPE].astype(BF16)
    nope_w = MLA_HEADS * MLA_NOPE
    for pair in range(MLA_HEADS // 2):
        qr = _rope(qf[:, nope_w + 128 * pair: nope_w + 128 * (pair + 1)], cos, sin, half) * scale
        qrt = qr.astype(BF16).T
        for j in range(2):
            qt_ref[0, 2 * pair + j, MLA_NOPE:MLA_QK, :] = qrt[MLA_ROPE * j: MLA_ROPE * (j + 1), :]
    for h in range(MLA_HEADS):
        qt_ref[0, h, 0:MLA_NOPE, :] = (qf[:, MLA_NOPE * h: MLA_NOPE * (h + 1)] * scale).astype(BF16).T
        k_ref[0, h, :, 0:MLA_NOPE] = kvf[:, MLA_NOPE * h: MLA_NOPE * (h + 1)].astype(BF16)
        k_ref[0, h, :, MLA_NOPE:MLA_QK] = kr
        vt_ref[0, h, 0:MLA_V, :] = kvf[:, nope_w + MLA_V * h: nope_w + MLA_V * (h + 1)].astype(BF16).T
        vt_ref[0, h, MLA_V:, :] = jnp.ones((MLA_VA - MLA_V, vt_ref.shape[3]), BF16)


def _mla_prep(proj, cos, sin, g_q, g_kv, w_uq, w_ukv, layer):
    b, r, _ = proj.shape
    tm = min(r, 1024)
    rk = MLA_RANK
    const = lambda shape: pl.BlockSpec(shape, lambda i, t: (0,) * len(shape))
    stacked = lambda w: pl.BlockSpec((None,) + w.shape[1:], lambda i, t: (layer, 0, 0))
    return pl.pallas_call(
        _mla_prep_kernel,
        grid=(b, r // tm),
        in_specs=[pl.BlockSpec((1, tm, rk), lambda i, t: (i, t, P_CQ // rk)),
                  pl.BlockSpec((1, tm, rk), lambda i, t: (i, t, P_CKV // rk)),
                  pl.BlockSpec((1, tm, 128), lambda i, t: (i, t, P_MISC // 128)),
                  pl.BlockSpec((tm, 128), lambda i, t: (t, 0)),
                  pl.BlockSpec((tm, 128), lambda i, t: (t, 0)),
                  const((1, rk)), const((1, rk)),
                  stacked(w_uq), stacked(w_ukv)],
        out_specs=[pl.BlockSpec((1, MLA_HEADS, MLA_QK, tm), lambda i, t: (i, 0, 0, t)),
                   pl.BlockSpec((1, MLA_HEADS, tm, MLA_QK), lambda i, t: (i, 0, t, 0)),
                   pl.BlockSpec((1, MLA_HEADS, MLA_VA, tm), lambda i, t: (i, 0, 0, t))],
        out_shape=[jax.ShapeDtypeStruct((b, MLA_HEADS, MLA_QK, r), BF16),
                   jax.ShapeDtypeStruct((b, MLA_HEADS, r, MLA_QK), BF16),
                   jax.ShapeDtypeStruct((b, MLA_HEADS, MLA_VA, r), BF16)],
        compiler_params=_params(2),
        name="mla_prep",
    )(proj, proj, proj, cos, sin, g_q, g_kv, w_uq, w_ukv)


def _softmax_pv(s_parts, vt_parts):
    m = functools.reduce(jnp.maximum, [jnp.max(s, axis=0, keepdims=True) for s in s_parts])
    acc = sum(_dot(vt, jnp.exp2(s - m).astype(BF16)) for s, vt in zip(s_parts, vt_parts))
    return (acc[:MLA_V] / acc[MLA_V:MLA_V + 1]).T


def _mla_attn_kernel(qt_ref, kl_ref, kc_ref, vlt_ref, vct_ref, o_ref, s_scr, m_scr, *, tk):
    @pl.when(pl.program_id(0) == 0)
    def _():
        s_scr[...] = jnp.zeros_like(s_scr)
        m_scr[...] = jnp.zeros_like(m_scr)

    n_lat = kl_ref.shape[2] // tk
    qt = qt_ref[0, 0]
    tq = qt.shape[1]
    m_prev = m_scr[...]
    m_new = jnp.full((1, tq), -jnp.inf, F32)
    acc = jnp.zeros((MLA_VA, tq), F32)
    for j in range(n_lat + 1):
        rows = slice(j * tk, (j + 1) * tk)
        p = jnp.exp2(s_scr[rows, :] - m_prev).astype(BF16)
        k = kc_ref[0, 0] if j == n_lat else kl_ref[0, 0, rows, :]
        vt = vct_ref[0, 0] if j == n_lat else vlt_ref[0, 0, :, rows]
        s = _dot(k, qt)
        s_scr[rows, :] = s
        m_new = jnp.maximum(m_new, jnp.max(s, axis=0, keepdims=True))
        acc = acc + _dot(vt, p)
    m_scr[...] = m_new
    o_ref[0] = (acc[:MLA_V] / acc[MLA_V:MLA_V + 1]).T.astype(o_ref.dtype)


def _mla_attention(qt, k_l, vt_l, k_c, vt_c):
    b, h, dq, n = qt.shape
    c = k_c.shape[2]
    tq = min(n, 2048)
    nt = n // tq
    tiles = b * h * nt

    def cur(u):
        t = jnp.minimum(u, tiles - 1)
        return t // (h * nt), (t // nt) % h, t % nt

    def prev(u):
        t = jnp.maximum(u - 1, 0)
        return t // (h * nt), (t // nt) % h, t % nt

    assert n % c == 0
    return pl.pallas_call(
        functools.partial(_mla_attn_kernel, tk=c),
        grid=(tiles + 1,),
        in_specs=[pl.BlockSpec((1, 1, dq, tq), lambda u: (cur(u)[0], cur(u)[1], 0, cur(u)[2])),
                  pl.BlockSpec((1, 1, n, dq), lambda u: (cur(u)[0], cur(u)[1], 0, 0)),
                  pl.BlockSpec((1, 1, c, dq), lambda u: (cur(u)[0], cur(u)[1], 0, 0)),
                  pl.BlockSpec((1, 1, MLA_VA, n), lambda u: (prev(u)[0], prev(u)[1], 0, 0)),
                  pl.BlockSpec((1, 1, MLA_VA, c), lambda u: (prev(u)[0], prev(u)[1], 0, 0))],
        out_specs=pl.BlockSpec((1, tq, MLA_V), lambda u: (prev(u)[0], prev(u)[2], prev(u)[1])),
        out_shape=jax.ShapeDtypeStruct((b, n, h * MLA_V), BF16),
        scratch_shapes=[pltpu.VMEM((n + c, tq), F32), pltpu.VMEM((1, tq), F32)],
        compiler_params=_params(1),
        name="mla_attention",
    )(qt, k_l, k_c, vt_l, vt_c)


def _mla_ctx_kernel(qt_ref, k_ref, vt_ref, o_ref):
    for h in range(qt_ref.shape[1]):
        o_ref[0, :, MLA_V * h: MLA_V * (h + 1)] = _softmax_pv(
            [_dot(k_ref[0, h], qt_ref[0, h])], [vt_ref[0, h]]).astype(o_ref.dtype)


def _mla_ctx_attention(qt, k, vt):
    b, h, dq, c = qt.shape
    return pl.pallas_call(
        _mla_ctx_kernel,
        grid=(b,),
        in_specs=[pl.BlockSpec((1, h, dq, c), lambda i: (i, 0, 0, 0)),
                  pl.BlockSpec((1, h, c, dq), lambda i: (i, 0, 0, 0)),
                  pl.BlockSpec((1, h, MLA_VA, c), lambda i: (i, 0, 0, 0))],
        out_specs=pl.BlockSpec((1, c, h * MLA_V), lambda i: (i, 0, 0)),
        out_shape=jax.ShapeDtypeStruct((b, c, h * MLA_V), BF16),
        compiler_params=_params(1),
        name="mla_ctx_attention",
    )(qt, k, vt)


def _swa_ctx_kernel(sink_ref, q_ref, k_ref, v_ref, o_ref):
    d = SWA_DIM
    for h in range(SWA_HEADS):
        g = h // SWA_GROUP
        q = (q_ref[0, :, d * h: d * (h + 1)].astype(F32) * d ** -0.5).astype(BF16)
        s = _dot_nt(q, k_ref[0, :, d * g: d * (g + 1)].astype(BF16))
        sink = sink_ref[0, h]
        m = jnp.maximum(jnp.max(s, axis=-1, keepdims=True), sink)
        p = jnp.exp(s - m)
        l = jnp.sum(p, axis=-1, keepdims=True) + jnp.exp(sink - m)
        o = _dot(p.astype(BF16), v_ref[0, :, d * g: d * (g + 1)].astype(BF16)) / l
        o_ref[0, :, d * h: d * (h + 1)] = o.astype(o_ref.dtype)


def _swa_ctx_attention(proj_c, sink):
    b, c, _ = proj_c.shape
    qw, kw = SWA_HEADS * SWA_DIM, SWA_KV_HEADS * SWA_DIM
    return pl.pallas_call(
        _swa_ctx_kernel,
        grid=(b,),
        in_specs=[pl.BlockSpec(memory_space=pltpu.SMEM),
                  pl.BlockSpec((1, c, qw), lambda i: (i, 0, P_SWA_Q // qw)),
                  pl.BlockSpec((1, c, kw), lambda i: (i, 0, P_SWA_K // kw)),
                  pl.BlockSpec((1, c, kw), lambda i: (i, 0, P_SWA_V // kw))],
        out_specs=pl.BlockSpec((1, c, qw), lambda i: (i, 0, 0)),
        out_shape=jax.ShapeDtypeStruct((b, c, qw), BF16),
        compiler_params=_params(1),
        name="swa_ctx_attention",
    )(sink, proj_c, proj_c, proj_c)


def _swa_window_start(t, qb, nb):
    return pl.multiple_of(jnp.clip(t * qb - 1, 0, nb - (qb + 2)) * SWA_BLOCK, SWA_BLOCK)


def _swa_kernel(sink_ref, q_ref, k_ref, kc_ref, v_ref, vc_ref, cos_ref, sin_ref, o_ref, s_scr, m_scr, *,
                qb, tiles, nt):
    u = pl.program_id(0)

    @pl.when(u == 0)
    def _():
        s_scr[...] = jnp.zeros_like(s_scr)
        m_scr[...] = jnp.zeros_like(m_scr)

    blk, d = SWA_BLOCK, SWA_DIM
    half = d // 4
    nq = qb * blk
    nw = (qb + 2) * blk
    nb = k_ref.shape[1] // blk
    ck = kc_ref.shape[1]
    cur, prev = jnp.minimum(u, tiles - 1), jnp.maximum(u - 1, 0)
    t_cur, t_prev = cur % nt, prev % nt
    q_start = pl.multiple_of(t_cur * nq, blk)
    k_start = _swa_window_start(t_cur, qb, nb)
    v_start = _swa_window_start(t_prev, qb, nb)
    groups = range(SWA_KV_HEADS)

    def sink_row(g):
        return jnp.concatenate(
            [jnp.full((1, nq), sink_ref[0, g * SWA_GROUP + i] * LOG2E, F32) for i in range(SWA_GROUP)], axis=1)

    def values_t(v):
        return jnp.concatenate([v.astype(F32).T.astype(BF16), jnp.ones((16, v.shape[0]), BF16)], axis=0)

    scale = d ** -0.5 * LOG2E
    cos_q, sin_q = cos_ref[pl.ds(q_start, nq), :], sin_ref[pl.ds(q_start, nq), :]
    q = [jnp.concatenate(
        [(_rope(q_ref[0, :, d * h: d * (h + 1)].astype(F32), cos_q, sin_q, half) * scale).astype(BF16)
         for h in range(g * SWA_GROUP, (g + 1) * SWA_GROUP)], axis=0) for g in groups]
    key = lax.broadcasted_iota(jnp.int32, (ck, nq), 0)
    tok = lax.broadcasted_iota(jnp.int32, (ck, nq), 1)
    m_prev = [m_scr[g] for g in groups]
    m_new = [sink_row(g) for g in groups]
    acc = [jnp.zeros((d + 16, SWA_GROUP * nq), F32) for g in groups]
    n_win = nw // ck
    for j in range(n_win + 1):
        rows = slice(j * ck, (j + 1) * ck)
        if j < n_win:
            kr = pl.ds(k_start + j * ck, ck)
            cos_k, sin_k = cos_ref[kr, :], sin_ref[kr, :]
            dist = key - tok + (k_start + j * ck - q_start)
            valid = (dist <= blk) & (dist >= -blk)
        for g in groups:
            cols = slice(d * g, d * (g + 1))
            p = jnp.exp2(s_scr[g, rows, :] - m_prev[g]).astype(BF16)
            if j < n_win:
                vt = values_t(v_ref[0, pl.ds(v_start + j * ck, ck), cols])
                k = _rope(k_ref[0, kr, cols].astype(F32), cos_k, sin_k, half).astype(BF16)
                s = _dot_nt(k, q[g])
                s = jnp.concatenate(
                    [jnp.where(valid, s[:, nq * i: nq * (i + 1)], -jnp.inf) for i in range(SWA_GROUP)], axis=1)
            else:
                vt = values_t(vc_ref[0, :, cols])
                s = _dot_nt(kc_ref[0, :, cols].astype(BF16), q[g])
            s_scr[g, rows, :] = s
            m_new[g] = jnp.maximum(m_new[g], jnp.max(s, axis=0, keepdims=True))
            acc[g] = acc[g] + _dot(vt, p)
    for g in groups:
        m_scr[g] = m_new[g]
        out_t = acc[g][:d] / (acc[g][d:d + 1] + jnp.exp2(sink_row(g) - m_prev[g]))
        for i in range(SWA_GROUP):
            h = g * SWA_GROUP + i
            o_ref[0, :, d * h: d * (h + 1)] = out_t[:, nq * i: nq * (i + 1)].T.astype(o_ref.dtype)


def _swa_attention(proj_l, proj_c, cos, sin, sink):
    b, n, _ = proj_l.shape
    c = proj_c.shape[1]
    blk, d = SWA_BLOCK, SWA_DIM
    qb = 2
    nq = qb * blk
    nt = n // nq
    tiles = b * nt
    assert ((qb + 2) * blk) % c == 0 and n // blk >= qb + 2
    cur = lambda u: jnp.minimum(u, tiles - 1)
    prev = lambda u: jnp.maximum(u - 1, 0)
    qw, kw = SWA_HEADS * d, SWA_KV_HEADS * d
    return pl.pallas_call(
        functools.partial(_swa_kernel, qb=qb, tiles=tiles, nt=nt),
        grid=(tiles + 1,),
        in_specs=[pl.BlockSpec(memory_space=pltpu.SMEM),
                  pl.BlockSpec((1, nq, qw), lambda u: (cur(u) // nt, cur(u) % nt, P_SWA_Q // qw)),
                  pl.BlockSpec((1, n, kw), lambda u: (cur(u) // nt, 0, P_SWA_K // kw)),
                  pl.BlockSpec((1, c, kw), lambda u: (cur(u) // nt, 0, P_SWA_K // kw)),
                  pl.BlockSpec((1, n, kw), lambda u: (prev(u) // nt, 0, P_SWA_V // kw)),
                  pl.BlockSpec((1, c, kw), lambda u: (prev(u) // nt, 0, P_SWA_V // kw)),
                  pl.BlockSpec((n, d), lambda u: (0, 0)),
                  pl.BlockSpec((n, d), lambda u: (0, 0))],
        out_specs=pl.BlockSpec((1, nq, qw), lambda u: (prev(u) // nt, prev(u) % nt, 0)),
        out_shape=jax.ShapeDtypeStruct((b, n, qw), BF16),
        scratch_shapes=[pltpu.VMEM((SWA_KV_HEADS, (qb + 2) * blk + c, SWA_GROUP * nq), F32),
                        pltpu.VMEM((SWA_KV_HEADS, 1, SWA_GROUP * nq), F32)],
        compiler_params=_params(1),
        name="swa_attention",
    )(sink, proj_l, proj_l, proj_c, proj_l, proj_c, cos, sin)


def _log_sigmoid(x):
    return jnp.minimum(x, 0.0) - jnp.log(1.0 + jnp.exp(-jnp.abs(x)))


def _gla_block_constants(rows):
    t = np.arange(rows)
    ti, tj = t[:, None], t[None, :]
    ci, cj = ti // GLA_CHUNK, tj // GLA_CHUNK
    ones, sel = [], []
    for reverse in (False, True):
        upto = (tj >= ti) if reverse else (tj <= ti)
        visible = (tj > ti) if reverse else (tj <= ti)
        earlier = (cj > ci) if reverse else (cj < ci)
        ones.append((ci == cj) & upto)
        sel.append(np.where((ci == cj) & visible, 1, np.where(earlier, 2, 0)))
    return jnp.asarray(np.stack(ones), BF16), jnp.asarray(np.stack(sel), jnp.int32)


def _gla_factors(q_ref, k_ref, misc_ref, i, wg, bg, ones, sel, reverse):
    rows = q_ref.shape[1]
    L = GLA_CHUNK
    assert rows == 4 * L
    width = GLA_HEADS * GLA_DK
    la = _log_sigmoid(_dot(misc_ref[i].astype(BF16), wg) + bg) * (1.0 / GLA_TAU)
    hi = la.astype(BF16)
    lo = (la - hi.astype(F32)).astype(BF16)
    sums = _dot(ones, jnp.concatenate([hi, lo], axis=1))
    b_in = sums[:, :width] + sums[:, width:]
    chunks = [b_in[c * L:(c + 1) * L] for c in range(rows // L)]
    order = list(reversed(range(len(chunks)))) if reverse else list(range(len(chunks)))
    offset = None
    for c in order:
        total = chunks[c][0:1] if reverse else chunks[c][L - 1:L]
        if offset is not None:
            chunks[c] = chunks[c] + offset
        offset = total if offset is None else offset + total
    b_abs = jnp.concatenate(chunks, axis=0)
    if reverse:
        g_mid, g_end = b_abs[rows // 2:rows // 2 + 1], b_abs[0:1]
    else:
        g_mid, g_end = b_abs[rows // 2 - 1:rows // 2], b_abs[rows - 1:rows]
    diag, cross = sel == 1, sel == 2
    q = q_ref[i].astype(F32) * GLA_DK ** -0.5
    k = k_ref[i].astype(F32)
    q_d, k_d = q * jnp.exp(b_in), k * jnp.exp(-b_in)
    q_x, k_x = q * jnp.exp(b_abs - g_mid), k * jnp.exp(g_mid - b_abs)
    q_s, k_s = q * jnp.exp(b_abs), k * jnp.exp(g_end - b_abs)
    decay = jnp.exp(g_end)
    return q_d, k_d, q_x, k_x, q_s, k_s, decay, diag, cross


def _gla_apply(factors, v_ref, state_ref, o_ref, i):
    q_d, k_d, q_x, k_x, q_s, k_s, decay, diag, cross = factors
    rows = q_d.shape[0]
    lane = lax.broadcasted_iota(jnp.int32, (rows, 2 * GLA_DK), 1)
    lane_s = lax.broadcasted_iota(jnp.int32, (GLA_DV, 2 * GLA_DK), 1)
    for pair in range(GLA_HEADS // 2):
        ps = slice(2 * GLA_DK * pair, 2 * GLA_DK * (pair + 1))
        kd_p, kx_p, ks_p = k_d[:, ps].astype(BF16), k_x[:, ps].astype(BF16), k_s[:, ps].astype(BF16)
        st = state_ref[i, pair]
        st_b = st.astype(BF16)
        ds = []
        for j in range(2):
            h = 2 * pair + j
            vs = slice(h * GLA_DV, (h + 1) * GLA_DV)
            mine = (lane // GLA_DK) == j
            qd_h = jnp.where(mine, q_d[:, ps], 0.0).astype(BF16)
            qx_h = jnp.where(mine, q_x[:, ps], 0.0).astype(BF16)
            qs_h = jnp.where(mine, q_s[:, ps], 0.0).astype(BF16)
            vh = v_ref[i, :, vs].astype(BF16)
            a = jnp.where(diag, _dot_nt(qd_h, kd_p), jnp.where(cross, _dot_nt(qx_h, kx_p), 0.0))
            o_ref[i, :, vs] = (_dot(a.astype(BF16), vh) + _dot_nt(qs_h, st_b)).astype(o_ref.dtype)
            ds.append(_dot_tn(vh, ks_p))
        state_ref[i, pair] = decay[:, ps] * st + jnp.where(lane_s < GLA_DK, ds[0], ds[1])


def _gla_kernel(qf_ref, kf_ref, vf_ref, mf_ref, qb_ref, kb_ref, vb_ref, mb_ref, wg_ref, bg_ref, ones_ref, sel_ref,
                sf0_ref, sb0_ref, of_ref, ob_ref, sf_ref, sb_ref, stf, stb):
    t = pl.program_id(1)

    @pl.when(t == 0)
    def _():
        stf[...] = sf0_ref[...]
        stb[...] = sb0_ref[...]

    width = GLA_HEADS * GLA_DK
    entries = range(qf_ref.shape[0])
    fwd = [_gla_factors(qf_ref, kf_ref, mf_ref, i, wg_ref[:, :width], bg_ref[:, :width], ones_ref[0], sel_ref[0],
                        False) for i in entries]
    bwd = [_gla_factors(qb_ref, kb_ref, mb_ref, i, wg_ref[:, width:], bg_ref[:, width:], ones_ref[1], sel_ref[1],
                        True) for i in entries]
    for i in entries:
        _gla_apply(fwd[i], vf_ref, stf, of_ref, i)
        _gla_apply(bwd[i], vb_ref, stb, ob_ref, i)

    @pl.when(t == pl.num_programs(1) - 1)
    def _():
        sf_ref[...] = stf[...]
        sb_ref[...] = stb[...]


def _gla_scan(proj, wg, bg, sf0, sb0, layer):
    b, r, _ = proj.shape
    tm = 256
    nt = r // tm
    kw, vw = GLA_HEADS * GLA_DK, GLA_HEADS * GLA_DV
    fwd = lambda t: t
    bwd = lambda t: nt - 1 - t

    nb = next(k for k in (4, 2, 1) if b % k == 0)
    ones, sel = _gla_block_constants(tm)

    def specs(order):
        return [pl.BlockSpec((nb, tm, kw), lambda i, t: (i, order(t), P_GLA_Q // kw)),
                pl.BlockSpec((nb, tm, kw), lambda i, t: (i, order(t), P_GLA_K // kw)),
                pl.BlockSpec((nb, tm, vw), lambda i, t: (i, order(t), P_GLA_V // vw)),
                pl.BlockSpec((nb, tm, 128), lambda i, t: (i, order(t), P_MISC // 128))]

    state_spec = pl.BlockSpec((nb,) + GLA_STATE, lambda i, t: (i, 0, 0, 0))
    state_shape = jax.ShapeDtypeStruct((b,) + GLA_STATE, F32)
    return pl.pallas_call(
        _gla_kernel,
        grid=(b // nb, nt),
        in_specs=specs(fwd) + specs(bwd) + [
            pl.BlockSpec((None,) + wg.shape[1:], lambda i, t: (layer, 0, 0)),
            pl.BlockSpec((None,) + bg.shape[1:], lambda i, t: (layer, 0, 0)),
            pl.BlockSpec(ones.shape, lambda i, t: (0, 0, 0)),
            pl.BlockSpec(sel.shape, lambda i, t: (0, 0, 0)),
            state_spec, state_spec],
        out_specs=[pl.BlockSpec((nb, tm, vw), lambda i, t: (i, fwd(t), 0)),
                   pl.BlockSpec((nb, tm, vw), lambda i, t: (i, bwd(t), 0)),
                   state_spec, state_spec],
        out_shape=[jax.ShapeDtypeStruct((b, r, vw), BF16), jax.ShapeDtypeStruct((b, r, vw), BF16),
                   state_shape, state_shape],
        scratch_shapes=[pltpu.VMEM((nb,) + GLA_STATE, F32), pltpu.VMEM((nb,) + GLA_STATE, F32)],
        compiler_params=_params(2),
        name="gla_scan",
    )(proj, proj, proj, proj, proj, proj, proj, proj, wg, bg, ones, sel, sf0, sb0)


def _outproj_kernel(x_ref, mod_ref, mla_ref, swa_ref, of_ref, ob_ref, r_ref, g_ref, w_ref, o_ref):
    o = of_ref[0].astype(F32) + ob_ref[0].astype(F32)
    gate = _silu(r_ref[0].astype(F32))
    parts = []
    for h in range(GLA_HEADS):
        vs = slice(h * GLA_DV, (h + 1) * GLA_DV)
        parts.append((_rms(o[:, vs], g_ref[:, vs]) * gate[:, vs]).astype(BF16))
    mix = _dot(jnp.concatenate([mla_ref[0], swa_ref[0]] + parts, axis=1), w_ref[...])
    o_ref[0] = x_ref[0] + mod_ref[0, 2:3, :] * mix


def _out_projection(x, mod, mla, swa, o_f, o_b, proj, g_out, w_out, layer):
    bx, r, d = x.shape
    tm = 512
    row = lambda width: pl.BlockSpec((1, tm, width), lambda b, t: (b, t, 0))
    gw = GLA_HEADS * GLA_DV
    return pl.pallas_call(
        _outproj_kernel,
        grid=(bx, r // tm),
        in_specs=[row(d),
                  pl.BlockSpec((1, N_MOD, d), lambda b, t: (b, 0, 0)),
                  row(mla.shape[2]), row(swa.shape[2]), row(gw), row(gw),
                  pl.BlockSpec((1, tm, gw), lambda b, t: (b, t, P_GLA_R // gw)),
                  pl.BlockSpec((1, gw), lambda b, t: (0, 0)),
                  pl.BlockSpec((None,) + w_out.shape[1:], lambda b, t: (layer, 0, 0))],
        out_specs=row(d),
        out_shape=jax.ShapeDtypeStruct(x.shape, F32),
        compiler_params=_params(2),
        name="out_projection",
    )(x, mod, mla, swa, o_f, o_b, proj, g_out, w_out)


def _ffn_kernel(x_ref, mod_ref, g_ref, wg_ref, wu_ref, wd_ref, gf_ref, o_ref, h_ref, *, final_norm):
    f = pl.program_id(2)
    row_chunks = [slice(r, r + ROW_CHUNK) for r in range(0, x_ref.shape[1], ROW_CHUNK)]
    chunk = (ROW_CHUNK, x_ref.shape[2])

    def gated_partial():
        h = h_ref[...]
        act = _silu(_dot(h, wg_ref[...])) * _dot(h, wu_ref[...])
        return mod_ref[0, 5:6, :] * _dot(act.astype(BF16), wd_ref[...])

    @pl.when(f == 0)
    def _():
        gain = jnp.broadcast_to(g_ref[...] * (1.0 + mod_ref[0, 4:5, :]), chunk)
        shift = jnp.broadcast_to(mod_ref[0, 3:4, :], chunk)
        for rows in row_chunks:
            x = x_ref[0, rows, :]
            rs = lax.rsqrt(jnp.mean(x * x, axis=-1, keepdims=True) + EPS)
            h_ref[rows, :] = (x * rs * gain + shift).astype(BF16)
        o_ref[0] = x_ref[0] + gated_partial()

    @pl.when(f > 0)
    def _():
        o_ref[0] += gated_partial()

    if final_norm:
        @pl.when(f == pl.num_programs(2) - 1)
        def _():
            g_fin = jnp.broadcast_to(gf_ref[...], chunk)
            for rows in row_chunks:
                y = o_ref[0, rows, :]
                o_ref[0, rows, :] = y * lax.rsqrt(jnp.mean(y * y, axis=-1, keepdims=True) + EPS) * g_fin


def _ffn(x, mod, g, w_gu, w_down, g_final, final_norm, layer):
    bx, r, d = x.shape
    hidden = w_down.shape[1]
    tm, tf = min(r, 1024), 512
    nf = hidden // tf
    return pl.pallas_call(
        functools.partial(_ffn_kernel, final_norm=final_norm),
        grid=(bx, r // tm, nf),
        in_specs=[pl.BlockSpec((1, tm, d), lambda b, t, f: (b, t, 0)),
                  pl.BlockSpec((1, N_MOD, d), lambda b, t, f: (b, 0, 0)),
                  pl.BlockSpec((1, d), lambda b, t, f: (0, 0)),
                  pl.BlockSpec((None, d, tf), lambda b, t, f: (layer, 0, f)),
                  pl.BlockSpec((None, d, tf), lambda b, t, f: (layer, 0, nf + f)),
                  pl.BlockSpec((None, tf, d), lambda b, t, f: (layer, f, 0)),
                  pl.BlockSpec((1, d), lambda b, t, f: (0, 0))],
        out_specs=pl.BlockSpec((1, tm, d), lambda b, t, f: (b, t, 0)),
        out_shape=jax.ShapeDtypeStruct(x.shape, F32),
        scratch_shapes=[pltpu.VMEM((tm, d), BF16)],
        compiler_params=_params(3),
        name="ffn",
    )(x, mod, g, w_gu, w_gu, w_down, g_final)


def _rope_tables(n, dim):
    half = dim // 4
    freqs = ROPE_THETA ** (-jnp.arange(half, dtype=F32) / half)
    pos = jnp.arange(n, dtype=jnp.int32)
    ang_r = (pos // GRID_W).astype(F32)[:, None] * freqs
    ang_c = (pos % GRID_W).astype(F32)[:, None] * freqs
    cos = jnp.concatenate([jnp.cos(ang_r)] * 2 + [jnp.cos(ang_c)] * 2, axis=1)
    sin = jnp.concatenate([-jnp.sin(ang_r), jnp.sin(ang_r), -jnp.sin(ang_c), jnp.sin(ang_c)], axis=1)
    reps = 128 // dim
    return jnp.tile(cos, (1, reps)), jnp.tile(sin, (1, reps))


def _cast_kernel(w_ref, o_ref):
    o_ref[...] = w_ref[...].astype(o_ref.dtype)


def _to_bf16(w):
    depth, rows, cols = w.shape
    tr = max(16, min(rows, (8 * 1024 * 1024) // (4 * cols) // 16 * 16))
    while rows % tr:
        tr -= 16
    return pl.pallas_call(
        _cast_kernel,
        grid=(depth, rows // tr),
        in_specs=[pl.BlockSpec((None, tr, cols), lambda l, t: (l, t, 0))],
        out_specs=pl.BlockSpec((None, tr, cols), lambda l, t: (l, t, 0)),
        out_shape=jax.ShapeDtypeStruct(w.shape, BF16),
        compiler_params=_params(2),
        name="weight_cast",
    )(w)


def _relayout_w_in(w_in):
    depth, d, _ = w_in.shape
    sizes = (MLA_RANK, MLA_RANK, MLA_ROPE, SWA_HEADS * SWA_DIM, SWA_KV_HEADS * SWA_DIM, SWA_KV_HEADS * SWA_DIM,
             GLA_HEADS * GLA_DK, GLA_HEADS * GLA_DK, GLA_HEADS * GLA_DV, 2 * GLA_GATE_RANK, GLA_HEADS * GLA_DV)
    offs = [0]
    for s in sizes:
        offs.append(offs[-1] + s)
    runs = [(3, 5), (0, 2), (5, 6), (2, 3), (9, 10), P_GLA_Q - P_MISC - MLA_ROPE - 2 * GLA_GATE_RANK, (6, 9), (10, 11)]

    def body(w_ref, o_ref):
        dst = 0
        for run in runs:
            if isinstance(run, int):
                o_ref[:, dst:dst + run] = jnp.zeros((o_ref.shape[0], run), o_ref.dtype)
                dst += run
            else:
                src, width = offs[run[0]], offs[run[1]] - offs[run[0]]
                o_ref[:, dst:dst + width] = w_ref[:, src:src + width].astype(o_ref.dtype)
                dst += width
        assert dst == P_WIDTH

    tr = 256
    return pl.pallas_call(
        body,
        grid=(depth, d // tr),
        in_specs=[pl.BlockSpec((None, tr, w_in.shape[2]), lambda l, t: (l, t, 0))],
        out_specs=pl.BlockSpec((None, tr, P_WIDTH), lambda l, t: (l, t, 0)),
        out_shape=jax.ShapeDtypeStruct((depth, d, P_WIDTH), BF16),
        compiler_params=_params(2),
        name="w_in_relayout",
    )(w_in)


def _relayout_mla(w_uq, w_ukv):
    depth, rk, _ = w_uq.shape
    uq = w_uq.reshape(depth, rk, MLA_HEADS, MLA_QK)
    uq = jnp.concatenate([uq[..., :MLA_NOPE].reshape(depth, rk, -1), uq[..., MLA_NOPE:].reshape(depth, rk, -1)], -1)
    ukv = w_ukv.reshape(depth, rk, MLA_HEADS, MLA_NOPE + MLA_V)
    ukv = jnp.concatenate([ukv[..., :MLA_NOPE].reshape(depth, rk, -1), ukv[..., MLA_NOPE:].reshape(depth, rk, -1)], -1)
    return uq.astype(BF16), ukv.astype(BF16)


def _relayout_gate(w_f, b_f, w_b, b_b):
    depth, rank, width = w_f.shape
    wg = jnp.zeros((depth, 128, 2 * width), F32)
    wg = wg.at[:, MISC_GATE_OFF:MISC_GATE_OFF + rank, :width].set(w_f)
    wg = wg.at[:, MISC_GATE_OFF + rank:MISC_GATE_OFF + 2 * rank, width:].set(w_b)
    bg = jnp.concatenate([b_f, b_b], axis=-1).reshape(depth, 1, 2 * width)
    return wg.astype(BF16), bg


def kernel(x, c, ctx, c_ctx, w_mod, b_mod, g_mix, g_ffn, w_in, g_mla_q, g_mla_kv, w_mla_uq, w_mla_ukv,
           swa_sink, w_gla_gate_f, b_gla_gate_f, w_gla_gate_b, b_gla_gate_b, g_gla_out, w_out, w_ffn_gu,
           w_ffn_down, g_final):
    B, N, D = x.shape
    C = ctx.shape[1]
    depth = w_mod.shape[0]

    cvec = jnp.concatenate([c, c_ctx[None, :], jnp.zeros((16 - B - 1, D), F32)], axis=0)
    mod = _modulation(cvec, w_mod, b_mod).reshape(depth, 16, N_MOD, D)

    w_in_p = _relayout_w_in(w_in)
    w_uq_p, w_ukv_p = _relayout_mla(w_mla_uq, w_mla_ukv)
    wg_p, bg_p = _relayout_gate(w_gla_gate_f, b_gla_gate_f, w_gla_gate_b, b_gla_gate_b)
    w_out_b = _to_bf16(w_out)
    w_gu_b = _to_bf16(w_ffn_gu)
    w_down_b = _to_bf16(w_ffn_down)

    cos_m, sin_m = _rope_tables(N, MLA_ROPE)
    cos_s, sin_s = _rope_tables(N, SWA_DIM)
    cos_id, sin_id = jnp.ones((C, 128), F32), jnp.zeros((C, 128), F32)
    state0 = jnp.zeros((B,) + GLA_STATE, F32)
    g_fin = g_final.reshape(1, D)

    xc = ctx.reshape(1, B * C, D)
    for l in range(depth):
        last = l == depth - 1
        mod_l, mod_c = mod[l, :B], mod[l, B:B + 1]
        g_mix_l, g_ffn_l = g_mix[l].reshape(1, D), g_ffn[l].reshape(1, D)
        g_q, g_kv = g_mla_q[l].reshape(1, -1), g_mla_kv[l].reshape(1, -1)
        g_out = g_gla_out[l].reshape(1, -1)
        sink = swa_sink[l].reshape(1, -1)

        proj_l = _in_projection(x, mod_l, g_mix_l, w_in_p, l)
        proj_c = _in_projection(xc, mod_c, g_mix_l, w_in_p, l).reshape(B, C, P_WIDTH)

        q_l, k_l, v_l = _mla_prep(proj_l, cos_m, sin_m, g_q, g_kv, w_uq_p, w_ukv_p, l)
        q_c, k_c, v_c = _mla_prep(proj_c, cos_id, sin_id, g_q, g_kv, w_uq_p, w_ukv_p, l)
        mla_l = _mla_attention(q_l, k_l, v_l, k_c, v_c)
        swa_l = _swa_attention(proj_l, proj_c, cos_s, sin_s, sink)

        of_c, ob_c, s_f, s_b = _gla_scan(proj_c, wg_p, bg_p, state0, state0, l)
        of_l, ob_l, _, _ = _gla_scan(proj_l, wg_p, bg_p, s_f, s_b, l)

        x = _out_projection(x, mod_l, mla_l, swa_l, of_l, ob_l, proj_l, g_out, w_out_b, l)
        x = _ffn(x, mod_l, g_ffn_l, w_gu_b, w_down_b, g_fin, last, l)

        if not last:
            mla_c = _mla_ctx_attention(q_c, k_c, v_c)
            swa_c = _swa_ctx_attention(proj_c, sink)
            flat = lambda a: a.reshape(1, B * C, a.shape[-1])
            xc = _out_projection(xc, mod_c, flat(mla_c), flat(swa_c), flat(of_c), flat(ob_c), flat(proj_c),
                                 g_out, w_out_b, l)
            xc = _ffn(xc, mod_c, g_ffn_l, w_gu_b, w_down_b, g_fin, False, l)
    return x
```
